```python
import math
import jax, jax.numpy as jnp
from jax import lax
import numpy as np

D_MODEL = 1024
BATCH = 32
SEQ = 256
DEPTH = 2
DEC_BATCH = 8
DEC_SEQ = 1024
PAST_LEN = 256

GRID_W = 64
HEAD_DIM = 64
BLOCK = 128
ROPE_BASE = 10000.0
EPS = 1e-6
WIN_HEADS = 8
WIN_KV_HEADS = 2
WIN_GROUP = WIN_HEADS // WIN_KV_HEADS
WINDOW = 128
WIN_Q_W = WIN_HEADS * HEAD_DIM
WIN_KV_W = WIN_KV_HEADS * HEAD_DIM
ATTN_SCALE = HEAD_DIM ** -0.5
RET_HEADS = 4
RET_CHUNK = 128
RET_W = RET_HEADS * HEAD_DIM
RET_K_SCALE = HEAD_DIM ** -0.5
MLA_HEADS = 4
MLA_NOPE = 64
MLA_ROPE = 32
MLA_V = 64
MLA_KV_RANK = 128
MLA_QK = MLA_NOPE + MLA_ROPE
MLA_Q_W = MLA_HEADS * MLA_QK
MLA_SCALE = MLA_QK ** -0.5
D_IN = WIN_Q_W + 2 * WIN_KV_W + 4 * RET_W + MLA_Q_W + MLA_KV_RANK + MLA_ROPE
MIX_W = WIN_Q_W + RET_W + MLA_HEADS * MLA_V
D_FF = 4 * D_MODEL
N_MOD = 6

kernel_name = 'hybrid_prefix_diffusion_step'


def rms_norm(x, gain):
    xf = x.astype(jnp.float32)
    y = xf * lax.rsqrt(jnp.mean(xf * xf, axis=-1, keepdims=True) + EPS)
    return (y * gain.astype(jnp.float32)).astype(x.dtype)


def head_layer_norm(x, gain):
    mu = jnp.mean(x, axis=-1, keepdims=True)
    var = jnp.mean(jnp.square(x - mu), axis=-1, keepdims=True)
    y = ((x - mu) * lax.rsqrt(var + EPS)).reshape(x.shape[:-2] + (-1,))
    return y * gain.astype(jnp.float32)


def axial_angles(n_tokens, dim):
    rows = n_tokens // GRID_W
    t = jnp.arange(rows * GRID_W)
    row = (t // GRID_W).astype(jnp.float32)
    col = (t % GRID_W).astype(jnp.float32)
    n_freq = dim // 4
    inv_freq = ROPE_BASE ** (-jnp.arange(n_freq, dtype=jnp.float32) / n_freq)
    return row[:, None] * inv_freq, col[:, None] * inv_freq


def rotate(x, ang):
    x1, x2 = jnp.split(x, 2, axis=-1)
    c, s = jnp.cos(ang), jnp.sin(ang)
    return jnp.concatenate([x1 * c - x2 * s, x1 * s + x2 * c], axis=-1)


def axial_rope(x, ang_row, ang_col):
    xr, xc = jnp.split(x.astype(jnp.float32), 2, axis=-1)
    out = jnp.concatenate([rotate(xr, ang_row[:, None, :]), rotate(xc, ang_col[:, None, :])], axis=-1)
    return out.astype(x.dtype)


def sink_softmax(s, sink):
    m = jnp.maximum(jnp.max(s, axis=-1, keepdims=True), sink)
    p = jnp.exp(s - m)
    return p / (jnp.sum(p, axis=-1, keepdims=True) + jnp.exp(sink - m))


def window_attn_context(q, k, v, sink):
    b, l = q.shape[:2]
    qg = q.reshape(b, l, WIN_KV_HEADS, WIN_GROUP, HEAD_DIM)
    s = jnp.einsum('bqkgd,bskd->bkgqs', qg, k).astype(jnp.float32) * ATTN_SCALE
    p = sink_softmax(s, sink.astype(jnp.float32).reshape(1, WIN_KV_HEADS, WIN_GROUP, 1, 1))
    o = jnp.einsum('bkgqs,bskd->bqkgd', p.astype(v.dtype), v)
    return o.reshape(b, l, WIN_Q_W)


def window_attn_latent(q, k, v, k_ctx, v_ctx, sink):
    b, t = q.shape[:2]
    nb = t // BLOCK
    qb = q.reshape(b, nb, BLOCK, WIN_KV_HEADS, WIN_GROUP, HEAD_DIM)

    def neighbours(a):
        ap = jnp.pad(a, ((0, 0), (BLOCK, BLOCK), (0, 0), (0, 0))).reshape(b, nb + 2, BLOCK, WIN_KV_HEADS, HEAD_DIM)
        return jnp.concatenate([ap[:, :-2], ap[:, 1:-1], ap[:, 2:]], axis=2)

    kn, vn = neighbours(k), neighbours(v)
    qpos = jnp.arange(nb)[:, None] * BLOCK + jnp.arange(BLOCK)[None, :]
    kpos = jnp.arange(nb)[:, None] * BLOCK - BLOCK + jnp.arange(3 * BLOCK)[None, :]
    kp = kpos[:, None, :]
    valid = (jnp.abs(kp - qpos[:, :, None]) <= WINDOW) & (kp >= 0) & (kp < t)
    s_loc = jnp.einsum('bnqkgd,bnskd->bnkgqs', qb, kn).astype(jnp.float32) * ATTN_SCALE
    s_loc = jnp.where(valid[None, :, None, None], s_loc, -jnp.inf)
    s_ctx = jnp.einsum('bnqkgd,bskd->bnkgqs', qb, k_ctx).astype(jnp.float32) * ATTN_SCALE
    p = sink_softmax(jnp.concatenate([s_loc, s_ctx], axis=-1),
                     sink.astype(jnp.float32).reshape(1, 1, WIN_KV_HEADS, WIN_GROUP, 1, 1))
    p_loc = p[..., :3 * BLOCK].astype(v.dtype)
    p_ctx = p[..., 3 * BLOCK:].astype(v.dtype)
    o = (jnp.einsum('bnkgqs,bnskd->bnqkgd', p_loc, vn)
         + jnp.einsum('bnkgqs,bskd->bnqkgd', p_ctx, v_ctx))
    return o.reshape(b, t, WIN_Q_W)


def retention_scan(q, k, v, log_gamma, s0):
    b, t, h, d = q.shape
    nc = t // RET_CHUNK
    idx = jnp.arange(RET_CHUNK, dtype=jnp.float32)
    rel = idx[:, None] - idx[None, :]
    lower = rel >= 0
    intra = jnp.where(lower[None], jnp.exp(jnp.where(lower, rel, 0.0)[None] * log_gamma[:, None, None]), 0.0)
    q_dec = jnp.exp((idx[:, None] + 1.0) * log_gamma[None, :])
    k_dec = jnp.exp((RET_CHUNK - 1.0 - idx[:, None]) * log_gamma[None, :])
    c_dec = jnp.exp(RET_CHUNK * log_gamma)

    def chunks(a):
        return a.reshape(b, nc, RET_CHUNK, h, d).transpose(1, 0, 2, 3, 4)

    def step(state, inp):
        qc, kc, vc = inp
        scores = jnp.einsum('bihd,bjhd->bhij', qc, kc) * intra
        o = (jnp.einsum('bhij,bjhe->bihe', scores, vc)
             + jnp.einsum('bihd,bhde->bihe', qc * q_dec[:, :, None], state))
        state = c_dec[:, None, None] * state + jnp.einsum('bjhd,bjhe->bhde', kc * k_dec[:, :, None], vc)
        return state, o

    s_fin, o = lax.scan(step, s0, (chunks(q), chunks(k), chunks(v)))
    return o.transpose(1, 0, 2, 3, 4).reshape(b, t, h, d), s_fin


def bidir_retention(q, k, v, g, decay_logit, gn_gain, s0):
    dt = q.dtype
    qf = q.astype(jnp.float32)
    kf = k.astype(jnp.float32) * RET_K_SCALE
    vf = v.astype(jnp.float32)
    log_gamma = jax.nn.log_sigmoid(decay_logit.astype(jnp.float32))
    s0 = s0.astype(jnp.float32)
    o_f, s_f = retention_scan(qf, kf, vf, log_gamma[0], s0[:, 0])
    o_b, s_b = retention_scan(jnp.flip(qf, 1), jnp.flip(kf, 1), jnp.flip(vf, 1), log_gamma[1], s0[:, 1])
    o = o_f + jnp.flip(o_b, 1)
    y = head_layer_norm(o, gn_gain) * jax.nn.silu(g.astype(jnp.float32))
    return y.astype(dt), jnp.stack([s_f, s_b], axis=1)


def mla_expand(ckv_n, w_kv_b):
    kv = (ckv_n @ w_kv_b).reshape(ckv_n.shape[:2] + (MLA_HEADS, MLA_NOPE + MLA_V))
    return kv[..., :MLA_NOPE], kv[..., MLA_NOPE:]


def mla_attend(q_nope, q_rope, k_nope, k_rope, v):
    s = (jnp.einsum('bqhd,bshd->bhqs', q_nope, k_nope)
         + jnp.einsum('bqhr,bsr->bhqs', q_rope, k_rope)).astype(jnp.float32) * MLA_SCALE
    p = jax.nn.softmax(s, axis=-1).astype(v.dtype)
    return jnp.einsum('bhqs,bshd->bqhd', p, v)


def mla_latent(q_nope, q_rope, k_nope, k_rope, v):
    b, t = q_nope.shape[:2]
    nb = t // BLOCK

    def blocks(a):
        return a.reshape((b, nb, BLOCK) + a.shape[2:]).swapaxes(0, 1)

    o = lax.map(lambda qs: mla_attend(qs[0], qs[1], k_nope, k_rope, v), (blocks(q_nope), blocks(q_rope)))
    return o.swapaxes(0, 1).reshape(b, t, MLA_HEADS * MLA_V)


def in_split_points():
    sizes = (WIN_Q_W, WIN_KV_W, WIN_KV_W, RET_W, RET_W, RET_W, RET_W, MLA_Q_W, MLA_KV_RANK, MLA_ROPE)
    points, acc = [], 0
    for s in sizes[:-1]:
        acc += s
        points.append(acc)
    return points


def trunk_layer(x, mod, lp, ctx=None):
    norm1, norm2, w_in, sink, decay, gn, kvn, w_kv_b, w_out, w_up, w_down = lp
    b, n, _ = x.shape
    shift1, scale1, gate1, shift2, scale2, gate2 = [mod[:, i][:, None, :] for i in range(N_MOD)]
    h = rms_norm(x, norm1) * (1.0 + scale1) + shift1
    proj = h @ w_in
    qa, ka, va, qb, kb, vb, gb, qc, ckv, krope = jnp.split(proj, in_split_points(), axis=-1)
    qa = qa.reshape(b, n, WIN_HEADS, HEAD_DIM)
    ka = ka.reshape(b, n, WIN_KV_HEADS, HEAD_DIM)
    va = va.reshape(b, n, WIN_KV_HEADS, HEAD_DIM)
    qb = qb.reshape(b, n, RET_HEADS, HEAD_DIM)
    kb = kb.reshape(b, n, RET_HEADS, HEAD_DIM)
    vb = vb.reshape(b, n, RET_HEADS, HEAD_DIM)
    qc = qc.reshape(b, n, MLA_HEADS, MLA_QK)
    q_nope, q_rope = qc[..., :MLA_NOPE], qc[..., MLA_NOPE:]
    ckv_n = rms_norm(ckv, kvn)
    k_nope, v_c = mla_expand(ckv_n, w_kv_b)
    if ctx is None:
        out_a = window_attn_context(qa, ka, va, sink)
        s_zero = jnp.zeros((b, 2, RET_HEADS, HEAD_DIM, HEAD_DIM), jnp.float32)
        out_b, s_ret = bidir_retention(qb, kb, vb, gb, decay, gn, s_zero)
        out_c = mla_attend(q_nope, q_rope, k_nope, krope, v_c).reshape(b, n, MLA_HEADS * MLA_V)
        new = (ka, va, ckv_n, krope, s_ret.astype(x.dtype))
    else:
        c_k, c_v, c_ckv, c_krope, c_state = ctx
        ar, ac = axial_angles(n, HEAD_DIM)
        out_a = window_attn_latent(axial_rope(qa, ar, ac), axial_rope(ka, ar, ac), va, c_k, c_v, sink)
        out_b, _ = bidir_retention(qb, kb, vb, gb, decay, gn, c_state)
        mr, mc = axial_angles(n, MLA_ROPE)
        q_rope = axial_rope(q_rope, mr, mc)
        krope = axial_rope(krope[:, :, None, :], mr, mc)[:, :, 0, :]
        ck_nope, cv_c = mla_expand(c_ckv, w_kv_b)
        out_c = mla_latent(q_nope, q_rope,
                           jnp.concatenate([k_nope, ck_nope], axis=1),
                           jnp.concatenate([krope, c_krope], axis=1),
                           jnp.concatenate([v_c, cv_c], axis=1))
        new = None
    mix = jnp.concatenate([out_a, out_b, out_c], axis=-1) @ w_out
    x = x + gate1 * mix
    h2 = rms_norm(x, norm2) * (1.0 + scale2) + shift2
    x = x + gate2 * (jnp.square(jax.nn.relu(h2 @ w_up)) @ w_down)
    return x, new


def setup_inputs(seed: int = 0) -> dict:
    key = jax.random.key(seed)
    ks = jax.random.split(key, 26)

    def nrm(k, shape, s=1.0):
        return jax.random.normal(k, shape, jnp.float32) * s

    base = 1.0 - 2.0 ** (-5.0 - jnp.arange(RET_HEADS, dtype=jnp.float32))
    decay_logit = jnp.log(base) - jnp.log1p(-base)
    return {
        'x_prompt': nrm(ks[0], (BATCH, SEQ, D_MODEL)),
        'x_sample': nrm(ks[1], (DEC_BATCH, DEC_SEQ, D_MODEL)),
        'cache_win_k': nrm(ks[2], (DEC_BATCH, DEPTH, PAST_LEN, WIN_KV_HEADS, HEAD_DIM)),
        'cache_win_v': nrm(ks[3], (DEC_BATCH, DEPTH, PAST_LEN, WIN_KV_HEADS, HEAD_DIM)),
        'cache_mla_ckv': nrm(ks[4], (DEC_BATCH, DEPTH, PAST_LEN, MLA_KV_RANK)),
        'cache_mla_krope': nrm(ks[5], (DEC_BATCH, DEPTH, PAST_LEN, MLA_ROPE)),
        'state_ret': nrm(ks[6], (DEC_BATCH, DEPTH, 2, RET_HEADS, HEAD_DIM, HEAD_DIM), 0.5),
        'c': nrm(ks[7], (DEC_BATCH, D_MODEL)),
        'c_ctx': nrm(ks[8], (D_MODEL,)),
        'w_mod': nrm(ks[9], (DEPTH, D_MODEL, N_MOD * D_MODEL), 0.5 * D_MODEL ** -0.5),
        'b_mod': nrm(ks[10], (DEPTH, N_MOD * D_MODEL), 0.02),
        'norm1': 1.0 + nrm(ks[11], (DEPTH, D_MODEL), 0.02),
        'norm2': 1.0 + nrm(ks[12], (DEPTH, D_MODEL), 0.02),
        'w_in': nrm(ks[13], (DEPTH, D_MODEL, D_IN), D_MODEL ** -0.5),
        'win_sink': nrm(ks[14], (DEPTH, WIN_HEADS), 0.5),
        'ret_decay': decay_logit[None, None, :] + nrm(ks[15], (DEPTH, 2, RET_HEADS), 0.1),
        'ret_gn': 1.0 + nrm(ks[16], (DEPTH, RET_W), 0.02),
        'mla_kv_norm': 1.0 + nrm(ks[17], (DEPTH, MLA_KV_RANK), 0.02),
        'w_kv_b': nrm(ks[18], (DEPTH, MLA_KV_RANK, MLA_HEADS * (MLA_NOPE + MLA_V)), MLA_KV_RANK ** -0.5),
        'w_out': nrm(ks[19], (DEPTH, MIX_W, D_MODEL), MIX_W ** -0.5),
        'w_up': nrm(ks[20], (DEPTH, D_MODEL, D_FF), D_MODEL ** -0.5),
        'w_down': nrm(ks[21], (DEPTH, D_FF, D_MODEL), D_FF ** -0.5),
        'final_norm': 1.0 + nrm(ks[22], (D_MODEL,), 0.02),
    }


def reference(x_prompt, x_sample, cache_win_k, cache_win_v, cache_mla_ckv, cache_mla_krope, state_ret,
              c, c_ctx, w_mod, b_mod, norm1, norm2, w_in, win_sink, ret_decay, ret_gn, mla_kv_norm,
              w_kv_b, w_out, w_up, w_down, final_norm):
    silu_ctx = jax.nn.silu(c_ctx)[None, :]
    silu_lat = jax.nn.silu(c)
    xp, xs = x_prompt, x_sample
    new_k, new_v, new_ckv, new_kr, new_s = [], [], [], [], []
    for l in range(DEPTH):
        lp = (norm1[l], norm2[l], w_in[l], win_sink[l], ret_decay[l], ret_gn[l], mla_kv_norm[l],
              w_kv_b[l], w_out[l], w_up[l], w_down[l])
        mod_ctx = (silu_ctx @ w_mod[l] + b_mod[l]).reshape(1, N_MOD, D_MODEL)
        mod_lat = (silu_lat @ w_mod[l] + b_mod[l]).reshape(-1, N_MOD, D_MODEL)
        xp, (k_l, v_l, ckv_l, kr_l, s_l) = trunk_layer(xp, mod_ctx, lp)
        ctx_l = (cache_win_k[:, l], cache_win_v[:, l], cache_mla_ckv[:, l], cache_mla_krope[:, l], state_ret[:, l])
        xs, _ = trunk_layer(xs, mod_lat, lp, ctx_l)
        new_k.append(k_l)
        new_v.append(v_l)
        new_ckv.append(ckv_l)
        new_kr.append(kr_l)
        new_s.append(s_l)
    y_prompt = rms_norm(xp, final_norm)
    y_sample = rms_norm(xs, final_norm)
    return (y_prompt, y_sample, jnp.stack(new_k, axis=1), jnp.stack(new_v, axis=1),
            jnp.stack(new_ckv, axis=1), jnp.stack(new_kr, axis=1), jnp.stack(new_s, axis=1))
```

```python
import functools

import numpy as np
import jax
import jax.numpy as jnp
from jax import lax
from jax.experimental import pallas as pl
from jax.experimental.pallas import tpu as pltpu

F32 = jnp.float32
BF16 = jnp.bfloat16

D_MODEL = 1024
DEPTH = 2
SEQ = 256
DEC_SEQ = 1024
PAST_LEN = 256
GRID_W = 64
HEAD_DIM = 64
ROPE_BASE = 10000.0
EPS = 1e-6
WIN_HEADS = 8
WINDOW = 128
ATTN_SCALE = HEAD_DIM ** -0.5
RET_HEADS = 4
RET_CHUNK = 128
RET_K_SCALE = HEAD_DIM ** -0.5
MLA_HEADS = 4
MLA_NOPE = 64
MLA_ROPE = 32
MLA_KV_RANK = 128
MLA_QK = MLA_NOPE + MLA_ROPE
MLA_SCALE = MLA_QK ** -0.5
D_IN = 2336
D_FF = 4 * D_MODEL
N_MOD = 6

LANES = 128
TOK = 1024
MLP_ROWS = 512
FF_CHUNK = 1024
VMEM_LIMIT = 60 * 1024 * 1024

C_QA, C_KA, C_VA, C_QB, C_KB, C_VB, C_GB, C_QC, C_CKV, C_KR, D_IN_P = (
    0, 512, 640, 768, 1024, 1280, 1536, 1792, 2304, 2432, 2560)
M_A, M_B, M_C = 0, 512, 768

NT_DIMS = (((1,), (1,)), ((), ()))


def _in_proj_columns():
    idx = []
    for g in range(4):
        idx += list(range(g * 64, (g + 1) * 64)) + list(range((4 + g) * 64, (5 + g) * 64))
    idx += list(range(512, 1792))
    for h in range(MLA_HEADS):
        idx += list(range(1792 + h * MLA_QK, 1792 + (h + 1) * MLA_QK)) + [-1] * 32
    idx += list(range(2176, 2304))
    idx += [-1] * 64 + list(range(2304, 2336)) + [-1] * 32
    return np.asarray(idx, np.int32)


def _mix_rows():
    idx = []
    for g in range(4):
        idx += list(range(g * 64, (g + 1) * 64)) + list(range((4 + g) * 64, (5 + g) * 64))
    idx += list(range(512, 1024))
    return np.asarray(idx, np.int32)


def _rope_tables(n_tokens, dim, lane0, period):
    quarter = dim // 4
    t = jnp.arange(n_tokens)
    row = (t // GRID_W).astype(F32)
    col = (t % GRID_W).astype(F32)
    inv_freq = ROPE_BASE ** (-jnp.arange(quarter, dtype=F32) / quarter)
    ar, ac = row[:, None] * inv_freq, col[:, None] * inv_freq
    cos = jnp.concatenate([jnp.cos(ar), jnp.cos(ar), jnp.cos(ac), jnp.cos(ac)], axis=-1)
    sin = jnp.concatenate([jnp.sin(ar), jnp.sin(ar), jnp.sin(ac), jnp.sin(ac)], axis=-1)
    first = np.tile(np.concatenate([np.ones(quarter), np.zeros(quarter)]), 2).astype(np.float32)
    c = jnp.ones((n_tokens, LANES), F32)
    sa = jnp.zeros((n_tokens, LANES), F32)
    sb = jnp.zeros((n_tokens, LANES), F32)
    for start in range(lane0, LANES, period):
        c = c.at[:, start:start + dim].set(cos)
        sa = sa.at[:, start:start + dim].set(-sin * first)
        sb = sb.at[:, start:start + dim].set(sin * (1.0 - first))
    return c, sa, sb


def _lane_iota(shape):
    return lax.broadcasted_iota(jnp.int32, shape, len(shape) - 1)


def _row_iota(shape):
    return lax.broadcasted_iota(jnp.int32, shape, len(shape) - 2)


def _norm_mod(x, gain, scale, shift):
    ms = jnp.mean(x * x, axis=-1, keepdims=True)
    return (x * lax.rsqrt(ms + EPS) * gain) * (1.0 + scale) + shift


def _log_sigmoid(x):
    return -(jnp.maximum(-x, 0.0) + jnp.log1p(jnp.exp(-jnp.abs(x))))


def _rope(x, c, sa, sb, quarter):
    return (x * c + pltpu.roll(x, LANES - quarter, 1) * sa + pltpu.roll(x, quarter, 1) * sb)


def _dot(a, b):
    return jnp.dot(a, b, preferred_element_type=F32)


def _dot_nt(a, b):
    return lax.dot_general(a, b, NT_DIMS, preferred_element_type=F32)


def _mod_kernel(c_ref, w_ref, b_ref, o_ref):
    cv = c_ref[...]
    s = cv * jax.nn.sigmoid(cv)
    o_ref[0] = _dot(s.astype(BF16), w_ref[0].astype(BF16)) + b_ref[0]


def _modulation(c_rows, w_mod, b_mod):
    tn = 1536
    nj = (N_MOD * D_MODEL) // tn
    return pl.pallas_call(
        _mod_kernel,
        out_shape=jax.ShapeDtypeStruct((DEPTH, 16, N_MOD * D_MODEL), F32),
        grid=(DEPTH, nj),
        in_specs=[
            pl.BlockSpec((16, D_MODEL), lambda l, j: (0, 0)),
            pl.BlockSpec((1, D_MODEL, tn), lambda l, j: (l, 0, j)),
            pl.BlockSpec((1, 1, tn), lambda l, j: (l, 0, j)),
        ],
        out_specs=pl.BlockSpec((1, 16, tn), lambda l, j: (l, 0, j)),
        compiler_params=pltpu.CompilerParams(
            dimension_semantics=("arbitrary", "arbitrary"), vmem_limit_bytes=VMEM_LIMIT),
        name="modulation",
    )(c_rows, w_mod, b_mod.reshape(DEPTH, 1, N_MOD * D_MODEL))


def _retention_tables(decay_ref, rt_scr, m2_scr):
    shape = (RET_CHUNK, LANES)
    lane_lo = _lane_iota(shape) < 64
    row = _row_iota(shape)
    row_lo = row < 64
    i = row.astype(F32)
    rel = i - _lane_iota(shape).astype(F32)
    for j in range(2):
        df0, df1 = decay_ref[0, 2 * j], decay_ref[0, 2 * j + 1]
        db0, db1 = decay_ref[1, 2 * j], decay_ref[1, 2 * j + 1]
        lgf = _log_sigmoid(jnp.where(lane_lo, df0, df1))
        lgb = _log_sigmoid(jnp.where(lane_lo, db0, db1))
        rt_scr[j, 0] = jnp.exp((i + 1.0) * lgf)
        rt_scr[j, 1] = jnp.exp((RET_CHUNK - i) * lgb)
        rt_scr[j, 2] = jnp.exp((RET_CHUNK - 1.0 - i) * lgf)
        rt_scr[j, 3] = jnp.exp(i * lgb)
        rt_scr[j, 4] = jnp.exp(RET_CHUNK * _log_sigmoid(jnp.where(row_lo, df0, df1)))
        rt_scr[j, 5] = jnp.exp(RET_CHUNK * _log_sigmoid(jnp.where(row_lo, db0, db1)))
        for hh, (df, db) in enumerate(((df0, db0), (df1, db1))):
            lf = _log_sigmoid(jnp.full(shape, df, F32))
            lb = _log_sigmoid(jnp.full(shape, db, F32))
            low, upp = rel >= 0.0, rel <= 0.0
            m = (jnp.where(low, jnp.exp(jnp.where(low, rel, 0.0) * lf), 0.0)
                 + jnp.where(upp, jnp.exp(jnp.where(upp, -rel, 0.0) * lb), 0.0))
            m2_scr[j, hh * RET_CHUNK:(hh + 1) * RET_CHUNK, :] = m


def _retention(base, nc, qb_scr, kb_scr, vb_scr, gb_scr, gn_ref, rt_scr, m2_scr,
               sball_scr, st_scr, mix_scr):
    shape = (RET_CHUNK, LANES)
    lane_lo = _lane_iota(shape) < 64
    blockdiag = lane_lo == (_row_iota(shape) < 64)

    def rows(c):
        return pl.ds(pl.multiple_of(base + c * RET_CHUNK, RET_CHUNK), RET_CHUNK)

    for j in range(2):
        cols = slice(j * LANES, (j + 1) * LANES)

        def state_update(dirn, c, cols=cols, j=j):
            kd = kb_scr[rows(c), cols] * rt_scr[j, 2 + dirn]
            upd = _dot(kd.T.astype(BF16), vb_scr[rows(c), cols])
            st_scr[dirn, j] = rt_scr[j, 4 + dirn] * st_scr[dirn, j] + jnp.where(blockdiag, upd, 0.0)

        def bwd_body(t, carry, j=j, state_update=state_update):
            c = nc - 1 - t
            sball_scr[c, j] = st_scr[1, j]
            state_update(1, c)
            return carry

        lax.fori_loop(0, nc, bwd_body, 0)

        def fwd_body(c, carry, cols=cols, j=j, state_update=state_update):
            q2 = qb_scr[rows(c), cols]
            k2 = kb_scr[rows(c), cols]
            v2 = vb_scr[rows(c), cols]
            qs = jnp.concatenate([jnp.where(lane_lo, q2, 0.0), jnp.where(lane_lo, 0.0, q2)],
                                 axis=0).astype(BF16)
            sc = _dot_nt(qs, k2.astype(BF16)) * m2_scr[j]
            o2 = _dot(sc.astype(BF16), v2)
            o = jnp.where(lane_lo, o2[:RET_CHUNK], o2[RET_CHUNK:])
            o = o + _dot((q2 * rt_scr[j, 0]).astype(BF16), st_scr[0, j].astype(BF16))
            o = o + _dot((q2 * rt_scr[j, 1]).astype(BF16), sball_scr[c, j].astype(BF16))
            s_lo = jnp.sum(jnp.where(lane_lo, o, 0.0), axis=-1, keepdims=True)
            s_hi = jnp.sum(jnp.where(lane_lo, 0.0, o), axis=-1, keepdims=True)
            d = o - jnp.where(lane_lo, s_lo, s_hi) * (1.0 / HEAD_DIM)
            dd = d * d
            v_lo = jnp.sum(jnp.where(lane_lo, dd, 0.0), axis=-1, keepdims=True)
            v_hi = jnp.sum(jnp.where(lane_lo, 0.0, dd), axis=-1, keepdims=True)
            var = jnp.where(lane_lo, v_lo, v_hi) * (1.0 / HEAD_DIM)
            g2 = gb_scr[rows(c), cols]
            y = d * lax.rsqrt(var + EPS) * gn_ref[:, cols] * (g2 * jax.nn.sigmoid(g2))
            mix_scr[rows(c), M_B + j * LANES:M_B + (j + 1) * LANES] = y.astype(BF16)
            state_update(0, c)
            return carry

        lax.fori_loop(0, nc, fwd_body, 0)


def _prologue(x_ref, mod_ref, n1_ref, h_scr):
    def body(r, carry):
        rows = pl.ds(pl.multiple_of(r * 256, 256), 256)
        h = _norm_mod(x_ref[rows, :], n1_ref[...], mod_ref[1:2, :], mod_ref[0:1, :])
        h_scr[rows, :] = h.astype(BF16)
        return carry
    lax.fori_loop(0, TOK // 256, body, 0)


def _epilogue(x_ref, mod_ref, wout_ref, mix_scr, xo_ref):
    def body(r, carry):
        rows = pl.ds(pl.multiple_of(r * 256, 256), 256)
        y = _dot(mix_scr[rows, :], wout_ref[...])
        xo_ref[rows, :] = x_ref[rows, :] + mod_ref[2:3, :] * y
        return carry
    lax.fori_loop(0, TOK // 256, body, 0)


def _softmax_stats(parts, floor=None):
    m = None
    for s in parts:
        pm = jnp.max(s, axis=-1, keepdims=True)
        m = pm if m is None else jnp.maximum(m, pm)
    if floor is not None:
        m = jnp.maximum(m, floor)
    ps = [jnp.exp(s - m) for s in parts]
    den = None
    for p in ps:
        d = jnp.sum(p, axis=-1, keepdims=True)
        den = d if den is None else den + d
    if floor is not None:
        den = den + jnp.exp(floor - m)
    return ps, den


CTX_PER_STEP = TOK // SEQ


def _ctx_mixer_kernel(x_ref, mod_ref, n1_ref, win_ref, sink_ref, decay_ref, gn_ref, kvn_ref,
                      wkvb_ref, wout_ref,
                      xo_ref, ko_ref, vo_ref, ckvo_ref, kro_ref, so_ref,
                      h_scr, qst_scr, k_scr, v_scr, qb_scr, kb_scr, vb_scr, gb_scr, qc_scr,
                      kcat_scr, kvf_scr, rt_scr, m2_scr, sball_scr, st_scr, mix_scr):
    _prologue(x_ref, mod_ref, n1_ref, h_scr)
    _retention_tables(decay_ref, rt_scr, m2_scr)
    lane_lo = _lane_iota((TOK, LANES)) < 64

    def proj(c0, c1):
        return _dot(h_scr[...], win_ref[:, c0:c1])

    for g in range(4):
        q = proj(C_QA + g * LANES, C_QA + (g + 1) * LANES) * ATTN_SCALE
        lo = jnp.where(lane_lo, q, 0.0).astype(BF16).reshape(CTX_PER_STEP, SEQ, LANES)
        hi = jnp.where(lane_lo, 0.0, q).astype(BF16).reshape(CTX_PER_STEP, SEQ, LANES)
        qst_scr[:, g * SEQ:(g + 1) * SEQ, :] = lo
        qst_scr[:, (4 + g) * SEQ:(5 + g) * SEQ, :] = hi
    ka = proj(C_KA, C_VA)
    ko_ref[...] = ka
    k_scr[...] = ka.astype(BF16)
    va = proj(C_VA, C_QB)
    vo_ref[...] = va
    v_scr[...] = va.astype(BF16)
    qb_scr[...] = proj(C_QB, C_KB)
    kb_scr[...] = proj(C_KB, C_VB) * RET_K_SCALE
    vb_scr[...] = proj(C_VB, C_GB).astype(BF16)
    gb_scr[...] = proj(C_GB, C_QC)
    qc_scr[...] = proj(C_QC, C_CKV).astype(BF16)
    ckv = proj(C_CKV, C_KR)
    ckv_n = ckv * lax.rsqrt(jnp.mean(ckv * ckv, axis=-1, keepdims=True) + EPS) * kvn_ref[...]
    ckvo_ref[...] = ckv_n
    kv = _dot(ckv_n.astype(BF16), wkvb_ref[...])
    kvf_scr[...] = kv.astype(BF16)
    kr = proj(C_KR, D_IN_P)
    kro_ref[...] = kr[:, 64:64 + MLA_ROPE]
    for h in range(MLA_HEADS):
        kcat_scr[h] = jnp.where(lane_lo, kv[:, h * LANES:(h + 1) * LANES], kr).astype(BF16)

    lane_lo_a = _lane_iota((SEQ, LANES)) < 64
    sink_col = jnp.concatenate(
        [jnp.full((SEQ, 1), sink_ref[h], F32) for h in range(WIN_HEADS)], axis=0)

    def seq_body(e, carry):
        base = pl.multiple_of(e * SEQ, SEQ)
        rows = pl.ds(base, SEQ)
        s = _dot_nt(qst_scr[e], k_scr[rows, :])
        (p,), den = _softmax_stats([s], sink_col)
        o = _dot(p.astype(BF16), v_scr[rows, :]) / den
        for g in range(4):
            merged = jnp.where(lane_lo_a, o[g * SEQ:(g + 1) * SEQ], o[(4 + g) * SEQ:(5 + g) * SEQ])
            mix_scr[rows, M_A + g * LANES:M_A + (g + 1) * LANES] = merged.astype(BF16)
        for dirn in range(2):
            for j in range(2):
                st_scr[dirn, j] = jnp.zeros((LANES, LANES), F32)
        _retention(base, SEQ // RET_CHUNK, qb_scr, kb_scr, vb_scr, gb_scr, gn_ref, rt_scr, m2_scr,
                   sball_scr, st_scr, mix_scr)
        for dirn in range(2):
            for j in range(2):
                st = st_scr[dirn, j]
                so_ref[e, dirn, 2 * j] = st[:64, :64]
                so_ref[e, dirn, 2 * j + 1] = st[64:, 64:]
        for jp in range(2):
            outs = []
            for h in (2 * jp, 2 * jp + 1):
                sc = _dot_nt(qc_scr[rows, h * LANES:(h + 1) * LANES], kcat_scr[h, rows, :]) * MLA_SCALE
                (pc,), dc = _softmax_stats([sc])
                outs.append(_dot(pc.astype(BF16), kvf_scr[rows, h * LANES:(h + 1) * LANES]) / dc)
            merged = jnp.where(lane_lo_a, pltpu.roll(outs[0], 64, 1), outs[1])
            mix_scr[rows, M_C + jp * LANES:M_C + (jp + 1) * LANES] = merged.astype(BF16)
        return carry

    lax.fori_loop(0, CTX_PER_STEP, seq_body, 0)
    _epilogue(x_ref, mod_ref, wout_ref, mix_scr, xo_ref)


def _const_spec(shape):
    return pl.BlockSpec(shape, lambda i: (0,) * len(shape), pipeline_mode=pl.Buffered(1))


def _smem_spec():
    return pl.BlockSpec(memory_space=pltpu.SMEM)


def _ctx_mixer(x, mod, n1, w_in_p, sink, decay, gn, kvn, w_kv_b, w_out_p):
    n_tok = x.shape[0]
    n_seq = n_tok // SEQ
    grid = (n_tok // TOK,)
    row_spec = lambda w: pl.BlockSpec((TOK, w), lambda i: (i, 0))
    return pl.pallas_call(
        _ctx_mixer_kernel,
        out_shape=(
            jax.ShapeDtypeStruct((n_tok, D_MODEL), F32),
            jax.ShapeDtypeStruct((n_tok, LANES), F32),
            jax.ShapeDtypeStruct((n_tok, LANES), F32),
            jax.ShapeDtypeStruct((n_tok, MLA_KV_RANK), F32),
            jax.ShapeDtypeStruct((n_tok, MLA_ROPE), F32),
            jax.ShapeDtypeStruct((n_seq, 2, RET_HEADS, HEAD_DIM, HEAD_DIM), F32),
        ),
        grid=grid,
        in_specs=[
            row_spec(D_MODEL),
            _const_spec((N_MOD, D_MODEL)),
            _const_spec((1, D_MODEL)),
            _const_spec((D_MODEL, D_IN_P)),
            _smem_spec(),
            _smem_spec(),
            _const_spec((1, RET_HEADS * HEAD_DIM)),
            _const_spec((1, MLA_KV_RANK)),
            _const_spec((MLA_KV_RANK, MLA_HEADS * LANES)),
            _const_spec((D_MODEL, D_MODEL)),
        ],
        out_specs=(
            row_spec(D_MODEL), row_spec(LANES), row_spec(LANES), row_spec(MLA_KV_RANK),
            row_spec(MLA_ROPE),
            pl.BlockSpec((CTX_PER_STEP, 2, RET_HEADS, HEAD_DIM, HEAD_DIM), lambda i: (i, 0, 0, 0, 0)),
        ),
        scratch_shapes=[
            pltpu.VMEM((TOK, D_MODEL), BF16),
            pltpu.VMEM((CTX_PER_STEP, WIN_HEADS * SEQ, LANES), BF16),
            pltpu.VMEM((TOK, LANES), BF16),
            pltpu.VMEM((TOK, LANES), BF16),
            pltpu.VMEM((TOK, 256), F32),
            pltpu.VMEM((TOK, 256), F32),
            pltpu.VMEM((TOK, 256), BF16),
            pltpu.VMEM((TOK, 256), F32),
            pltpu.VMEM((TOK, MLA_HEADS * LANES), BF16),
            pltpu.VMEM((MLA_HEADS, TOK, LANES), BF16),
            pltpu.VMEM((TOK, MLA_HEADS * LANES), BF16),
            pltpu.VMEM((2, 6, RET_CHUNK, LANES), F32),
            pltpu.VMEM((2, 2 * RET_CHUNK, LANES), F32),
            pltpu.VMEM((SEQ // RET_CHUNK, 2, LANES, LANES), F32),
            pltpu.VMEM((2, 2, LANES, LANES), F32),
            pltpu.VMEM((TOK, D_MODEL), BF16),
        ],
        compiler_params=pltpu.CompilerParams(
            dimension_semantics=("arbitrary",), vmem_limit_bytes=VMEM_LIMIT),
        name="ctx_mixer",
    )(x, mod, n1, w_in_p, sink, decay, gn, kvn, w_kv_b, w_out_p)


N_BLK = DEC_SEQ // LANES
KEYS_LOC = 3 * LANES
MLA_QB = 256
MLA_KEYS = DEC_SEQ + PAST_LEN


def _lat_mixer_kernel(x_ref, mod_ref, n1_ref, win_ref, sink_ref, decay_ref, gn_ref, kvn_ref,
                      wkvb_ref, wout_ref, ck_ref, cv_ref, cckv_ref, ckr_ref, s0_ref,
                      rc_ref, rsa_ref, rsb_ref, mc_ref, msa_ref, msb_ref,
                      xo_ref,
                      h_scr, qst_scr, kpad_scr, vpad_scr, qb_scr, kb_scr, vb_scr, gb_scr, qc_scr,
                      kcat_scr, kvf_scr, rt_scr, m2_scr, sball_scr, st_scr, mix_scr):
    _prologue(x_ref, mod_ref, n1_ref, h_scr)
    _retention_tables(decay_ref, rt_scr, m2_scr)
    lane_lo = _lane_iota((TOK, LANES)) < 64

    def proj(c0, c1):
        return _dot(h_scr[...], win_ref[:, c0:c1])

    rc, rsa, rsb = rc_ref[...], rsa_ref[...], rsb_ref[...]
    for g in range(4):
        q = _rope(proj(C_QA + g * LANES, C_QA + (g + 1) * LANES), rc, rsa, rsb, 16) * ATTN_SCALE
        lo = jnp.where(lane_lo, q, 0.0).astype(BF16).reshape(N_BLK, LANES, LANES)
        hi = jnp.where(lane_lo, 0.0, q).astype(BF16).reshape(N_BLK, LANES, LANES)
        qst_scr[:, g * LANES:(g + 1) * LANES, :] = lo
        qst_scr[:, (4 + g) * LANES:(5 + g) * LANES, :] = hi
    zpad = jnp.zeros((LANES, LANES), BF16)
    for scr in (kpad_scr, vpad_scr):
        scr[0:LANES, :] = zpad
        scr[LANES + TOK:, :] = zpad
    kpad_scr[LANES:LANES + TOK, :] = _rope(proj(C_KA, C_VA), rc, rsa, rsb, 16).astype(BF16)
    vpad_scr[LANES:LANES + TOK, :] = proj(C_VA, C_QB).astype(BF16)
    qb_scr[...] = proj(C_QB, C_KB)
    kb_scr[...] = proj(C_KB, C_VB) * RET_K_SCALE
    vb_scr[...] = proj(C_VB, C_GB).astype(BF16)
    gb_scr[...] = proj(C_GB, C_QC)
    mc, msa, msb = mc_ref[...], msa_ref[...], msb_ref[...]
    for h in range(MLA_HEADS):
        qh = proj(C_QC + h * LANES, C_QC + (h + 1) * LANES)
        qc_scr[:, h * LANES:(h + 1) * LANES] = _rope(qh, mc, msa, msb, 8).astype(BF16)
    ckv = proj(C_CKV, C_KR)
    ckv_n = ckv * lax.rsqrt(jnp.mean(ckv * ckv, axis=-1, keepdims=True) + EPS) * kvn_ref[...]
    kv = _dot(ckv_n.astype(BF16), wkvb_ref[...])
    kvf_scr[0:TOK, :] = kv.astype(BF16)
    kr = _rope(proj(C_KR, D_IN_P), mc, msa, msb, 8)
    ckv_c = _dot(cckv_ref[...].astype(BF16), wkvb_ref[...])
    kvf_scr[TOK:, :] = ckv_c.astype(BF16)
    ckr = ckr_ref[...]
    lane_lo_c = _lane_iota((PAST_LEN, LANES)) < 64
    for h in range(MLA_HEADS):
        kcat_scr[h, 0:TOK, :] = jnp.where(lane_lo, kv[:, h * LANES:(h + 1) * LANES], kr).astype(BF16)
        kcat_scr[h, TOK:, :] = jnp.where(
            lane_lo_c, ckv_c[:, h * LANES:(h + 1) * LANES], ckr).astype(BF16)

    ck = ck_ref[...].astype(BF16)
    cv = cv_ref[...].astype(BF16)
    sink_col = jnp.concatenate(
        [jnp.full((LANES, 1), sink_ref[h], F32) for h in range(WIN_HEADS)], axis=0)
    lane_lo_b = _lane_iota((LANES, LANES)) < 64

    def win_body(n, carry):
        r0 = pl.multiple_of(n * LANES, LANES)
        qst = qst_scr[n]
        s_loc = _dot_nt(qst, kpad_scr[pl.ds(r0, KEYS_LOC), :])
        qi = _row_iota((WIN_HEADS * LANES, KEYS_LOC)) & (LANES - 1)
        sj = _lane_iota((WIN_HEADS * LANES, KEYS_LOC))
        kpos = sj + (r0 - LANES)
        valid = (sj >= qi) & (sj <= qi + 2 * WINDOW) & (kpos >= 0) & (kpos < DEC_SEQ)
        s_loc = jnp.where(valid, s_loc, -jnp.inf)
        s_ctx = _dot_nt(qst, ck)
        (p_loc, p_ctx), den = _softmax_stats([s_loc, s_ctx], sink_col)
        o = (_dot(p_loc.astype(BF16), vpad_scr[pl.ds(r0, KEYS_LOC), :])
             + _dot(p_ctx.astype(BF16), cv)) / den
        for g in range(4):
            merged = jnp.where(lane_lo_b, o[g * LANES:(g + 1) * LANES],
                               o[(4 + g) * LANES:(5 + g) * LANES])
            mix_scr[pl.ds(r0, LANES), M_A + g * LANES:M_A + (g + 1) * LANES] = merged.astype(BF16)
        return carry

    lax.fori_loop(0, N_BLK, win_body, 0)

    for dirn in range(2):
        for j in range(2):
            st_scr[dirn, j] = s0_ref[dirn, j]
    _retention(0, DEC_SEQ // RET_CHUNK, qb_scr, kb_scr, vb_scr, gb_scr, gn_ref, rt_scr, m2_scr,
               sball_scr, st_scr, mix_scr)

    lane_lo_m = _lane_iota((MLA_QB, LANES)) < 64

    def mla_body(t, carry):
        rows = pl.ds(pl.multiple_of(t * MLA_QB, MLA_QB), MLA_QB)
        for jp in range(2):
            outs = []
            for h in (2 * jp, 2 * jp + 1):
                sc = _dot_nt(qc_scr[rows, h * LANES:(h + 1) * LANES], kcat_scr[h]) * MLA_SCALE
                (pc,), dc = _softmax_stats([sc])
                outs.append(_dot(pc.astype(BF16), kvf_scr[:, h * LANES:(h + 1) * LANES]) / dc)
            merged = jnp.where(lane_lo_m, pltpu.roll(outs[0], 64, 1), outs[1])
            mix_scr[rows, M_C + jp * LANES:M_C + (jp + 1) * LANES] = merged.astype(BF16)
        return carry

    lax.fori_loop(0, DEC_SEQ // MLA_QB, mla_body, 0)
    _epilogue(x_ref, mod_ref, wout_ref, mix_scr, xo_ref)


def _lat_mixer(x, mod, n1, w_in_p, sink, decay, gn, kvn, w_kv_b, w_out_p,
               ck, cv, cckv, ckr, s0, rope_a, rope_m):
    n_tok = x.shape[0]
    n_seq = n_tok // DEC_SEQ
    seq_spec = lambda shape: pl.BlockSpec((None,) + shape, lambda i: (i,) + (0,) * len(shape))
    return pl.pallas_call(
        _lat_mixer_kernel,
        out_shape=jax.ShapeDtypeStruct((n_tok, D_MODEL), F32),
        grid=(n_seq,),
        in_specs=[
            pl.BlockSpec((TOK, D_MODEL), lambda i: (i, 0)),
            seq_spec((N_MOD, D_MODEL)),
            _const_spec((1, D_MODEL)),
            _const_spec((D_MODEL, D_IN_P)),
            _smem_spec(),
            _smem_spec(),
            _const_spec((1, RET_HEADS * HEAD_DIM)),
            _const_spec((1, MLA_KV_RANK)),
            _const_spec((MLA_KV_RANK, MLA_HEADS * LANES)),
            _const_spec((D_MODEL, D_MODEL)),
            seq_spec((PAST_LEN, LANES)),
            seq_spec((PAST_LEN, LANES)),
            seq_spec((PAST_LEN, MLA_KV_RANK)),
            seq_spec((PAST_LEN, LANES)),
            seq_spec((2, 2, LANES, LANES)),
        ] + [_const_spec((DEC_SEQ, LANES))] * 6,
        out_specs=pl.BlockSpec((TOK, D_MODEL), lambda i: (i, 0)),
        scratch_shapes=[
            pltpu.VMEM((TOK, D_MODEL), BF16),
            pltpu.VMEM((N_BLK, WIN_HEADS * LANES, LANES), BF16),
            pltpu.VMEM((TOK + 2 * LANES, LANES), BF16),
            pltpu.VMEM((TOK + 2 * LANES, LANES), BF16),
            pltpu.VMEM((TOK, 256), F32),
            pltpu.VMEM((TOK, 256), F32),
            pltpu.VMEM((TOK, 256), BF16),
            pltpu.VMEM((TOK, 256), F32),
            pltpu.VMEM((TOK, MLA_HEADS * LANES), BF16),
            pltpu.VMEM((MLA_HEADS, MLA_KEYS, LANES), BF16),
            pltpu.VMEM((MLA_KEYS, MLA_HEADS * LANES), BF16),
            pltpu.VMEM((2, 6, RET_CHUNK, LANES), F32),
            pltpu.VMEM((2, 2 * RET_CHUNK, LANES), F32),
            pltpu.VMEM((DEC_SEQ // RET_CHUNK, 2, LANES, LANES), F32),
            pltpu.VMEM((2, 2, LANES, LANES), F32),
            pltpu.VMEM((TOK, D_MODEL), BF16),
        ],
        compiler_params=pltpu.CompilerParams(
            dimension_semantics=("arbitrary",), vmem_limit_bytes=VMEM_LIMIT),
        name="lat_mixer",
    )(x, mod, n1, w_in_p, sink, decay, gn, kvn, w_kv_b, w_out_p, ck, cv, cckv, ckr, s0,
      *rope_a, *rope_m)


def _mlp_kernel(final, x_ref, mod_ref, n2_ref, wup_ref, wdn_ref, fn_ref, o_ref):
    x = x_ref[...]
    h2 = _norm_mod(x, n2_ref[...], mod_ref[4:5, :], mod_ref[3:4, :]).astype(BF16)
    acc = None
    for c in range(D_FF // FF_CHUNK):
        cols = slice(c * FF_CHUNK, (c + 1) * FF_CHUNK)
        u = jnp.maximum(_dot(h2, wup_ref[:, cols]), 0.0)
        part = _dot((u * u).astype(BF16), wdn_ref[cols, :])
        acc = part if acc is None else acc + part
    y = x + mod_ref[5:6, :] * acc
    if final:
        y = y * lax.rsqrt(jnp.mean(y * y, axis=-1, keepdims=True) + EPS) * fn_ref[...]
    o_ref[...] = y


def _mlp(x, mod, n2, w_up, w_down, final_norm, final):
    n_tok = x.shape[0]
    per_mod = n_tok // mod.shape[0] // MLP_ROWS
    return pl.pallas_call(
        functools.partial(_mlp_kernel, final),
        out_shape=jax.ShapeDtypeStruct((n_tok, D_MODEL), F32),
        grid=(n_tok // MLP_ROWS,),
        in_specs=[
            pl.BlockSpec((MLP_ROWS, D_MODEL), lambda i: (i, 0)),
            pl.BlockSpec((None, N_MOD, D_MODEL), lambda i: (i // per_mod, 0, 0)),
            _const_spec((1, D_MODEL)),
            _const_spec((D_MODEL, D_FF)),
            _const_spec((D_FF, D_MODEL)),
            _const_spec((1, D_MODEL)),
        ],
        out_specs=pl.BlockSpec((MLP_ROWS, D_MODEL), lambda i: (i, 0)),
        compiler_params=pltpu.CompilerParams(
            dimension_semantics=("arbitrary",), vmem_limit_bytes=VMEM_LIMIT),
        name="mlp",
    )(x, mod, n2, w_up, w_down, final_norm)


def kernel(x_prompt, x_sample, cache_win_k, cache_win_v, cache_mla_ckv, cache_mla_krope, state_ret,
           c, c_ctx, w_mod, b_mod, norm1, norm2, w_in, win_sink, ret_decay, ret_gn, mla_kv_norm,
           w_kv_b, w_out, w_up, w_down, final_norm):
    n_ctx, n_lat = x_prompt.shape[0], x_sample.shape[0]

    cols = _in_proj_columns()
    w_in_p = jnp.where(cols >= 0, jnp.take(w_in, np.maximum(cols, 0), axis=2), 0.0).astype(BF16)
    w_out_p = jnp.take(w_out, _mix_rows(), axis=1).astype(BF16)
    w_kv_b16, w_up16, w_down16 = w_kv_b.astype(BF16), w_up.astype(BF16), w_down.astype(BF16)

    c_rows = jnp.zeros((16, D_MODEL), F32).at[0].set(c_ctx).at[1:1 + n_lat].set(c)
    mod = _modulation(c_rows, w_mod, b_mod).reshape(DEPTH, 16, N_MOD, D_MODEL)

    rope_a = _rope_tables(DEC_SEQ, HEAD_DIM, 0, HEAD_DIM)
    rope_m = _rope_tables(DEC_SEQ, MLA_ROPE, MLA_NOPE, LANES)

    ck = cache_win_k.reshape(n_lat, DEPTH, PAST_LEN, LANES)
    cv = cache_win_v.reshape(n_lat, DEPTH, PAST_LEN, LANES)
    ckr = jnp.pad(cache_mla_krope, ((0, 0), (0, 0), (0, 0), (MLA_NOPE, LANES - MLA_NOPE - MLA_ROPE)))
    sr = state_ret.reshape(n_lat, DEPTH, 2, 2, 2, HEAD_DIM, HEAD_DIM)
    zero = jnp.zeros_like(sr[:, :, :, :, 0])
    s0 = jnp.concatenate([jnp.concatenate([sr[:, :, :, :, 0], zero], axis=-1),
                          jnp.concatenate([zero, sr[:, :, :, :, 1]], axis=-1)], axis=-2)

    xp = x_prompt.reshape(n_ctx * SEQ, D_MODEL)
    xs = x_sample.reshape(n_lat * DEC_SEQ, D_MODEL)
    new = [[] for _ in range(5)]
    for l in range(DEPTH):
        last = l == DEPTH - 1
        shared = (norm1[l][None], w_in_p[l], win_sink[l], ret_decay[l], ret_gn[l][None],
                  mla_kv_norm[l][None], w_kv_b16[l], w_out_p[l])
        mod_ctx, mod_lat = mod[l, 0:1], mod[l, 1:1 + n_lat]
        xp, k_l, v_l, ckv_l, kr_l, s_l = _ctx_mixer(xp, mod_ctx[0], *shared)
        xp = _mlp(xp, mod_ctx, norm2[l][None], w_up16[l], w_down16[l], final_norm[None], last)
        xs = _lat_mixer(xs, mod_lat, *shared, ck[:, l], cv[:, l], cache_mla_ckv[:, l], ckr[:, l],
                        s0[:, l], rope_a, rope_m)
        xs = _mlp(xs, mod_lat, norm2[l][None], w_up16[l], w_down16[l], final_norm[None], last)
        for lst, val in zip(new, (k_l, v_l, ckv_l, kr_l, s_l)):
            lst.append(val)
    new_k = jnp.stack([a.reshape(n_ctx, SEQ, 2, HEAD_DIM) for a in new[0]], axis=1)
    new_v = jnp.stack([a.reshape(n_ctx, SEQ, 2, HEAD_DIM) for a in new[1]], axis=1)
    new_ckv = jnp.stack([a.reshape(n_ctx, SEQ, MLA_KV_RANK) for a in new[2]], axis=1)
    new_kr = jnp.stack([a.reshape(n_ctx, SEQ, MLA_ROPE) for a in new[3]], axis=1)
    new_s = jnp.stack(new[4], axis=1)
    return (xp.reshape(n_ctx, SEQ, D_MODEL), xs.reshape(n_lat, DEC_SEQ, D_MODEL),
            new_k, new_v, new_ckv, new_kr, new_s)
```

```python
import functools

import numpy as np
import jax
import jax.numpy as jnp
from jax import lax
from jax.experimental import pallas as pl
from jax.experimental.pallas import tpu as pltpu

F32 = jnp.float32
BF16 = jnp.bfloat16

D_MODEL = 1024
DEPTH = 2
SEQ = 256
DEC_SEQ = 1024
PAST_LEN = 256
GRID_W = 64
HEAD_DIM = 64
ROPE_BASE = 10000.0
EPS = 1e-6
WIN_HEADS = 8
WINDOW = 128
ATTN_SCALE = HEAD_DIM ** -0.5
RET_HEADS = 4
RET_CHUNK = 128
RET_K_SCALE = HEAD_DIM ** -0.5
MLA_HEADS = 4
MLA_NOPE = 64
MLA_ROPE = 32
MLA_KV_RANK = 128
MLA_QK = MLA_NOPE + MLA_ROPE
MLA_SCALE = MLA_QK ** -0.5
D_IN = 2336
D_FF = 4 * D_MODEL
N_MOD = 6

LANES = 128
TOK = 1024
MLP_ROWS = 512
FF_CHUNK = 1024
VMEM_LIMIT = 60 * 1024 * 1024

C_QA, C_KA, C_VA, C_QB, C_KB, C_VB, C_GB, C_QC, C_CKV, C_KR, D_IN_P = (
    0, 512, 640, 768, 1024, 1280, 1536, 1792, 2304, 2432, 2560)
M_A, M_B, M_C = 0, 512, 768

NT_DIMS = (((1,), (1,)), ((), ()))


def _in_proj_columns():
    idx = []
    for g in range(4):
        idx += list(range(g * 64, (g + 1) * 64)) + list(range((4 + g) * 64, (5 + g) * 64))
    idx += list(range(512, 1792))
    for h in range(MLA_HEADS):
        idx += list(range(1792 + h * MLA_QK, 1792 + (h + 1) * MLA_QK)) + [-1] * 32
    idx += list(range(2176, 2304))
    idx += [-1] * 64 + list(range(2304, 2336)) + [-1] * 32
    return np.asarray(idx, np.int32)


def _mix_rows():
    idx = []
    for g in range(4):
        idx += list(range(g * 64, (g + 1) * 64)) + list(range((4 + g) * 64, (5 + g) * 64))
    idx += list(range(512, 1024))
    return np.asarray(idx, np.int32)


def _rope_tables(n_tokens, dim, lane0, period):
    quarter = dim // 4
    t = jnp.arange(n_tokens)
    row = (t // GRID_W).astype(F32)
    col = (t % GRID_W).astype(F32)
    inv_freq = ROPE_BASE ** (-jnp.arange(quarter, dtype=F32) / quarter)
    ar, ac = row[:, None] * inv_freq, col[:, None] * inv_freq
    cos = jnp.concatenate([jnp.cos(ar), jnp.cos(ar), jnp.cos(ac), jnp.cos(ac)], axis=-1)
    sin = jnp.concatenate([jnp.sin(ar), jnp.sin(ar), jnp.sin(ac), jnp.sin(ac)], axis=-1)
    first = np.tile(np.concatenate([np.ones(quarter), np.zeros(quarter)]), 2).astype(np.float32)
    c = jnp.ones((n_tokens, LANES), F32)
    sa = jnp.zeros((n_tokens, LANES), F32)
    sb = jnp.zeros((n_tokens, LANES), F32)
    for start in range(lane0, LANES, period):
        c = c.at[:, start:start + dim].set(cos)
        sa = sa.at[:, start:start + dim].set(-sin * first)
        sb = sb.at[:, start:start + dim].set(sin * (1.0 - first))
    return c, sa, sb


def _lane_iota(shape):
    return lax.broadcasted_iota(jnp.int32, shape, len(shape) - 1)


def _row_iota(shape):
    return lax.broadcasted_iota(jnp.int32, shape, len(shape) - 2)


def _ds(start, size):
    if isinstance(start, int):
        return pl.ds(start, size)
    return pl.ds(pl.multiple_of(start, LANES), size)


def _norm_mod(x, gain, scale, shift):
    ms = jnp.mean(x * x, axis=-1, keepdims=True)
    return (x * lax.rsqrt(ms + EPS) * gain) * (1.0 + scale) + shift


def _log_sigmoid(x):
    return -(jnp.maximum(-x, 0.0) + jnp.log1p(jnp.exp(-jnp.abs(x))))


def _rope(x, c, sa, sb, quarter):
    return (x * c + pltpu.roll(x, LANES - quarter, 1) * sa + pltpu.roll(x, quarter, 1) * sb)


def _dot(a, b):
    return jnp.dot(a, b, preferred_element_type=F32)


def _dot_nt(a, b):
    return lax.dot_general(a, b, NT_DIMS, preferred_element_type=F32)


def _mod_kernel(c_ref, w_ref, b_ref, o_ref):
    cv = c_ref[...]
    s = cv * jax.nn.sigmoid(cv)
    o_ref[0] = _dot(s.astype(BF16), w_ref[0].astype(BF16)) + b_ref[0]


def _modulation(c_rows, w_mod, b_mod):
    tn = 1536
    nj = (N_MOD * D_MODEL) // tn
    return pl.pallas_call(
        _mod_kernel,
        out_shape=jax.ShapeDtypeStruct((DEPTH, 16, N_MOD * D_MODEL), F32),
        grid=(DEPTH, nj),
        in_specs=[
            pl.BlockSpec((16, D_MODEL), lambda l, j: (0, 0)),
            pl.BlockSpec((1, D_MODEL, tn), lambda l, j: (l, 0, j)),
            pl.BlockSpec((1, 1, tn), lambda l, j: (l, 0, j)),
        ],
        out_specs=pl.BlockSpec((1, 16, tn), lambda l, j: (l, 0, j)),
        compiler_params=pltpu.CompilerParams(
            dimension_semantics=("arbitrary", "arbitrary"), vmem_limit_bytes=VMEM_LIMIT),
        name="modulation",
    )(c_rows, w_mod, b_mod.reshape(DEPTH, 1, N_MOD * D_MODEL))


def _pipeline(n_steps, scores, softmax, values):
    scores(0, 0)
    scores(1, 1)
    softmax(0, 0)

    def body(i, carry):
        t = 2 * i + 1
        scores(t + 1, 0)
        softmax(t, 1)
        values(t - 1, 0)
        scores(t + 2, 1)
        softmax(t + 1, 0)
        values(t, 1)
        return carry

    lax.fori_loop(0, n_steps // 2 - 1, body, 0)
    softmax(n_steps - 1, 1)
    values(n_steps - 2, 0)
    values(n_steps - 1, 1)


def _softmax_tile(parts, floor=None, scale=None):
    m = None
    for s in parts:
        pm = jnp.max(s, axis=-1, keepdims=True)
        m = pm if m is None else jnp.maximum(m, pm)
    if floor is not None:
        m = jnp.maximum(m, floor)
    if scale is None:
        ps = [jnp.exp(s - m).astype(BF16) for s in parts]
    else:
        ps = [jnp.exp((s - m) * scale).astype(BF16) for s in parts]
    extra = None if floor is None else jnp.exp(floor - m)
    return ps, extra


def _retention_tables(decay_ref, rt_scr, m2_scr):
    shape = (RET_CHUNK, LANES)
    lane_lo = _lane_iota(shape) < 64
    row = _row_iota(shape)
    row_lo = row < 64
    i = row.astype(F32)
    rel = i - _lane_iota(shape).astype(F32)
    for j in range(2):
        df0, df1 = decay_ref[0, 2 * j], decay_ref[0, 2 * j + 1]
        db0, db1 = decay_ref[1, 2 * j], decay_ref[1, 2 * j + 1]
        lgf = _log_sigmoid(jnp.where(lane_lo, df0, df1))
        lgb = _log_sigmoid(jnp.where(lane_lo, db0, db1))
        rt_scr[j, 0] = jnp.exp((i + 1.0) * lgf)
        rt_scr[j, 1] = jnp.exp((RET_CHUNK - i) * lgb)
        rt_scr[j, 2] = jnp.exp((RET_CHUNK - 1.0 - i) * lgf)
        rt_scr[j, 3] = jnp.exp(i * lgb)
        rt_scr[j, 4] = jnp.exp(RET_CHUNK * _log_sigmoid(jnp.where(row_lo, df0, df1)))
        rt_scr[j, 5] = jnp.exp(RET_CHUNK * _log_sigmoid(jnp.where(row_lo, db0, db1)))
        for hh, (df, db) in enumerate(((df0, db0), (df1, db1))):
            lf = _log_sigmoid(jnp.full(shape, df, F32))
            lb = _log_sigmoid(jnp.full(shape, db, F32))
            low, upp = rel >= 0.0, rel <= 0.0
            m = (jnp.where(low, jnp.exp(jnp.where(low, rel, 0.0) * lf), 0.0)
                 + jnp.where(upp, jnp.exp(jnp.where(upp, -rel, 0.0) * lb), 0.0))
            m2_scr[j, hh * RET_CHUNK:(hh + 1) * RET_CHUNK, :] = m


def _retention(seq_bases, nc, qb_scr, kb_scr, vb_scr, gb_scr, gn_ref, rt_scr, m2_scr,
               sball_scr, st_scr, mix_scr):
    shape = (RET_CHUNK, LANES)
    lane_lo = _lane_iota(shape) < 64
    blockdiag = lane_lo == (_row_iota(shape) < 64)
    chains = [(si, base, j) for si, base in enumerate(seq_bases) for j in range(2)]

    def state_update(dirn, c):
        for si, base, j in chains:
            rows, cols = _ds(base + c * RET_CHUNK, RET_CHUNK), slice(j * LANES, (j + 1) * LANES)
            kd = kb_scr[rows, cols] * rt_scr[j, 2 + dirn]
            upd = _dot(kd.T.astype(BF16), vb_scr[rows, cols])
            st_scr[si, dirn, j] = (rt_scr[j, 4 + dirn] * st_scr[si, dirn, j]
                                   + jnp.where(blockdiag, upd, 0.0))

    def bwd_body(t, carry):
        c = nc - 1 - t
        for si, _, j in chains:
            sball_scr[c, si, j] = st_scr[si, 1, j]
        state_update(1, c)
        return carry

    lax.fori_loop(0, nc, bwd_body, 0)

    def fwd_body(c, carry):
        for si, base, j in chains:
            rows, cols = _ds(base + c * RET_CHUNK, RET_CHUNK), slice(j * LANES, (j + 1) * LANES)
            q2 = qb_scr[rows, cols]
            qs = jnp.concatenate([jnp.where(lane_lo, q2, 0.0), jnp.where(lane_lo, 0.0, q2)],
                                 axis=0).astype(BF16)
            sc = _dot_nt(qs, kb_scr[rows, cols].astype(BF16)) * m2_scr[j]
            o2 = _dot(sc.astype(BF16), vb_scr[rows, cols])
            o = jnp.where(lane_lo, o2[:RET_CHUNK], o2[RET_CHUNK:])
            o = o + _dot((q2 * rt_scr[j, 0]).astype(BF16), st_scr[si, 0, j].astype(BF16))
            o = o + _dot((q2 * rt_scr[j, 1]).astype(BF16), sball_scr[c, si, j].astype(BF16))
            s_lo = jnp.sum(jnp.where(lane_lo, o, 0.0), axis=-1, keepdims=True)
            s_hi = jnp.sum(jnp.where(lane_lo, 0.0, o), axis=-1, keepdims=True)
            d = o - jnp.where(lane_lo, s_lo, s_hi) * (1.0 / HEAD_DIM)
            dd = d * d
            v_lo = jnp.sum(jnp.where(lane_lo, dd, 0.0), axis=-1, keepdims=True)
            v_hi = jnp.sum(jnp.where(lane_lo, 0.0, dd), axis=-1, keepdims=True)
            var = jnp.where(lane_lo, v_lo, v_hi) * (1.0 / HEAD_DIM)
            g2 = gb_scr[rows, cols]
            y = d * lax.rsqrt(var + EPS) * gn_ref[:, cols] * (g2 * jax.nn.sigmoid(g2))
            mix_scr[rows, M_B + j * LANES:M_B + (j + 1) * LANES] = y.astype(BF16)
        state_update(0, c)
        return carry

    lax.fori_loop(0, nc, fwd_body, 0)


def _prologue(x_ref, mod_ref, n1_ref, h_scr):
    def body(r, carry):
        rows = _ds(r * 256, 256)
        h = _norm_mod(x_ref[rows, :], n1_ref[...], mod_ref[1:2, :], mod_ref[0:1, :])
        h_scr[rows, :] = h.astype(BF16)
        return carry
    lax.fori_loop(0, TOK // 256, body, 0)


def _epilogue(x_ref, mod_ref, wout_ref, mix_scr, xo_ref):
    def body(r, carry):
        rows = _ds(r * 256, 256)
        y = _dot(mix_scr[rows, :], wout_ref[...])
        xo_ref[rows, :] = x_ref[rows, :] + mod_ref[2:3, :] * y
        return carry
    lax.fori_loop(0, TOK // 256, body, 0)


def _ret_scratch(n_seq, nc):
    return [
        pltpu.VMEM((TOK, 256), F32),
        pltpu.VMEM((TOK, 256), F32),
        pltpu.VMEM((TOK, 256), BF16),
        pltpu.VMEM((TOK, 256), F32),
        pltpu.VMEM((2, 6, RET_CHUNK, LANES), F32),
        pltpu.VMEM((2, 2 * RET_CHUNK, LANES), F32),
        pltpu.VMEM((nc, n_seq, 2, LANES, LANES), F32),
        pltpu.VMEM((n_seq, 2, 2, LANES, LANES), F32),
    ]


def _ret_project(proj, qb_scr, kb_scr, vb_scr, gb_scr):
    qb_scr[...] = proj(C_QB, C_KB)
    kb_scr[...] = proj(C_KB, C_VB) * RET_K_SCALE
    vb_scr[...] = proj(C_VB, C_GB).astype(BF16)
    gb_scr[...] = proj(C_GB, C_QC)


CTX_PER_STEP = TOK // SEQ
CTX_WROWS = WIN_HEADS * SEQ
CTX_MROWS = MLA_HEADS * SEQ


def _ctx_mixer_kernel(x_ref, mod_ref, n1_ref, win_ref, sink_ref, decay_ref, gn_ref, kvn_ref,
                      wkvb_ref, wout_ref,
                      xo_ref, ko_ref, vo_ref, ckvo_ref, kro_ref, so_ref,
                      h_scr, mix_scr):
    _prologue(x_ref, mod_ref, n1_ref, h_scr)

    def proj(c0, c1):
        return _dot(h_scr[...], win_ref[:, c0:c1])

    def attention(qst_scr, k_scr, vaug_scr, qc_scr, kcat_scr, kvaug_scr,
                  sw0, sw1, pw0, pw1, ew0, ew1, sm0, sm1, pm0, pm1):
        lane_lo = _lane_iota((TOK, LANES)) < 64
        ones = jnp.ones((TOK, LANES), BF16)
        qa = proj(C_QA, C_KA) * ATTN_SCALE
        for g in range(4):
            q = qa[:, g * LANES:(g + 1) * LANES]
            lo = jnp.where(lane_lo, q, 0.0).astype(BF16).reshape(CTX_PER_STEP, SEQ, LANES)
            hi = jnp.where(lane_lo, 0.0, q).astype(BF16).reshape(CTX_PER_STEP, SEQ, LANES)
            qst_scr[:, g * SEQ:(g + 1) * SEQ, :] = lo
            qst_scr[:, (4 + g) * SEQ:(5 + g) * SEQ, :] = hi
        kva = proj(C_KA, C_QB)
        ko_ref[...] = kva[:, :LANES]
        vo_ref[...] = kva[:, LANES:]
        k_scr[...] = kva[:, :LANES].astype(BF16)
        vaug_scr[:, :LANES] = kva[:, LANES:].astype(BF16)
        vaug_scr[:, LANES:] = ones
        qc = proj(C_QC, C_CKV).astype(BF16)
        for h in range(MLA_HEADS):
            qc_scr[h] = qc[:, h * LANES:(h + 1) * LANES]
        ckr = proj(C_CKV, D_IN_P)
        ckv, kr = ckr[:, :LANES], ckr[:, LANES:]
        ckv_n = ckv * lax.rsqrt(jnp.mean(ckv * ckv, axis=-1, keepdims=True) + EPS) * kvn_ref[...]
        ckvo_ref[...] = ckv_n
        kro_ref[...] = kr[:, 64:64 + MLA_ROPE]
        kv = _dot(ckv_n.astype(BF16), wkvb_ref[...])
        for h in range(MLA_HEADS):
            kvh = kv[:, h * LANES:(h + 1) * LANES]
            kcat_scr[h] = jnp.where(lane_lo, kvh, kr).astype(BF16)
            kvaug_scr[h, :, :LANES] = kvh.astype(BF16)
            kvaug_scr[h, :, LANES:] = ones

        sw, pw, ew, sm, pm = (sw0, sw1), (pw0, pw1), (ew0, ew1), (sm0, sm1), (pm0, pm1)
        lane_lo_s = _lane_iota((SEQ, LANES)) < 64

        def scores(e, b):
            rows = _ds(e * SEQ, SEQ)
            sw[b][...] = _dot_nt(qst_scr[e], k_scr[rows, :])
            for h in range(MLA_HEADS):
                sm[b][h * SEQ:(h + 1) * SEQ, :] = _dot_nt(qc_scr[h, rows, :], kcat_scr[h, rows, :])

        def softmax(e, b):
            for h in range(WIN_HEADS):
                hr = slice(h * SEQ, (h + 1) * SEQ)
                sink = jnp.full((SEQ, 1), sink_ref[h], F32)
                (p,), extra = _softmax_tile([sw[b][hr, :]], floor=sink)
                pw[b][hr, :] = p
                ew[b][hr, :] = jnp.broadcast_to(extra, (SEQ, LANES))
            for h in range(MLA_HEADS):
                hr = slice(h * SEQ, (h + 1) * SEQ)
                (p,), _ = _softmax_tile([sm[b][hr, :]], scale=MLA_SCALE)
                pm[b][hr, :] = p

        def values(e, b):
            rows = _ds(e * SEQ, SEQ)
            oa = _dot(pw[b][...], vaug_scr[rows, :])
            o = oa[:, :LANES] / (oa[:, LANES:] + ew[b][...])
            for g in range(4):
                merged = jnp.where(lane_lo_s, o[g * SEQ:(g + 1) * SEQ], o[(4 + g) * SEQ:(5 + g) * SEQ])
                mix_scr[rows, M_A + g * LANES:M_A + (g + 1) * LANES] = merged.astype(BF16)
            for jp in range(2):
                outs = []
                for h in (2 * jp, 2 * jp + 1):
                    oc = _dot(pm[b][h * SEQ:(h + 1) * SEQ, :], kvaug_scr[h, rows, :])
                    outs.append(oc[:, :LANES] / oc[:, LANES:])
                merged = jnp.where(lane_lo_s, pltpu.roll(outs[0], 64, 1), outs[1])
                mix_scr[rows, M_C + jp * LANES:M_C + (jp + 1) * LANES] = merged.astype(BF16)

        _pipeline(CTX_PER_STEP, scores, softmax, values)

    pl.run_scoped(
        attention,
        pltpu.VMEM((CTX_PER_STEP, CTX_WROWS, LANES), BF16),
        pltpu.VMEM((TOK, LANES), BF16),
        pltpu.VMEM((TOK, 2 * LANES), BF16),
        pltpu.VMEM((MLA_HEADS, TOK, LANES), BF16),
        pltpu.VMEM((MLA_HEADS, TOK, LANES), BF16),
        pltpu.VMEM((MLA_HEADS, TOK, 2 * LANES), BF16),
        pltpu.VMEM((CTX_WROWS, SEQ), F32), pltpu.VMEM((CTX_WROWS, SEQ), F32),
        pltpu.VMEM((CTX_WROWS, SEQ), BF16), pltpu.VMEM((CTX_WROWS, SEQ), BF16),
        pltpu.VMEM((CTX_WROWS, LANES), F32), pltpu.VMEM((CTX_WROWS, LANES), F32),
        pltpu.VMEM((CTX_MROWS, SEQ), F32), pltpu.VMEM((CTX_MROWS, SEQ), F32),
        pltpu.VMEM((CTX_MROWS, SEQ), BF16), pltpu.VMEM((CTX_MROWS, SEQ), BF16),
    )

    def retention(qb_scr, kb_scr, vb_scr, gb_scr, rt_scr, m2_scr, sball_scr, st_scr):
        _ret_project(proj, qb_scr, kb_scr, vb_scr, gb_scr)
        _retention_tables(decay_ref, rt_scr, m2_scr)
        st_scr[...] = jnp.zeros(st_scr.shape, F32)
        _retention([e * SEQ for e in range(CTX_PER_STEP)], SEQ // RET_CHUNK, qb_scr, kb_scr, vb_scr,
                   gb_scr, gn_ref, rt_scr, m2_scr, sball_scr, st_scr, mix_scr)
        for e in range(CTX_PER_STEP):
            for dirn in range(2):
                for j in range(2):
                    st = st_scr[e, dirn, j]
                    so_ref[e, dirn, 2 * j] = st[:64, :64]
                    so_ref[e, dirn, 2 * j + 1] = st[64:, 64:]

    pl.run_scoped(retention, *_ret_scratch(CTX_PER_STEP, SEQ // RET_CHUNK))
    _epilogue(x_ref, mod_ref, wout_ref, mix_scr, xo_ref)


def _const_spec(shape):
    return pl.BlockSpec(shape, lambda i: (0,) * len(shape), pipeline_mode=pl.Buffered(1))


def _smem_spec():
    return pl.BlockSpec(memory_space=pltpu.SMEM)


def _ctx_mixer(x, mod, n1, w_in_p, sink, decay, gn, kvn, w_kv_b, w_out_p):
    n_tok = x.shape[0]
    n_seq = n_tok // SEQ
    grid = (n_tok // TOK,)
    row_spec = lambda w: pl.BlockSpec((TOK, w), lambda i: (i, 0))
    return pl.pallas_call(
        _ctx_mixer_kernel,
        out_shape=(
            jax.ShapeDtypeStruct((n_tok, D_MODEL), F32),
            jax.ShapeDtypeStruct((n_tok, LANES), F32),
            jax.ShapeDtypeStruct((n_tok, LANES), F32),
            jax.ShapeDtypeStruct((n_tok, MLA_KV_RANK), F32),
            jax.ShapeDtypeStruct((n_tok, MLA_ROPE), F32),
            jax.ShapeDtypeStruct((n_seq, 2, RET_HEADS, HEAD_DIM, HEAD_DIM), F32),
        ),
        grid=grid,
        in_specs=[
            row_spec(D_MODEL),
            _const_spec((N_MOD, D_MODEL)),
            _const_spec((1, D_MODEL)),
            _const_spec((D_MODEL, D_IN_P)),
            _smem_spec(),
            _smem_spec(),
            _const_spec((1, RET_HEADS * HEAD_DIM)),
            _const_spec((1, MLA_KV_RANK)),
            _const_spec((MLA_KV_RANK, MLA_HEADS * LANES)),
            _const_spec((D_MODEL, D_MODEL)),
        ],
        out_specs=(
            row_spec(D_MODEL), row_spec(LANES), row_spec(LANES), row_spec(MLA_KV_RANK),
            row_spec(MLA_ROPE),
            pl.BlockSpec((CTX_PER_STEP, 2, RET_HEADS, HEAD_DIM, HEAD_DIM), lambda i: (i, 0, 0, 0, 0)),
        ),
        scratch_shapes=[
            pltpu.VMEM((TOK, D_MODEL), BF16),
            pltpu.VMEM((TOK, D_MODEL), BF16),
        ],
        compiler_params=pltpu.CompilerParams(
            dimension_semantics=("arbitrary",), vmem_limit_bytes=VMEM_LIMIT),
        name="ctx_mixer",
    )(x, mod, n1, w_in_p, sink, decay, gn, kvn, w_kv_b, w_out_p)


N_BLK = DEC_SEQ // LANES
KEYS_LOC = 3 * LANES
KEYS_WIN = KEYS_LOC + PAST_LEN
WIN_ROWS = WIN_HEADS * LANES
MLA_QB = 256
MLA_KEYS = DEC_SEQ + PAST_LEN
MLA_HALF = MLA_KEYS // 2


def _lat_mixer_kernel(x_ref, mod_ref, n1_ref, win_ref, sink_ref, decay_ref, gn_ref, kvn_ref,
                      wkvb_ref, wout_ref, ck_ref, cv_ref, cckv_ref, ckr_ref, s0_ref,
                      rc_ref, rsa_ref, rsb_ref, mc_ref, msa_ref, msb_ref,
                      xo_ref,
                      h_scr, mix_scr, s0_scr, s1_scr, p0_scr, p1_scr):
    _prologue(x_ref, mod_ref, n1_ref, h_scr)
    sbuf, pbuf = (s0_scr, s1_scr), (p0_scr, p1_scr)

    def proj(c0, c1):
        return _dot(h_scr[...], win_ref[:, c0:c1])

    def window(qst_scr, kpad_scr, vaug_scr, ckb_scr, cvaug_scr, bias_scr, e0_scr, e1_scr):
        ebuf = (e0_scr, e1_scr)
        lane_lo = _lane_iota((TOK, LANES)) < 64
        rc, rsa, rsb = rc_ref[...], rsa_ref[...], rsb_ref[...]
        qa = proj(C_QA, C_KA)
        for g in range(4):
            q = _rope(qa[:, g * LANES:(g + 1) * LANES], rc, rsa, rsb, 16) * ATTN_SCALE
            lo = jnp.where(lane_lo, q, 0.0).astype(BF16).reshape(N_BLK, LANES, LANES)
            hi = jnp.where(lane_lo, 0.0, q).astype(BF16).reshape(N_BLK, LANES, LANES)
            qst_scr[:, g * LANES:(g + 1) * LANES, :] = lo
            qst_scr[:, (4 + g) * LANES:(5 + g) * LANES, :] = hi
        kva = proj(C_KA, C_QB)
        zpad = jnp.zeros((LANES, LANES), BF16)
        kpad_scr[0:LANES, :] = zpad
        kpad_scr[LANES + TOK:, :] = zpad
        vaug_scr[0:LANES, :LANES] = zpad
        vaug_scr[LANES + TOK:, :LANES] = zpad
        kpad_scr[LANES:LANES + TOK, :] = _rope(kva[:, :LANES], rc, rsa, rsb, 16).astype(BF16)
        vaug_scr[LANES:LANES + TOK, :LANES] = kva[:, LANES:].astype(BF16)
        vaug_scr[:, LANES:] = jnp.ones((TOK + 2 * LANES, LANES), BF16)
        ckb_scr[...] = ck_ref[...].astype(BF16)
        cvaug_scr[:, :LANES] = cv_ref[...].astype(BF16)
        cvaug_scr[:, LANES:] = jnp.ones((PAST_LEN, LANES), BF16)
        qi, kj = _row_iota((LANES, LANES)), _lane_iota((LANES, LANES))
        bias_scr[0] = jnp.full((LANES, LANES), -jnp.inf, F32)
        bias_scr[1] = jnp.where(kj >= qi, 0.0, -jnp.inf)
        bias_scr[2] = jnp.where(kj <= qi, 0.0, -jnp.inf)
        lane_lo_b = _lane_iota((LANES, LANES)) < 64

        def scores(n, b):
            q = qst_scr[n]
            sbuf[b][:, :KEYS_LOC] = _dot_nt(q, kpad_scr[_ds(n * LANES, KEYS_LOC), :])
            sbuf[b][:, KEYS_LOC:] = _dot_nt(q, ckb_scr[...])

        def softmax(n, b):
            if isinstance(n, int):
                i_prev, i_next = (1 if n > 0 else 0), (2 if n < N_BLK - 1 else 0)
            else:
                i_prev, i_next = jnp.where(n > 0, 1, 0), jnp.where(n < N_BLK - 1, 2, 0)
            b_prev, b_next = bias_scr[i_prev], bias_scr[i_next]
            for h in range(WIN_HEADS):
                hr = slice(h * LANES, (h + 1) * LANES)
                parts = [sbuf[b][hr, 0:LANES] + b_prev, sbuf[b][hr, LANES:2 * LANES],
                         sbuf[b][hr, 2 * LANES:KEYS_LOC] + b_next, sbuf[b][hr, KEYS_LOC:]]
                ps, extra = _softmax_tile(parts, floor=jnp.full((LANES, 1), sink_ref[h], F32))
                pbuf[b][hr, 0:LANES] = ps[0]
                pbuf[b][hr, LANES:2 * LANES] = ps[1]
                pbuf[b][hr, 2 * LANES:KEYS_LOC] = ps[2]
                pbuf[b][hr, KEYS_LOC:] = ps[3]
                ebuf[b][hr, :] = jnp.broadcast_to(extra, (LANES, LANES))

        def values(n, b):
            oa = (_dot(pbuf[b][:, :KEYS_LOC], vaug_scr[_ds(n * LANES, KEYS_LOC), :])
                  + _dot(pbuf[b][:, KEYS_LOC:], cvaug_scr[...]))
            o = oa[:, :LANES] / (oa[:, LANES:] + ebuf[b][...])
            for g in range(4):
                merged = jnp.where(lane_lo_b, o[g * LANES:(g + 1) * LANES],
                                   o[(4 + g) * LANES:(5 + g) * LANES])
                mix_scr[_ds(n * LANES, LANES), M_A + g * LANES:M_A + (g + 1) * LANES] = (
                    merged.astype(BF16))

        _pipeline(N_BLK, scores, softmax, values)

    pl.run_scoped(
        window,
        pltpu.VMEM((N_BLK, WIN_ROWS, LANES), BF16),
        pltpu.VMEM((TOK + 2 * LANES, LANES), BF16),
        pltpu.VMEM((TOK + 2 * LANES, 2 * LANES), BF16),
        pltpu.VMEM((PAST_LEN, LANES), BF16),
        pltpu.VMEM((PAST_LEN, 2 * LANES), BF16),
        pltpu.VMEM((3, LANES, LANES), F32),
        pltpu.VMEM((WIN_ROWS, LANES), F32), pltpu.VMEM((WIN_ROWS, LANES), F32),
    )

    def retention(qb_scr, kb_scr, vb_scr, gb_scr, rt_scr, m2_scr, sball_scr, st_scr):
        _ret_project(proj, qb_scr, kb_scr, vb_scr, gb_scr)
        _retention_tables(decay_ref, rt_scr, m2_scr)
        st_scr[0] = s0_ref[...]
        _retention([0], DEC_SEQ // RET_CHUNK, qb_scr, kb_scr, vb_scr, gb_scr, gn_ref, rt_scr,
                   m2_scr, sball_scr, st_scr, mix_scr)

    pl.run_scoped(retention, *_ret_scratch(1, DEC_SEQ // RET_CHUNK))

    def latent(qc_scr, kcat_scr, kvaug_scr, mixc_scr):
        lane_lo = _lane_iota((TOK, LANES)) < 64
        mc, msa, msb = mc_ref[...], msa_ref[...], msb_ref[...]
        qc = proj(C_QC, C_CKV)
        for h in range(MLA_HEADS):
            qc_scr[h] = _rope(qc[:, h * LANES:(h + 1) * LANES], mc, msa, msb, 8).astype(BF16)
        ckr = proj(C_CKV, D_IN_P)
        ckv = ckr[:, :LANES]
        kr = _rope(ckr[:, LANES:], mc, msa, msb, 8)
        ckv_n = ckv * lax.rsqrt(jnp.mean(ckv * ckv, axis=-1, keepdims=True) + EPS) * kvn_ref[...]
        kv = _dot(ckv_n.astype(BF16), wkvb_ref[...])
        kv_c = _dot(cckv_ref[...].astype(BF16), wkvb_ref[...])
        kr_c = ckr_ref[...]
        lane_lo_c = _lane_iota((PAST_LEN, LANES)) < 64
        for h in range(MLA_HEADS):
            kvh, kvh_c = kv[:, h * LANES:(h + 1) * LANES], kv_c[:, h * LANES:(h + 1) * LANES]
            kcat_scr[h, 0:TOK, :] = jnp.where(lane_lo, kvh, kr).astype(BF16)
            kcat_scr[h, TOK:, :] = jnp.where(lane_lo_c, kvh_c, kr_c).astype(BF16)
            kvaug_scr[h, 0:TOK, :LANES] = kvh.astype(BF16)
            kvaug_scr[h, TOK:, :LANES] = kvh_c.astype(BF16)
            kvaug_scr[h, :, LANES:] = jnp.ones((MLA_KEYS, LANES), BF16)
        lane_lo_m = _lane_iota((MLA_QB, LANES)) < 64
        n_qb = DEC_SEQ // MLA_QB

        def split(t):
            if isinstance(t, int):
                return t // n_qb, t % n_qb
            return lax.shift_right_logical(t, 2), lax.bitwise_and(t, n_qb - 1)

        def scores(t, b):
            jp, qb = split(t)
            for i in range(2):
                h = 2 * jp + i
                q = qc_scr[h, _ds(qb * MLA_QB, MLA_QB), :]
                for part in range(2):
                    r0 = (2 * i + part) * MLA_QB
                    sbuf[b][r0:r0 + MLA_QB, :] = _dot_nt(
                        q, kcat_scr[h, part * MLA_HALF:(part + 1) * MLA_HALF, :])

        def softmax(t, b):
            for i in range(2):
                for rt in range(MLA_QB // LANES):
                    ra = 2 * i * MLA_QB + rt * LANES
                    rb = ra + MLA_QB
                    ps, _ = _softmax_tile([sbuf[b][ra:ra + LANES, :], sbuf[b][rb:rb + LANES, :]],
                                          scale=MLA_SCALE)
                    pbuf[b][ra:ra + LANES, :] = ps[0]
                    pbuf[b][rb:rb + LANES, :] = ps[1]

        def values(t, b):
            jp, qb = split(t)
            outs = []
            for i in range(2):
                h = 2 * jp + i
                r0 = 2 * i * MLA_QB
                oc = (_dot(pbuf[b][r0:r0 + MLA_QB, :], kvaug_scr[h, 0:MLA_HALF, :])
                      + _dot(pbuf[b][r0 + MLA_QB:r0 + 2 * MLA_QB, :], kvaug_scr[h, MLA_HALF:, :]))
                outs.append(oc[:, :LANES] / oc[:, LANES:])
            merged = jnp.where(lane_lo_m, pltpu.roll(outs[0], 64, 1), outs[1])
            mixc_scr[jp, _ds(qb * MLA_QB, MLA_QB), :] = merged.astype(BF16)

        _pipeline(2 * n_qb, scores, softmax, values)
        for jp in range(2):
            mix_scr[:, M_C + jp * LANES:M_C + (jp + 1) * LANES] = mixc_scr[jp]

    pl.run_scoped(
        latent,
        pltpu.VMEM((MLA_HEADS, TOK, LANES), BF16),
        pltpu.VMEM((MLA_HEADS, MLA_KEYS, LANES), BF16),
        pltpu.VMEM((MLA_HEADS, MLA_KEYS, 2 * LANES), BF16),
        pltpu.VMEM((2, TOK, LANES), BF16),
    )
    _epilogue(x_ref, mod_ref, wout_ref, mix_scr, xo_ref)


def _lat_mixer(x, mod, n1, w_in_p, sink, decay, gn, kvn, w_kv_b, w_out_p,
               ck, cv, cckv, ckr, s0, rope_a, rope_m):
    n_tok = x.shape[0]
    n_seq = n_tok // DEC_SEQ
    seq_spec = lambda shape: pl.BlockSpec((None,) + shape, lambda i: (i,) + (0,) * len(shape))
    assert WIN_ROWS == 4 * MLA_QB and KEYS_WIN == MLA_HALF
    return pl.pallas_call(
        _lat_mixer_kernel,
        out_shape=jax.ShapeDtypeStruct((n_tok, D_MODEL), F32),
        grid=(n_seq,),
        in_specs=[
            pl.BlockSpec((TOK, D_MODEL), lambda i: (i, 0)),
            seq_spec((N_MOD, D_MODEL)),
            _const_spec((1, D_MODEL)),
            _const_spec((D_MODEL, D_IN_P)),
            _smem_spec(),
            _smem_spec(),
            _const_spec((1, RET_HEADS * HEAD_DIM)),
            _const_spec((1, MLA_KV_RANK)),
            _const_spec((MLA_KV_RANK, MLA_HEADS * LANES)),
            _const_spec((D_MODEL, D_MODEL)),
            seq_spec((PAST_LEN, LANES)),
            seq_spec((PAST_LEN, LANES)),
            seq_spec((PAST_LEN, MLA_KV_RANK)),
            seq_spec((PAST_LEN, LANES)),
            seq_spec((2, 2, LANES, LANES)),
        ] + [_const_spec((DEC_SEQ, LANES))] * 6,
        out_specs=pl.BlockSpec((TOK, D_MODEL), lambda i: (i, 0)),
        scratch_shapes=[
            pltpu.VMEM((TOK, D_MODEL), BF16),
            pltpu.VMEM((TOK, D_MODEL), BF16),
            pltpu.VMEM((WIN_ROWS, KEYS_WIN), F32), pltpu.VMEM((WIN_ROWS, KEYS_WIN), F32),
            pltpu.VMEM((WIN_ROWS, KEYS_WIN), BF16), pltpu.VMEM((WIN_ROWS, KEYS_WIN), BF16),
        ],
        compiler_params=pltpu.CompilerParams(
            dimension_semantics=("arbitrary",), vmem_limit_bytes=VMEM_LIMIT),
        name="lat_mixer",
    )(x, mod, n1, w_in_p, sink, decay, gn, kvn, w_kv_b, w_out_p, ck, cv, cckv, ckr, s0,
      *rope_a, *rope_m)


def _mlp_kernel(final, x_ref, mod_ref, n2_ref, wup_ref, wdn_ref, fn_ref, o_ref):
    x = x_ref[...]
    h2 = _norm_mod(x, n2_ref[...], mod_ref[4:5, :], mod_ref[3:4, :]).astype(BF16)
    acc = None
    for c in range(D_FF // FF_CHUNK):
        cols = slice(c * FF_CHUNK, (c + 1) * FF_CHUNK)
        u = jnp.maximum(_dot(h2, wup_ref[:, cols]), 0.0)
        part = _dot((u * u).astype(BF16), wdn_ref[cols, :])
        acc = part if acc is None else acc + part
    y = x + mod_ref[5:6, :] * acc
    if final:
        y = y * lax.rsqrt(jnp.mean(y * y, axis=-1, keepdims=True) + EPS) * fn_ref[...]
    o_ref[...] = y


def _mlp(x, mod, n2, w_up, w_down, final_norm, final):
    n_tok = x.shape[0]
    per_mod = n_tok // mod.shape[0] // MLP_ROWS
    return pl.pallas_call(
        functools.partial(_mlp_kernel, final),
        out_shape=jax.ShapeDtypeStruct((n_tok, D_MODEL), F32),
        grid=(n_tok // MLP_ROWS,),
        in_specs=[
            pl.BlockSpec((MLP_ROWS, D_MODEL), lambda i: (i, 0)),
            pl.BlockSpec((None, N_MOD, D_MODEL), lambda i: (i // per_mod, 0, 0)),
            _const_spec((1, D_MODEL)),
            _const_spec((D_MODEL, D_FF)),
            _const_spec((D_FF, D_MODEL)),
            _const_spec((1, D_MODEL)),
        ],
        out_specs=pl.BlockSpec((MLP_ROWS, D_MODEL), lambda i: (i, 0)),
        compiler_params=pltpu.CompilerParams(
            dimension_semantics=("arbitrary",), vmem_limit_bytes=VMEM_LIMIT),
        name="mlp",
    )(x, mod, n2, w_up, w_down, final_norm)


def kernel(x_prompt, x_sample, cache_win_k, cache_win_v, cache_mla_ckv, cache_mla_krope, state_ret,
           c, c_ctx, w_mod, b_mod, norm1, norm2, w_in, win_sink, ret_decay, ret_gn, mla_kv_norm,
           w_kv_b, w_out, w_up, w_down, final_norm):
    n_ctx, n_lat = x_prompt.shape[0], x_sample.shape[0]

    cols = _in_proj_columns()
    w_in_p = jnp.where(cols >= 0, jnp.take(w_in, np.maximum(cols, 0), axis=2), 0.0).astype(BF16)
    w_out_p = jnp.take(w_out, _mix_rows(), axis=1).astype(BF16)
    w_kv_b16, w_up16, w_down16 = w_kv_b.astype(BF16), w_up.astype(BF16), w_down.astype(BF16)

    c_rows = jnp.zeros((16, D_MODEL), F32).at[0].set(c_ctx).at[1:1 + n_lat].set(c)
    mod = _modulation(c_rows, w_mod, b_mod).reshape(DEPTH, 16, N_MOD, D_MODEL)

    rope_a = _rope_tables(DEC_SEQ, HEAD_DIM, 0, HEAD_DIM)
    rope_m = _rope_tables(DEC_SEQ, MLA_ROPE, MLA_NOPE, LANES)

    ck = cache_win_k.reshape(n_lat, DEPTH, PAST_LEN, LANES)
    cv = cache_win_v.reshape(n_lat, DEPTH, PAST_LEN, LANES)
    ckr = jnp.pad(cache_mla_krope, ((0, 0), (0, 0), (0, 0), (MLA_NOPE, LANES - MLA_NOPE - MLA_ROPE)))
    sr = state_ret.reshape(n_lat, DEPTH, 2, 2, 2, HEAD_DIM, HEAD_DIM)
    zero = jnp.zeros_like(sr[:, :, :, :, 0])
    s0 = jnp.concatenate([jnp.concatenate([sr[:, :, :, :, 0], zero], axis=-1),
                          jnp.concatenate([zero, sr[:, :, :, :, 1]], axis=-1)], axis=-2)

    xp = x_prompt.reshape(n_ctx * SEQ, D_MODEL)
    xs = x_sample.reshape(n_lat * DEC_SEQ, D_MODEL)
    new = [[] for _ in range(5)]
    for l in range(DEPTH):
        last = l == DEPTH - 1
        shared = (norm1[l][None], w_in_p[l], win_sink[l], ret_decay[l], ret_gn[l][None],
                  mla_kv_norm[l][None], w_kv_b16[l], w_out_p[l])
        mod_ctx, mod_lat = mod[l, 0:1], mod[l, 1:1 + n_lat]
        xp, k_l, v_l, ckv_l, kr_l, s_l = _ctx_mixer(xp, mod_ctx[0], *shared)
        xp = _mlp(xp, mod_ctx, norm2[l][None], w_up16[l], w_down16[l], final_norm[None], last)
        xs = _lat_mixer(xs, mod_lat, *shared, ck[:, l], cv[:, l], cache_mla_ckv[:, l], ckr[:, l],
                        s0[:, l], rope_a, rope_m)
        xs = _mlp(xs, mod_lat, norm2[l][None], w_up16[l], w_down16[l], final_norm[None], last)
        for lst, val in zip(new, (k_l, v_l, ckv_l, kr_l, s_l)):
            lst.append(val)
    new_k = jnp.stack([a.reshape(n_ctx, SEQ, 2, HEAD_DIM) for a in new[0]], axis=1)
    new_v = jnp.stack([a.reshape(n_ctx, SEQ, 2, HEAD_DIM) for a in new[1]], axis=1)
    new_ckv = jnp.stack([a.reshape(n_ctx, SEQ, MLA_KV_RANK) for a in new[2]], axis=1)
    new_kr = jnp.stack([a.reshape(n_ctx, SEQ, MLA_ROPE) for a in new[3]], axis=1)
    new_s = jnp.stack(new[4], axis=1)
    return (xp.reshape(n_ctx, SEQ, D_MODEL), xs.reshape(n_lat, DEC_SEQ, D_MODEL),
            new_k, new_v, new_ckv, new_kr, new_s)
```

```python
import functools

import numpy as np
import jax
import jax.numpy as jnp
from jax import lax
from jax.experimental import pallas as pl
from jax.experimental.pallas import tpu as pltpu

F32 = jnp.float32
BF16 = jnp.bfloat16

D_MODEL = 1024
DEPTH = 2
SEQ = 256
DEC_SEQ = 1024
PAST_LEN = 256
GRID_W = 64
HEAD_DIM = 64
ROPE_BASE = 10000.0
EPS = 1e-6
WIN_HEADS = 8
WINDOW = 128
ATTN_SCALE = HEAD_DIM ** -0.5
RET_HEADS = 4
RET_CHUNK = 128
RET_K_SCALE = HEAD_DIM ** -0.5
MLA_HEADS = 4
MLA_NOPE = 64
MLA_ROPE = 32
MLA_KV_RANK = 128
MLA_QK = MLA_NOPE + MLA_ROPE
MLA_SCALE = MLA_QK ** -0.5
D_IN = 2336
D_FF = 4 * D_MODEL
N_MOD = 6

LANES = 128
TOK = 1024
MLP_ROWS = 512
FF_CHUNK = 1024
VMEM_LIMIT = 60 * 1024 * 1024

C_QA, C_KA, C_VA, C_QB, C_KB, C_VB, C_GB, C_QC, C_CKV, C_KR, D_IN_P = (
    0, 512, 640, 768, 1024, 1280, 1536, 1792, 2304, 2432, 2560)
M_A, M_B, M_C = 0, 512, 768

NT_DIMS = (((1,), (1,)), ((), ()))


def _in_proj_columns():
    idx = []
    for g in range(4):
        idx += list(range(g * 64, (g + 1) * 64)) + list(range((4 + g) * 64, (5 + g) * 64))
    idx += list(range(512, 1792))
    for h in range(MLA_HEADS):
        idx += list(range(1792 + h * MLA_QK, 1792 + (h + 1) * MLA_QK)) + [-1] * 32
    idx += list(range(2176, 2304))
    idx += [-1] * 64 + list(range(2304, 2336)) + [-1] * 32
    return np.asarray(idx, np.int32)


def _mix_rows():
    idx = []
    for g in range(4):
        idx += list(range(g * 64, (g + 1) * 64)) + list(range((4 + g) * 64, (5 + g) * 64))
    idx += list(range(512, 1024))
    return np.asarray(idx, np.int32)


def _take_runs(w, idx, axis):
    pieces, i = [], 0
    while i < len(idx):
        j = i + 1
        if idx[i] < 0:
            while j < len(idx) and idx[j] < 0:
                j += 1
            shape = list(w.shape)
            shape[axis] = j - i
            pieces.append(jnp.zeros(shape, w.dtype))
        else:
            while j < len(idx) and idx[j] == idx[j - 1] + 1:
                j += 1
            pieces.append(lax.slice_in_dim(w, int(idx[i]), int(idx[j - 1]) + 1, axis=axis))
        i = j
    return jnp.concatenate(pieces, axis=axis)


def _rope_tables(n_tokens, dim, lane0, period):
    quarter = dim // 4
    t = np.arange(n_tokens)
    row = (t // GRID_W).astype(np.float64)
    col = (t % GRID_W).astype(np.float64)
    inv_freq = ROPE_BASE ** (-np.arange(quarter, dtype=np.float64) / quarter)
    ar, ac = row[:, None] * inv_freq, col[:, None] * inv_freq
    cos = np.concatenate([np.cos(ar), np.cos(ar), np.cos(ac), np.cos(ac)], axis=-1)
    sin = np.concatenate([np.sin(ar), np.sin(ar), np.sin(ac), np.sin(ac)], axis=-1)
    first = np.tile(np.concatenate([np.ones(quarter), np.zeros(quarter)]), 2)
    c = np.ones((n_tokens, LANES))
    sa = np.zeros((n_tokens, LANES))
    sb = np.zeros((n_tokens, LANES))
    for start in range(lane0, LANES, period):
        c[:, start:start + dim] = cos
        sa[:, start:start + dim] = -sin * first
        sb[:, start:start + dim] = sin * (1.0 - first)
    return tuple(jnp.asarray(a, F32) for a in (c, sa, sb))


def _lane_iota(shape):
    return lax.broadcasted_iota(jnp.int32, shape, len(shape) - 1)


def _row_iota(shape):
    return lax.broadcasted_iota(jnp.int32, shape, len(shape) - 2)


def _ds(start, size):
    if isinstance(start, int):
        return pl.ds(start, size)
    return pl.ds(pl.multiple_of(start, LANES), size)


def _norm_mod(x, gain, scale, shift):
    ms = jnp.mean(x * x, axis=-1, keepdims=True)
    return (x * lax.rsqrt(ms + EPS) * gain) * (1.0 + scale) + shift


def _log_sigmoid(x):
    return -(jnp.maximum(-x, 0.0) + jnp.log1p(jnp.exp(-jnp.abs(x))))


def _rope(x, c, sa, sb, quarter):
    return (x * c + pltpu.roll(x, LANES - quarter, 1) * sa + pltpu.roll(x, quarter, 1) * sb)


def _dot(a, b):
    return jnp.dot(a, b, preferred_element_type=F32)


def _dot_nt(a, b):
    return lax.dot_general(a, b, NT_DIMS, preferred_element_type=F32)


def _mod_kernel(c_ref, w_ref, b_ref, o_ref):
    cv = c_ref[...]
    s = cv * jax.nn.sigmoid(cv)
    o_ref[0] = _dot(s.astype(BF16), w_ref[0].astype(BF16)) + b_ref[0]


def _modulation(c_rows, w_mod, b_mod):
    tn = 1536
    nj = (N_MOD * D_MODEL) // tn
    return pl.pallas_call(
        _mod_kernel,
        out_shape=jax.ShapeDtypeStruct((DEPTH, 16, N_MOD * D_MODEL), F32),
        grid=(DEPTH, nj),
        in_specs=[
            pl.BlockSpec((16, D_MODEL), lambda l, j: (0, 0)),
            pl.BlockSpec((1, D_MODEL, tn), lambda l, j: (l, 0, j)),
            pl.BlockSpec((1, 1, tn), lambda l, j: (l, 0, j)),
        ],
        out_specs=pl.BlockSpec((1, 16, tn), lambda l, j: (l, 0, j)),
        compiler_params=pltpu.CompilerParams(
            dimension_semantics=("arbitrary", "arbitrary"), vmem_limit_bytes=VMEM_LIMIT),
        name="modulation",
    )(c_rows, w_mod, b_mod.reshape(DEPTH, 1, N_MOD * D_MODEL))


def _pipeline(n_steps, scores, softmax, values):
    scores(0, 0)
    scores(1, 1)
    softmax(0, 0)

    def body(i, carry):
        t = 2 * i + 1
        scores(t + 1, 0)
        softmax(t, 1)
        values(t - 1, 0)
        scores(t + 2, 1)
        softmax(t + 1, 0)
        values(t, 1)
        return carry

    lax.fori_loop(0, n_steps // 2 - 1, body, 0)
    softmax(n_steps - 1, 1)
    values(n_steps - 2, 0)
    values(n_steps - 1, 1)


def _softmax_tile(parts, floor=None, scale=None):
    m = None
    for s in parts:
        pm = jnp.max(s, axis=-1, keepdims=True)
        m = pm if m is None else jnp.maximum(m, pm)
    if floor is not None:
        m = jnp.maximum(m, floor)
    if scale is None:
        ps = [jnp.exp(s - m).astype(BF16) for s in parts]
    else:
        ps = [jnp.exp((s - m) * scale).astype(BF16) for s in parts]
    extra = None if floor is None else jnp.exp(floor - m)
    return ps, extra


def _retention_tables(decay_ref, rt_scr, m2_scr):
    shape = (RET_CHUNK, LANES)
    lane_lo = _lane_iota(shape) < 64
    row = _row_iota(shape)
    row_lo = row < 64
    i = row.astype(F32)
    rel = i - _lane_iota(shape).astype(F32)
    for j in range(2):
        df0, df1 = decay_ref[0, 2 * j], decay_ref[0, 2 * j + 1]
        db0, db1 = decay_ref[1, 2 * j], decay_ref[1, 2 * j + 1]
        lgf = _log_sigmoid(jnp.where(lane_lo, df0, df1))
        lgb = _log_sigmoid(jnp.where(lane_lo, db0, db1))
        rt_scr[j, 0] = jnp.exp((i + 1.0) * lgf)
        rt_scr[j, 1] = jnp.exp((RET_CHUNK - i) * lgb)
        rt_scr[j, 2] = jnp.exp((RET_CHUNK - 1.0 - i) * lgf)
        rt_scr[j, 3] = jnp.exp(i * lgb)
        rt_scr[j, 4] = jnp.exp(RET_CHUNK * _log_sigmoid(jnp.where(row_lo, df0, df1)))
        rt_scr[j, 5] = jnp.exp(RET_CHUNK * _log_sigmoid(jnp.where(row_lo, db0, db1)))
        for hh, (df, db) in enumerate(((df0, db0), (df1, db1))):
            lf = _log_sigmoid(jnp.full(shape, df, F32))
            lb = _log_sigmoid(jnp.full(shape, db, F32))
            low, upp = rel >= 0.0, rel <= 0.0
            m = (jnp.where(low, jnp.exp(jnp.where(low, rel, 0.0) * lf), 0.0)
                 + jnp.where(upp, jnp.exp(jnp.where(upp, -rel, 0.0) * lb), 0.0))
            m2_scr[j, hh * RET_CHUNK:(hh + 1) * RET_CHUNK, :] = m


def _retention(seq_bases, nc, qb_scr, kb_scr, vb_scr, gb_scr, gn_ref, rt_scr, m2_scr,
               sball_scr, st_scr, mix_scr):
    shape = (RET_CHUNK, LANES)
    lane_lo = _lane_iota(shape) < 64
    blockdiag = lane_lo == (_row_iota(shape) < 64)
    chains = [(si, base, j) for si, base in enumerate(seq_bases) for j in range(2)]

    def state_update(dirn, c):
        for si, base, j in chains:
            rows, cols = _ds(base + c * RET_CHUNK, RET_CHUNK), slice(j * LANES, (j + 1) * LANES)
            kd = kb_scr[rows, cols] * rt_scr[j, 2 + dirn]
            upd = _dot(kd.T.astype(BF16), vb_scr[rows, cols])
            st_scr[si, dirn, j] = (rt_scr[j, 4 + dirn] * st_scr[si, dirn, j]
                                   + jnp.where(blockdiag, upd, 0.0))

    def bwd_body(t, carry):
        c = nc - 1 - t
        for si, _, j in chains:
            sball_scr[c, si, j] = st_scr[si, 1, j]
        state_update(1, c)
        return carry

    lax.fori_loop(0, nc, bwd_body, 0)

    def fwd_body(c, carry):
        for si, base, j in chains:
            rows, cols = _ds(base + c * RET_CHUNK, RET_CHUNK), slice(j * LANES, (j + 1) * LANES)
            q2 = qb_scr[rows, cols]
            qs = jnp.concatenate([jnp.where(lane_lo, q2, 0.0), jnp.where(lane_lo, 0.0, q2)],
                                 axis=0).astype(BF16)
            sc = _dot_nt(qs, kb_scr[rows, cols].astype(BF16)) * m2_scr[j]
            o2 = _dot(sc.astype(BF16), vb_scr[rows, cols])
            o = jnp.where(lane_lo, o2[:RET_CHUNK], o2[RET_CHUNK:])
            o = o + _dot((q2 * rt_scr[j, 0]).astype(BF16), st_scr[si, 0, j].astype(BF16))
            o = o + _dot((q2 * rt_scr[j, 1]).astype(BF16), sball_scr[c, si, j].astype(BF16))
            s_lo = jnp.sum(jnp.where(lane_lo, o, 0.0), axis=-1, keepdims=True)
            s_hi = jnp.sum(jnp.where(lane_lo, 0.0, o), axis=-1, keepdims=True)
            d = o - jnp.where(lane_lo, s_lo, s_hi) * (1.0 / HEAD_DIM)
            dd = d * d
            v_lo = jnp.sum(jnp.where(lane_lo, dd, 0.0), axis=-1, keepdims=True)
            v_hi = jnp.sum(jnp.where(lane_lo, 0.0, dd), axis=-1, keepdims=True)
            var = jnp.where(lane_lo, v_lo, v_hi) * (1.0 / HEAD_DIM)
            g2 = gb_scr[rows, cols]
            y = d * lax.rsqrt(var + EPS) * gn_ref[:, cols] * (g2 * jax.nn.sigmoid(g2))
            mix_scr[rows, M_B + j * LANES:M_B + (j + 1) * LANES] = y.astype(BF16)
        state_update(0, c)
        return carry

    lax.fori_loop(0, nc, fwd_body, 0)


def _prologue(x_ref, mod_ref, n1_ref, h_scr):
    def body(r, carry):
        rows = _ds(r * 256, 256)
        h = _norm_mod(x_ref[rows, :], n1_ref[...], mod_ref[1:2, :], mod_ref[0:1, :])
        h_scr[rows, :] = h.astype(BF16)
        return carry
    lax.fori_loop(0, TOK // 256, body, 0)


def _epilogue(x_ref, mod_ref, wout_ref, mix_scr, xo_ref):
    def body(r, carry):
        rows = _ds(r * 256, 256)
        y = _dot(mix_scr[rows, :], wout_ref[...])
        xo_ref[rows, :] = x_ref[rows, :] + mod_ref[2:3, :] * y
        return carry
    lax.fori_loop(0, TOK // 256, body, 0)


def _ret_scratch(n_seq, nc):
    return [
        pltpu.VMEM((TOK, 256), F32),
        pltpu.VMEM((TOK, 256), F32),
        pltpu.VMEM((TOK, 256), BF16),
        pltpu.VMEM((TOK, 256), F32),
        pltpu.VMEM((2, 6, RET_CHUNK, LANES), F32),
        pltpu.VMEM((2, 2 * RET_CHUNK, LANES), F32),
        pltpu.VMEM((nc, n_seq, 2, LANES, LANES), F32),
        pltpu.VMEM((n_seq, 2, 2, LANES, LANES), F32),
    ]


def _ret_project(proj, qb_scr, kb_scr, vb_scr, gb_scr):
    qb_scr[...] = proj(C_QB, C_KB)
    kb_scr[...] = proj(C_KB, C_VB) * RET_K_SCALE
    vb_scr[...] = proj(C_VB, C_GB).astype(BF16)
    gb_scr[...] = proj(C_GB, C_QC)


CTX_PER_STEP = TOK // SEQ
CTX_WROWS = WIN_HEADS * SEQ
CTX_MROWS = MLA_HEADS * SEQ


def _ctx_mixer_kernel(n_alias, x_ref, mod_ref, n1_ref, win_ref, sink_ref, decay_ref, gn_ref, kvn_ref,
                      wkvb_ref, wout_ref, *refs):
    xo_ref, ko_ref, vo_ref, ckvo_ref, kro_ref, so_ref, h_scr, mix_scr = refs[n_alias:]
    _prologue(x_ref, mod_ref, n1_ref, h_scr)
    per_seq = lambda a: a.reshape(CTX_PER_STEP, SEQ, a.shape[-1])

    def proj(c0, c1):
        return _dot(h_scr[...], win_ref[:, c0:c1])

    def attention(qst_scr, k_scr, vaug_scr, qc_scr, kcat_scr, kvaug_scr,
                  sw0, sw1, pw0, pw1, ew0, ew1, sm0, sm1, pm0, pm1):
        lane_lo = _lane_iota((TOK, LANES)) < 64
        ones = jnp.ones((TOK, LANES), BF16)
        qa = proj(C_QA, C_KA) * ATTN_SCALE
        for g in range(4):
            q = qa[:, g * LANES:(g + 1) * LANES]
            lo = jnp.where(lane_lo, q, 0.0).astype(BF16).reshape(CTX_PER_STEP, SEQ, LANES)
            hi = jnp.where(lane_lo, 0.0, q).astype(BF16).reshape(CTX_PER_STEP, SEQ, LANES)
            qst_scr[:, g * SEQ:(g + 1) * SEQ, :] = lo
            qst_scr[:, (4 + g) * SEQ:(5 + g) * SEQ, :] = hi
        kva = proj(C_KA, C_QB)
        ko_ref[...] = per_seq(kva[:, :LANES])
        vo_ref[...] = per_seq(kva[:, LANES:])
        k_scr[...] = kva[:, :LANES].astype(BF16)
        vaug_scr[:, :LANES] = kva[:, LANES:].astype(BF16)
        vaug_scr[:, LANES:] = ones
        qc = proj(C_QC, C_CKV).astype(BF16)
        for h in range(MLA_HEADS):
            qc_scr[h] = qc[:, h * LANES:(h + 1) * LANES]
        ckr = proj(C_CKV, D_IN_P)
        ckv, kr = ckr[:, :LANES], ckr[:, LANES:]
        ckv_n = ckv * lax.rsqrt(jnp.mean(ckv * ckv, axis=-1, keepdims=True) + EPS) * kvn_ref[...]
        ckvo_ref[...] = per_seq(ckv_n)
        kro_ref[...] = per_seq(kr[:, 64:64 + MLA_ROPE])
        kv = _dot(ckv_n.astype(BF16), wkvb_ref[...])
        for h in range(MLA_HEADS):
            kvh = kv[:, h * LANES:(h + 1) * LANES]
            kcat_scr[h] = jnp.where(lane_lo, kvh, kr).astype(BF16)
            kvaug_scr[h, :, :LANES] = kvh.astype(BF16)
            kvaug_scr[h, :, LANES:] = ones

        sw, pw, ew, sm, pm = (sw0, sw1), (pw0, pw1), (ew0, ew1), (sm0, sm1), (pm0, pm1)
        lane_lo_s = _lane_iota((SEQ, LANES)) < 64

        def scores(e, b):
            rows = _ds(e * SEQ, SEQ)
            sw[b][...] = _dot_nt(qst_scr[e], k_scr[rows, :])
            for h in range(MLA_HEADS):
                sm[b][h * SEQ:(h + 1) * SEQ, :] = _dot_nt(qc_scr[h, rows, :], kcat_scr[h, rows, :])

        def softmax(e, b):
            for h in range(WIN_HEADS):
                hr = slice(h * SEQ, (h + 1) * SEQ)
                sink = jnp.full((SEQ, 1), sink_ref[h], F32)
                (p,), extra = _softmax_tile([sw[b][hr, :]], floor=sink)
                pw[b][hr, :] = p
                ew[b][hr, :] = jnp.broadcast_to(extra, (SEQ, LANES))
            for h in range(MLA_HEADS):
                hr = slice(h * SEQ, (h + 1) * SEQ)
                (p,), _ = _softmax_tile([sm[b][hr, :]], scale=MLA_SCALE)
                pm[b][hr, :] = p

        def values(e, b):
            rows = _ds(e * SEQ, SEQ)
            oa = _dot(pw[b][...], vaug_scr[rows, :])
            o = oa[:, :LANES] / (oa[:, LANES:] + ew[b][...])
            for g in range(4):
                merged = jnp.where(lane_lo_s, o[g * SEQ:(g + 1) * SEQ], o[(4 + g) * SEQ:(5 + g) * SEQ])
                mix_scr[rows, M_A + g * LANES:M_A + (g + 1) * LANES] = merged.astype(BF16)
            for jp in range(2):
                outs = []
                for h in (2 * jp, 2 * jp + 1):
                    oc = _dot(pm[b][h * SEQ:(h + 1) * SEQ, :], kvaug_scr[h, rows, :])
                    outs.append(oc[:, :LANES] / oc[:, LANES:])
                merged = jnp.where(lane_lo_s, pltpu.roll(outs[0], 64, 1), outs[1])
                mix_scr[rows, M_C + jp * LANES:M_C + (jp + 1) * LANES] = merged.astype(BF16)

        _pipeline(CTX_PER_STEP, scores, softmax, values)

    pl.run_scoped(
        attention,
        pltpu.VMEM((CTX_PER_STEP, CTX_WROWS, LANES), BF16),
        pltpu.VMEM((TOK, LANES), BF16),
        pltpu.VMEM((TOK, 2 * LANES), BF16),
        pltpu.VMEM((MLA_HEADS, TOK, LANES), BF16),
        pltpu.VMEM((MLA_HEADS, TOK, LANES), BF16),
        pltpu.VMEM((MLA_HEADS, TOK, 2 * LANES), BF16),
        pltpu.VMEM((CTX_WROWS, SEQ), F32), pltpu.VMEM((CTX_WROWS, SEQ), F32),
        pltpu.VMEM((CTX_WROWS, SEQ), BF16), pltpu.VMEM((CTX_WROWS, SEQ), BF16),
        pltpu.VMEM((CTX_WROWS, LANES), F32), pltpu.VMEM((CTX_WROWS, LANES), F32),
        pltpu.VMEM((CTX_MROWS, SEQ), F32), pltpu.VMEM((CTX_MROWS, SEQ), F32),
        pltpu.VMEM((CTX_MROWS, SEQ), BF16), pltpu.VMEM((CTX_MROWS, SEQ), BF16),
    )

    def retention(qb_scr, kb_scr, vb_scr, gb_scr, rt_scr, m2_scr, sball_scr, st_scr):
        _ret_project(proj, qb_scr, kb_scr, vb_scr, gb_scr)
        _retention_tables(decay_ref, rt_scr, m2_scr)
        st_scr[...] = jnp.zeros(st_scr.shape, F32)
        _retention([e * SEQ for e in range(CTX_PER_STEP)], SEQ // RET_CHUNK, qb_scr, kb_scr, vb_scr,
                   gb_scr, gn_ref, rt_scr, m2_scr, sball_scr, st_scr, mix_scr)
        for e in range(CTX_PER_STEP):
            for dirn in range(2):
                for j in range(2):
                    st = st_scr[e, dirn, j]
                    so_ref[e, dirn, 2 * j] = st[:64, :64]
                    so_ref[e, dirn, 2 * j + 1] = st[64:, 64:]

    pl.run_scoped(retention, *_ret_scratch(CTX_PER_STEP, SEQ // RET_CHUNK))
    _epilogue(x_ref, mod_ref, wout_ref, mix_scr, xo_ref)


def _const_spec(shape, layer=None):
    if layer is None:
        return pl.BlockSpec(shape, lambda i: (0,) * len(shape), pipeline_mode=pl.Buffered(1))
    return pl.BlockSpec((None,) + shape, lambda i: (layer,) + (0,) * len(shape),
                        pipeline_mode=pl.Buffered(1))


def _smem_spec():
    return pl.BlockSpec(memory_space=pltpu.SMEM)


def _ctx_mixer(layer, x, mod, n1, w_in_p, sink, decay, gn, kvn, w_kv_b, w_out_p, prev_state):
    n_tok = x.shape[0]
    n_seq = n_tok // SEQ
    row_spec = lambda w: pl.BlockSpec((TOK, w), lambda i: (i, 0))
    state_spec = lambda *tail: pl.BlockSpec(
        (CTX_PER_STEP, None) + tail, lambda i: (i, layer) + (0,) * len(tail))
    state_tails = [(SEQ, LANES), (SEQ, LANES), (SEQ, MLA_KV_RANK), (SEQ, MLA_ROPE),
                   (2, RET_HEADS, HEAD_DIM, HEAD_DIM)]
    n_in = 10
    return pl.pallas_call(
        functools.partial(_ctx_mixer_kernel, len(prev_state)),
        out_shape=[jax.ShapeDtypeStruct((n_tok, D_MODEL), F32)] + [
            jax.ShapeDtypeStruct((n_seq, DEPTH) + tail, F32) for tail in state_tails],
        grid=(n_tok // TOK,),
        in_specs=[
            row_spec(D_MODEL),
            _const_spec((N_MOD, D_MODEL)),
            _const_spec((1, D_MODEL)),
            _const_spec((D_MODEL, D_IN_P), layer),
            _smem_spec(),
            _smem_spec(),
            _const_spec((1, RET_HEADS * HEAD_DIM)),
            _const_spec((1, MLA_KV_RANK)),
            _const_spec((MLA_KV_RANK, MLA_HEADS * LANES), layer),
            _const_spec((D_MODEL, D_MODEL), layer),
        ] + [pl.BlockSpec(memory_space=pl.ANY)] * len(prev_state),
        out_specs=[row_spec(D_MODEL)] + [state_spec(*tail) for tail in state_tails],
        input_output_aliases={n_in + k: 1 + k for k in range(len(prev_state))},
        scratch_shapes=[
            pltpu.VMEM((TOK, D_MODEL), BF16),
            pltpu.VMEM((TOK, D_MODEL), BF16),
        ],
        compiler_params=pltpu.CompilerParams(
            dimension_semantics=("arbitrary",), vmem_limit_bytes=VMEM_LIMIT),
        name="ctx_mixer",
    )(x, mod, n1, w_in_p, sink, decay, gn, kvn, w_kv_b, w_out_p, *prev_state)


N_BLK = DEC_SEQ // LANES
KEYS_LOC = 3 * LANES
KEYS_WIN = KEYS_LOC + PAST_LEN
WIN_ROWS = WIN_HEADS * LANES
MLA_QB = 256
MLA_KEYS = DEC_SEQ + PAST_LEN
MLA_HALF = MLA_KEYS // 2


def _lat_mixer_kernel(x_ref, mod_ref, n1_ref, win_ref, sink_ref, decay_ref, gn_ref, kvn_ref,
                      wkvb_ref, wout_ref, ck_ref, cv_ref, cckv_ref, ckr_ref, s0_ref,
                      rc_ref, rsa_ref, rsb_ref, mc_ref, msa_ref, msb_ref,
                      xo_ref,
                      h_scr, mix_scr, s0_scr, s1_scr, p0_scr, p1_scr):
    _prologue(x_ref, mod_ref, n1_ref, h_scr)
    sbuf, pbuf = (s0_scr, s1_scr), (p0_scr, p1_scr)

    def proj(c0, c1):
        return _dot(h_scr[...], win_ref[:, c0:c1])

    def window(qst_scr, kpad_scr, vaug_scr, ckb_scr, cvaug_scr, bias_scr, e0_scr, e1_scr):
        ebuf = (e0_scr, e1_scr)
        lane_lo = _lane_iota((TOK, LANES)) < 64
        rc, rsa, rsb = rc_ref[...], rsa_ref[...], rsb_ref[...]
        qa = proj(C_QA, C_KA)
        for g in range(4):
            q = _rope(qa[:, g * LANES:(g + 1) * LANES], rc, rsa, rsb, 16) * ATTN_SCALE
            lo = jnp.where(lane_lo, q, 0.0).astype(BF16).reshape(N_BLK, LANES, LANES)
            hi = jnp.where(lane_lo, 0.0, q).astype(BF16).reshape(N_BLK, LANES, LANES)
            qst_scr[:, g * LANES:(g + 1) * LANES, :] = lo
            qst_scr[:, (4 + g) * LANES:(5 + g) * LANES, :] = hi
        kva = proj(C_KA, C_QB)
        zpad = jnp.zeros((LANES, LANES), BF16)
        kpad_scr[0:LANES, :] = zpad
        kpad_scr[LANES + TOK:, :] = zpad
        vaug_scr[0:LANES, :LANES] = zpad
        vaug_scr[LANES + TOK:, :LANES] = zpad
        kpad_scr[LANES:LANES + TOK, :] = _rope(kva[:, :LANES], rc, rsa, rsb, 16).astype(BF16)
        vaug_scr[LANES:LANES + TOK, :LANES] = kva[:, LANES:].astype(BF16)
        vaug_scr[:, LANES:] = jnp.ones((TOK + 2 * LANES, LANES), BF16)
        ckb_scr[...] = ck_ref[...].astype(BF16)
        cvaug_scr[:, :LANES] = cv_ref[...].astype(BF16)
        cvaug_scr[:, LANES:] = jnp.ones((PAST_LEN, LANES), BF16)
        qi, kj = _row_iota((LANES, LANES)), _lane_iota((LANES, LANES))
        bias_scr[0] = jnp.full((LANES, LANES), -jnp.inf, F32)
        bias_scr[1] = jnp.where(kj >= qi, 0.0, -jnp.inf)
        bias_scr[2] = jnp.where(kj <= qi, 0.0, -jnp.inf)
        lane_lo_b = _lane_iota((LANES, LANES)) < 64

        def scores(n, b):
            q = qst_scr[n]
            sbuf[b][:, :KEYS_LOC] = _dot_nt(q, kpad_scr[_ds(n * LANES, KEYS_LOC), :])
            sbuf[b][:, KEYS_LOC:] = _dot_nt(q, ckb_scr[...])

        def softmax(n, b):
            if isinstance(n, int):
                i_prev, i_next = (1 if n > 0 else 0), (2 if n < N_BLK - 1 else 0)
            else:
                i_prev, i_next = jnp.where(n > 0, 1, 0), jnp.where(n < N_BLK - 1, 2, 0)
            b_prev, b_next = bias_scr[i_prev], bias_scr[i_next]
            for h in range(WIN_HEADS):
                hr = slice(h * LANES, (h + 1) * LANES)
                parts = [sbuf[b][hr, 0:LANES] + b_prev, sbuf[b][hr, LANES:2 * LANES],
                         sbuf[b][hr, 2 * LANES:KEYS_LOC] + b_next, sbuf[b][hr, KEYS_LOC:]]
                ps, extra = _softmax_tile(parts, floor=jnp.full((LANES, 1), sink_ref[h], F32))
                pbuf[b][hr, 0:LANES] = ps[0]
                pbuf[b][hr, LANES:2 * LANES] = ps[1]
                pbuf[b][hr, 2 * LANES:KEYS_LOC] = ps[2]
                pbuf[b][hr, KEYS_LOC:] = ps[3]
                ebuf[b][hr, :] = jnp.broadcast_to(extra, (LANES, LANES))

        def values(n, b):
            oa = (_dot(pbuf[b][:, :KEYS_LOC], vaug_scr[_ds(n * LANES, KEYS_LOC), :])
                  + _dot(pbuf[b][:, KEYS_LOC:], cvaug_scr[...]))
            o = oa[:, :LANES] / (oa[:, LANES:] + ebuf[b][...])
            for g in range(4):
                merged = jnp.where(lane_lo_b, o[g * LANES:(g + 1) * LANES],
                                   o[(4 + g) * LANES:(5 + g) * LANES])
                mix_scr[_ds(n * LANES, LANES), M_A + g * LANES:M_A + (g + 1) * LANES] = (
                    merged.astype(BF16))

        _pipeline(N_BLK, scores, softmax, values)

    pl.run_scoped(
        window,
        pltpu.VMEM((N_BLK, WIN_ROWS, LANES), BF16),
        pltpu.VMEM((TOK + 2 * LANES, LANES), BF16),
        pltpu.VMEM((TOK + 2 * LANES, 2 * LANES), BF16),
        pltpu.VMEM((PAST_LEN, LANES), BF16),
        pltpu.VMEM((PAST_LEN, 2 * LANES), BF16),
        pltpu.VMEM((3, LANES, LANES), F32),
        pltpu.VMEM((WIN_ROWS, LANES), F32), pltpu.VMEM((WIN_ROWS, LANES), F32),
    )

    def retention(qb_scr, kb_scr, vb_scr, gb_scr, rt_scr, m2_scr, sball_scr, st_scr):
        _ret_project(proj, qb_scr, kb_scr, vb_scr, gb_scr)
        _retention_tables(decay_ref, rt_scr, m2_scr)
        st_scr[0] = s0_ref[...]
        _retention([0], DEC_SEQ // RET_CHUNK, qb_scr, kb_scr, vb_scr, gb_scr, gn_ref, rt_scr,
                   m2_scr, sball_scr, st_scr, mix_scr)

    pl.run_scoped(retention, *_ret_scratch(1, DEC_SEQ // RET_CHUNK))

    def latent(qc_scr, kcat_scr, kvaug_scr, mixc_scr):
        lane_lo = _lane_iota((TOK, LANES)) < 64
        mc, msa, msb = mc_ref[...], msa_ref[...], msb_ref[...]
        qc = proj(C_QC, C_CKV)
        for h in range(MLA_HEADS):
            qc_scr[h] = _rope(qc[:, h * LANES:(h + 1) * LANES], mc, msa, msb, 8).astype(BF16)
        ckr = proj(C_CKV, D_IN_P)
        ckv = ckr[:, :LANES]
        kr = _rope(ckr[:, LANES:], mc, msa, msb, 8)
        ckv_n = ckv * lax.rsqrt(jnp.mean(ckv * ckv, axis=-1, keepdims=True) + EPS) * kvn_ref[...]
        kv = _dot(ckv_n.astype(BF16), wkvb_ref[...])
        kv_c = _dot(cckv_ref[...].astype(BF16), wkvb_ref[...])
        kr_c = ckr_ref[...]
        lane_lo_c = _lane_iota((PAST_LEN, LANES)) < 64
        for h in range(MLA_HEADS):
            kvh, kvh_c = kv[:, h * LANES:(h + 1) * LANES], kv_c[:, h * LANES:(h + 1) * LANES]
            kcat_scr[h, 0:TOK, :] = jnp.where(lane_lo, kvh, kr).astype(BF16)
            kcat_scr[h, TOK:, :] = jnp.where(lane_lo_c, kvh_c, kr_c).astype(BF16)
            kvaug_scr[h, 0:TOK, :LANES] = kvh.astype(BF16)
            kvaug_scr[h, TOK:, :LANES] = kvh_c.astype(BF16)
            kvaug_scr[h, :, LANES:] = jnp.ones((MLA_KEYS, LANES), BF16)
        lane_lo_m = _lane_iota((MLA_QB, LANES)) < 64
        n_qb = DEC_SEQ // MLA_QB

        def split(t):
            if isinstance(t, int):
                return t // n_qb, t % n_qb
            return lax.shift_right_logical(t, 2), lax.bitwise_and(t, n_qb - 1)

        def scores(t, b):
            jp, qb = split(t)
            for i in range(2):
                h = 2 * jp + i
                q = qc_scr[h, _ds(qb * MLA_QB, MLA_QB), :]
                for part in range(2):
                    r0 = (2 * i + part) * MLA_QB
                    sbuf[b][r0:r0 + MLA_QB, :] = _dot_nt(
                        q, kcat_scr[h, part * MLA_HALF:(part + 1) * MLA_HALF, :])

        def softmax(t, b):
            for i in range(2):
                for rt in range(MLA_QB // LANES):
                    ra = 2 * i * MLA_QB + rt * LANES
                    rb = ra + MLA_QB
                    ps, _ = _softmax_tile([sbuf[b][ra:ra + LANES, :], sbuf[b][rb:rb + LANES, :]],
                                          scale=MLA_SCALE)
                    pbuf[b][ra:ra + LANES, :] = ps[0]
                    pbuf[b][rb:rb + LANES, :] = ps[1]

        def values(t, b):
            jp, qb = split(t)
            outs = []
            for i in range(2):
                h = 2 * jp + i
                r0 = 2 * i * MLA_QB
                oc = (_dot(pbuf[b][r0:r0 + MLA_QB, :], kvaug_scr[h, 0:MLA_HALF, :])
                      + _dot(pbuf[b][r0 + MLA_QB:r0 + 2 * MLA_QB, :], kvaug_scr[h, MLA_HALF:, :]))
                outs.append(oc[:, :LANES] / oc[:, LANES:])
            merged = jnp.where(lane_lo_m, pltpu.roll(outs[0], 64, 1), outs[1])
            mixc_scr[jp, _ds(qb * MLA_QB, MLA_QB), :] = merged.astype(BF16)

        _pipeline(2 * n_qb, scores, softmax, values)
        for jp in range(2):
            mix_scr[:, M_C + jp * LANES:M_C + (jp + 1) * LANES] = mixc_scr[jp]

    pl.run_scoped(
        latent,
        pltpu.VMEM((MLA_HEADS, TOK, LANES), BF16),
        pltpu.VMEM((MLA_HEADS, MLA_KEYS, LANES), BF16),
        pltpu.VMEM((MLA_HEADS, MLA_KEYS, 2 * LANES), BF16),
        pltpu.VMEM((2, TOK, LANES), BF16),
    )
    _epilogue(x_ref, mod_ref, wout_ref, mix_scr, xo_ref)


def _lat_mixer(layer, x, mod, n1, w_in_p, sink, decay, gn, kvn, w_kv_b, w_out_p,
               ck, cv, cckv, ckr, s0, rope_a, rope_m):
    n_tok = x.shape[0]
    n_seq = n_tok // DEC_SEQ
    seq_spec = lambda shape: pl.BlockSpec(
        (None, None) + shape, lambda i: (i, layer) + (0,) * len(shape))
    assert WIN_ROWS == 4 * MLA_QB and KEYS_WIN == MLA_HALF
    return pl.pallas_call(
        _lat_mixer_kernel,
        out_shape=jax.ShapeDtypeStruct((n_tok, D_MODEL), F32),
        grid=(n_seq,),
        in_specs=[
            pl.BlockSpec((TOK, D_MODEL), lambda i: (i, 0)),
            pl.BlockSpec((None, N_MOD, D_MODEL), lambda i: (i, 0, 0)),
            _const_spec((1, D_MODEL)),
            _const_spec((D_MODEL, D_IN_P), layer),
            _smem_spec(),
            _smem_spec(),
            _const_spec((1, RET_HEADS * HEAD_DIM)),
            _const_spec((1, MLA_KV_RANK)),
            _const_spec((MLA_KV_RANK, MLA_HEADS * LANES), layer),
            _const_spec((D_MODEL, D_MODEL), layer),
            seq_spec((PAST_LEN, LANES)),
            seq_spec((PAST_LEN, LANES)),
            seq_spec((PAST_LEN, MLA_KV_RANK)),
            seq_spec((PAST_LEN, LANES)),
            seq_spec((2, 2, LANES, LANES)),
        ] + [_const_spec((DEC_SEQ, LANES))] * 6,
        out_specs=pl.BlockSpec((TOK, D_MODEL), lambda i: (i, 0)),
        scratch_shapes=[
            pltpu.VMEM((TOK, D_MODEL), BF16),
            pltpu.VMEM((TOK, D_MODEL), BF16),
            pltpu.VMEM((WIN_ROWS, KEYS_WIN), F32), pltpu.VMEM((WIN_ROWS, KEYS_WIN), F32),
            pltpu.VMEM((WIN_ROWS, KEYS_WIN), BF16), pltpu.VMEM((WIN_ROWS, KEYS_WIN), BF16),
        ],
        compiler_params=pltpu.CompilerParams(
            dimension_semantics=("arbitrary",), vmem_limit_bytes=VMEM_LIMIT),
        name="lat_mixer",
    )(x, mod, n1, w_in_p, sink, decay, gn, kvn, w_kv_b, w_out_p, ck, cv, cckv, ckr, s0,
      *rope_a, *rope_m)


def _mlp_kernel(final, x_ref, mod_ref, n2_ref, wup_ref, wdn_ref, fn_ref, o_ref):
    x = x_ref[...]
    h2 = _norm_mod(x, n2_ref[...], mod_ref[4:5, :], mod_ref[3:4, :]).astype(BF16)
    acc = None
    for c in range(D_FF // FF_CHUNK):
        cols = slice(c * FF_CHUNK, (c + 1) * FF_CHUNK)
        u = jnp.maximum(_dot(h2, wup_ref[:, cols]), 0.0)
        part = _dot((u * u).astype(BF16), wdn_ref[cols, :])
        acc = part if acc is None else acc + part
    y = x + mod_ref[5:6, :] * acc
    if final:
        y = y * lax.rsqrt(jnp.mean(y * y, axis=-1, keepdims=True) + EPS) * fn_ref[...]
    o_ref[...] = y


def _mlp(layer, x, mod, n2, w_up, w_down, final_norm, final):
    n_tok = x.shape[0]
    per_mod = n_tok // mod.shape[0] // MLP_ROWS
    return pl.pallas_call(
        functools.partial(_mlp_kernel, final),
        out_shape=jax.ShapeDtypeStruct((n_tok, D_MODEL), F32),
        grid=(n_tok // MLP_ROWS,),
        in_specs=[
            pl.BlockSpec((MLP_ROWS, D_MODEL), lambda i: (i, 0)),
            pl.BlockSpec((None, N_MOD, D_MODEL), lambda i: (i // per_mod, 0, 0)),
            _const_spec((1, D_MODEL)),
            _const_spec((D_MODEL, D_FF), layer),
            _const_spec((D_FF, D_MODEL), layer),
            _const_spec((1, D_MODEL)),
        ],
        out_specs=pl.BlockSpec((MLP_ROWS, D_MODEL), lambda i: (i, 0)),
        compiler_params=pltpu.CompilerParams(
            dimension_semantics=("arbitrary",), vmem_limit_bytes=VMEM_LIMIT),
        name="mlp",
    )(x, mod, n2, w_up, w_down, final_norm)


def kernel(x_prompt, x_sample, cache_win_k, cache_win_v, cache_mla_ckv, cache_mla_krope, state_ret,
           c, c_ctx, w_mod, b_mod, norm1, norm2, w_in, win_sink, ret_decay, ret_gn, mla_kv_norm,
           w_kv_b, w_out, w_up, w_down, final_norm):
    n_ctx, n_lat = x_prompt.shape[0], x_sample.shape[0]

    w_in_p = _take_runs(w_in, _in_proj_columns(), 2).astype(BF16)
    w_out_p = _take_runs(w_out, _mix_rows(), 1).astype(BF16)
    w_kv_b16, w_up16, w_down16 = w_kv_b.astype(BF16), w_up.astype(BF16), w_down.astype(BF16)

    c_rows = jnp.zeros((16, D_MODEL), F32).at[0].set(c_ctx).at[1:1 + n_lat].set(c)
    mod = _modulation(c_rows, w_mod, b_mod).reshape(DEPTH, 16, N_MOD, D_MODEL)

    rope_a = _rope_tables(DEC_SEQ, HEAD_DIM, 0, HEAD_DIM)
    rope_m = _rope_tables(DEC_SEQ, MLA_ROPE, MLA_NOPE, LANES)

    ck = cache_win_k.reshape(n_lat, DEPTH, PAST_LEN, LANES)
    cv = cache_win_v.reshape(n_lat, DEPTH, PAST_LEN, LANES)
    ckr = jnp.pad(cache_mla_krope, ((0, 0), (0, 0), (0, 0), (MLA_NOPE, LANES - MLA_NOPE - MLA_ROPE)))
    sr = state_ret.reshape(n_lat, DEPTH, 2, 2, 2, HEAD_DIM, HEAD_DIM)
    zero = jnp.zeros_like(sr[:, :, :, :, 0])
    s0 = jnp.concatenate([jnp.concatenate([sr[:, :, :, :, 0], zero], axis=-1),
                          jnp.concatenate([zero, sr[:, :, :, :, 1]], axis=-1)], axis=-2)

    xp = x_prompt.reshape(n_ctx * SEQ, D_MODEL)
    xs = x_sample.reshape(n_lat * DEC_SEQ, D_MODEL)
    state = ()
    for l in range(DEPTH):
        last = l == DEPTH - 1
        shared = (norm1[l][None], w_in_p, win_sink[l], ret_decay[l], ret_gn[l][None],
                  mla_kv_norm[l][None], w_kv_b16, w_out_p)
        mod_ctx, mod_lat = mod[l, 0:1], mod[l, 1:1 + n_lat]
        xp, *state = _ctx_mixer(l, xp, mod_ctx[0], *shared, state)
        xp = _mlp(l, xp, mod_ctx, norm2[l][None], w_up16, w_down16, final_norm[None], last)
        xs = _lat_mixer(l, xs, mod_lat, *shared, ck, cv, cache_mla_ckv, ckr, s0, rope_a, rope_m)
        xs = _mlp(l, xs, mod_lat, norm2[l][None], w_up16, w_down16, final_norm[None], last)
    new_k, new_v, new_ckv, new_kr, new_s = state
    return (xp.reshape(n_ctx, SEQ, D_MODEL), xs.reshape(n_lat, DEC_SEQ, D_MODEL),
            new_k.reshape(n_ctx, DEPTH, SEQ, 2, HEAD_DIM), new_v.reshape(n_ctx, DEPTH, SEQ, 2, HEAD_DIM),
            new_ckv, new_kr, new_s)
```

```python
import functools

import numpy as np
import jax
import jax.numpy as jnp
from jax import lax
from jax.experimental import pallas as pl
from jax.experimental.pallas import tpu as pltpu

F32 = jnp.float32
BF16 = jnp.bfloat16

D_MODEL = 1024
DEPTH = 2
SEQ = 256
DEC_SEQ = 1024
PAST_LEN = 256
GRID_W = 64
HEAD_DIM = 64
ROPE_BASE = 10000.0
EPS = 1e-6
WIN_HEADS = 8
WINDOW = 128
ATTN_SCALE = HEAD_DIM ** -0.5
RET_HEADS = 4
RET_CHUNK = 128
RET_K_SCALE = HEAD_DIM ** -0.5
MLA_HEADS = 4
MLA_NOPE = 64
MLA_ROPE = 32
MLA_KV_RANK = 128
MLA_QK = MLA_NOPE + MLA_ROPE
MLA_SCALE = MLA_QK ** -0.5
D_IN = 2336
D_FF = 4 * D_MODEL
N_MOD = 6

LANES = 128
TOK = 1024
MLP_ROWS = 512
FF_CHUNK = 1024
VMEM_LIMIT = 60 * 1024 * 1024

C_QA, C_KA, C_VA, C_QB, C_KB, C_VB, C_GB, C_QC, C_CKV, C_KR, D_IN_P = (
    0, 512, 640, 768, 1024, 1280, 1536, 1792, 2304, 2432, 2560)
M_A, M_B, M_C = 0, 512, 768

NT_DIMS = (((1,), (1,)), ((), ()))


def _in_proj_columns():
    idx = []
    for g in range(4):
        idx += list(range(g * 64, (g + 1) * 64)) + list(range((4 + g) * 64, (5 + g) * 64))
    idx += list(range(512, 1792))
    for h in range(MLA_HEADS):
        idx += list(range(1792 + h * MLA_QK, 1792 + (h + 1) * MLA_QK)) + [-1] * 32
    idx += list(range(2176, 2304))
    idx += [-1] * 64 + list(range(2304, 2336)) + [-1] * 32
    return np.asarray(idx, np.int32)


def _mix_rows():
    idx = []
    for g in range(4):
        idx += list(range(g * 64, (g + 1) * 64)) + list(range((4 + g) * 64, (5 + g) * 64))
    idx += list(range(512, 1024))
    return np.asarray(idx, np.int32)


def _take_runs(w, idx, axis):
    pieces, i = [], 0
    while i < len(idx):
        j = i + 1
        if idx[i] < 0:
            while j < len(idx) and idx[j] < 0:
                j += 1
            shape = list(w.shape)
            shape[axis] = j - i
            pieces.append(jnp.zeros(shape, w.dtype))
        else:
            while j < len(idx) and idx[j] == idx[j - 1] + 1:
                j += 1
            pieces.append(lax.slice_in_dim(w, int(idx[i]), int(idx[j - 1]) + 1, axis=axis))
        i = j
    return jnp.concatenate(pieces, axis=axis)


def _rope_tables(n_tokens, dim, lane0, period):
    quarter = dim // 4
    t = np.arange(n_tokens)
    row = (t // GRID_W).astype(np.float64)
    col = (t % GRID_W).astype(np.float64)
    inv_freq = ROPE_BASE ** (-np.arange(quarter, dtype=np.float64) / quarter)
    ar, ac = row[:, None] * inv_freq, col[:, None] * inv_freq
    cos = np.concatenate([np.cos(ar), np.cos(ar), np.cos(ac), np.cos(ac)], axis=-1)
    sin = np.concatenate([np.sin(ar), np.sin(ar), np.sin(ac), np.sin(ac)], axis=-1)
    first = np.tile(np.concatenate([np.ones(quarter), np.zeros(quarter)]), 2)
    c = np.ones((n_tokens, LANES))
    sa = np.zeros((n_tokens, LANES))
    sb = np.zeros((n_tokens, LANES))
    for start in range(lane0, LANES, period):
        c[:, start:start + dim] = cos
        sa[:, start:start + dim] = -sin * first
        sb[:, start:start + dim] = sin * (1.0 - first)
    return tuple(jnp.asarray(a, F32) for a in (c, sa, sb))


def _lane_iota(shape):
    return lax.broadcasted_iota(jnp.int32, shape, len(shape) - 1)


def _row_iota(shape):
    return lax.broadcasted_iota(jnp.int32, shape, len(shape) - 2)


def _ds(start, size):
    if isinstance(start, int):
        return pl.ds(start, size)
    return pl.ds(pl.multiple_of(start, LANES), size)


def _norm_mod(x, gain, scale, shift):
    ms = jnp.mean(x * x, axis=-1, keepdims=True)
    return (x * lax.rsqrt(ms + EPS) * gain) * (1.0 + scale) + shift


def _log_sigmoid(x):
    return -(jnp.maximum(-x, 0.0) + jnp.log1p(jnp.exp(-jnp.abs(x))))


def _rope(x, c, sa, sb, quarter):
    return (x * c + pltpu.roll(x, LANES - quarter, 1) * sa + pltpu.roll(x, quarter, 1) * sb)


def _dot(a, b):
    return jnp.dot(a, b, preferred_element_type=F32)


def _dot_nt(a, b):
    return lax.dot_general(a, b, NT_DIMS, preferred_element_type=F32)


def _mod_kernel(c_ref, w_ref, b_ref, o_ref):
    cv = c_ref[...]
    s = cv * jax.nn.sigmoid(cv)
    o_ref[0] = _dot(s.astype(BF16), w_ref[0].astype(BF16)) + b_ref[0]


def _modulation(c_rows, w_mod, b_mod):
    tn = 1536
    nj = (N_MOD * D_MODEL) // tn
    return pl.pallas_call(
        _mod_kernel,
        out_shape=jax.ShapeDtypeStruct((DEPTH, 16, N_MOD * D_MODEL), F32),
        grid=(DEPTH, nj),
        in_specs=[
            pl.BlockSpec((16, D_MODEL), lambda l, j: (0, 0)),
            pl.BlockSpec((1, D_MODEL, tn), lambda l, j: (l, 0, j)),
            pl.BlockSpec((1, 1, tn), lambda l, j: (l, 0, j)),
        ],
        out_specs=pl.BlockSpec((1, 16, tn), lambda l, j: (l, 0, j)),
        compiler_params=pltpu.CompilerParams(
            dimension_semantics=("arbitrary", "arbitrary"), vmem_limit_bytes=VMEM_LIMIT),
        name="modulation",
    )(c_rows, w_mod, b_mod.reshape(DEPTH, 1, N_MOD * D_MODEL))


def _pipeline(n_steps, scores, softmax, values):
    scores(0, 0)
    scores(1, 1)
    softmax(0, 0)

    def body(i, carry):
        t = 2 * i + 1
        scores(t + 1, 0)
        softmax(t, 1)
        values(t - 1, 0)
        scores(t + 2, 1)
        softmax(t + 1, 0)
        values(t, 1)
        return carry

    lax.fori_loop(0, n_steps // 2 - 1, body, 0)
    softmax(n_steps - 1, 1)
    values(n_steps - 2, 0)
    values(n_steps - 1, 1)


def _softmax_tile(parts, floor=None, scale=None):
    m = None
    for s in parts:
        pm = jnp.max(s, axis=-1, keepdims=True)
        m = pm if m is None else jnp.maximum(m, pm)
    if floor is not None:
        m = jnp.maximum(m, floor)
    if scale is None:
        ps = [jnp.exp(s - m).astype(BF16) for s in parts]
    else:
        ps = [jnp.exp((s - m) * scale).astype(BF16) for s in parts]
    extra = None if floor is None else jnp.exp(floor - m)
    return ps, extra


def _retention_tables(decay_ref, rt_scr, m2_scr):
    shape = (RET_CHUNK, LANES)
    lane_lo = _lane_iota(shape) < 64
    row = _row_iota(shape)
    row_lo = row < 64
    i = row.astype(F32)
    rel = i - _lane_iota(shape).astype(F32)
    for j in range(2):
        df0, df1 = decay_ref[0, 2 * j], decay_ref[0, 2 * j + 1]
        db0, db1 = decay_ref[1, 2 * j], decay_ref[1, 2 * j + 1]
        lgf = _log_sigmoid(jnp.where(lane_lo, df0, df1))
        lgb = _log_sigmoid(jnp.where(lane_lo, db0, db1))
        rt_scr[j, 0] = jnp.exp((i + 1.0) * lgf)
        rt_scr[j, 1] = jnp.exp((RET_CHUNK - i) * lgb)
        rt_scr[j, 2] = jnp.exp((RET_CHUNK - 1.0 - i) * lgf)
        rt_scr[j, 3] = jnp.exp(i * lgb)
        rt_scr[j, 4] = jnp.exp(RET_CHUNK * _log_sigmoid(jnp.where(row_lo, df0, df1)))
        rt_scr[j, 5] = jnp.exp(RET_CHUNK * _log_sigmoid(jnp.where(row_lo, db0, db1)))
        for hh, (df, db) in enumerate(((df0, db0), (df1, db1))):
            lf = _log_sigmoid(jnp.full(shape, df, F32))
            lb = _log_sigmoid(jnp.full(shape, db, F32))
            low, upp = rel >= 0.0, rel <= 0.0
            m = (jnp.where(low, jnp.exp(jnp.where(low, rel, 0.0) * lf), 0.0)
                 + jnp.where(upp, jnp.exp(jnp.where(upp, -rel, 0.0) * lb), 0.0))
            m2_scr[j, hh * RET_CHUNK:(hh + 1) * RET_CHUNK, :] = m


def _retention_states(seq_bases, nc, kb_scr, vb_scr, rt_scr, upd_scr, sall_scr, st_scr):
    blockdiag2 = _lane_iota((2 * RET_CHUNK, LANES)) < 64
    blockdiag2 = blockdiag2 == ((_row_iota((2 * RET_CHUNK, LANES)) & (RET_CHUNK - 1)) < 64)
    chains = [(si, base, j) for si, base in enumerate(seq_bases) for j in range(2)]

    def upd_body(c, carry):
        for si, base, j in chains:
            rows, cols = _ds(base + c * RET_CHUNK, RET_CHUNK), slice(j * LANES, (j + 1) * LANES)
            k2 = kb_scr[rows, cols]
            kd = jnp.concatenate([k2 * rt_scr[j, 2], k2 * rt_scr[j, 3]], axis=1)
            upd = _dot(kd.T.astype(BF16), vb_scr[rows, cols])
            upd_scr[c, si, j] = jnp.where(blockdiag2, upd, 0.0)
        return carry

    lax.fori_loop(0, nc, upd_body, 0, unroll=max(1, 8 // len(chains)))

    def scan_body(t, carry):
        cf, cb = t, nc - 1 - t
        for si, _, j in chains:
            sf, sb = st_scr[si, 0, j], st_scr[si, 1, j]
            sall_scr[cf, si, j, :RET_CHUNK, :] = sf.astype(BF16)
            sall_scr[cb, si, j, RET_CHUNK:, :] = sb.astype(BF16)
            st_scr[si, 0, j] = rt_scr[j, 4] * sf + upd_scr[cf, si, j, :RET_CHUNK, :]
            st_scr[si, 1, j] = rt_scr[j, 5] * sb + upd_scr[cb, si, j, RET_CHUNK:, :]
        return carry

    lax.fori_loop(0, nc, scan_body, 0)


def _retention_chunk(si, base, c, qb_scr, kb_scr, vb_scr, gb_scr, gn_ref, rt_scr, m2_scr, sall_scr,
                     mix_scr):
    shape = (RET_CHUNK, LANES)
    lane_lo = _lane_iota(shape) < 64
    rows = _ds(base + c * RET_CHUNK, RET_CHUNK)
    for j in range(2):
        cols = slice(j * LANES, (j + 1) * LANES)
        q2 = qb_scr[rows, cols]
        qs = jnp.concatenate([jnp.where(lane_lo, q2, 0.0), jnp.where(lane_lo, 0.0, q2)],
                             axis=0).astype(BF16)
        sc = _dot_nt(qs, kb_scr[rows, cols].astype(BF16)) * m2_scr[j]
        o2 = _dot(sc.astype(BF16), vb_scr[rows, cols])
        o = jnp.where(lane_lo, o2[:RET_CHUNK], o2[RET_CHUNK:])
        qd = jnp.concatenate([q2 * rt_scr[j, 0], q2 * rt_scr[j, 1]], axis=1).astype(BF16)
        o = o + _dot(qd, sall_scr[c, si, j])
        s_lo = jnp.sum(jnp.where(lane_lo, o, 0.0), axis=-1, keepdims=True)
        s_hi = jnp.sum(jnp.where(lane_lo, 0.0, o), axis=-1, keepdims=True)
        d = o - jnp.where(lane_lo, s_lo, s_hi) * (1.0 / HEAD_DIM)
        dd = d * d
        v_lo = jnp.sum(jnp.where(lane_lo, dd, 0.0), axis=-1, keepdims=True)
        v_hi = jnp.sum(jnp.where(lane_lo, 0.0, dd), axis=-1, keepdims=True)
        var = jnp.where(lane_lo, v_lo, v_hi) * (1.0 / HEAD_DIM)
        g2 = gb_scr[rows, cols]
        y = d * lax.rsqrt(var + EPS) * gn_ref[:, cols] * (g2 * jax.nn.sigmoid(g2))
        mix_scr[rows, M_B + j * LANES:M_B + (j + 1) * LANES] = y.astype(BF16)


def _prologue(x_ref, mod_ref, n1_ref, h_scr):
    def body(r, carry):
        rows = _ds(r * 256, 256)
        h = _norm_mod(x_ref[rows, :], n1_ref[...], mod_ref[1:2, :], mod_ref[0:1, :])
        h_scr[rows, :] = h.astype(BF16)
        return carry
    lax.fori_loop(0, TOK // 256, body, 0)


def _epilogue(x_ref, mod_ref, wout_ref, mix_scr, xo_ref):
    def body(r, carry):
        rows = _ds(r * 256, 256)
        y = _dot(mix_scr[rows, :], wout_ref[...])
        xo_ref[rows, :] = x_ref[rows, :] + mod_ref[2:3, :] * y
        return carry
    lax.fori_loop(0, TOK // 256, body, 0)


def _ret_scratch(n_seq, nc):
    return [
        pltpu.VMEM((TOK, 256), F32),
        pltpu.VMEM((TOK, 256), F32),
        pltpu.VMEM((TOK, 256), BF16),
        pltpu.VMEM((TOK, 256), F32),
        pltpu.VMEM((2, 6, RET_CHUNK, LANES), F32),
        pltpu.VMEM((2, 2 * RET_CHUNK, LANES), F32),
        pltpu.VMEM((nc, n_seq, 2, 2 * LANES, LANES), F32),
        pltpu.VMEM((nc, n_seq, 2, 2 * LANES, LANES), BF16),
        pltpu.VMEM((n_seq, 2, 2, LANES, LANES), F32),
    ]


def _ret_project(proj, qb_scr, kb_scr, vb_scr, gb_scr):
    qb_scr[...] = proj(C_QB, C_KB)
    kb_scr[...] = proj(C_KB, C_VB) * RET_K_SCALE
    vb_scr[...] = proj(C_VB, C_GB).astype(BF16)
    gb_scr[...] = proj(C_GB, C_QC)


CTX_PER_STEP = TOK // SEQ
CTX_WROWS = WIN_HEADS * SEQ
CTX_MROWS = MLA_HEADS * SEQ


def _ctx_mixer_kernel(n_alias, x_ref, mod_ref, n1_ref, win_ref, sink_ref, decay_ref, gn_ref, kvn_ref,
                      wkvb_ref, wout_ref, *refs):
    xo_ref, ko_ref, vo_ref, ckvo_ref, kro_ref, so_ref, h_scr, mix_scr = refs[n_alias:]
    _prologue(x_ref, mod_ref, n1_ref, h_scr)
    per_seq = lambda a: a.reshape(CTX_PER_STEP, SEQ, a.shape[-1])

    def proj(c0, c1):
        return _dot(h_scr[...], win_ref[:, c0:c1])

    ret = []

    def attention(qst_scr, k_scr, vaug_scr, qc_scr, kcat_scr, kvaug_scr,
                  sw0, sw1, pw0, pw1, ew0, ew1, sm0, sm1, pm0, pm1):
        lane_lo = _lane_iota((TOK, LANES)) < 64
        ones = jnp.ones((TOK, LANES), BF16)
        qa = proj(C_QA, C_KA) * ATTN_SCALE
        for g in range(4):
            q = qa[:, g * LANES:(g + 1) * LANES]
            lo = jnp.where(lane_lo, q, 0.0).astype(BF16).reshape(CTX_PER_STEP, SEQ, LANES)
            hi = jnp.where(lane_lo, 0.0, q).astype(BF16).reshape(CTX_PER_STEP, SEQ, LANES)
            qst_scr[:, g * SEQ:(g + 1) * SEQ, :] = lo
            qst_scr[:, (4 + g) * SEQ:(5 + g) * SEQ, :] = hi
        kva = proj(C_KA, C_QB)
        ko_ref[...] = per_seq(kva[:, :LANES])
        vo_ref[...] = per_seq(kva[:, LANES:])
        k_scr[...] = kva[:, :LANES].astype(BF16)
        vaug_scr[:, :LANES] = kva[:, LANES:].astype(BF16)
        vaug_scr[:, LANES:] = ones
        qc = proj(C_QC, C_CKV).astype(BF16)
        for h in range(MLA_HEADS):
            qc_scr[h] = qc[:, h * LANES:(h + 1) * LANES]
        ckr = proj(C_CKV, D_IN_P)
        ckv, kr = ckr[:, :LANES], ckr[:, LANES:]
        ckv_n = ckv * lax.rsqrt(jnp.mean(ckv * ckv, axis=-1, keepdims=True) + EPS) * kvn_ref[...]
        ckvo_ref[...] = per_seq(ckv_n)
        kro_ref[...] = per_seq(kr[:, 64:64 + MLA_ROPE])
        kv = _dot(ckv_n.astype(BF16), wkvb_ref[...])
        for h in range(MLA_HEADS):
            kvh = kv[:, h * LANES:(h + 1) * LANES]
            kcat_scr[h] = jnp.where(lane_lo, kvh, kr).astype(BF16)
            kvaug_scr[h, :, :LANES] = kvh.astype(BF16)
            kvaug_scr[h, :, LANES:] = ones

        sw, pw, ew, sm, pm = (sw0, sw1), (pw0, pw1), (ew0, ew1), (sm0, sm1), (pm0, pm1)
        lane_lo_s = _lane_iota((SEQ, LANES)) < 64

        def scores(e, b):
            rows = _ds(e * SEQ, SEQ)
            sw[b][...] = _dot_nt(qst_scr[e], k_scr[rows, :])
            for h in range(MLA_HEADS):
                sm[b][h * SEQ:(h + 1) * SEQ, :] = _dot_nt(qc_scr[h, rows, :], kcat_scr[h, rows, :])

        def softmax(e, b):
            for h in range(WIN_HEADS):
                hr = slice(h * SEQ, (h + 1) * SEQ)
                sink = jnp.full((SEQ, 1), sink_ref[h], F32)
                (p,), extra = _softmax_tile([sw[b][hr, :]], floor=sink)
                pw[b][hr, :] = p
                ew[b][hr, :] = jnp.broadcast_to(extra, (SEQ, LANES))
            for h in range(MLA_HEADS):
                hr = slice(h * SEQ, (h + 1) * SEQ)
                (p,), _ = _softmax_tile([sm[b][hr, :]], scale=MLA_SCALE)
                pm[b][hr, :] = p
            for c in range(SEQ // RET_CHUNK):
                _retention_chunk(e, e * SEQ, c, *ret[0])

        def values(e, b):
            rows = _ds(e * SEQ, SEQ)
            oa = _dot(pw[b][...], vaug_scr[rows, :])
            o = oa[:, :LANES] / (oa[:, LANES:] + ew[b][...])
            for g in range(4):
                merged = jnp.where(lane_lo_s, o[g * SEQ:(g + 1) * SEQ], o[(4 + g) * SEQ:(5 + g) * SEQ])
                mix_scr[rows, M_A + g * LANES:M_A + (g + 1) * LANES] = merged.astype(BF16)
            for jp in range(2):
                outs = []
                for h in (2 * jp, 2 * jp + 1):
                    oc = _dot(pm[b][h * SEQ:(h + 1) * SEQ, :], kvaug_scr[h, rows, :])
                    outs.append(oc[:, :LANES] / oc[:, LANES:])
                merged = jnp.where(lane_lo_s, pltpu.roll(outs[0], 64, 1), outs[1])
                mix_scr[rows, M_C + jp * LANES:M_C + (jp + 1) * LANES] = merged.astype(BF16)

        _pipeline(CTX_PER_STEP, scores, softmax, values)

    attention_scratch = (
        pltpu.VMEM((CTX_PER_STEP, CTX_WROWS, LANES), BF16),
        pltpu.VMEM((TOK, LANES), BF16),
        pltpu.VMEM((TOK, 2 * LANES), BF16),
        pltpu.VMEM((MLA_HEADS, TOK, LANES), BF16),
        pltpu.VMEM((MLA_HEADS, TOK, LANES), BF16),
        pltpu.VMEM((MLA_HEADS, TOK, 2 * LANES), BF16),
        pltpu.VMEM((CTX_WROWS, SEQ), F32), pltpu.VMEM((CTX_WROWS, SEQ), F32),
        pltpu.VMEM((CTX_WROWS, SEQ), BF16), pltpu.VMEM((CTX_WROWS, SEQ), BF16),
        pltpu.VMEM((CTX_WROWS, LANES), F32), pltpu.VMEM((CTX_WROWS, LANES), F32),
        pltpu.VMEM((CTX_MROWS, SEQ), F32), pltpu.VMEM((CTX_MROWS, SEQ), F32),
        pltpu.VMEM((CTX_MROWS, SEQ), BF16), pltpu.VMEM((CTX_MROWS, SEQ), BF16),
    )

    def retention(qb_scr, kb_scr, vb_scr, gb_scr, rt_scr, m2_scr, upd_scr, sall_scr, st_scr):
        _ret_project(proj, qb_scr, kb_scr, vb_scr, gb_scr)
        _retention_tables(decay_ref, rt_scr, m2_scr)
        st_scr[...] = jnp.zeros(st_scr.shape, F32)
        _retention_states([e * SEQ for e in range(CTX_PER_STEP)], SEQ // RET_CHUNK, kb_scr, vb_scr,
                          rt_scr, upd_scr, sall_scr, st_scr)
        for e in range(CTX_PER_STEP):
            for dirn in range(2):
                for j in range(2):
                    st = st_scr[e, dirn, j]
                    so_ref[e, dirn, 2 * j] = st[:64, :64]
                    so_ref[e, dirn, 2 * j + 1] = st[64:, 64:]
        ret.append((qb_scr, kb_scr, vb_scr, gb_scr, gn_ref, rt_scr, m2_scr, sall_scr, mix_scr))
        pl.run_scoped(attention, *attention_scratch)

    pl.run_scoped(retention, *_ret_scratch(CTX_PER_STEP, SEQ // RET_CHUNK))
    _epilogue(x_ref, mod_ref, wout_ref, mix_scr, xo_ref)


def _const_spec(shape, layer=None):
    if layer is None:
        return pl.BlockSpec(shape, lambda i: (0,) * len(shape), pipeline_mode=pl.Buffered(1))
    return pl.BlockSpec((None,) + shape, lambda i: (layer,) + (0,) * len(shape),
                        pipeline_mode=pl.Buffered(1))


def _smem_spec():
    return pl.BlockSpec(memory_space=pltpu.SMEM)


def _ctx_mixer(layer, x, mod, n1, w_in_p, sink, decay, gn, kvn, w_kv_b, w_out_p, prev_state):
    n_tok = x.shape[0]
    n_seq = n_tok // SEQ
    row_spec = lambda w: pl.BlockSpec((TOK, w), lambda i: (i, 0))
    state_spec = lambda *tail: pl.BlockSpec(
        (CTX_PER_STEP, None) + tail, lambda i: (i, layer) + (0,) * len(tail))
    state_tails = [(SEQ, LANES), (SEQ, LANES), (SEQ, MLA_KV_RANK), (SEQ, MLA_ROPE),
                   (2, RET_HEADS, HEAD_DIM, HEAD_DIM)]
    n_in = 10
    return pl.pallas_call(
        functools.partial(_ctx_mixer_kernel, len(prev_state)),
        out_shape=[jax.ShapeDtypeStruct((n_tok, D_MODEL), F32)] + [
            jax.ShapeDtypeStruct((n_seq, DEPTH) + tail, F32) for tail in state_tails],
        grid=(n_tok // TOK,),
        in_specs=[
            row_spec(D_MODEL),
            _const_spec((N_MOD, D_MODEL)),
            _const_spec((1, D_MODEL)),
            _const_spec((D_MODEL, D_IN_P), layer),
            _smem_spec(),
            _smem_spec(),
            _const_spec((1, RET_HEADS * HEAD_DIM)),
            _const_spec((1, MLA_KV_RANK)),
            _const_spec((MLA_KV_RANK, MLA_HEADS * LANES), layer),
            _const_spec((D_MODEL, D_MODEL), layer),
        ] + [pl.BlockSpec(memory_space=pl.ANY)] * len(prev_state),
        out_specs=[row_spec(D_MODEL)] + [state_spec(*tail) for tail in state_tails],
        input_output_aliases={n_in + k: 1 + k for k in range(len(prev_state))},
        scratch_shapes=[
            pltpu.VMEM((TOK, D_MODEL), BF16),
            pltpu.VMEM((TOK, D_MODEL), BF16),
        ],
        compiler_params=pltpu.CompilerParams(
            dimension_semantics=("arbitrary",), vmem_limit_bytes=VMEM_LIMIT),
        name="ctx_mixer",
    )(x, mod, n1, w_in_p, sink, decay, gn, kvn, w_kv_b, w_out_p, *prev_state)


N_BLK = DEC_SEQ // LANES
KEYS_LOC = 3 * LANES
KEYS_WIN = KEYS_LOC + PAST_LEN
WIN_ROWS = WIN_HEADS * LANES
MLA_QB = 256
MLA_KEYS = DEC_SEQ + PAST_LEN
MLA_HALF = MLA_KEYS // 2


def _lat_mixer_kernel(x_ref, mod_ref, n1_ref, win_ref, sink_ref, decay_ref, gn_ref, kvn_ref,
                      wkvb_ref, wout_ref, ck_ref, cv_ref, cckv_ref, ckr_ref, s0_ref,
                      rc_ref, rsa_ref, rsb_ref, mc_ref, msa_ref, msb_ref,
                      xo_ref,
                      h_scr, mix_scr, s0_scr, s1_scr, p0_scr, p1_scr):
    _prologue(x_ref, mod_ref, n1_ref, h_scr)
    sbuf, pbuf = (s0_scr, s1_scr), (p0_scr, p1_scr)

    def proj(c0, c1):
        return _dot(h_scr[...], win_ref[:, c0:c1])

    ret = []

    def window(qst_scr, kpad_scr, vaug_scr, ckb_scr, cvaug_scr, bias_scr, e0_scr, e1_scr):
        ebuf = (e0_scr, e1_scr)
        lane_lo = _lane_iota((TOK, LANES)) < 64
        rc, rsa, rsb = rc_ref[...], rsa_ref[...], rsb_ref[...]
        qa = proj(C_QA, C_KA)
        for g in range(4):
            q = _rope(qa[:, g * LANES:(g + 1) * LANES], rc, rsa, rsb, 16) * ATTN_SCALE
            lo = jnp.where(lane_lo, q, 0.0).astype(BF16).reshape(N_BLK, LANES, LANES)
            hi = jnp.where(lane_lo, 0.0, q).astype(BF16).reshape(N_BLK, LANES, LANES)
            qst_scr[:, g * LANES:(g + 1) * LANES, :] = lo
            qst_scr[:, (4 + g) * LANES:(5 + g) * LANES, :] = hi
        kva = proj(C_KA, C_QB)
        zpad = jnp.zeros((LANES, LANES), BF16)
        kpad_scr[0:LANES, :] = zpad
        kpad_scr[LANES + TOK:, :] = zpad
        vaug_scr[0:LANES, :LANES] = zpad
        vaug_scr[LANES + TOK:, :LANES] = zpad
        kpad_scr[LANES:LANES + TOK, :] = _rope(kva[:, :LANES], rc, rsa, rsb, 16).astype(BF16)
        vaug_scr[LANES:LANES + TOK, :LANES] = kva[:, LANES:].astype(BF16)
        vaug_scr[:, LANES:] = jnp.ones((TOK + 2 * LANES, LANES), BF16)
        ckb_scr[...] = ck_ref[...].astype(BF16)
        cvaug_scr[:, :LANES] = cv_ref[...].astype(BF16)
        cvaug_scr[:, LANES:] = jnp.ones((PAST_LEN, LANES), BF16)
        qi, kj = _row_iota((LANES, LANES)), _lane_iota((LANES, LANES))
        bias_scr[0] = jnp.full((LANES, LANES), -jnp.inf, F32)
        bias_scr[1] = jnp.where(kj >= qi, 0.0, -jnp.inf)
        bias_scr[2] = jnp.where(kj <= qi, 0.0, -jnp.inf)
        lane_lo_b = _lane_iota((LANES, LANES)) < 64

        def scores(n, b):
            q = qst_scr[n]
            sbuf[b][:, :KEYS_LOC] = _dot_nt(q, kpad_scr[_ds(n * LANES, KEYS_LOC), :])
            sbuf[b][:, KEYS_LOC:] = _dot_nt(q, ckb_scr[...])

        def softmax(n, b):
            if isinstance(n, int):
                i_prev, i_next = (1 if n > 0 else 0), (2 if n < N_BLK - 1 else 0)
            else:
                i_prev, i_next = jnp.where(n > 0, 1, 0), jnp.where(n < N_BLK - 1, 2, 0)
            b_prev, b_next = bias_scr[i_prev], bias_scr[i_next]
            for h in range(WIN_HEADS):
                hr = slice(h * LANES, (h + 1) * LANES)
                parts = [sbuf[b][hr, 0:LANES] + b_prev, sbuf[b][hr, LANES:2 * LANES],
                         sbuf[b][hr, 2 * LANES:KEYS_LOC] + b_next, sbuf[b][hr, KEYS_LOC:]]
                ps, extra = _softmax_tile(parts, floor=jnp.full((LANES, 1), sink_ref[h], F32))
                pbuf[b][hr, 0:LANES] = ps[0]
                pbuf[b][hr, LANES:2 * LANES] = ps[1]
                pbuf[b][hr, 2 * LANES:KEYS_LOC] = ps[2]
                pbuf[b][hr, KEYS_LOC:] = ps[3]
                ebuf[b][hr, :] = jnp.broadcast_to(extra, (LANES, LANES))
            _retention_chunk(0, 0, n, *ret[0])

        def values(n, b):
            oa = (_dot(pbuf[b][:, :KEYS_LOC], vaug_scr[_ds(n * LANES, KEYS_LOC), :])
                  + _dot(pbuf[b][:, KEYS_LOC:], cvaug_scr[...]))
            o = oa[:, :LANES] / (oa[:, LANES:] + ebuf[b][...])
            for g in range(4):
                merged = jnp.where(lane_lo_b, o[g * LANES:(g + 1) * LANES],
                                   o[(4 + g) * LANES:(5 + g) * LANES])
                mix_scr[_ds(n * LANES, LANES), M_A + g * LANES:M_A + (g + 1) * LANES] = (
                    merged.astype(BF16))

        _pipeline(N_BLK, scores, softmax, values)

    window_scratch = (
        pltpu.VMEM((N_BLK, WIN_ROWS, LANES), BF16),
        pltpu.VMEM((TOK + 2 * LANES, LANES), BF16),
        pltpu.VMEM((TOK + 2 * LANES, 2 * LANES), BF16),
        pltpu.VMEM((PAST_LEN, LANES), BF16),
        pltpu.VMEM((PAST_LEN, 2 * LANES), BF16),
        pltpu.VMEM((3, LANES, LANES), F32),
        pltpu.VMEM((WIN_ROWS, LANES), F32), pltpu.VMEM((WIN_ROWS, LANES), F32),
    )

    def retention(qb_scr, kb_scr, vb_scr, gb_scr, rt_scr, m2_scr, upd_scr, sall_scr, st_scr):
        _ret_project(proj, qb_scr, kb_scr, vb_scr, gb_scr)
        _retention_tables(decay_ref, rt_scr, m2_scr)
        st_scr[0] = s0_ref[...]
        _retention_states([0], DEC_SEQ // RET_CHUNK, kb_scr, vb_scr, rt_scr, upd_scr, sall_scr, st_scr)
        ret.append((qb_scr, kb_scr, vb_scr, gb_scr, gn_ref, rt_scr, m2_scr, sall_scr, mix_scr))
        pl.run_scoped(window, *window_scratch)

    assert DEC_SEQ // RET_CHUNK == N_BLK
    pl.run_scoped(retention, *_ret_scratch(1, DEC_SEQ // RET_CHUNK))

    def latent(qc_scr, kcat_scr, kvaug_scr, mixc_scr):
        lane_lo = _lane_iota((TOK, LANES)) < 64
        mc, msa, msb = mc_ref[...], msa_ref[...], msb_ref[...]
        qc = proj(C_QC, C_CKV)
        for h in range(MLA_HEADS):
            qc_scr[h] = _rope(qc[:, h * LANES:(h + 1) * LANES], mc, msa, msb, 8).astype(BF16)
        ckr = proj(C_CKV, D_IN_P)
        ckv = ckr[:, :LANES]
        kr = _rope(ckr[:, LANES:], mc, msa, msb, 8)
        ckv_n = ckv * lax.rsqrt(jnp.mean(ckv * ckv, axis=-1, keepdims=True) + EPS) * kvn_ref[...]
        kv = _dot(ckv_n.astype(BF16), wkvb_ref[...])
        kv_c = _dot(cckv_ref[...].astype(BF16), wkvb_ref[...])
        kr_c = ckr_ref[...]
        lane_lo_c = _lane_iota((PAST_LEN, LANES)) < 64
        for h in range(MLA_HEADS):
            kvh, kvh_c = kv[:, h * LANES:(h + 1) * LANES], kv_c[:, h * LANES:(h + 1) * LANES]
            kcat_scr[h, 0:TOK, :] = jnp.where(lane_lo, kvh, kr).astype(BF16)
            kcat_scr[h, TOK:, :] = jnp.where(lane_lo_c, kvh_c, kr_c).astype(BF16)
            kvaug_scr[h, 0:TOK, :LANES] = kvh.astype(BF16)
            kvaug_scr[h, TOK:, :LANES] = kvh_c.astype(BF16)
            kvaug_scr[h, :, LANES:] = jnp.ones((MLA_KEYS, LANES), BF16)
        lane_lo_m = _lane_iota((MLA_QB, LANES)) < 64
        n_qb = DEC_SEQ // MLA_QB

        def split(t):
            if isinstance(t, int):
                return t // n_qb, t % n_qb
            return lax.shift_right_logical(t, 2), lax.bitwise_and(t, n_qb - 1)

        def scores(t, b):
            jp, qb = split(t)
            for i in range(2):
                h = 2 * jp + i
                q = qc_scr[h, _ds(qb * MLA_QB, MLA_QB), :]
                for part in range(2):
                    r0 = (2 * i + part) * MLA_QB
                    sbuf[b][r0:r0 + MLA_QB, :] = _dot_nt(
                        q, kcat_scr[h, part * MLA_HALF:(part + 1) * MLA_HALF, :])

        def softmax(t, b):
            for i in range(2):
                for rt in range(MLA_QB // LANES):
                    ra = 2 * i * MLA_QB + rt * LANES
                    rb = ra + MLA_QB
                    ps, _ = _softmax_tile([sbuf[b][ra:ra + LANES, :], sbuf[b][rb:rb + LANES, :]],
                                          scale=MLA_SCALE)
                    pbuf[b][ra:ra + LANES, :] = ps[0]
                    pbuf[b][rb:rb + LANES, :] = ps[1]

        def values(t, b):
            jp, qb = split(t)
            outs = []
            for i in range(2):
                h = 2 * jp + i
                r0 = 2 * i * MLA_QB
                oc = (_dot(pbuf[b][r0:r0 + MLA_QB, :], kvaug_scr[h, 0:MLA_HALF, :])
                      + _dot(pbuf[b][r0 + MLA_QB:r0 + 2 * MLA_QB, :], kvaug_scr[h, MLA_HALF:, :]))
                outs.append(oc[:, :LANES] / oc[:, LANES:])
            merged = jnp.where(lane_lo_m, pltpu.roll(outs[0], 64, 1), outs[1])
            mixc_scr[jp, _ds(qb * MLA_QB, MLA_QB), :] = merged.astype(BF16)

        _pipeline(2 * n_qb, scores, softmax, values)
        for jp in range(2):
            mix_scr[:, M_C + jp * LANES:M_C + (jp + 1) * LANES] = mixc_scr[jp]

    pl.run_scoped(
        latent,
        pltpu.VMEM((MLA_HEADS, TOK, LANES), BF16),
        pltpu.VMEM((MLA_HEADS, MLA_KEYS, LANES), BF16),
        pltpu.VMEM((MLA_HEADS, MLA_KEYS, 2 * LANES), BF16),
        pltpu.VMEM((2, TOK, LANES), BF16),
    )
    _epilogue(x_ref, mod_ref, wout_ref, mix_scr, xo_ref)


def _lat_mixer(layer, x, mod, n1, w_in_p, sink, decay, gn, kvn, w_kv_b, w_out_p,
               ck, cv, cckv, ckr, s0, rope_a, rope_m):
    n_tok = x.shape[0]
    n_seq = n_tok // DEC_SEQ
    seq_spec = lambda shape: pl.BlockSpec(
        (None, None) + shape, lambda i: (i, layer) + (0,) * len(shape))
    assert WIN_ROWS == 4 * MLA_QB and KEYS_WIN == MLA_HALF
    return pl.pallas_call(
        _lat_mixer_kernel,
        out_shape=jax.ShapeDtypeStruct((n_tok, D_MODEL), F32),
        grid=(n_seq,),
        in_specs=[
            pl.BlockSpec((TOK, D_MODEL), lambda i: (i, 0)),
            pl.BlockSpec((None, N_MOD, D_MODEL), lambda i: (i, 0, 0)),
            _const_spec((1, D_MODEL)),
            _const_spec((D_MODEL, D_IN_P), layer),
            _smem_spec(),
            _smem_spec(),
            _const_spec((1, RET_HEADS * HEAD_DIM)),
            _const_spec((1, MLA_KV_RANK)),
            _const_spec((MLA_KV_RANK, MLA_HEADS * LANES), layer),
            _const_spec((D_MODEL, D_MODEL), layer),
            seq_spec((PAST_LEN, LANES)),
            seq_spec((PAST_LEN, LANES)),
            seq_spec((PAST_LEN, MLA_KV_RANK)),
            seq_spec((PAST_LEN, LANES)),
            seq_spec((2, 2, LANES, LANES)),
        ] + [_const_spec((DEC_SEQ, LANES))] * 6,
        out_specs=pl.BlockSpec((TOK, D_MODEL), lambda i: (i, 0)),
        scratch_shapes=[
            pltpu.VMEM((TOK, D_MODEL), BF16),
            pltpu.VMEM((TOK, D_MODEL), BF16),
            pltpu.VMEM((WIN_ROWS, KEYS_WIN), F32), pltpu.VMEM((WIN_ROWS, KEYS_WIN), F32),
            pltpu.VMEM((WIN_ROWS, KEYS_WIN), BF16), pltpu.VMEM((WIN_ROWS, KEYS_WIN), BF16),
        ],
        compiler_params=pltpu.CompilerParams(
            dimension_semantics=("arbitrary",), vmem_limit_bytes=VMEM_LIMIT),
        name="lat_mixer",
    )(x, mod, n1, w_in_p, sink, decay, gn, kvn, w_kv_b, w_out_p, ck, cv, cckv, ckr, s0,
      *rope_a, *rope_m)


def _mlp_kernel(final, x_ref, mod_ref, n2_ref, wup_ref, wdn_ref, fn_ref, o_ref):
    x = x_ref[...]
    h2 = _norm_mod(x, n2_ref[...], mod_ref[4:5, :], mod_ref[3:4, :]).astype(BF16)
    acc = None
    for c in range(D_FF // FF_CHUNK):
        cols = slice(c * FF_CHUNK, (c + 1) * FF_CHUNK)
        u = jnp.maximum(_dot(h2, wup_ref[:, cols]), 0.0)
        part = _dot((u * u).astype(BF16), wdn_ref[cols, :])
        acc = part if acc is None else acc + part
    y = x + mod_ref[5:6, :] * acc
    if final:
        y = y * lax.rsqrt(jnp.mean(y * y, axis=-1, keepdims=True) + EPS) * fn_ref[...]
    o_ref[...] = y


def _mlp(layer, x, mod, n2, w_up, w_down, final_norm, final):
    n_tok = x.shape[0]
    per_mod = n_tok // mod.shape[0] // MLP_ROWS
    return pl.pallas_call(
        functools.partial(_mlp_kernel, final),
        out_shape=jax.ShapeDtypeStruct((n_tok, D_MODEL), F32),
        grid=(n_tok // MLP_ROWS,),
        in_specs=[
            pl.BlockSpec((MLP_ROWS, D_MODEL), lambda i: (i, 0)),
            pl.BlockSpec((None, N_MOD, D_MODEL), lambda i: (i // per_mod, 0, 0)),
            _const_spec((1, D_MODEL)),
            _const_spec((D_MODEL, D_FF), layer),
            _const_spec((D_FF, D_MODEL), layer),
            _const_spec((1, D_MODEL)),
        ],
        out_specs=pl.BlockSpec((MLP_ROWS, D_MODEL), lambda i: (i, 0)),
        compiler_params=pltpu.CompilerParams(
            dimension_semantics=("arbitrary",), vmem_limit_bytes=VMEM_LIMIT),
        name="mlp",
    )(x, mod, n2, w_up, w_down, final_norm)


def kernel(x_prompt, x_sample, cache_win_k, cache_win_v, cache_mla_ckv, cache_mla_krope, state_ret,
           c, c_ctx, w_mod, b_mod, norm1, norm2, w_in, win_sink, ret_decay, ret_gn, mla_kv_norm,
           w_kv_b, w_out, w_up, w_down, final_norm):
    n_ctx, n_lat = x_prompt.shape[0], x_sample.shape[0]

    w_in_p = _take_runs(w_in, _in_proj_columns(), 2).astype(BF16)
    w_out_p = _take_runs(w_out, _mix_rows(), 1).astype(BF16)
    w_kv_b16, w_up16, w_down16 = w_kv_b.astype(BF16), w_up.astype(BF16), w_down.astype(BF16)

    c_rows = jnp.zeros((16, D_MODEL), F32).at[0].set(c_ctx).at[1:1 + n_lat].set(c)
    mod = _modulation(c_rows, w_mod, b_mod).reshape(DEPTH, 16, N_MOD, D_MODEL)

    rope_a = _rope_tables(DEC_SEQ, HEAD_DIM, 0, HEAD_DIM)
    rope_m = _rope_tables(DEC_SEQ, MLA_ROPE, MLA_NOPE, LANES)

    ck = cache_win_k.reshape(n_lat, DEPTH, PAST_LEN, LANES)
    cv = cache_win_v.reshape(n_lat, DEPTH, PAST_LEN, LANES)
    ckr = jnp.pad(cache_mla_krope, ((0, 0), (0, 0), (0, 0), (MLA_NOPE, LANES - MLA_NOPE - MLA_ROPE)))
    sr = state_ret.reshape(n_lat, DEPTH, 2, 2, 2, HEAD_DIM, HEAD_DIM)
    zero = jnp.zeros_like(sr[:, :, :, :, 0])
    s0 = jnp.concatenate([jnp.concatenate([sr[:, :, :, :, 0], zero], axis=-1),
                          jnp.concatenate([zero, sr[:, :, :, :, 1]], axis=-1)], axis=-2)

    xp = x_prompt.reshape(n_ctx * SEQ, D_MODEL)
    xs = x_sample.reshape(n_lat * DEC_SEQ, D_MODEL)
    state = ()
    for l in range(DEPTH):
        last = l == DEPTH - 1
        shared = (norm1[l][None], w_in_p, win_sink[l], ret_decay[l], ret_gn[l][None],
                  mla_kv_norm[l][None], w_kv_b16, w_out_p)
        mod_ctx, mod_lat = mod[l, 0:1], mod[l, 1:1 + n_lat]
        xp, *state = _ctx_mixer(l, xp, mod_ctx[0], *shared, state)
        xp = _mlp(l, xp, mod_ctx, norm2[l][None], w_up16, w_down16, final_norm[None], last)
        xs = _lat_mixer(l, xs, mod_lat, *shared, ck, cv, cache_mla_ckv, ckr, s0, rope_a, rope_m)
        xs = _mlp(l, xs, mod_lat, norm2[l][None], w_up16, w_down16, final_norm[None], last)
    new_k, new_v, new_ckv, new_kr, new_s = state
    return (xp.reshape(n_ctx, SEQ, D_MODEL), xs.reshape(n_lat, DEC_SEQ, D_MODEL),
            new_k.reshape(n_ctx, DEPTH, SEQ, 2, HEAD_DIM), new_v.reshape(n_ctx, DEPTH, SEQ, 2, HEAD_DIM),
            new_ckv, new_kr, new_s)
```

```python
import functools

import numpy as np
import jax
import jax.numpy as jnp
from jax import lax
from jax.experimental import pallas as pl
from jax.experimental.pallas import tpu as pltpu

F32 = jnp.float32
BF16 = jnp.bfloat16

D_MODEL = 1024
DEPTH = 2
SEQ = 256
DEC_SEQ = 1024
PAST_LEN = 256
GRID_W = 64
HEAD_DIM = 64
ROPE_BASE = 10000.0
EPS = 1e-6
WIN_HEADS = 8
WINDOW = 128
ATTN_SCALE = HEAD_DIM ** -0.5
RET_HEADS = 4
RET_CHUNK = 128
RET_K_SCALE = HEAD_DIM ** -0.5
MLA_HEADS = 4
MLA_NOPE = 64
MLA_ROPE = 32
MLA_KV_RANK = 128
MLA_QK = MLA_NOPE + MLA_ROPE
MLA_SCALE = MLA_QK ** -0.5
D_IN = 2336
D_FF = 4 * D_MODEL
N_MOD = 6

LANES = 128
TOK = 1024
MLP_ROWS = 512
FF_CHUNK = 1024
VMEM_LIMIT = 60 * 1024 * 1024

C_QA, C_KA, C_VA, C_QB, C_KB, C_VB, C_GB, C_QC, C_CKV, C_KR, D_IN_P = (
    0, 512, 640, 768, 1024, 1280, 1536, 1792, 2304, 2432, 2560)
M_A, M_B, M_C = 0, 512, 768

NT_DIMS = (((1,), (1,)), ((), ()))


def _in_proj_columns():
    idx = []
    for g in range(4):
        idx += list(range(g * 64, (g + 1) * 64)) + list(range((4 + g) * 64, (5 + g) * 64))
    idx += list(range(512, 1792))
    for h in range(MLA_HEADS):
        idx += list(range(1792 + h * MLA_QK, 1792 + (h + 1) * MLA_QK)) + [-1] * 32
    idx += list(range(2176, 2304))
    idx += [-1] * 64 + list(range(2304, 2336)) + [-1] * 32
    return np.asarray(idx, np.int32)


def _mix_rows():
    idx = []
    for g in range(4):
        idx += list(range(g * 64, (g + 1) * 64)) + list(range((4 + g) * 64, (5 + g) * 64))
    idx += list(range(512, 1024))
    return np.asarray(idx, np.int32)


def _take_runs(w, idx, axis):
    pieces, i = [], 0
    while i < len(idx):
        j = i + 1
        if idx[i] < 0:
            while j < len(idx) and idx[j] < 0:
                j += 1
            shape = list(w.shape)
            shape[axis] = j - i
            pieces.append(jnp.zeros(shape, w.dtype))
        else:
            while j < len(idx) and idx[j] == idx[j - 1] + 1:
                j += 1
            pieces.append(lax.slice_in_dim(w, int(idx[i]), int(idx[j - 1]) + 1, axis=axis))
        i = j
    return jnp.concatenate(pieces, axis=axis)


def _rope_tables(n_tokens, dim, lane0, period):
    quarter = dim // 4
    t = np.arange(n_tokens)
    row = (t // GRID_W).astype(np.float64)
    col = (t % GRID_W).astype(np.float64)
    inv_freq = ROPE_BASE ** (-np.arange(quarter, dtype=np.float64) / quarter)
    ar, ac = row[:, None] * inv_freq, col[:, None] * inv_freq
    cos = np.concatenate([np.cos(ar), np.cos(ar), np.cos(ac), np.cos(ac)], axis=-1)
    sin = np.concatenate([np.sin(ar), np.sin(ar), np.sin(ac), np.sin(ac)], axis=-1)
    first = np.tile(np.concatenate([np.ones(quarter), np.zeros(quarter)]), 2)
    c = np.ones((n_tokens, LANES))
    sa = np.zeros((n_tokens, LANES))
    sb = np.zeros((n_tokens, LANES))
    for start in range(lane0, LANES, period):
        c[:, start:start + dim] = cos
        sa[:, start:start + dim] = -sin * first
        sb[:, start:start + dim] = sin * (1.0 - first)
    return tuple(jnp.asarray(a, F32) for a in (c, sa, sb))


def _lane_iota(shape):
    return lax.broadcasted_iota(jnp.int32, shape, len(shape) - 1)


def _row_iota(shape):
    return lax.broadcasted_iota(jnp.int32, shape, len(shape) - 2)


def _ds(start, size):
    if isinstance(start, int):
        return pl.ds(start, size)
    return pl.ds(pl.multiple_of(start, LANES), size)


def _norm_mod(x, gain, scale, shift):
    ms = jnp.mean(x * x, axis=-1, keepdims=True)
    return (x * lax.rsqrt(ms + EPS) * gain) * (1.0 + scale) + shift


def _log_sigmoid(x):
    return -(jnp.maximum(-x, 0.0) + jnp.log1p(jnp.exp(-jnp.abs(x))))


def _rope(x, c, sa, sb, quarter):
    return (x * c + pltpu.roll(x, LANES - quarter, 1) * sa + pltpu.roll(x, quarter, 1) * sb)


def _dot(a, b):
    return jnp.dot(a, b, preferred_element_type=F32)


def _dot_nt(a, b):
    return lax.dot_general(a, b, NT_DIMS, preferred_element_type=F32)


def _mod_kernel(c_ref, w_ref, b_ref, o_ref):
    cv = c_ref[...]
    s = cv * jax.nn.sigmoid(cv)
    o_ref[0] = _dot(s.astype(BF16), w_ref[0].astype(BF16)) + b_ref[0]


def _modulation(c_rows, w_mod, b_mod):
    tn = 1536
    nj = (N_MOD * D_MODEL) // tn
    return pl.pallas_call(
        _mod_kernel,
        out_shape=jax.ShapeDtypeStruct((DEPTH, 16, N_MOD * D_MODEL), F32),
        grid=(DEPTH, nj),
        in_specs=[
            pl.BlockSpec((16, D_MODEL), lambda l, j: (0, 0)),
            pl.BlockSpec((1, D_MODEL, tn), lambda l, j: (l, 0, j)),
            pl.BlockSpec((1, 1, tn), lambda l, j: (l, 0, j)),
        ],
        out_specs=pl.BlockSpec((1, 16, tn), lambda l, j: (l, 0, j)),
        compiler_params=pltpu.CompilerParams(
            dimension_semantics=("arbitrary", "arbitrary"), vmem_limit_bytes=VMEM_LIMIT),
        name="modulation",
    )(c_rows, w_mod, b_mod.reshape(DEPTH, 1, N_MOD * D_MODEL))


def _pipeline(n_steps, scores, softmax, values):
    scores(0, 0)
    scores(1, 1)
    softmax(0, 0)

    def body(i, carry):
        t = 2 * i + 1
        scores(t + 1, 0)
        softmax(t, 1)
        values(t - 1, 0)
        scores(t + 2, 1)
        softmax(t + 1, 0)
        values(t, 1)
        return carry

    lax.fori_loop(0, n_steps // 2 - 1, body, 0)
    softmax(n_steps - 1, 1)
    values(n_steps - 2, 0)
    values(n_steps - 1, 1)


def _softmax_tile(parts, floor=None, scale=None):
    m = None
    for s in parts:
        pm = jnp.max(s, axis=-1, keepdims=True)
        m = pm if m is None else jnp.maximum(m, pm)
    if floor is not None:
        m = jnp.maximum(m, floor)
    if scale is None:
        ps = [jnp.exp(s - m).astype(BF16) for s in parts]
    else:
        ps = [jnp.exp((s - m) * scale).astype(BF16) for s in parts]
    extra = None if floor is None else jnp.exp(floor - m)
    return ps, extra


def _retention_tables(decay_ref, rt_scr, m2_scr):
    shape = (RET_CHUNK, LANES)
    lane_lo = _lane_iota(shape) < 64
    row = _row_iota(shape)
    row_lo = row < 64
    i = row.astype(F32)
    rel = i - _lane_iota(shape).astype(F32)
    for j in range(2):
        df0, df1 = decay_ref[0, 2 * j], decay_ref[0, 2 * j + 1]
        db0, db1 = decay_ref[1, 2 * j], decay_ref[1, 2 * j + 1]
        lgf = _log_sigmoid(jnp.where(lane_lo, df0, df1))
        lgb = _log_sigmoid(jnp.where(lane_lo, db0, db1))
        rt_scr[j, 0] = jnp.exp((i + 1.0) * lgf)
        rt_scr[j, 1] = jnp.exp((RET_CHUNK - i) * lgb)
        rt_scr[j, 2] = jnp.exp((RET_CHUNK - 1.0 - i) * lgf)
        rt_scr[j, 3] = jnp.exp(i * lgb)
        rt_scr[j, 4] = jnp.exp(RET_CHUNK * _log_sigmoid(jnp.where(row_lo, df0, df1)))
        rt_scr[j, 5] = jnp.exp(RET_CHUNK * _log_sigmoid(jnp.where(row_lo, db0, db1)))
        for hh, (df, db) in enumerate(((df0, db0), (df1, db1))):
            lf = _log_sigmoid(jnp.full(shape, df, F32))
            lb = _log_sigmoid(jnp.full(shape, db, F32))
            low, upp = rel >= 0.0, rel <= 0.0
            m = (jnp.where(low, jnp.exp(jnp.where(low, rel, 0.0) * lf), 0.0)
                 + jnp.where(upp, jnp.exp(jnp.where(upp, -rel, 0.0) * lb), 0.0))
            m2_scr[j, :, hh * RET_CHUNK:(hh + 1) * RET_CHUNK] = m


N_CHUNK = TOK // RET_CHUNK


def _ret_scratch(n_seq, chunks_per_stage):
    per_chunk = lambda dtype: pltpu.VMEM((N_CHUNK, 2, 2 * RET_CHUNK, LANES), dtype)
    stage = lambda dtype: pltpu.VMEM((2 * chunks_per_stage, RET_CHUNK, 2 * LANES), dtype)
    return [
        pltpu.VMEM((TOK, 2 * LANES), BF16),
        pltpu.VMEM((TOK, 4 * LANES), BF16),
        pltpu.VMEM((TOK, 2 * LANES), F32),
        per_chunk(BF16),
        pltpu.VMEM((TOK, 2 * LANES), BF16),
        per_chunk(BF16),
        pltpu.VMEM((TOK, 2 * LANES), F32),
        pltpu.VMEM((2, 6, RET_CHUNK, LANES), F32),
        pltpu.VMEM((2, RET_CHUNK, 2 * LANES), F32),
        per_chunk(F32),
        per_chunk(BF16),
        pltpu.VMEM((n_seq, 2, 2, LANES, LANES), F32),
        stage(F32), stage(F32), stage(BF16), stage(BF16),
    ]


class _Retention:
    def __init__(self, scratch, gn_ref, mix_scr):
        (self.q, self.qd, self.k, self.kbd, self.v, self.vbd, self.g, self.rt, self.m2, self.upd,
         self.sall, self.st, rs0, rs1, rp0, rp1) = scratch
        self.rs, self.rp = (rs0, rs1), (rp0, rp1)
        self.gn_ref, self.mix = gn_ref, mix_scr

    def prepare(self, proj, decay_ref, seq_chunks):
        _retention_tables(decay_ref, self.rt, self.m2)
        lane_lo = _lane_iota((TOK, LANES)) < 64
        per_chunk = lambda a: a.reshape(N_CHUNK, RET_CHUNK, LANES)
        q = proj(C_QB, C_KB)
        k = proj(C_KB, C_VB) * RET_K_SCALE
        v = proj(C_VB, C_GB)
        self.g[...] = proj(C_GB, C_QC)
        self.q[...] = q.astype(BF16)
        self.k[...] = k
        self.v[...] = v.astype(BF16)
        for j in range(2):
            cols = slice(j * LANES, (j + 1) * LANES)
            q3 = per_chunk(q[:, cols])
            for d in range(2):
                self.qd[:, (2 * j + d) * LANES:(2 * j + d + 1) * LANES] = (
                    (q3 * self.rt[j, d]).reshape(TOK, LANES).astype(BF16))
            for src, dst in ((k[:, cols], self.kbd), (v[:, cols], self.vbd)):
                dst[:, j, :RET_CHUNK, :] = per_chunk(jnp.where(lane_lo, src, 0.0).astype(BF16))
                dst[:, j, RET_CHUNK:, :] = per_chunk(jnp.where(lane_lo, 0.0, src).astype(BF16))

        blockdiag2 = _lane_iota((2 * RET_CHUNK, LANES)) < 64
        blockdiag2 = blockdiag2 == ((_row_iota((2 * RET_CHUNK, LANES)) & (RET_CHUNK - 1)) < 64)

        def upd_body(c, carry):
            rows = _ds(c * RET_CHUNK, RET_CHUNK)
            for j in range(2):
                cols = slice(j * LANES, (j + 1) * LANES)
                k2 = self.k[rows, cols]
                kd = jnp.concatenate([k2 * self.rt[j, 2], k2 * self.rt[j, 3]], axis=1)
                upd = _dot(kd.T.astype(BF16), self.v[rows, cols])
                self.upd[c, j] = jnp.where(blockdiag2, upd, 0.0)
            return carry

        lax.fori_loop(0, N_CHUNK, upd_body, 0, unroll=4)

        n_per_seq = seq_chunks[0][1]
        assert all(n == n_per_seq for _, n in seq_chunks)

        def scan_body(t, carry):
            for si, (first, n) in enumerate(seq_chunks):
                cf, cb = first + t, first + n - 1 - t
                for j in range(2):
                    sf, sb = self.st[si, 0, j], self.st[si, 1, j]
                    self.sall[cf, j, :RET_CHUNK, :] = sf.astype(BF16)
                    self.sall[cb, j, RET_CHUNK:, :] = sb.astype(BF16)
                    self.st[si, 0, j] = self.rt[j, 4] * sf + self.upd[cf, j, :RET_CHUNK, :]
                    self.st[si, 1, j] = self.rt[j, 5] * sb + self.upd[cb, j, RET_CHUNK:, :]
            return carry

        lax.fori_loop(0, n_per_seq, scan_body, 0)

    def scores(self, chunks, b):
        for i, c in enumerate(chunks):
            rows = _ds(c * RET_CHUNK, RET_CHUNK)
            for j in range(2):
                self.rs[b][2 * i + j] = _dot_nt(self.q[rows, j * LANES:(j + 1) * LANES], self.kbd[c, j])

    def mask(self, chunks, b):
        for i in range(len(chunks)):
            for j in range(2):
                self.rp[b][2 * i + j] = (self.rs[b][2 * i + j] * self.m2[j]).astype(BF16)

    def values(self, chunks, b):
        lane_lo = _lane_iota((RET_CHUNK, LANES)) < 64
        for i, c in enumerate(chunks):
            rows = _ds(c * RET_CHUNK, RET_CHUNK)
            for j in range(2):
                cols = slice(j * LANES, (j + 1) * LANES)
                o = (_dot(self.rp[b][2 * i + j], self.vbd[c, j])
                     + _dot(self.qd[rows, 2 * j * LANES:2 * (j + 1) * LANES], self.sall[c, j]))
                s_lo = jnp.sum(jnp.where(lane_lo, o, 0.0), axis=-1, keepdims=True)
                s_hi = jnp.sum(jnp.where(lane_lo, 0.0, o), axis=-1, keepdims=True)
                d = o - jnp.where(lane_lo, s_lo, s_hi) * (1.0 / HEAD_DIM)
                dd = d * d
                v_lo = jnp.sum(jnp.where(lane_lo, dd, 0.0), axis=-1, keepdims=True)
                v_hi = jnp.sum(jnp.where(lane_lo, 0.0, dd), axis=-1, keepdims=True)
                var = jnp.where(lane_lo, v_lo, v_hi) * (1.0 / HEAD_DIM)
                g2 = self.g[rows, cols]
                y = d * lax.rsqrt(var + EPS) * self.gn_ref[:, cols] * (g2 * jax.nn.sigmoid(g2))
                self.mix[rows, M_B + j * LANES:M_B + (j + 1) * LANES] = y.astype(BF16)


X_ROWS = 256
N_XCHUNK = TOK // X_ROWS


class _XStream:
    SCRATCH = [pltpu.VMEM((2, X_ROWS, D_MODEL), F32), pltpu.VMEM((2, X_ROWS, D_MODEL), F32),
               pltpu.SemaphoreType.DMA((2,)), pltpu.SemaphoreType.DMA((2,))]

    def __init__(self, x_hbm, xo_hbm, scratch):
        self.x_hbm, self.xo_hbm = x_hbm, xo_hbm
        self.xin, self.xout, self.sem_in, self.sem_out = scratch
        self.row0 = pl.program_id(0) * TOK

    def _rows(self, r):
        return pl.ds(pl.multiple_of(self.row0 + r * X_ROWS, X_ROWS), X_ROWS)

    def load(self, r):
        return pltpu.make_async_copy(self.x_hbm.at[self._rows(r), :], self.xin.at[r % 2],
                                     self.sem_in.at[r % 2])

    def store(self, r):
        return pltpu.make_async_copy(self.xout.at[r % 2], self.xo_hbm.at[self._rows(r), :],
                                     self.sem_out.at[r % 2])

    def prologue(self, mod_ref, n1_ref, h_scr):
        self.load(0).start()
        for r in range(N_XCHUNK):
            if r + 1 < N_XCHUNK:
                self.load(r + 1).start()
            self.load(r).wait()
            h = _norm_mod(self.xin[r % 2], n1_ref[...], mod_ref[1:2, :], mod_ref[0:1, :])
            h_scr[r * X_ROWS:(r + 1) * X_ROWS, :] = h.astype(BF16)
        self.load(0).start()
        self.load(1).start()

    def epilogue(self, mod_ref, wout_ref, mix_scr):
        for r in range(N_XCHUNK):
            self.load(r).wait()
            if r >= 2:
                self.store(r - 2).wait()
            y = _dot(mix_scr[r * X_ROWS:(r + 1) * X_ROWS, :], wout_ref[...])
            self.xout[r % 2] = self.xin[r % 2] + mod_ref[2:3, :] * y
            self.store(r).start()
            if r + 2 < N_XCHUNK:
                self.load(r + 2).start()
        for r in range(N_XCHUNK - 2, N_XCHUNK):
            self.store(r).wait()


CTX_PER_STEP = TOK // SEQ
CTX_WROWS = WIN_HEADS * SEQ
CTX_MROWS = MLA_HEADS * SEQ


def _ctx_mixer_kernel(n_alias, x_ref, mod_ref, n1_ref, win_ref, sink_ref, decay_ref, gn_ref, kvn_ref,
                      wkvb_ref, wout_ref, *refs):
    xo_ref, ko_ref, vo_ref, ckvo_ref, kro_ref, so_ref, h_scr, mix_scr = refs[n_alias:n_alias + 8]
    xs = _XStream(x_ref, xo_ref, refs[n_alias + 8:])
    xs.prologue(mod_ref, n1_ref, h_scr)
    per_seq = lambda a: a.reshape(CTX_PER_STEP, SEQ, a.shape[-1])
    if n_alias == 0:
        for ref in (ko_ref, vo_ref, ckvo_ref, kro_ref, so_ref):
            ref[:, 1:] = jnp.zeros((ref.shape[0], ref.shape[1] - 1) + ref.shape[2:], F32)
        ko_ref, vo_ref, ckvo_ref, kro_ref, so_ref = (
            ref.at[:, 0] for ref in (ko_ref, vo_ref, ckvo_ref, kro_ref, so_ref))
    chunks_per_seq = SEQ // RET_CHUNK

    def proj(c0, c1):
        return _dot(h_scr[...], win_ref[:, c0:c1])

    ret = []
    seq_chunks = lambda e: [chunks_per_seq * e + c for c in range(chunks_per_seq)]

    def attention(qst_scr, k_scr, vaug_scr, qc_scr, kcat_scr, kvaug_scr,
                  sw0, sw1, pw0, pw1, ew0, ew1, sm0, sm1, pm0, pm1):
        lane_lo = _lane_iota((TOK, LANES)) < 64
        ones = jnp.ones((TOK, LANES), BF16)
        qa = proj(C_QA, C_KA) * ATTN_SCALE
        for g in range(4):
            q = qa[:, g * LANES:(g + 1) * LANES]
            lo = jnp.where(lane_lo, q, 0.0).astype(BF16).reshape(CTX_PER_STEP, SEQ, LANES)
            hi = jnp.where(lane_lo, 0.0, q).astype(BF16).reshape(CTX_PER_STEP, SEQ, LANES)
            qst_scr[:, g * SEQ:(g + 1) * SEQ, :] = lo
            qst_scr[:, (4 + g) * SEQ:(5 + g) * SEQ, :] = hi
        kva = proj(C_KA, C_QB)
        ko_ref[...] = per_seq(kva[:, :LANES])
        vo_ref[...] = per_seq(kva[:, LANES:])
        k_scr[...] = kva[:, :LANES].astype(BF16)
        vaug_scr[:, :LANES] = kva[:, LANES:].astype(BF16)
        vaug_scr[:, LANES:] = ones
        qc = proj(C_QC, C_CKV).astype(BF16)
        for h in range(MLA_HEADS):
            qc_scr[h] = qc[:, h * LANES:(h + 1) * LANES]
        ckr = proj(C_CKV, D_IN_P)
        ckv, kr = ckr[:, :LANES], ckr[:, LANES:]
        ckv_n = ckv * lax.rsqrt(jnp.mean(ckv * ckv, axis=-1, keepdims=True) + EPS) * kvn_ref[...]
        ckvo_ref[...] = per_seq(ckv_n)
        kro_ref[...] = per_seq(kr[:, 64:64 + MLA_ROPE])
        kv = _dot(ckv_n.astype(BF16), wkvb_ref[...])
        for h in range(MLA_HEADS):
            kvh = kv[:, h * LANES:(h + 1) * LANES]
            kcat_scr[h] = jnp.where(lane_lo, kvh, kr).astype(BF16)
            kvaug_scr[h, :, :LANES] = kvh.astype(BF16)
            kvaug_scr[h, :, LANES:] = ones

        sw, pw, ew, sm, pm = (sw0, sw1), (pw0, pw1), (ew0, ew1), (sm0, sm1), (pm0, pm1)
        lane_lo_s = _lane_iota((SEQ, LANES)) < 64

        def scores(e, b):
            rows = _ds(e * SEQ, SEQ)
            sw[b][...] = _dot_nt(qst_scr[e], k_scr[rows, :])
            for h in range(MLA_HEADS):
                sm[b][h * SEQ:(h + 1) * SEQ, :] = _dot_nt(qc_scr[h, rows, :], kcat_scr[h, rows, :])
            ret[0].scores(seq_chunks(e), b)

        def softmax(e, b):
            for h in range(WIN_HEADS):
                hr = slice(h * SEQ, (h + 1) * SEQ)
                sink = jnp.full((SEQ, 1), sink_ref[h], F32)
                (p,), extra = _softmax_tile([sw[b][hr, :]], floor=sink)
                pw[b][hr, :] = p
                ew[b][hr, :] = jnp.broadcast_to(extra, (SEQ, LANES))
            for h in range(MLA_HEADS):
                hr = slice(h * SEQ, (h + 1) * SEQ)
                (p,), _ = _softmax_tile([sm[b][hr, :]], scale=MLA_SCALE)
                pm[b][hr, :] = p
            ret[0].mask(seq_chunks(e), b)

        def values(e, b):
            ret[0].values(seq_chunks(e), b)
            rows = _ds(e * SEQ, SEQ)
            oa = _dot(pw[b][...], vaug_scr[rows, :])
            o = oa[:, :LANES] / (oa[:, LANES:] + ew[b][...])
            for g in range(4):
                merged = jnp.where(lane_lo_s, o[g * SEQ:(g + 1) * SEQ], o[(4 + g) * SEQ:(5 + g) * SEQ])
                mix_scr[rows, M_A + g * LANES:M_A + (g + 1) * LANES] = merged.astype(BF16)
            for jp in range(2):
                outs = []
                for h in (2 * jp, 2 * jp + 1):
                    oc = _dot(pm[b][h * SEQ:(h + 1) * SEQ, :], kvaug_scr[h, rows, :])
                    outs.append(oc[:, :LANES] / oc[:, LANES:])
                merged = jnp.where(lane_lo_s, pltpu.roll(outs[0], 64, 1), outs[1])
                mix_scr[rows, M_C + jp * LANES:M_C + (jp + 1) * LANES] = merged.astype(BF16)

        _pipeline(CTX_PER_STEP, scores, softmax, values)

    attention_scratch = (
        pltpu.VMEM((CTX_PER_STEP, CTX_WROWS, LANES), BF16),
        pltpu.VMEM((TOK, LANES), BF16),
        pltpu.VMEM((TOK, 2 * LANES), BF16),
        pltpu.VMEM((MLA_HEADS, TOK, LANES), BF16),
        pltpu.VMEM((MLA_HEADS, TOK, LANES), BF16),
        pltpu.VMEM((MLA_HEADS, TOK, 2 * LANES), BF16),
        pltpu.VMEM((CTX_WROWS, SEQ), F32), pltpu.VMEM((CTX_WROWS, SEQ), F32),
        pltpu.VMEM((CTX_WROWS, SEQ), BF16), pltpu.VMEM((CTX_WROWS, SEQ), BF16),
        pltpu.VMEM((CTX_WROWS, LANES), F32), pltpu.VMEM((CTX_WROWS, LANES), F32),
        pltpu.VMEM((CTX_MROWS, SEQ), F32), pltpu.VMEM((CTX_MROWS, SEQ), F32),
        pltpu.VMEM((CTX_MROWS, SEQ), BF16), pltpu.VMEM((CTX_MROWS, SEQ), BF16),
    )

    def retention(*scratch):
        r = _Retention(scratch, gn_ref, mix_scr)
        r.st[...] = jnp.zeros(r.st.shape, F32)
        r.prepare(proj, decay_ref, [(chunks_per_seq * e, chunks_per_seq) for e in range(CTX_PER_STEP)])
        for e in range(CTX_PER_STEP):
            for dirn in range(2):
                for j in range(2):
                    st = r.st[e, dirn, j]
                    so_ref[e, dirn, 2 * j] = st[:64, :64]
                    so_ref[e, dirn, 2 * j + 1] = st[64:, 64:]
        ret.append(r)
        pl.run_scoped(attention, *attention_scratch)

    pl.run_scoped(retention, *_ret_scratch(CTX_PER_STEP, chunks_per_seq))
    xs.epilogue(mod_ref, wout_ref, mix_scr)


def _const_spec(shape, layer=None):
    if layer is None:
        return pl.BlockSpec(shape, lambda i: (0,) * len(shape), pipeline_mode=pl.Buffered(1))
    return pl.BlockSpec((None,) + shape, lambda i: (layer,) + (0,) * len(shape),
                        pipeline_mode=pl.Buffered(1))


def _smem_spec():
    return pl.BlockSpec(memory_space=pltpu.SMEM)


def _ctx_mixer(layer, x, mod, n1, w_in_p, sink, decay, gn, kvn, w_kv_b, w_out_p, prev_state):
    n_tok = x.shape[0]
    n_seq = n_tok // SEQ
    hbm_spec = pl.BlockSpec(memory_space=pl.ANY)
    if prev_state:
        state_spec = lambda *tail: pl.BlockSpec(
            (CTX_PER_STEP, None) + tail, lambda i: (i, layer) + (0,) * len(tail))
    else:
        assert layer == 0
        state_spec = lambda *tail: pl.BlockSpec(
            (CTX_PER_STEP, DEPTH) + tail, lambda i: (i, 0) + (0,) * len(tail))
    state_tails = [(SEQ, LANES), (SEQ, LANES), (SEQ, MLA_KV_RANK), (SEQ, MLA_ROPE),
                   (2, RET_HEADS, HEAD_DIM, HEAD_DIM)]
    n_in = 10
    return pl.pallas_call(
        functools.partial(_ctx_mixer_kernel, len(prev_state)),
        out_shape=[jax.ShapeDtypeStruct((n_tok, D_MODEL), F32)] + [
            jax.ShapeDtypeStruct((n_seq, DEPTH) + tail, F32) for tail in state_tails],
        grid=(n_tok // TOK,),
        in_specs=[
            hbm_spec,
            _const_spec((N_MOD, D_MODEL)),
            _const_spec((1, D_MODEL)),
            _const_spec((D_MODEL, D_IN_P), layer),
            _smem_spec(),
            _smem_spec(),
            _const_spec((1, RET_HEADS * HEAD_DIM)),
            _const_spec((1, MLA_KV_RANK)),
            _const_spec((MLA_KV_RANK, MLA_HEADS * LANES), layer),
            _const_spec((D_MODEL, D_MODEL), layer),
        ] + [hbm_spec] * len(prev_state),
        out_specs=[hbm_spec] + [state_spec(*tail) for tail in state_tails],
        input_output_aliases={n_in + k: 1 + k for k in range(len(prev_state))},
        scratch_shapes=[
            pltpu.VMEM((TOK, D_MODEL), BF16),
            pltpu.VMEM((TOK, D_MODEL), BF16),
        ] + _XStream.SCRATCH,
        compiler_params=pltpu.CompilerParams(
            dimension_semantics=("arbitrary",), vmem_limit_bytes=VMEM_LIMIT),
        name="ctx_mixer",
    )(x, mod, n1, w_in_p, sink, decay, gn, kvn, w_kv_b, w_out_p, *prev_state)


N_BLK = DEC_SEQ // LANES
KEYS_LOC = 3 * LANES
KEYS_WIN = KEYS_LOC + PAST_LEN
WIN_ROWS = WIN_HEADS * LANES
MLA_QB = 256
MLA_KEYS = DEC_SEQ + PAST_LEN
MLA_HALF = MLA_KEYS // 2


def _lat_mixer_kernel(x_ref, mod_ref, n1_ref, win_ref, sink_ref, decay_ref, gn_ref, kvn_ref,
                      wkvb_ref, wout_ref, ck_ref, cv_ref, cckv_ref, ckr_ref, s0_ref,
                      rc_ref, rsa_ref, rsb_ref, mc_ref, msa_ref, msb_ref,
                      xo_ref,
                      h_scr, mix_scr, s0_scr, s1_scr, p0_scr, p1_scr, *x_scratch):
    xs = _XStream(x_ref, xo_ref, x_scratch)
    xs.prologue(mod_ref, n1_ref, h_scr)
    sbuf, pbuf = (s0_scr, s1_scr), (p0_scr, p1_scr)

    def proj(c0, c1):
        return _dot(h_scr[...], win_ref[:, c0:c1])

    ret = []

    def window(qst_scr, kpad_scr, vaug_scr, ckb_scr, cvaug_scr, bias_scr, e0_scr, e1_scr):
        ebuf = (e0_scr, e1_scr)
        lane_lo = _lane_iota((TOK, LANES)) < 64
        rc, rsa, rsb = rc_ref[...], rsa_ref[...], rsb_ref[...]
        qa = proj(C_QA, C_KA)
        for g in range(4):
            q = _rope(qa[:, g * LANES:(g + 1) * LANES], rc, rsa, rsb, 16) * ATTN_SCALE
            lo = jnp.where(lane_lo, q, 0.0).astype(BF16).reshape(N_BLK, LANES, LANES)
            hi = jnp.where(lane_lo, 0.0, q).astype(BF16).reshape(N_BLK, LANES, LANES)
            qst_scr[:, g * LANES:(g + 1) * LANES, :] = lo
            qst_scr[:, (4 + g) * LANES:(5 + g) * LANES, :] = hi
        kva = proj(C_KA, C_QB)
        zpad = jnp.zeros((LANES, LANES), BF16)
        kpad_scr[0:LANES, :] = zpad
        kpad_scr[LANES + TOK:, :] = zpad
        vaug_scr[0:LANES, :LANES] = zpad
        vaug_scr[LANES + TOK:, :LANES] = zpad
        kpad_scr[LANES:LANES + TOK, :] = _rope(kva[:, :LANES], rc, rsa, rsb, 16).astype(BF16)
        vaug_scr[LANES:LANES + TOK, :LANES] = kva[:, LANES:].astype(BF16)
        vaug_scr[:, LANES:] = jnp.ones((TOK + 2 * LANES, LANES), BF16)
        ckb_scr[...] = ck_ref[...].astype(BF16)
        cvaug_scr[:, :LANES] = cv_ref[...].astype(BF16)
        cvaug_scr[:, LANES:] = jnp.ones((PAST_LEN, LANES), BF16)
        qi, kj = _row_iota((LANES, LANES)), _lane_iota((LANES, LANES))
        bias_scr[0] = jnp.full((LANES, LANES), -jnp.inf, F32)
        bias_scr[1] = jnp.where(kj >= qi, 0.0, -jnp.inf)
        bias_scr[2] = jnp.where(kj <= qi, 0.0, -jnp.inf)
        lane_lo_b = _lane_iota((LANES, LANES)) < 64

        def scores(n, b):
            q = qst_scr[n]
            sbuf[b][:, :KEYS_LOC] = _dot_nt(q, kpad_scr[_ds(n * LANES, KEYS_LOC), :])
            sbuf[b][:, KEYS_LOC:] = _dot_nt(q, ckb_scr[...])
            ret[0].scores([n], b)

        def softmax(n, b):
            if isinstance(n, int):
                i_prev, i_next = (1 if n > 0 else 0), (2 if n < N_BLK - 1 else 0)
            else:
                i_prev, i_next = jnp.where(n > 0, 1, 0), jnp.where(n < N_BLK - 1, 2, 0)
            b_prev, b_next = bias_scr[i_prev], bias_scr[i_next]
            for h in range(WIN_HEADS):
                hr = slice(h * LANES, (h + 1) * LANES)
                parts = [sbuf[b][hr, 0:LANES] + b_prev, sbuf[b][hr, LANES:2 * LANES],
                         sbuf[b][hr, 2 * LANES:KEYS_LOC] + b_next, sbuf[b][hr, KEYS_LOC:]]
                ps, extra = _softmax_tile(parts, floor=jnp.full((LANES, 1), sink_ref[h], F32))
                pbuf[b][hr, 0:LANES] = ps[0]
                pbuf[b][hr, LANES:2 * LANES] = ps[1]
                pbuf[b][hr, 2 * LANES:KEYS_LOC] = ps[2]
                pbuf[b][hr, KEYS_LOC:] = ps[3]
                ebuf[b][hr, :] = jnp.broadcast_to(extra, (LANES, LANES))
            ret[0].mask([n], b)

        def values(n, b):
            ret[0].values([n], b)
            oa = (_dot(pbuf[b][:, :KEYS_LOC], vaug_scr[_ds(n * LANES, KEYS_LOC), :])
                  + _dot(pbuf[b][:, KEYS_LOC:], cvaug_scr[...]))
            o = oa[:, :LANES] / (oa[:, LANES:] + ebuf[b][...])
            for g in range(4):
                merged = jnp.where(lane_lo_b, o[g * LANES:(g + 1) * LANES],
                                   o[(4 + g) * LANES:(5 + g) * LANES])
                mix_scr[_ds(n * LANES, LANES), M_A + g * LANES:M_A + (g + 1) * LANES] = (
                    merged.astype(BF16))

        _pipeline(N_BLK, scores, softmax, values)

    window_scratch = (
        pltpu.VMEM((N_BLK, WIN_ROWS, LANES), BF16),
        pltpu.VMEM((TOK + 2 * LANES, LANES), BF16),
        pltpu.VMEM((TOK + 2 * LANES, 2 * LANES), BF16),
        pltpu.VMEM((PAST_LEN, LANES), BF16),
        pltpu.VMEM((PAST_LEN, 2 * LANES), BF16),
        pltpu.VMEM((3, LANES, LANES), F32),
        pltpu.VMEM((WIN_ROWS, LANES), F32), pltpu.VMEM((WIN_ROWS, LANES), F32),
    )

    def retention(*scratch):
        r = _Retention(scratch, gn_ref, mix_scr)
        r.st[0] = s0_ref[...]
        r.prepare(proj, decay_ref, [(0, N_CHUNK)])
        ret.append(r)
        pl.run_scoped(window, *window_scratch)

    assert N_CHUNK == N_BLK
    pl.run_scoped(retention, *_ret_scratch(1, 1))

    def latent(qc_scr, kcat_scr, kvaug_scr, mixc_scr):
        lane_lo = _lane_iota((TOK, LANES)) < 64
        mc, msa, msb = mc_ref[...], msa_ref[...], msb_ref[...]
        qc = proj(C_QC, C_CKV)
        for h in range(MLA_HEADS):
            qc_scr[h] = _rope(qc[:, h * LANES:(h + 1) * LANES], mc, msa, msb, 8).astype(BF16)
        ckr = proj(C_CKV, D_IN_P)
        ckv = ckr[:, :LANES]
        kr = _rope(ckr[:, LANES:], mc, msa, msb, 8)
        ckv_n = ckv * lax.rsqrt(jnp.mean(ckv * ckv, axis=-1, keepdims=True) + EPS) * kvn_ref[...]
        kv = _dot(ckv_n.astype(BF16), wkvb_ref[...])
        kv_c = _dot(cckv_ref[...].astype(BF16), wkvb_ref[...])
        kr_c = ckr_ref[...]
        lane_lo_c = _lane_iota((PAST_LEN, LANES)) < 64
        for h in range(MLA_HEADS):
            kvh, kvh_c = kv[:, h * LANES:(h + 1) * LANES], kv_c[:, h * LANES:(h + 1) * LANES]
            kcat_scr[h, 0:TOK, :] = jnp.where(lane_lo, kvh, kr).astype(BF16)
            kcat_scr[h, TOK:, :] = jnp.where(lane_lo_c, kvh_c, kr_c).astype(BF16)
            kvaug_scr[h, 0:TOK, :LANES] = kvh.astype(BF16)
            kvaug_scr[h, TOK:, :LANES] = kvh_c.astype(BF16)
            kvaug_scr[h, :, LANES:] = jnp.ones((MLA_KEYS, LANES), BF16)
        lane_lo_m = _lane_iota((MLA_QB, LANES)) < 64
        n_qb = DEC_SEQ // MLA_QB

        def split(t):
            if isinstance(t, int):
                return t // n_qb, t % n_qb
            return lax.shift_right_logical(t, 2), lax.bitwise_and(t, n_qb - 1)

        def scores(t, b):
            jp, qb = split(t)
            for i in range(2):
                h = 2 * jp + i
                q = qc_scr[h, _ds(qb * MLA_QB, MLA_QB), :]
                for part in range(2):
                    r0 = (2 * i + part) * MLA_QB
                    sbuf[b][r0:r0 + MLA_QB, :] = _dot_nt(
                        q, kcat_scr[h, part * MLA_HALF:(part + 1) * MLA_HALF, :])

        def softmax(t, b):
            for i in range(2):
                for rt in range(MLA_QB // LANES):
                    ra = 2 * i * MLA_QB + rt * LANES
                    rb = ra + MLA_QB
                    ps, _ = _softmax_tile([sbuf[b][ra:ra + LANES, :], sbuf[b][rb:rb + LANES, :]],
                                          scale=MLA_SCALE)
                    pbuf[b][ra:ra + LANES, :] = ps[0]
                    pbuf[b][rb:rb + LANES, :] = ps[1]

        def values(t, b):
            jp, qb = split(t)
            outs = []
            for i in range(2):
                h = 2 * jp + i
                r0 = 2 * i * MLA_QB
                oc = (_dot(pbuf[b][r0:r0 + MLA_QB, :], kvaug_scr[h, 0:MLA_HALF, :])
                      + _dot(pbuf[b][r0 + MLA_QB:r0 + 2 * MLA_QB, :], kvaug_scr[h, MLA_HALF:, :]))
                outs.append(oc[:, :LANES] / oc[:, LANES:])
            merged = jnp.where(lane_lo_m, pltpu.roll(outs[0], 64, 1), outs[1])
            mixc_scr[jp, _ds(qb * MLA_QB, MLA_QB), :] = merged.astype(BF16)

        _pipeline(2 * n_qb, scores, softmax, values)
        for jp in range(2):
            mix_scr[:, M_C + jp * LANES:M_C + (jp + 1) * LANES] = mixc_scr[jp]

    pl.run_scoped(
        latent,
        pltpu.VMEM((MLA_HEADS, TOK, LANES), BF16),
        pltpu.VMEM((MLA_HEADS, MLA_KEYS, LANES), BF16),
        pltpu.VMEM((MLA_HEADS, MLA_KEYS, 2 * LANES), BF16),
        pltpu.VMEM((2, TOK, LANES), BF16),
    )
    xs.epilogue(mod_ref, wout_ref, mix_scr)


def _lat_mixer(layer, x, mod, n1, w_in_p, sink, decay, gn, kvn, w_kv_b, w_out_p,
               ck, cv, cckv, ckr, s0, rope_a, rope_m):
    n_tok = x.shape[0]
    n_seq = n_tok // DEC_SEQ
    seq_spec = lambda shape: pl.BlockSpec(
        (None, None) + shape, lambda i: (i, layer) + (0,) * len(shape))
    assert WIN_ROWS == 4 * MLA_QB and KEYS_WIN == MLA_HALF
    return pl.pallas_call(
        _lat_mixer_kernel,
        out_shape=jax.ShapeDtypeStruct((n_tok, D_MODEL), F32),
        grid=(n_seq,),
        in_specs=[
            pl.BlockSpec(memory_space=pl.ANY),
            pl.BlockSpec((None, N_MOD, D_MODEL), lambda i: (i, 0, 0)),
            _const_spec((1, D_MODEL)),
            _const_spec((D_MODEL, D_IN_P), layer),
            _smem_spec(),
            _smem_spec(),
            _const_spec((1, RET_HEADS * HEAD_DIM)),
            _const_spec((1, MLA_KV_RANK)),
            _const_spec((MLA_KV_RANK, MLA_HEADS * LANES), layer),
            _const_spec((D_MODEL, D_MODEL), layer),
            seq_spec((PAST_LEN, LANES)),
            seq_spec((PAST_LEN, LANES)),
            seq_spec((PAST_LEN, MLA_KV_RANK)),
            seq_spec((PAST_LEN, LANES)),
            seq_spec((2, 2, LANES, LANES)),
        ] + [_const_spec((DEC_SEQ, LANES))] * 6,
        out_specs=pl.BlockSpec(memory_space=pl.ANY),
        scratch_shapes=[
            pltpu.VMEM((TOK, D_MODEL), BF16),
            pltpu.VMEM((TOK, D_MODEL), BF16),
            pltpu.VMEM((WIN_ROWS, KEYS_WIN), F32), pltpu.VMEM((WIN_ROWS, KEYS_WIN), F32),
            pltpu.VMEM((WIN_ROWS, KEYS_WIN), BF16), pltpu.VMEM((WIN_ROWS, KEYS_WIN), BF16),
        ] + _XStream.SCRATCH,
        compiler_params=pltpu.CompilerParams(
            dimension_semantics=("arbitrary",), vmem_limit_bytes=VMEM_LIMIT),
        name="lat_mixer",
    )(x, mod, n1, w_in_p, sink, decay, gn, kvn, w_kv_b, w_out_p, ck, cv, cckv, ckr, s0,
      *rope_a, *rope_m)


def _mlp_kernel(final, x_ref, mod_ref, n2_ref, wup_ref, wdn_ref, fn_ref, o_ref):
    x = x_ref[...]
    h2 = _norm_mod(x, n2_ref[...], mod_ref[4:5, :], mod_ref[3:4, :]).astype(BF16)
    acc = None
    for c in range(D_FF // FF_CHUNK):
        cols = slice(c * FF_CHUNK, (c + 1) * FF_CHUNK)
        u = jnp.maximum(_dot(h2, wup_ref[:, cols]), 0.0)
        part = _dot((u * u).astype(BF16), wdn_ref[cols, :])
        acc = part if acc is None else acc + part
    y = x + mod_ref[5:6, :] * acc
    if final:
        y = y * lax.rsqrt(jnp.mean(y * y, axis=-1, keepdims=True) + EPS) * fn_ref[...]
    o_ref[...] = y


def _mlp(layer, x, mod, n2, w_up, w_down, final_norm, final):
    n_tok = x.shape[0]
    per_mod = n_tok // mod.shape[0] // MLP_ROWS
    return pl.pallas_call(
        functools.partial(_mlp_kernel, final),
        out_shape=jax.ShapeDtypeStruct((n_tok, D_MODEL), F32),
        grid=(n_tok // MLP_ROWS,),
        in_specs=[
            pl.BlockSpec((MLP_ROWS, D_MODEL), lambda i: (i, 0)),
            pl.BlockSpec((None, N_MOD, D_MODEL), lambda i: (i // per_mod, 0, 0)),
            _const_spec((1, D_MODEL)),
            _const_spec((D_MODEL, D_FF), layer),
            _const_spec((D_FF, D_MODEL), layer),
            _const_spec((1, D_MODEL)),
        ],
        out_specs=pl.BlockSpec((MLP_ROWS, D_MODEL), lambda i: (i, 0)),
        compiler_params=pltpu.CompilerParams(
            dimension_semantics=("arbitrary",), vmem_limit_bytes=VMEM_LIMIT),
        name="mlp",
    )(x, mod, n2, w_up, w_down, final_norm)


def kernel(x_prompt, x_sample, cache_win_k, cache_win_v, cache_mla_ckv, cache_mla_krope, state_ret,
           c, c_ctx, w_mod, b_mod, norm1, norm2, w_in, win_sink, ret_decay, ret_gn, mla_kv_norm,
           w_kv_b, w_out, w_up, w_down, final_norm):
    n_ctx, n_lat = x_prompt.shape[0], x_sample.shape[0]

    w_in_p = _take_runs(w_in, _in_proj_columns(), 2).astype(BF16)
    w_out_p = _take_runs(w_out, _mix_rows(), 1).astype(BF16)
    w_kv_b16, w_up16, w_down16 = w_kv_b.astype(BF16), w_up.astype(BF16), w_down.astype(BF16)

    c_rows = jnp.zeros((16, D_MODEL), F32).at[0].set(c_ctx).at[1:1 + n_lat].set(c)
    mod = _modulation(c_rows, w_mod, b_mod).reshape(DEPTH, 16, N_MOD, D_MODEL)

    rope_a = _rope_tables(DEC_SEQ, HEAD_DIM, 0, HEAD_DIM)
    rope_m = _rope_tables(DEC_SEQ, MLA_ROPE, MLA_NOPE, LANES)

    ck = cache_win_k.reshape(n_lat, DEPTH, PAST_LEN, LANES)
    cv = cache_win_v.reshape(n_lat, DEPTH, PAST_LEN, LANES)
    ckr = jnp.pad(cache_mla_krope, ((0, 0), (0, 0), (0, 0), (MLA_NOPE, LANES - MLA_NOPE - MLA_ROPE)))
    sr = state_ret.reshape(n_lat, DEPTH, 2, 2, 2, HEAD_DIM, HEAD_DIM)
    zero = jnp.zeros_like(sr[:, :, :, :, 0])
    s0 = jnp.concatenate([jnp.concatenate([sr[:, :, :, :, 0], zero], axis=-1),
                          jnp.concatenate([zero, sr[:, :, :, :, 1]], axis=-1)], axis=-2)

    xp = x_prompt.reshape(n_ctx * SEQ, D_MODEL)
    xs = x_sample.reshape(n_lat * DEC_SEQ, D_MODEL)
    state = ()
    for l in range(DEPTH):
        last = l == DEPTH - 1
        shared = (norm1[l][None], w_in_p, win_sink[l], ret_decay[l], ret_gn[l][None],
                  mla_kv_norm[l][None], w_kv_b16, w_out_p)
        mod_ctx, mod_lat = mod[l, 0:1], mod[l, 1:1 + n_lat]
        xp, *state = _ctx_mixer(l, xp, mod_ctx[0], *shared, state)
        xp = _mlp(l, xp, mod_ctx, norm2[l][None], w_up16, w_down16, final_norm[None], last)
        xs = _lat_mixer(l, xs, mod_lat, *shared, ck, cv, cache_mla_ckv, ckr, s0, rope_a, rope_m)
        xs = _mlp(l, xs, mod_lat, norm2[l][None], w_up16, w_down16, final_norm[None], last)
    new_k, new_v, new_ckv, new_kr, new_s = state
    return (xp.reshape(n_ctx, SEQ, D_MODEL), xs.reshape(n_lat, DEC_SEQ, D_MODEL),
            new_k.reshape(n_ctx, DEPTH, SEQ, 2, HEAD_DIM), new_v.reshape(n_ctx, DEPTH, SEQ, 2, HEAD_DIM),
            new_ckv, new_kr, new_s)
```

```python
import functools

import numpy as np
import jax
import jax.numpy as jnp
from jax import lax
from jax.experimental import pallas as pl
from jax.experimental.pallas import tpu as pltpu

F32 = jnp.float32
BF16 = jnp.bfloat16

D_MODEL = 1024
DEPTH = 2
SEQ = 256
DEC_SEQ = 1024
PAST_LEN = 256
GRID_W = 64
HEAD_DIM = 64
ROPE_BASE = 10000.0
EPS = 1e-6
WIN_HEADS = 8
WINDOW = 128
ATTN_SCALE = HEAD_DIM ** -0.5
RET_HEADS = 4
RET_CHUNK = 128
RET_K_SCALE = HEAD_DIM ** -0.5
MLA_HEADS = 4
MLA_NOPE = 64
MLA_ROPE = 32
MLA_KV_RANK = 128
MLA_QK = MLA_NOPE + MLA_ROPE
MLA_SCALE = MLA_QK ** -0.5
D_IN = 2336
D_FF = 4 * D_MODEL
N_MOD = 6

LANES = 128
TOK = 1024
MLP_ROWS = 512
FF_CHUNK = 1024
VMEM_LIMIT = 60 * 1024 * 1024

C_QA, C_KA, C_VA, C_QB, C_KB, C_VB, C_GB, C_QC, C_CKV, C_KR, D_IN_P = (
    0, 512, 640, 768, 1024, 1280, 1536, 1792, 2304, 2432, 2560)
M_A, M_B, M_C = 0, 512, 768

NT_DIMS = (((1,), (1,)), ((), ()))


def _in_proj_columns():
    idx = []
    for g in range(4):
        idx += list(range(g * 64, (g + 1) * 64)) + list(range((4 + g) * 64, (5 + g) * 64))
    idx += list(range(512, 1792))
    for h in range(MLA_HEADS):
        idx += list(range(1792 + h * MLA_QK, 1792 + (h + 1) * MLA_QK)) + [-1] * 32
    idx += list(range(2176, 2304))
    idx += [-1] * 64 + list(range(2304, 2336)) + [-1] * 32
    return np.asarray(idx, np.int32)


def _mix_rows():
    idx = []
    for g in range(4):
        idx += list(range(g * 64, (g + 1) * 64)) + list(range((4 + g) * 64, (5 + g) * 64))
    idx += list(range(512, 1024))
    return np.asarray(idx, np.int32)


def _take_runs(w, idx, axis):
    pieces, i = [], 0
    while i < len(idx):
        j = i + 1
        if idx[i] < 0:
            while j < len(idx) and idx[j] < 0:
                j += 1
            shape = list(w.shape)
            shape[axis] = j - i
            pieces.append(jnp.zeros(shape, w.dtype))
        else:
            while j < len(idx) and idx[j] == idx[j - 1] + 1:
                j += 1
            pieces.append(lax.slice_in_dim(w, int(idx[i]), int(idx[j - 1]) + 1, axis=axis))
        i = j
    return jnp.concatenate(pieces, axis=axis)


def _rope_tables(n_tokens, dim, lane0, period):
    quarter = dim // 4
    t = np.arange(n_tokens)
    row = (t // GRID_W).astype(np.float64)
    col = (t % GRID_W).astype(np.float64)
    inv_freq = ROPE_BASE ** (-np.arange(quarter, dtype=np.float64) / quarter)
    ar, ac = row[:, None] * inv_freq, col[:, None] * inv_freq
    cos = np.concatenate([np.cos(ar), np.cos(ar), np.cos(ac), np.cos(ac)], axis=-1)
    sin = np.concatenate([np.sin(ar), np.sin(ar), np.sin(ac), np.sin(ac)], axis=-1)
    first = np.tile(np.concatenate([np.ones(quarter), np.zeros(quarter)]), 2)
    c = np.ones((n_tokens, LANES))
    sa = np.zeros((n_tokens, LANES))
    sb = np.zeros((n_tokens, LANES))
    for start in range(lane0, LANES, period):
        c[:, start:start + dim] = cos
        sa[:, start:start + dim] = -sin * first
        sb[:, start:start + dim] = sin * (1.0 - first)
    return tuple(jnp.asarray(a, F32) for a in (c, sa, sb))


def _lane_iota(shape):
    return lax.broadcasted_iota(jnp.int32, shape, len(shape) - 1)


def _row_iota(shape):
    return lax.broadcasted_iota(jnp.int32, shape, len(shape) - 2)


def _ds(start, size):
    if isinstance(start, int):
        return pl.ds(start, size)
    return pl.ds(pl.multiple_of(start, LANES), size)


def _norm_mod(x, gain, scale, shift):
    ms = jnp.mean(x * x, axis=-1, keepdims=True)
    return (x * lax.rsqrt(ms + EPS) * gain) * (1.0 + scale) + shift


def _log_sigmoid(x):
    return -(jnp.maximum(-x, 0.0) + jnp.log1p(jnp.exp(-jnp.abs(x))))


def _rope(x, c, sa, sb, quarter):
    return (x * c + pltpu.roll(x, LANES - quarter, 1) * sa + pltpu.roll(x, quarter, 1) * sb)


def _dot(a, b):
    return jnp.dot(a, b, preferred_element_type=F32)


def _dot_nt(a, b):
    return lax.dot_general(a, b, NT_DIMS, preferred_element_type=F32)


def _mod_kernel(c_ref, w_ref, b_ref, o_ref):
    cv = c_ref[...]
    s = cv * jax.nn.sigmoid(cv)
    o_ref[0] = _dot(s.astype(BF16), w_ref[0].astype(BF16)) + b_ref[0]


def _modulation(c_rows, w_mod, b_mod):
    tn = 1536
    nj = (N_MOD * D_MODEL) // tn
    return pl.pallas_call(
        _mod_kernel,
        out_shape=jax.ShapeDtypeStruct((DEPTH, 16, N_MOD * D_MODEL), F32),
        grid=(DEPTH, nj),
        in_specs=[
            pl.BlockSpec((16, D_MODEL), lambda l, j: (0, 0)),
            pl.BlockSpec((1, D_MODEL, tn), lambda l, j: (l, 0, j)),
            pl.BlockSpec((1, 1, tn), lambda l, j: (l, 0, j)),
        ],
        out_specs=pl.BlockSpec((1, 16, tn), lambda l, j: (l, 0, j)),
        compiler_params=pltpu.CompilerParams(
            dimension_semantics=("arbitrary", "arbitrary"), vmem_limit_bytes=VMEM_LIMIT),
        name="modulation",
    )(c_rows, w_mod, b_mod.reshape(DEPTH, 1, N_MOD * D_MODEL))


def _pipeline(n_steps, scores, softmax, values):
    scores(0, 0)
    scores(1, 1)
    softmax(0, 0)

    def body(i, carry):
        t = 2 * i + 1
        scores(t + 1, 0)
        softmax(t, 1)
        values(t - 1, 0)
        scores(t + 2, 1)
        softmax(t + 1, 0)
        values(t, 1)
        return carry

    lax.fori_loop(0, n_steps // 2 - 1, body, 0)
    softmax(n_steps - 1, 1)
    values(n_steps - 2, 0)
    values(n_steps - 1, 1)


def _softmax_tile(parts, floor=None, scale=None):
    m = None
    for s in parts:
        pm = jnp.max(s, axis=-1, keepdims=True)
        m = pm if m is None else jnp.maximum(m, pm)
    if floor is not None:
        m = jnp.maximum(m, floor)
    if scale is None:
        ps = [jnp.exp(s - m).astype(BF16) for s in parts]
    else:
        ps = [jnp.exp((s - m) * scale).astype(BF16) for s in parts]
    extra = None if floor is None else jnp.exp(floor - m)
    return ps, extra


def _retention_tables(decay_ref, rt_scr, m2_scr):
    shape = (RET_CHUNK, LANES)
    lane_lo = _lane_iota(shape) < 64
    row = _row_iota(shape)
    row_lo = row < 64
    i = row.astype(F32)
    rel = i - _lane_iota(shape).astype(F32)
    for j in range(2):
        df0, df1 = decay_ref[0, 2 * j], decay_ref[0, 2 * j + 1]
        db0, db1 = decay_ref[1, 2 * j], decay_ref[1, 2 * j + 1]
        lgf = _log_sigmoid(jnp.where(lane_lo, df0, df1))
        lgb = _log_sigmoid(jnp.where(lane_lo, db0, db1))
        rt_scr[j, 0] = jnp.exp((i + 1.0) * lgf)
        rt_scr[j, 1] = jnp.exp((RET_CHUNK - i) * lgb)
        rt_scr[j, 2] = jnp.exp((RET_CHUNK - 1.0 - i) * lgf)
        rt_scr[j, 3] = jnp.exp(i * lgb)
        rt_scr[j, 4] = jnp.exp(RET_CHUNK * _log_sigmoid(jnp.where(row_lo, df0, df1)))
        rt_scr[j, 5] = jnp.exp(RET_CHUNK * _log_sigmoid(jnp.where(row_lo, db0, db1)))
        for hh, (df, db) in enumerate(((df0, db0), (df1, db1))):
            lf = _log_sigmoid(jnp.full(shape, df, F32))
            lb = _log_sigmoid(jnp.full(shape, db, F32))
            low, upp = rel >= 0.0, rel <= 0.0
            m = (jnp.where(low, jnp.exp(jnp.where(low, rel, 0.0) * lf), 0.0)
                 + jnp.where(upp, jnp.exp(jnp.where(upp, -rel, 0.0) * lb), 0.0))
            m2_scr[j, :, hh * RET_CHUNK:(hh + 1) * RET_CHUNK] = m


N_CHUNK = TOK // RET_CHUNK


def _ret_scratch(n_seq, chunks_per_stage):
    per_chunk = lambda dtype: pltpu.VMEM((N_CHUNK, 2, 2 * RET_CHUNK, LANES), dtype)
    stage = lambda dtype: pltpu.VMEM((2 * chunks_per_stage, RET_CHUNK, 2 * LANES), dtype)
    return [
        pltpu.VMEM((TOK, 2 * LANES), BF16),
        pltpu.VMEM((TOK, 4 * LANES), BF16),
        pltpu.VMEM((TOK, 2 * LANES), F32),
        per_chunk(BF16),
        pltpu.VMEM((TOK, 2 * LANES), BF16),
        per_chunk(BF16),
        pltpu.VMEM((TOK, 2 * LANES), F32),
        pltpu.VMEM((2, 6, RET_CHUNK, LANES), F32),
        pltpu.VMEM((2, RET_CHUNK, 2 * LANES), F32),
        per_chunk(F32),
        per_chunk(BF16),
        pltpu.VMEM((n_seq, 2, 2, LANES, LANES), F32),
        stage(F32), stage(F32), stage(BF16), stage(BF16),
    ]


class _Retention:
    def __init__(self, scratch, gn_ref, mix_scr):
        (self.q, self.qd, self.k, self.kbd, self.v, self.vbd, self.g, self.rt, self.m2, self.upd,
         self.sall, self.st, rs0, rs1, rp0, rp1) = scratch
        self.rs, self.rp = (rs0, rs1), (rp0, rp1)
        self.gn_ref, self.mix = gn_ref, mix_scr

    def prepare(self, proj, decay_ref, seq_chunks):
        _retention_tables(decay_ref, self.rt, self.m2)
        lane_lo = _lane_iota((TOK, LANES)) < 64
        per_chunk = lambda a: a.reshape(N_CHUNK, RET_CHUNK, LANES)
        q = proj(C_QB, C_KB)
        k = proj(C_KB, C_VB) * RET_K_SCALE
        v = proj(C_VB, C_GB)
        self.g[...] = proj(C_GB, C_QC)
        self.q[...] = q.astype(BF16)
        self.k[...] = k
        self.v[...] = v.astype(BF16)
        for j in range(2):
            cols = slice(j * LANES, (j + 1) * LANES)
            q3 = per_chunk(q[:, cols])
            for d in range(2):
                self.qd[:, (2 * j + d) * LANES:(2 * j + d + 1) * LANES] = (
                    (q3 * self.rt[j, d]).reshape(TOK, LANES).astype(BF16))
            for src, dst in ((k[:, cols], self.kbd), (v[:, cols], self.vbd)):
                dst[:, j, :RET_CHUNK, :] = per_chunk(jnp.where(lane_lo, src, 0.0).astype(BF16))
                dst[:, j, RET_CHUNK:, :] = per_chunk(jnp.where(lane_lo, 0.0, src).astype(BF16))

        blockdiag2 = _lane_iota((2 * RET_CHUNK, LANES)) < 64
        blockdiag2 = blockdiag2 == ((_row_iota((2 * RET_CHUNK, LANES)) & (RET_CHUNK - 1)) < 64)

        def upd_body(c, carry):
            rows = _ds(c * RET_CHUNK, RET_CHUNK)
            for j in range(2):
                cols = slice(j * LANES, (j + 1) * LANES)
                k2 = self.k[rows, cols]
                kd = jnp.concatenate([k2 * self.rt[j, 2], k2 * self.rt[j, 3]], axis=1)
                upd = _dot(kd.T.astype(BF16), self.v[rows, cols])
                self.upd[c, j] = jnp.where(blockdiag2, upd, 0.0)
            return carry

        lax.fori_loop(0, N_CHUNK, upd_body, 0, unroll=4)

        n_per_seq = seq_chunks[0][1]
        assert all(n == n_per_seq for _, n in seq_chunks)

        def scan_body(t, carry):
            for si, (first, n) in enumerate(seq_chunks):
                cf, cb = first + t, first + n - 1 - t
                for j in range(2):
                    sf, sb = self.st[si, 0, j], self.st[si, 1, j]
                    self.sall[cf, j, :RET_CHUNK, :] = sf.astype(BF16)
                    self.sall[cb, j, RET_CHUNK:, :] = sb.astype(BF16)
                    self.st[si, 0, j] = self.rt[j, 4] * sf + self.upd[cf, j, :RET_CHUNK, :]
                    self.st[si, 1, j] = self.rt[j, 5] * sb + self.upd[cb, j, RET_CHUNK:, :]
            return carry

        lax.fori_loop(0, n_per_seq, scan_body, 0)

    def scores(self, chunks, b):
        for i, c in enumerate(chunks):
            rows = _ds(c * RET_CHUNK, RET_CHUNK)
            for j in range(2):
                self.rs[b][2 * i + j] = _dot_nt(self.q[rows, j * LANES:(j + 1) * LANES], self.kbd[c, j])

    def mask(self, chunks, b):
        for i in range(len(chunks)):
            for j in range(2):
                self.rp[b][2 * i + j] = (self.rs[b][2 * i + j] * self.m2[j]).astype(BF16)

    def values(self, chunks, b):
        lane_lo = _lane_iota((RET_CHUNK, LANES)) < 64
        for i, c in enumerate(chunks):
            rows = _ds(c * RET_CHUNK, RET_CHUNK)
            for j in range(2):
                cols = slice(j * LANES, (j + 1) * LANES)
                o = (_dot(self.rp[b][2 * i + j], self.vbd[c, j])
                     + _dot(self.qd[rows, 2 * j * LANES:2 * (j + 1) * LANES], self.sall[c, j]))
                s_lo = jnp.sum(jnp.where(lane_lo, o, 0.0), axis=-1, keepdims=True)
                s_hi = jnp.sum(jnp.where(lane_lo, 0.0, o), axis=-1, keepdims=True)
                d = o - jnp.where(lane_lo, s_lo, s_hi) * (1.0 / HEAD_DIM)
                dd = d * d
                v_lo = jnp.sum(jnp.where(lane_lo, dd, 0.0), axis=-1, keepdims=True)
                v_hi = jnp.sum(jnp.where(lane_lo, 0.0, dd), axis=-1, keepdims=True)
                var = jnp.where(lane_lo, v_lo, v_hi) * (1.0 / HEAD_DIM)
                g2 = self.g[rows, cols]
                y = d * lax.rsqrt(var + EPS) * self.gn_ref[:, cols] * (g2 * jax.nn.sigmoid(g2))
                self.mix[rows, M_B + j * LANES:M_B + (j + 1) * LANES] = y.astype(BF16)


X_ROWS = 256
N_XCHUNK = TOK // X_ROWS


class _XStream:
    SCRATCH = [pltpu.VMEM((N_XCHUNK, X_ROWS, D_MODEL), F32), pltpu.VMEM((2, X_ROWS, D_MODEL), F32),
               pltpu.SemaphoreType.DMA((N_XCHUNK,)), pltpu.SemaphoreType.DMA((2,))]

    def __init__(self, x_hbm, xo_hbm, scratch):
        self.x_hbm, self.xo_hbm = x_hbm, xo_hbm
        self.xin, self.xout, self.sem_in, self.sem_out = scratch
        self.step, self.n_steps = pl.program_id(0), pl.num_programs(0)

    @staticmethod
    def _rows(step, r):
        return pl.ds(pl.multiple_of(step * TOK + r * X_ROWS, X_ROWS), X_ROWS)

    def load(self, step, r):
        return pltpu.make_async_copy(self.x_hbm.at[self._rows(step, r), :], self.xin.at[r],
                                     self.sem_in.at[r])

    def store(self, r):
        return pltpu.make_async_copy(self.xout.at[r % 2], self.xo_hbm.at[self._rows(self.step, r), :],
                                     self.sem_out.at[r % 2])

    def prologue(self, mod_ref, n1_ref, h_scr):
        @pl.when(self.step == 0)
        def _():
            for r in range(N_XCHUNK):
                self.load(self.step, r).start()

        for r in range(N_XCHUNK):
            self.load(self.step, r).wait()
            h = _norm_mod(self.xin[r], n1_ref[...], mod_ref[1:2, :], mod_ref[0:1, :])
            h_scr[r * X_ROWS:(r + 1) * X_ROWS, :] = h.astype(BF16)

    def epilogue(self, mod_ref, wout_ref, mix_scr):
        for r in range(N_XCHUNK):
            if r >= 2:
                self.store(r - 2).wait()
            else:
                @pl.when(self.step > 0)
                def _():
                    self.store(r).wait()

            y = _dot(mix_scr[r * X_ROWS:(r + 1) * X_ROWS, :], wout_ref[...])
            self.xout[r % 2] = self.xin[r] + mod_ref[2:3, :] * y
            self.store(r).start()

            @pl.when(self.step + 1 < self.n_steps)
            def _():
                self.load(self.step + 1, r).start()

        @pl.when(self.step + 1 == self.n_steps)
        def _():
            for r in range(N_XCHUNK - 2, N_XCHUNK):
                self.store(r).wait()


CTX_PER_STEP = TOK // SEQ
CTX_WROWS = WIN_HEADS * SEQ
CTX_MROWS = MLA_HEADS * SEQ


def _ctx_mixer_kernel(n_alias, x_ref, mod_ref, n1_ref, win_ref, sink_ref, decay_ref, gn_ref, kvn_ref,
                      wkvb_ref, wout_ref, *refs):
    xo_ref, ko_ref, vo_ref, ckvo_ref, kro_ref, so_ref, h_scr, mix_scr = refs[n_alias:n_alias + 8]
    xs = _XStream(x_ref, xo_ref, refs[n_alias + 8:])
    xs.prologue(mod_ref, n1_ref, h_scr)
    per_seq = lambda a: a.reshape(CTX_PER_STEP, SEQ, a.shape[-1])
    if n_alias == 0:
        for ref in (ko_ref, vo_ref, ckvo_ref, kro_ref, so_ref):
            ref[:, 1:] = jnp.zeros((ref.shape[0], ref.shape[1] - 1) + ref.shape[2:], F32)
        ko_ref, vo_ref, ckvo_ref, kro_ref, so_ref = (
            ref.at[:, 0] for ref in (ko_ref, vo_ref, ckvo_ref, kro_ref, so_ref))
    chunks_per_seq = SEQ // RET_CHUNK

    def proj(c0, c1):
        return _dot(h_scr[...], win_ref[:, c0:c1])

    ret = []
    seq_chunks = lambda e: [chunks_per_seq * e + c for c in range(chunks_per_seq)]

    def attention(qst_scr, k_scr, vaug_scr, qc_scr, kcat_scr, kvaug_scr,
                  sw0, sw1, pw0, pw1, ew0, ew1, sm0, sm1, pm0, pm1):
        lane_lo = _lane_iota((TOK, LANES)) < 64
        ones = jnp.ones((TOK, LANES), BF16)
        qa = proj(C_QA, C_KA) * ATTN_SCALE
        for g in range(4):
            q = qa[:, g * LANES:(g + 1) * LANES]
            lo = jnp.where(lane_lo, q, 0.0).astype(BF16).reshape(CTX_PER_STEP, SEQ, LANES)
            hi = jnp.where(lane_lo, 0.0, q).astype(BF16).reshape(CTX_PER_STEP, SEQ, LANES)
            qst_scr[:, g * SEQ:(g + 1) * SEQ, :] = lo
            qst_scr[:, (4 + g) * SEQ:(5 + g) * SEQ, :] = hi
        kva = proj(C_KA, C_QB)
        ko_ref[...] = per_seq(kva[:, :LANES])
        vo_ref[...] = per_seq(kva[:, LANES:])
        k_scr[...] = kva[:, :LANES].astype(BF16)
        vaug_scr[:, :LANES] = kva[:, LANES:].astype(BF16)
        vaug_scr[:, LANES:] = ones
        qc = proj(C_QC, C_CKV).astype(BF16)
        for h in range(MLA_HEADS):
            qc_scr[h] = qc[:, h * LANES:(h + 1) * LANES]
        ckr = proj(C_CKV, D_IN_P)
        ckv, kr = ckr[:, :LANES], ckr[:, LANES:]
        ckv_n = ckv * lax.rsqrt(jnp.mean(ckv * ckv, axis=-1, keepdims=True) + EPS) * kvn_ref[...]
        ckvo_ref[...] = per_seq(ckv_n)
        kro_ref[...] = per_seq(kr[:, 64:64 + MLA_ROPE])
        kv = _dot(ckv_n.astype(BF16), wkvb_ref[...])
        for h in range(MLA_HEADS):
            kvh = kv[:, h * LANES:(h + 1) * LANES]
            kcat_scr[h] = jnp.where(lane_lo, kvh, kr).astype(BF16)
            kvaug_scr[h, :, :LANES] = kvh.astype(BF16)
            kvaug_scr[h, :, LANES:] = ones

        sw, pw, ew, sm, pm = (sw0, sw1), (pw0, pw1), (ew0, ew1), (sm0, sm1), (pm0, pm1)
        lane_lo_s = _lane_iota((SEQ, LANES)) < 64

        def scores(e, b):
            rows = _ds(e * SEQ, SEQ)
            sw[b][...] = _dot_nt(qst_scr[e], k_scr[rows, :])
            for h in range(MLA_HEADS):
                sm[b][h * SEQ:(h + 1) * SEQ, :] = _dot_nt(qc_scr[h, rows, :], kcat_scr[h, rows, :])
            ret[0].scores(seq_chunks(e), b)

        def softmax(e, b):
            for h in range(WIN_HEADS):
                hr = slice(h * SEQ, (h + 1) * SEQ)
                sink = jnp.full((SEQ, 1), sink_ref[h], F32)
                (p,), extra = _softmax_tile([sw[b][hr, :]], floor=sink)
                pw[b][hr, :] = p
                ew[b][hr, :] = jnp.broadcast_to(extra, (SEQ, LANES))
            for h in range(MLA_HEADS):
                hr = slice(h * SEQ, (h + 1) * SEQ)
                (p,), _ = _softmax_tile([sm[b][hr, :]], scale=MLA_SCALE)
                pm[b][hr, :] = p
            ret[0].mask(seq_chunks(e), b)

        def values(e, b):
            ret[0].values(seq_chunks(e), b)
            rows = _ds(e * SEQ, SEQ)
            oa = _dot(pw[b][...], vaug_scr[rows, :])
            o = oa[:, :LANES] / (oa[:, LANES:] + ew[b][...])
            for g in range(4):
                merged = jnp.where(lane_lo_s, o[g * SEQ:(g + 1) * SEQ], o[(4 + g) * SEQ:(5 + g) * SEQ])
                mix_scr[rows, M_A + g * LANES:M_A + (g + 1) * LANES] = merged.astype(BF16)
            for jp in range(2):
                outs = []
                for h in (2 * jp, 2 * jp + 1):
                    oc = _dot(pm[b][h * SEQ:(h + 1) * SEQ, :], kvaug_scr[h, rows, :])
                    outs.append(oc[:, :LANES] / oc[:, LANES:])
                merged = jnp.where(lane_lo_s, pltpu.roll(outs[0], 64, 1), outs[1])
                mix_scr[rows, M_C + jp * LANES:M_C + (jp + 1) * LANES] = merged.astype(BF16)

        _pipeline(CTX_PER_STEP, scores, softmax, values)

    attention_scratch = (
        pltpu.VMEM((CTX_PER_STEP, CTX_WROWS, LANES), BF16),
        pltpu.VMEM((TOK, LANES), BF16),
        pltpu.VMEM((TOK, 2 * LANES), BF16),
        pltpu.VMEM((MLA_HEADS, TOK, LANES), BF16),
        pltpu.VMEM((MLA_HEADS, TOK, LANES), BF16),
        pltpu.VMEM((MLA_HEADS, TOK, 2 * LANES), BF16),
        pltpu.VMEM((CTX_WROWS, SEQ), F32), pltpu.VMEM((CTX_WROWS, SEQ), F32),
        pltpu.VMEM((CTX_WROWS, SEQ), BF16), pltpu.VMEM((CTX_WROWS, SEQ), BF16),
        pltpu.VMEM((CTX_WROWS, LANES), F32), pltpu.VMEM((CTX_WROWS, LANES), F32),
        pltpu.VMEM((CTX_MROWS, SEQ), F32), pltpu.VMEM((CTX_MROWS, SEQ), F32),
        pltpu.VMEM((CTX_MROWS, SEQ), BF16), pltpu.VMEM((CTX_MROWS, SEQ), BF16),
    )

    def retention(*scratch):
        r = _Retention(scratch, gn_ref, mix_scr)
        r.st[...] = jnp.zeros(r.st.shape, F32)
        r.prepare(proj, decay_ref, [(chunks_per_seq * e, chunks_per_seq) for e in range(CTX_PER_STEP)])
        for e in range(CTX_PER_STEP):
            for dirn in range(2):
                for j in range(2):
                    st = r.st[e, dirn, j]
                    so_ref[e, dirn, 2 * j] = st[:64, :64]
                    so_ref[e, dirn, 2 * j + 1] = st[64:, 64:]
        ret.append(r)
        pl.run_scoped(attention, *attention_scratch)

    pl.run_scoped(retention, *_ret_scratch(CTX_PER_STEP, chunks_per_seq))
    xs.epilogue(mod_ref, wout_ref, mix_scr)


def _const_spec(shape, layer=None):
    if layer is None:
        return pl.BlockSpec(shape, lambda i: (0,) * len(shape), pipeline_mode=pl.Buffered(1))
    return pl.BlockSpec((None,) + shape, lambda i: (layer,) + (0,) * len(shape),
                        pipeline_mode=pl.Buffered(1))


def _smem_spec():
    return pl.BlockSpec(memory_space=pltpu.SMEM)


def _ctx_mixer(layer, x, mod, n1, w_in_p, sink, decay, gn, kvn, w_kv_b, w_out_p, prev_state):
    n_tok = x.shape[0]
    n_seq = n_tok // SEQ
    hbm_spec = pl.BlockSpec(memory_space=pl.ANY)
    if prev_state:
        state_spec = lambda *tail: pl.BlockSpec(
            (CTX_PER_STEP, None) + tail, lambda i: (i, layer) + (0,) * len(tail))
    else:
        assert layer == 0
        state_spec = lambda *tail: pl.BlockSpec(
            (CTX_PER_STEP, DEPTH) + tail, lambda i: (i, 0) + (0,) * len(tail))
    state_tails = [(SEQ, LANES), (SEQ, LANES), (SEQ, MLA_KV_RANK), (SEQ, MLA_ROPE),
                   (2, RET_HEADS, HEAD_DIM, HEAD_DIM)]
    n_in = 10
    return pl.pallas_call(
        functools.partial(_ctx_mixer_kernel, len(prev_state)),
        out_shape=[jax.ShapeDtypeStruct((n_tok, D_MODEL), F32)] + [
            jax.ShapeDtypeStruct((n_seq, DEPTH) + tail, F32) for tail in state_tails],
        grid=(n_tok // TOK,),
        in_specs=[
            hbm_spec,
            _const_spec((N_MOD, D_MODEL)),
            _const_spec((1, D_MODEL)),
            _const_spec((D_MODEL, D_IN_P), layer),
            _smem_spec(),
            _smem_spec(),
            _const_spec((1, RET_HEADS * HEAD_DIM)),
            _const_spec((1, MLA_KV_RANK)),
            _const_spec((MLA_KV_RANK, MLA_HEADS * LANES), layer),
            _const_spec((D_MODEL, D_MODEL), layer),
        ] + [hbm_spec] * len(prev_state),
        out_specs=[hbm_spec] + [state_spec(*tail) for tail in state_tails],
        input_output_aliases={n_in + k: 1 + k for k in range(len(prev_state))},
        scratch_shapes=[
            pltpu.VMEM((TOK, D_MODEL), BF16),
            pltpu.VMEM((TOK, D_MODEL), BF16),
        ] + _XStream.SCRATCH,
        compiler_params=pltpu.CompilerParams(
            dimension_semantics=("arbitrary",), vmem_limit_bytes=VMEM_LIMIT),
        name="ctx_mixer",
    )(x, mod, n1, w_in_p, sink, decay, gn, kvn, w_kv_b, w_out_p, *prev_state)


N_BLK = DEC_SEQ // LANES
KEYS_LOC = 3 * LANES
KEYS_WIN = KEYS_LOC + PAST_LEN
WIN_ROWS = WIN_HEADS * LANES
MLA_QB = 256
MLA_KEYS = DEC_SEQ + PAST_LEN
MLA_HALF = MLA_KEYS // 2


def _lat_mixer_kernel(x_ref, mod_ref, n1_ref, win_ref, sink_ref, decay_ref, gn_ref, kvn_ref,
                      wkvb_ref, wout_ref, ck_ref, cv_ref, cckv_ref, ckr_ref, s0_ref,
                      rc_ref, rsa_ref, rsb_ref, mc_ref, msa_ref, msb_ref,
                      xo_ref,
                      h_scr, mix_scr, s0_scr, s1_scr, p0_scr, p1_scr, *x_scratch):
    xs = _XStream(x_ref, xo_ref, x_scratch)
    xs.prologue(mod_ref, n1_ref, h_scr)
    sbuf, pbuf = (s0_scr, s1_scr), (p0_scr, p1_scr)

    def proj(c0, c1):
        return _dot(h_scr[...], win_ref[:, c0:c1])

    ret = []

    def window(qst_scr, kpad_scr, vaug_scr, ckb_scr, cvaug_scr, bias_scr, e0_scr, e1_scr):
        ebuf = (e0_scr, e1_scr)
        lane_lo = _lane_iota((TOK, LANES)) < 64
        rc, rsa, rsb = rc_ref[...], rsa_ref[...], rsb_ref[...]
        qa = proj(C_QA, C_KA)
        for g in range(4):
            q = _rope(qa[:, g * LANES:(g + 1) * LANES], rc, rsa, rsb, 16) * ATTN_SCALE
            lo = jnp.where(lane_lo, q, 0.0).astype(BF16).reshape(N_BLK, LANES, LANES)
            hi = jnp.where(lane_lo, 0.0, q).astype(BF16).reshape(N_BLK, LANES, LANES)
            qst_scr[:, g * LANES:(g + 1) * LANES, :] = lo
            qst_scr[:, (4 + g) * LANES:(5 + g) * LANES, :] = hi
        kva = proj(C_KA, C_QB)
        zpad = jnp.zeros((LANES, LANES), BF16)
        kpad_scr[0:LANES, :] = zpad
        kpad_scr[LANES + TOK:, :] = zpad
        vaug_scr[0:LANES, :LANES] = zpad
        vaug_scr[LANES + TOK:, :LANES] = zpad
        kpad_scr[LANES:LANES + TOK, :] = _rope(kva[:, :LANES], rc, rsa, rsb, 16).astype(BF16)
        vaug_scr[LANES:LANES + TOK, :LANES] = kva[:, LANES:].astype(BF16)
        vaug_scr[:, LANES:] = jnp.ones((TOK + 2 * LANES, LANES), BF16)
        ckb_scr[...] = ck_ref[...].astype(BF16)
        cvaug_scr[:, :LANES] = cv_ref[...].astype(BF16)
        cvaug_scr[:, LANES:] = jnp.ones((PAST_LEN, LANES), BF16)
        qi, kj = _row_iota((LANES, LANES)), _lane_iota((LANES, LANES))
        bias_scr[0] = jnp.full((LANES, LANES), -jnp.inf, F32)
        bias_scr[1] = jnp.where(kj >= qi, 0.0, -jnp.inf)
        bias_scr[2] = jnp.where(kj <= qi, 0.0, -jnp.inf)
        lane_lo_b = _lane_iota((LANES, LANES)) < 64

        def scores(n, b):
            q = qst_scr[n]
            sbuf[b][:, :KEYS_LOC] = _dot_nt(q, kpad_scr[_ds(n * LANES, KEYS_LOC), :])
            sbuf[b][:, KEYS_LOC:] = _dot_nt(q, ckb_scr[...])
            ret[0].scores([n], b)

        def softmax(n, b):
            if isinstance(n, int):
                i_prev, i_next = (1 if n > 0 else 0), (2 if n < N_BLK - 1 else 0)
            else:
                i_prev, i_next = jnp.where(n > 0, 1, 0), jnp.where(n < N_BLK - 1, 2, 0)
            b_prev, b_next = bias_scr[i_prev], bias_scr[i_next]
            for h in range(WIN_HEADS):
                hr = slice(h * LANES, (h + 1) * LANES)
                parts = [sbuf[b][hr, 0:LANES] + b_prev, sbuf[b][hr, LANES:2 * LANES],
                         sbuf[b][hr, 2 * LANES:KEYS_LOC] + b_next, sbuf[b][hr, KEYS_LOC:]]
                ps, extra = _softmax_tile(parts, floor=jnp.full((LANES, 1), sink_ref[h], F32))
                pbuf[b][hr, 0:LANES] = ps[0]
                pbuf[b][hr, LANES:2 * LANES] = ps[1]
                pbuf[b][hr, 2 * LANES:KEYS_LOC] = ps[2]
                pbuf[b][hr, KEYS_LOC:] = ps[3]
                ebuf[b][hr, :] = jnp.broadcast_to(extra, (LANES, LANES))
            ret[0].mask([n], b)

        def values(n, b):
            ret[0].values([n], b)
            oa = (_dot(pbuf[b][:, :KEYS_LOC], vaug_scr[_ds(n * LANES, KEYS_LOC), :])
                  + _dot(pbuf[b][:, KEYS_LOC:], cvaug_scr[...]))
            o = oa[:, :LANES] / (oa[:, LANES:] + ebuf[b][...])
            for g in range(4):
                merged = jnp.where(lane_lo_b, o[g * LANES:(g + 1) * LANES],
                                   o[(4 + g) * LANES:(5 + g) * LANES])
                mix_scr[_ds(n * LANES, LANES), M_A + g * LANES:M_A + (g + 1) * LANES] = (
                    merged.astype(BF16))

        _pipeline(N_BLK, scores, softmax, values)

    window_scratch = (
        pltpu.VMEM((N_BLK, WIN_ROWS, LANES), BF16),
        pltpu.VMEM((TOK + 2 * LANES, LANES), BF16),
        pltpu.VMEM((TOK + 2 * LANES, 2 * LANES), BF16),
        pltpu.VMEM((PAST_LEN, LANES), BF16),
        pltpu.VMEM((PAST_LEN, 2 * LANES), BF16),
        pltpu.VMEM((3, LANES, LANES), F32),
        pltpu.VMEM((WIN_ROWS, LANES), F32), pltpu.VMEM((WIN_ROWS, LANES), F32),
    )

    def retention(*scratch):
        r = _Retention(scratch, gn_ref, mix_scr)
        r.st[0] = s0_ref[...]
        r.prepare(proj, decay_ref, [(0, N_CHUNK)])
        ret.append(r)
        pl.run_scoped(window, *window_scratch)

    assert N_CHUNK == N_BLK
    pl.run_scoped(retention, *_ret_scratch(1, 1))

    def latent(qc_scr, kcat_scr, kvaug_scr, mixc_scr):
        lane_lo = _lane_iota((TOK, LANES)) < 64
        mc, msa, msb = mc_ref[...], msa_ref[...], msb_ref[...]
        qc = proj(C_QC, C_CKV)
        for h in range(MLA_HEADS):
            qc_scr[h] = _rope(qc[:, h * LANES:(h + 1) * LANES], mc, msa, msb, 8).astype(BF16)
        ckr = proj(C_CKV, D_IN_P)
        ckv = ckr[:, :LANES]
        kr = _rope(ckr[:, LANES:], mc, msa, msb, 8)
        ckv_n = ckv * lax.rsqrt(jnp.mean(ckv * ckv, axis=-1, keepdims=True) + EPS) * kvn_ref[...]
        kv = _dot(ckv_n.astype(BF16), wkvb_ref[...])
        kv_c = _dot(cckv_ref[...].astype(BF16), wkvb_ref[...])
        kr_c = ckr_ref[...]
        lane_lo_c = _lane_iota((PAST_LEN, LANES)) < 64
        for h in range(MLA_HEADS):
            kvh, kvh_c = kv[:, h * LANES:(h + 1) * LANES], kv_c[:, h * LANES:(h + 1) * LANES]
            kcat_scr[h, 0:TOK, :] = jnp.where(lane_lo, kvh, kr).astype(BF16)
            kcat_scr[h, TOK:, :] = jnp.where(lane_lo_c, kvh_c, kr_c).astype(BF16)
            kvaug_scr[h, 0:TOK, :LANES] = kvh.astype(BF16)
            kvaug_scr[h, TOK:, :LANES] = kvh_c.astype(BF16)
            kvaug_scr[h, :, LANES:] = jnp.ones((MLA_KEYS, LANES), BF16)
        lane_lo_m = _lane_iota((MLA_QB, LANES)) < 64
        n_qb = DEC_SEQ // MLA_QB

        def split(t):
            if isinstance(t, int):
                return t // n_qb, t % n_qb
            return lax.shift_right_logical(t, 2), lax.bitwise_and(t, n_qb - 1)

        def scores(t, b):
            jp, qb = split(t)
            for i in range(2):
                h = 2 * jp + i
                q = qc_scr[h, _ds(qb * MLA_QB, MLA_QB), :]
                for part in range(2):
                    r0 = (2 * i + part) * MLA_QB
                    sbuf[b][r0:r0 + MLA_QB, :] = _dot_nt(
                        q, kcat_scr[h, part * MLA_HALF:(part + 1) * MLA_HALF, :])

        def softmax(t, b):
            for i in range(2):
                for rt in range(MLA_QB // LANES):
                    ra = 2 * i * MLA_QB + rt * LANES
                    rb = ra + MLA_QB
                    ps, _ = _softmax_tile([sbuf[b][ra:ra + LANES, :], sbuf[b][rb:rb + LANES, :]],
                                          scale=MLA_SCALE)
                    pbuf[b][ra:ra + LANES, :] = ps[0]
                    pbuf[b][rb:rb + LANES, :] = ps[1]

        def values(t, b):
            jp, qb = split(t)
            outs = []
            for i in range(2):
                h = 2 * jp + i
                r0 = 2 * i * MLA_QB
                oc = (_dot(pbuf[b][r0:r0 + MLA_QB, :], kvaug_scr[h, 0:MLA_HALF, :])
                      + _dot(pbuf[b][r0 + MLA_QB:r0 + 2 * MLA_QB, :], kvaug_scr[h, MLA_HALF:, :]))
                outs.append(oc[:, :LANES] / oc[:, LANES:])
            merged = jnp.where(lane_lo_m, pltpu.roll(outs[0], 64, 1), outs[1])
            mixc_scr[jp, _ds(qb * MLA_QB, MLA_QB), :] = merged.astype(BF16)

        _pipeline(2 * n_qb, scores, softmax, values)
        for jp in range(2):
            mix_scr[:, M_C + jp * LANES:M_C + (jp + 1) * LANES] = mixc_scr[jp]

    pl.run_scoped(
        latent,
        pltpu.VMEM((MLA_HEADS, TOK, LANES), BF16),
        pltpu.VMEM((MLA_HEADS, MLA_KEYS, LANES), BF16),
        pltpu.VMEM((MLA_HEADS, MLA_KEYS, 2 * LANES), BF16),
        pltpu.VMEM((2, TOK, LANES), BF16),
    )
    xs.epilogue(mod_ref, wout_ref, mix_scr)


def _lat_mixer(layer, x, mod, n1, w_in_p, sink, decay, gn, kvn, w_kv_b, w_out_p,
               ck, cv, cckv, ckr, s0, rope_a, rope_m):
    n_tok = x.shape[0]
    n_seq = n_tok // DEC_SEQ
    seq_spec = lambda shape: pl.BlockSpec(
        (None, None) + shape, lambda i: (i, layer) + (0,) * len(shape))
    assert WIN_ROWS == 4 * MLA_QB and KEYS_WIN == MLA_HALF
    return pl.pallas_call(
        _lat_mixer_kernel,
        out_shape=jax.ShapeDtypeStruct((n_tok, D_MODEL), F32),
        grid=(n_seq,),
        in_specs=[
            pl.BlockSpec(memory_space=pl.ANY),
            pl.BlockSpec((None, N_MOD, D_MODEL), lambda i: (i, 0, 0)),
            _const_spec((1, D_MODEL)),
            _const_spec((D_MODEL, D_IN_P), layer),
            _smem_spec(),
            _smem_spec(),
            _const_spec((1, RET_HEADS * HEAD_DIM)),
            _const_spec((1, MLA_KV_RANK)),
            _const_spec((MLA_KV_RANK, MLA_HEADS * LANES), layer),
            _const_spec((D_MODEL, D_MODEL), layer),
            seq_spec((PAST_LEN, LANES)),
            seq_spec((PAST_LEN, LANES)),
            seq_spec((PAST_LEN, MLA_KV_RANK)),
            seq_spec((PAST_LEN, LANES)),
            seq_spec((2, 2, LANES, LANES)),
        ] + [_const_spec((DEC_SEQ, LANES))] * 6,
        out_specs=pl.BlockSpec(memory_space=pl.ANY),
        scratch_shapes=[
            pltpu.VMEM((TOK, D_MODEL), BF16),
            pltpu.VMEM((TOK, D_MODEL), BF16),
            pltpu.VMEM((WIN_ROWS, KEYS_WIN), F32), pltpu.VMEM((WIN_ROWS, KEYS_WIN), F32),
            pltpu.VMEM((WIN_ROWS, KEYS_WIN), BF16), pltpu.VMEM((WIN_ROWS, KEYS_WIN), BF16),
        ] + _XStream.SCRATCH,
        compiler_params=pltpu.CompilerParams(
            dimension_semantics=("arbitrary",), vmem_limit_bytes=VMEM_LIMIT),
        name="lat_mixer",
    )(x, mod, n1, w_in_p, sink, decay, gn, kvn, w_kv_b, w_out_p, ck, cv, cckv, ckr, s0,
      *rope_a, *rope_m)


def _mlp_kernel(final, x_ref, mod_ref, n2_ref, wup_ref, wdn_ref, fn_ref, o_ref):
    x = x_ref[...]
    h2 = _norm_mod(x, n2_ref[...], mod_ref[4:5, :], mod_ref[3:4, :]).astype(BF16)
    acc = None
    for c in range(D_FF // FF_CHUNK):
        cols = slice(c * FF_CHUNK, (c + 1) * FF_CHUNK)
        u = jnp.maximum(_dot(h2, wup_ref[:, cols]), 0.0)
        part = _dot((u * u).astype(BF16), wdn_ref[cols, :])
        acc = part if acc is None else acc + part
    y = x + mod_ref[5:6, :] * acc
    if final:
        y = y * lax.rsqrt(jnp.mean(y * y, axis=-1, keepdims=True) + EPS) * fn_ref[...]
    o_ref[...] = y


def _mlp(layer, x, mod, n2, w_up, w_down, final_norm, final):
    n_tok = x.shape[0]
    per_mod = n_tok // mod.shape[0] // MLP_ROWS
    return pl.pallas_call(
        functools.partial(_mlp_kernel, final),
        out_shape=jax.ShapeDtypeStruct((n_tok, D_MODEL), F32),
        grid=(n_tok // MLP_ROWS,),
        in_specs=[
            pl.BlockSpec((MLP_ROWS, D_MODEL), lambda i: (i, 0)),
            pl.BlockSpec((None, N_MOD, D_MODEL), lambda i: (i // per_mod, 0, 0)),
            _const_spec((1, D_MODEL)),
            _const_spec((D_MODEL, D_FF), layer),
            _const_spec((D_FF, D_MODEL), layer),
            _const_spec((1, D_MODEL)),
        ],
        out_specs=pl.BlockSpec((MLP_ROWS, D_MODEL), lambda i: (i, 0)),
        compiler_params=pltpu.CompilerParams(
            dimension_semantics=("arbitrary",), vmem_limit_bytes=VMEM_LIMIT),
        name="mlp",
    )(x, mod, n2, w_up, w_down, final_norm)


def kernel(x_prompt, x_sample, cache_win_k, cache_win_v, cache_mla_ckv, cache_mla_krope, state_ret,
           c, c_ctx, w_mod, b_mod, norm1, norm2, w_in, win_sink, ret_decay, ret_gn, mla_kv_norm,
           w_kv_b, w_out, w_up, w_down, final_norm):
    n_ctx, n_lat = x_prompt.shape[0], x_sample.shape[0]

    w_in_p = _take_runs(w_in, _in_proj_columns(), 2).astype(BF16)
    w_out_p = _take_runs(w_out, _mix_rows(), 1).astype(BF16)
    w_kv_b16, w_up16, w_down16 = w_kv_b.astype(BF16), w_up.astype(BF16), w_down.astype(BF16)

    c_rows = jnp.zeros((16, D_MODEL), F32).at[0].set(c_ctx).at[1:1 + n_lat].set(c)
    mod = _modulation(c_rows, w_mod, b_mod).reshape(DEPTH, 16, N_MOD, D_MODEL)

    rope_a = _rope_tables(DEC_SEQ, HEAD_DIM, 0, HEAD_DIM)
    rope_m = _rope_tables(DEC_SEQ, MLA_ROPE, MLA_NOPE, LANES)

    ck = cache_win_k.reshape(n_lat, DEPTH, PAST_LEN, LANES)
    cv = cache_win_v.reshape(n_lat, DEPTH, PAST_LEN, LANES)
    ckr = jnp.pad(cache_mla_krope, ((0, 0), (0, 0), (0, 0), (MLA_NOPE, LANES - MLA_NOPE - MLA_ROPE)))
    sr = state_ret.reshape(n_lat, DEPTH, 2, 2, 2, HEAD_DIM, HEAD_DIM)
    zero = jnp.zeros_like(sr[:, :, :, :, 0])
    s0 = jnp.concatenate([jnp.concatenate([sr[:, :, :, :, 0], zero], axis=-1),
                          jnp.concatenate([zero, sr[:, :, :, :, 1]], axis=-1)], axis=-2)

    xp = x_prompt.reshape(n_ctx * SEQ, D_MODEL)
    xs = x_sample.reshape(n_lat * DEC_SEQ, D_MODEL)
    state = ()
    for l in range(DEPTH):
        last = l == DEPTH - 1
        shared = (norm1[l][None], w_in_p, win_sink[l], ret_decay[l], ret_gn[l][None],
                  mla_kv_norm[l][None], w_kv_b16, w_out_p)
        mod_ctx, mod_lat = mod[l, 0:1], mod[l, 1:1 + n_lat]
        xp, *state = _ctx_mixer(l, xp, mod_ctx[0], *shared, state)
        xp = _mlp(l, xp, mod_ctx, norm2[l][None], w_up16, w_down16, final_norm[None], last)
        xs = _lat_mixer(l, xs, mod_lat, *shared, ck, cv, cache_mla_ckv, ckr, s0, rope_a, rope_m)
        xs = _mlp(l, xs, mod_lat, norm2[l][None], w_up16, w_down16, final_norm[None], last)
    new_k, new_v, new_ckv, new_kr, new_s = state
    return (xp.reshape(n_ctx, SEQ, D_MODEL), xs.reshape(n_lat, DEC_SEQ, D_MODEL),
            new_k.reshape(n_ctx, DEPTH, SEQ, 2, HEAD_DIM), new_v.reshape(n_ctx, DEPTH, SEQ, 2, HEAD_DIM),
            new_ckv, new_kr, new_s)
```

```python
import functools

import numpy as np
import jax
import jax.numpy as jnp
from jax import lax
from jax.experimental import pallas as pl
from jax.experimental.pallas import tpu as pltpu

F32 = jnp.float32
BF16 = jnp.bfloat16

D_MODEL = 1024
DEPTH = 2
SEQ = 256
DEC_SEQ = 1024
PAST_LEN = 256
GRID_W = 64
HEAD_DIM = 64
ROPE_BASE = 10000.0
EPS = 1e-6
WIN_HEADS = 8
WINDOW = 128
ATTN_SCALE = HEAD_DIM ** -0.5
RET_HEADS = 4
RET_CHUNK = 128
RET_K_SCALE = HEAD_DIM ** -0.5
MLA_HEADS = 4
MLA_NOPE = 64
MLA_ROPE = 32
MLA_KV_RANK = 128
MLA_QK = MLA_NOPE + MLA_ROPE
MLA_SCALE = MLA_QK ** -0.5
D_IN = 2336
D_FF = 4 * D_MODEL
N_MOD = 6

LANES = 128
TOK = 1024
MLP_ROWS = 1024
FF_CHUNK = 1024
VMEM_LIMIT = 60 * 1024 * 1024

C_QA, C_KA, C_VA, C_QB, C_KB, C_VB, C_GB, C_QC, C_CKV, C_KR, D_IN_P = (
    0, 512, 640, 768, 1024, 1280, 1536, 1792, 2304, 2432, 2560)
M_A, M_B, M_C = 0, 512, 768

NT_DIMS = (((1,), (1,)), ((), ()))


def _in_proj_columns():
    idx = []
    for g in range(4):
        idx += list(range(g * 64, (g + 1) * 64)) + list(range((4 + g) * 64, (5 + g) * 64))
    idx += list(range(512, 1792))
    for h in range(MLA_HEADS):
        idx += list(range(1792 + h * MLA_QK, 1792 + (h + 1) * MLA_QK)) + [-1] * 32
    idx += list(range(2176, 2304))
    idx += [-1] * 64 + list(range(2304, 2336)) + [-1] * 32
    return np.asarray(idx, np.int32)


def _mix_rows():
    idx = []
    for g in range(4):
        idx += list(range(g * 64, (g + 1) * 64)) + list(range((4 + g) * 64, (5 + g) * 64))
    idx += list(range(512, 1024))
    return np.asarray(idx, np.int32)


def _take_runs(w, idx, axis):
    pieces, i = [], 0
    while i < len(idx):
        j = i + 1
        if idx[i] < 0:
            while j < len(idx) and idx[j] < 0:
                j += 1
            shape = list(w.shape)
            shape[axis] = j - i
            pieces.append(jnp.zeros(shape, w.dtype))
        else:
            while j < len(idx) and idx[j] == idx[j - 1] + 1:
                j += 1
            pieces.append(lax.slice_in_dim(w, int(idx[i]), int(idx[j - 1]) + 1, axis=axis))
        i = j
    return jnp.concatenate(pieces, axis=axis)


def _rope_tables(n_tokens, dim, lane0, period):
    quarter = dim // 4
    t = np.arange(n_tokens)
    row = (t // GRID_W).astype(np.float64)
    col = (t % GRID_W).astype(np.float64)
    inv_freq = ROPE_BASE ** (-np.arange(quarter, dtype=np.float64) / quarter)
    ar, ac = row[:, None] * inv_freq, col[:, None] * inv_freq
    cos = np.concatenate([np.cos(ar), np.cos(ar), np.cos(ac), np.cos(ac)], axis=-1)
    sin = np.concatenate([np.sin(ar), np.sin(ar), np.sin(ac), np.sin(ac)], axis=-1)
    first = np.tile(np.concatenate([np.ones(quarter), np.zeros(quarter)]), 2)
    c = np.ones((n_tokens, LANES))
    sa = np.zeros((n_tokens, LANES))
    sb = np.zeros((n_tokens, LANES))
    for start in range(lane0, LANES, period):
        c[:, start:start + dim] = cos
        sa[:, start:start + dim] = -sin * first
        sb[:, start:start + dim] = sin * (1.0 - first)
    return tuple(jnp.asarray(a, F32) for a in (c, sa, sb))


def _lane_iota(shape):
    return lax.broadcasted_iota(jnp.int32, shape, len(shape) - 1)


def _row_iota(shape):
    return lax.broadcasted_iota(jnp.int32, shape, len(shape) - 2)


def _ds(start, size):
    if isinstance(start, int):
        return pl.ds(start, size)
    return pl.ds(pl.multiple_of(start, LANES), size)


def _norm_mod(x, gain, scale, shift):
    ms = jnp.mean(x * x, axis=-1, keepdims=True)
    return (x * lax.rsqrt(ms + EPS) * gain) * (1.0 + scale) + shift


def _log_sigmoid(x):
    return -(jnp.maximum(-x, 0.0) + jnp.log1p(jnp.exp(-jnp.abs(x))))


def _rope(x, c, sa, sb, quarter):
    return (x * c + pltpu.roll(x, LANES - quarter, 1) * sa + pltpu.roll(x, quarter, 1) * sb)


def _dot(a, b):
    return jnp.dot(a, b, preferred_element_type=F32)


def _dot_nt(a, b):
    return lax.dot_general(a, b, NT_DIMS, preferred_element_type=F32)


def _mod_kernel(c_ref, w_ref, b_ref, o_ref):
    cv = c_ref[...]
    s = cv * jax.nn.sigmoid(cv)
    o_ref[0] = _dot(s.astype(BF16), w_ref[0].astype(BF16)) + b_ref[0]


def _modulation(c_rows, w_mod, b_mod):
    tn = 1536
    nj = (N_MOD * D_MODEL) // tn
    return pl.pallas_call(
        _mod_kernel,
        out_shape=jax.ShapeDtypeStruct((DEPTH, 16, N_MOD * D_MODEL), F32),
        grid=(DEPTH, nj),
        in_specs=[
            pl.BlockSpec((16, D_MODEL), lambda l, j: (0, 0)),
            pl.BlockSpec((1, D_MODEL, tn), lambda l, j: (l, 0, j)),
            pl.BlockSpec((1, 1, tn), lambda l, j: (l, 0, j)),
        ],
        out_specs=pl.BlockSpec((1, 16, tn), lambda l, j: (l, 0, j)),
        compiler_params=pltpu.CompilerParams(
            dimension_semantics=("arbitrary", "arbitrary"), vmem_limit_bytes=VMEM_LIMIT),
        name="modulation",
    )(c_rows, w_mod, b_mod.reshape(DEPTH, 1, N_MOD * D_MODEL))


W_IN_ROWS = 256


def _in_proj_layout_kernel(w_ref, o_ref):
    o_ref[0] = _take_runs(w_ref[0], _in_proj_columns(), 1).astype(BF16)


def _in_proj_layout(w_in):
    return pl.pallas_call(
        _in_proj_layout_kernel,
        out_shape=jax.ShapeDtypeStruct((DEPTH, D_MODEL, D_IN_P), BF16),
        grid=(DEPTH, D_MODEL // W_IN_ROWS),
        in_specs=[pl.BlockSpec((1, W_IN_ROWS, D_IN), lambda l, r: (l, r, 0))],
        out_specs=pl.BlockSpec((1, W_IN_ROWS, D_IN_P), lambda l, r: (l, r, 0)),
        compiler_params=pltpu.CompilerParams(
            dimension_semantics=("arbitrary", "arbitrary"), vmem_limit_bytes=VMEM_LIMIT),
        name="in_proj_layout",
    )(w_in)


def _pipeline(n_steps, scores, softmax, values):
    scores(0, 0)
    scores(1, 1)
    softmax(0, 0)

    def body(i, carry):
        t = 2 * i + 1
        scores(t + 1, 0)
        softmax(t, 1)
        values(t - 1, 0)
        scores(t + 2, 1)
        softmax(t + 1, 0)
        values(t, 1)
        return carry

    lax.fori_loop(0, n_steps // 2 - 1, body, 0)
    softmax(n_steps - 1, 1)
    values(n_steps - 2, 0)
    values(n_steps - 1, 1)


def _softmax_tile(parts, floor=None, scale=None):
    m = None
    for s in parts:
        pm = jnp.max(s, axis=-1, keepdims=True)
        m = pm if m is None else jnp.maximum(m, pm)
    if floor is not None:
        m = jnp.maximum(m, floor)
    if scale is None:
        ps = [jnp.exp(s - m).astype(BF16) for s in parts]
    else:
        ps = [jnp.exp((s - m) * scale).astype(BF16) for s in parts]
    extra = None if floor is None else jnp.exp(floor - m)
    return ps, extra


def _retention_tables(decay_ref, rt_scr, m2_scr):
    shape = (RET_CHUNK, LANES)
    lane_lo = _lane_iota(shape) < 64
    row = _row_iota(shape)
    row_lo = row < 64
    i = row.astype(F32)
    rel = i - _lane_iota(shape).astype(F32)
    for j in range(2):
        df0, df1 = decay_ref[0, 2 * j], decay_ref[0, 2 * j + 1]
        db0, db1 = decay_ref[1, 2 * j], decay_ref[1, 2 * j + 1]
        lgf = _log_sigmoid(jnp.where(lane_lo, df0, df1))
        lgb = _log_sigmoid(jnp.where(lane_lo, db0, db1))
        rt_scr[j, 0] = jnp.exp((i + 1.0) * lgf)
        rt_scr[j, 1] = jnp.exp((RET_CHUNK - i) * lgb)
        rt_scr[j, 2] = jnp.exp((RET_CHUNK - 1.0 - i) * lgf)
        rt_scr[j, 3] = jnp.exp(i * lgb)
        rt_scr[j, 4] = jnp.exp(RET_CHUNK * _log_sigmoid(jnp.where(row_lo, df0, df1)))
        rt_scr[j, 5] = jnp.exp(RET_CHUNK * _log_sigmoid(jnp.where(row_lo, db0, db1)))
        for hh, (df, db) in enumerate(((df0, db0), (df1, db1))):
            lf = _log_sigmoid(jnp.full(shape, df, F32))
            lb = _log_sigmoid(jnp.full(shape, db, F32))
            low, upp = rel >= 0.0, rel <= 0.0
            m = (jnp.where(low, jnp.exp(jnp.where(low, rel, 0.0) * lf), 0.0)
                 + jnp.where(upp, jnp.exp(jnp.where(upp, -rel, 0.0) * lb), 0.0))
            m2_scr[j, :, hh * RET_CHUNK:(hh + 1) * RET_CHUNK] = m


N_CHUNK = TOK // RET_CHUNK


def _ret_scratch(n_seq, chunks_per_stage):
    per_chunk = lambda dtype: pltpu.VMEM((N_CHUNK, 2, 2 * RET_CHUNK, LANES), dtype)
    stage = lambda dtype: pltpu.VMEM((2 * chunks_per_stage, RET_CHUNK, 2 * LANES), dtype)
    return [
        pltpu.VMEM((TOK, 2 * LANES), BF16),
        pltpu.VMEM((TOK, 4 * LANES), BF16),
        pltpu.VMEM((TOK, 2 * LANES), F32),
        per_chunk(BF16),
        pltpu.VMEM((TOK, 2 * LANES), BF16),
        per_chunk(BF16),
        pltpu.VMEM((TOK, 2 * LANES), F32),
        pltpu.VMEM((2, 6, RET_CHUNK, LANES), F32),
        pltpu.VMEM((2, RET_CHUNK, 2 * LANES), F32),
        per_chunk(F32),
        per_chunk(BF16),
        pltpu.VMEM((n_seq, 2, 2, LANES, LANES), F32),
        stage(F32), stage(F32), stage(BF16), stage(BF16),
    ]


class _Retention:
    def __init__(self, scratch, gn_ref, mix_scr):
        (self.q, self.qd, self.k, self.kbd, self.v, self.vbd, self.g, self.rt, self.m2, self.upd,
         self.sall, self.st, rs0, rs1, rp0, rp1) = scratch
        self.rs, self.rp = (rs0, rs1), (rp0, rp1)
        self.gn_ref, self.mix = gn_ref, mix_scr

    def prepare(self, proj, decay_ref, seq_chunks):
        _retention_tables(decay_ref, self.rt, self.m2)
        lane_lo = _lane_iota((TOK, LANES)) < 64
        per_chunk = lambda a: a.reshape(N_CHUNK, RET_CHUNK, LANES)
        q = proj(C_QB, C_KB)
        k = proj(C_KB, C_VB) * RET_K_SCALE
        v = proj(C_VB, C_GB)
        self.g[...] = proj(C_GB, C_QC)
        self.q[...] = q.astype(BF16)
        self.k[...] = k
        self.v[...] = v.astype(BF16)
        for j in range(2):
            cols = slice(j * LANES, (j + 1) * LANES)
            q3 = per_chunk(q[:, cols])
            for d in range(2):
                self.qd[:, (2 * j + d) * LANES:(2 * j + d + 1) * LANES] = (
                    (q3 * self.rt[j, d]).reshape(TOK, LANES).astype(BF16))
            for src, dst in ((k[:, cols], self.kbd), (v[:, cols], self.vbd)):
                dst[:, j, :RET_CHUNK, :] = per_chunk(jnp.where(lane_lo, src, 0.0).astype(BF16))
                dst[:, j, RET_CHUNK:, :] = per_chunk(jnp.where(lane_lo, 0.0, src).astype(BF16))

        blockdiag2 = _lane_iota((2 * RET_CHUNK, LANES)) < 64
        blockdiag2 = blockdiag2 == ((_row_iota((2 * RET_CHUNK, LANES)) & (RET_CHUNK - 1)) < 64)

        def upd_body(c, carry):
            rows = _ds(c * RET_CHUNK, RET_CHUNK)
            for j in range(2):
                cols = slice(j * LANES, (j + 1) * LANES)
                k2 = self.k[rows, cols]
                kd = jnp.concatenate([k2 * self.rt[j, 2], k2 * self.rt[j, 3]], axis=1)
                upd = _dot(kd.T.astype(BF16), self.v[rows, cols])
                self.upd[c, j] = jnp.where(blockdiag2, upd, 0.0)
            return carry

        lax.fori_loop(0, N_CHUNK, upd_body, 0, unroll=4)

        n_per_seq = seq_chunks[0][1]
        assert all(n == n_per_seq for _, n in seq_chunks)

        def scan_body(t, carry):
            for si, (first, n) in enumerate(seq_chunks):
                cf, cb = first + t, first + n - 1 - t
                for j in range(2):
                    sf, sb = self.st[si, 0, j], self.st[si, 1, j]
                    self.sall[cf, j, :RET_CHUNK, :] = sf.astype(BF16)
                    self.sall[cb, j, RET_CHUNK:, :] = sb.astype(BF16)
                    self.st[si, 0, j] = self.rt[j, 4] * sf + self.upd[cf, j, :RET_CHUNK, :]
                    self.st[si, 1, j] = self.rt[j, 5] * sb + self.upd[cb, j, RET_CHUNK:, :]
            return carry

        lax.fori_loop(0, n_per_seq, scan_body, 0)

    def scores(self, chunks, b):
        for i, c in enumerate(chunks):
            rows = _ds(c * RET_CHUNK, RET_CHUNK)
            for j in range(2):
                self.rs[b][2 * i + j] = _dot_nt(self.q[rows, j * LANES:(j + 1) * LANES], self.kbd[c, j])

    def mask(self, chunks, b):
        for i in range(len(chunks)):
            for j in range(2):
                self.rp[b][2 * i + j] = (self.rs[b][2 * i + j] * self.m2[j]).astype(BF16)

    def values(self, chunks, b):
        lane_lo = _lane_iota((RET_CHUNK, LANES)) < 64
        for i, c in enumerate(chunks):
            rows = _ds(c * RET_CHUNK, RET_CHUNK)
            for j in range(2):
                cols = slice(j * LANES, (j + 1) * LANES)
                o = (_dot(self.rp[b][2 * i + j], self.vbd[c, j])
                     + _dot(self.qd[rows, 2 * j * LANES:2 * (j + 1) * LANES], self.sall[c, j]))
                s_lo = jnp.sum(jnp.where(lane_lo, o, 0.0), axis=-1, keepdims=True)
                s_hi = jnp.sum(jnp.where(lane_lo, 0.0, o), axis=-1, keepdims=True)
                d = o - jnp.where(lane_lo, s_lo, s_hi) * (1.0 / HEAD_DIM)
                dd = d * d
                v_lo = jnp.sum(jnp.where(lane_lo, dd, 0.0), axis=-1, keepdims=True)
                v_hi = jnp.sum(jnp.where(lane_lo, 0.0, dd), axis=-1, keepdims=True)
                var = jnp.where(lane_lo, v_lo, v_hi) * (1.0 / HEAD_DIM)
                g2 = self.g[rows, cols]
                y = d * lax.rsqrt(var + EPS) * self.gn_ref[:, cols] * (g2 * jax.nn.sigmoid(g2))
                self.mix[rows, M_B + j * LANES:M_B + (j + 1) * LANES] = y.astype(BF16)


X_ROWS = 256
N_XCHUNK = TOK // X_ROWS


class _XStream:
    SCRATCH = [pltpu.VMEM((N_XCHUNK, X_ROWS, D_MODEL), F32), pltpu.VMEM((2, X_ROWS, D_MODEL), F32),
               pltpu.SemaphoreType.DMA((N_XCHUNK,)), pltpu.SemaphoreType.DMA((2,))]

    def __init__(self, x_hbm, xo_hbm, scratch):
        self.x_hbm, self.xo_hbm = x_hbm, xo_hbm
        self.xin, self.xout, self.sem_in, self.sem_out = scratch
        self.step, self.n_steps = pl.program_id(0), pl.num_programs(0)

    @staticmethod
    def _rows(step, r):
        return pl.ds(pl.multiple_of(step * TOK + r * X_ROWS, X_ROWS), X_ROWS)

    def load(self, step, r):
        return pltpu.make_async_copy(self.x_hbm.at[self._rows(step, r), :], self.xin.at[r],
                                     self.sem_in.at[r])

    def store(self, r):
        return pltpu.make_async_copy(self.xout.at[r % 2], self.xo_hbm.at[self._rows(self.step, r), :],
                                     self.sem_out.at[r % 2])

    def prologue(self, mod_ref, n1_ref, h_scr):
        @pl.when(self.step == 0)
        def _():
            for r in range(N_XCHUNK):
                self.load(self.step, r).start()

        for r in range(N_XCHUNK):
            self.load(self.step, r).wait()
            h = _norm_mod(self.xin[r], n1_ref[...], mod_ref[1:2, :], mod_ref[0:1, :])
            h_scr[r * X_ROWS:(r + 1) * X_ROWS, :] = h.astype(BF16)

    def epilogue(self, mod_ref, wout_ref, mix_scr):
        for r in range(N_XCHUNK):
            if r >= 2:
                self.store(r - 2).wait()
            else:
                @pl.when(self.step > 0)
                def _():
                    self.store(r).wait()

            y = _dot(mix_scr[r * X_ROWS:(r + 1) * X_ROWS, :], wout_ref[...])
            self.xout[r % 2] = self.xin[r] + mod_ref[2:3, :] * y
            self.store(r).start()

            @pl.when(self.step + 1 < self.n_steps)
            def _():
                self.load(self.step + 1, r).start()

        @pl.when(self.step + 1 == self.n_steps)
        def _():
            for r in range(N_XCHUNK - 2, N_XCHUNK):
                self.store(r).wait()


CTX_PER_STEP = TOK // SEQ
CTX_WROWS = WIN_HEADS * SEQ
CTX_MROWS = MLA_HEADS * SEQ


def _ctx_mixer_kernel(n_alias, x_ref, mod_ref, n1_ref, win_ref, sink_ref, decay_ref, gn_ref, kvn_ref,
                      wkvb_ref, wout_ref, *refs):
    xo_ref, ko_ref, vo_ref, ckvo_ref, kro_ref, so_ref, h_scr, mix_scr = refs[n_alias:n_alias + 8]
    xs = _XStream(x_ref, xo_ref, refs[n_alias + 8:])
    xs.prologue(mod_ref, n1_ref, h_scr)
    per_seq = lambda a: a.reshape(CTX_PER_STEP, SEQ, a.shape[-1])
    if n_alias == 0:
        for ref in (ko_ref, vo_ref, ckvo_ref, kro_ref, so_ref):
            ref[:, 1:] = jnp.zeros((ref.shape[0], ref.shape[1] - 1) + ref.shape[2:], F32)
        ko_ref, vo_ref, ckvo_ref, kro_ref, so_ref = (
            ref.at[:, 0] for ref in (ko_ref, vo_ref, ckvo_ref, kro_ref, so_ref))
    chunks_per_seq = SEQ // RET_CHUNK

    def proj(c0, c1):
        return _dot(h_scr[...], win_ref[:, c0:c1])

    ret = []
    seq_chunks = lambda e: [chunks_per_seq * e + c for c in range(chunks_per_seq)]

    def attention(qst_scr, k_scr, vaug_scr, qc_scr, kcat_scr, kvaug_scr,
                  sw0, sw1, pw0, pw1, ew0, ew1, sm0, sm1, pm0, pm1):
        lane_lo = _lane_iota((TOK, LANES)) < 64
        ones = jnp.ones((TOK, LANES), BF16)
        qa = proj(C_QA, C_KA) * ATTN_SCALE
        for g in range(4):
            q = qa[:, g * LANES:(g + 1) * LANES]
            lo = jnp.where(lane_lo, q, 0.0).astype(BF16).reshape(CTX_PER_STEP, SEQ, LANES)
            hi = jnp.where(lane_lo, 0.0, q).astype(BF16).reshape(CTX_PER_STEP, SEQ, LANES)
            qst_scr[:, g * SEQ:(g + 1) * SEQ, :] = lo
            qst_scr[:, (4 + g) * SEQ:(5 + g) * SEQ, :] = hi
        kva = proj(C_KA, C_QB)
        ko_ref[...] = per_seq(kva[:, :LANES])
        vo_ref[...] = per_seq(kva[:, LANES:])
        k_scr[...] = kva[:, :LANES].astype(BF16)
        vaug_scr[:, :LANES] = kva[:, LANES:].astype(BF16)
        vaug_scr[:, LANES:] = ones
        qc = proj(C_QC, C_CKV).astype(BF16)
        for h in range(MLA_HEADS):
            qc_scr[h] = qc[:, h * LANES:(h + 1) * LANES]
        ckr = proj(C_CKV, D_IN_P)
        ckv, kr = ckr[:, :LANES], ckr[:, LANES:]
        ckv_n = ckv * lax.rsqrt(jnp.mean(ckv * ckv, axis=-1, keepdims=True) + EPS) * kvn_ref[...]
        ckvo_ref[...] = per_seq(ckv_n)
        kro_ref[...] = per_seq(kr[:, 64:64 + MLA_ROPE])
        kv = _dot(ckv_n.astype(BF16), wkvb_ref[...])
        for h in range(MLA_HEADS):
            kvh = kv[:, h * LANES:(h + 1) * LANES]
            kcat_scr[h] = jnp.where(lane_lo, kvh, kr).astype(BF16)
            kvaug_scr[h, :, :LANES] = kvh.astype(BF16)
            kvaug_scr[h, :, LANES:] = ones

        sw, pw, ew, sm, pm = (sw0, sw1), (pw0, pw1), (ew0, ew1), (sm0, sm1), (pm0, pm1)
        lane_lo_s = _lane_iota((SEQ, LANES)) < 64

        def scores(e, b):
            rows = _ds(e * SEQ, SEQ)
            sw[b][...] = _dot_nt(qst_scr[e], k_scr[rows, :])
            for h in range(MLA_HEADS):
                sm[b][h * SEQ:(h + 1) * SEQ, :] = _dot_nt(qc_scr[h, rows, :], kcat_scr[h, rows, :])
            ret[0].scores(seq_chunks(e), b)

        def softmax(e, b):
            for h in range(WIN_HEADS):
                hr = slice(h * SEQ, (h + 1) * SEQ)
                sink = jnp.full((SEQ, 1), sink_ref[h], F32)
                (p,), extra = _softmax_tile([sw[b][hr, :]], floor=sink)
                pw[b][hr, :] = p
                ew[b][hr, :] = jnp.broadcast_to(extra, (SEQ, LANES))
            for h in range(MLA_HEADS):
                hr = slice(h * SEQ, (h + 1) * SEQ)
                (p,), _ = _softmax_tile([sm[b][hr, :]], scale=MLA_SCALE)
                pm[b][hr, :] = p
            ret[0].mask(seq_chunks(e), b)

        def values(e, b):
            ret[0].values(seq_chunks(e), b)
            rows = _ds(e * SEQ, SEQ)
            oa = _dot(pw[b][...], vaug_scr[rows, :])
            o = oa[:, :LANES] / (oa[:, LANES:] + ew[b][...])
            for g in range(4):
                merged = jnp.where(lane_lo_s, o[g * SEQ:(g + 1) * SEQ], o[(4 + g) * SEQ:(5 + g) * SEQ])
                mix_scr[rows, M_A + g * LANES:M_A + (g + 1) * LANES] = merged.astype(BF16)
            for jp in range(2):
                outs = []
                for h in (2 * jp, 2 * jp + 1):
                    oc = _dot(pm[b][h * SEQ:(h + 1) * SEQ, :], kvaug_scr[h, rows, :])
                    outs.append(oc[:, :LANES] / oc[:, LANES:])
                merged = jnp.where(lane_lo_s, pltpu.roll(outs[0], 64, 1), outs[1])
                mix_scr[rows, M_C + jp * LANES:M_C + (jp + 1) * LANES] = merged.astype(BF16)

        _pipeline(CTX_PER_STEP, scores, softmax, values)

    attention_scratch = (
        pltpu.VMEM((CTX_PER_STEP, CTX_WROWS, LANES), BF16),
        pltpu.VMEM((TOK, LANES), BF16),
        pltpu.VMEM((TOK, 2 * LANES), BF16),
        pltpu.VMEM((MLA_HEADS, TOK, LANES), BF16),
        pltpu.VMEM((MLA_HEADS, TOK, LANES), BF16),
        pltpu.VMEM((MLA_HEADS, TOK, 2 * LANES), BF16),
        pltpu.VMEM((CTX_WROWS, SEQ), F32), pltpu.VMEM((CTX_WROWS, SEQ), F32),
        pltpu.VMEM((CTX_WROWS, SEQ), BF16), pltpu.VMEM((CTX_WROWS, SEQ), BF16),
        pltpu.VMEM((CTX_WROWS, LANES), F32), pltpu.VMEM((CTX_WROWS, LANES), F32),
        pltpu.VMEM((CTX_MROWS, SEQ), F32), pltpu.VMEM((CTX_MROWS, SEQ), F32),
        pltpu.VMEM((CTX_MROWS, SEQ), BF16), pltpu.VMEM((CTX_MROWS, SEQ), BF16),
    )

    def retention(*scratch):
        r = _Retention(scratch, gn_ref, mix_scr)
        r.st[...] = jnp.zeros(r.st.shape, F32)
        r.prepare(proj, decay_ref, [(chunks_per_seq * e, chunks_per_seq) for e in range(CTX_PER_STEP)])
        for e in range(CTX_PER_STEP):
            for dirn in range(2):
                for j in range(2):
                    st = r.st[e, dirn, j]
                    so_ref[e, dirn, 2 * j] = st[:64, :64]
                    so_ref[e, dirn, 2 * j + 1] = st[64:, 64:]
        ret.append(r)
        pl.run_scoped(attention, *attention_scratch)

    pl.run_scoped(retention, *_ret_scratch(CTX_PER_STEP, chunks_per_seq))
    xs.epilogue(mod_ref, wout_ref, mix_scr)


def _const_spec(shape, layer=None):
    if layer is None:
        return pl.BlockSpec(shape, lambda i: (0,) * len(shape), pipeline_mode=pl.Buffered(1))
    return pl.BlockSpec((None,) + shape, lambda i: (layer,) + (0,) * len(shape),
                        pipeline_mode=pl.Buffered(1))


def _smem_spec():
    return pl.BlockSpec(memory_space=pltpu.SMEM)


def _ctx_mixer(layer, x, mod, n1, w_in_p, sink, decay, gn, kvn, w_kv_b, w_out_p, prev_state):
    n_tok = x.shape[0]
    n_seq = n_tok // SEQ
    hbm_spec = pl.BlockSpec(memory_space=pl.ANY)
    if prev_state:
        state_spec = lambda *tail: pl.BlockSpec(
            (CTX_PER_STEP, None) + tail, lambda i: (i, layer) + (0,) * len(tail))
    else:
        assert layer == 0
        state_spec = lambda *tail: pl.BlockSpec(
            (CTX_PER_STEP, DEPTH) + tail, lambda i: (i, 0) + (0,) * len(tail))
    state_tails = [(SEQ, LANES), (SEQ, LANES), (SEQ, MLA_KV_RANK), (SEQ, MLA_ROPE),
                   (2, RET_HEADS, HEAD_DIM, HEAD_DIM)]
    n_in = 10
    return pl.pallas_call(
        functools.partial(_ctx_mixer_kernel, len(prev_state)),
        out_shape=[jax.ShapeDtypeStruct((n_tok, D_MODEL), F32)] + [
            jax.ShapeDtypeStruct((n_seq, DEPTH) + tail, F32) for tail in state_tails],
        grid=(n_tok // TOK,),
        in_specs=[
            hbm_spec,
            _const_spec((N_MOD, D_MODEL)),
            _const_spec((1, D_MODEL)),
            _const_spec((D_MODEL, D_IN_P), layer),
            _smem_spec(),
            _smem_spec(),
            _const_spec((1, RET_HEADS * HEAD_DIM)),
            _const_spec((1, MLA_KV_RANK)),
            _const_spec((MLA_KV_RANK, MLA_HEADS * LANES), layer),
            _const_spec((D_MODEL, D_MODEL), layer),
        ] + [hbm_spec] * len(prev_state),
        out_specs=[hbm_spec] + [state_spec(*tail) for tail in state_tails],
        input_output_aliases={n_in + k: 1 + k for k in range(len(prev_state))},
        scratch_shapes=[
            pltpu.VMEM((TOK, D_MODEL), BF16),
            pltpu.VMEM((TOK, D_MODEL), BF16),
        ] + _XStream.SCRATCH,
        compiler_params=pltpu.CompilerParams(
            dimension_semantics=("arbitrary",), vmem_limit_bytes=VMEM_LIMIT),
        name="ctx_mixer",
    )(x, mod, n1, w_in_p, sink, decay, gn, kvn, w_kv_b, w_out_p, *prev_state)


N_BLK = DEC_SEQ // LANES
KEYS_LOC = 3 * LANES
KEYS_WIN = KEYS_LOC + PAST_LEN
WIN_ROWS = WIN_HEADS * LANES
MLA_QB = 256
MLA_KEYS = DEC_SEQ + PAST_LEN
MLA_HALF = MLA_KEYS // 2


def _lat_mixer_kernel(x_ref, mod_ref, n1_ref, win_ref, sink_ref, decay_ref, gn_ref, kvn_ref,
                      wkvb_ref, wout_ref, ck_ref, cv_ref, cckv_ref, ckr_ref, s0_ref,
                      rc_ref, rsa_ref, rsb_ref, mc_ref, msa_ref, msb_ref,
                      xo_ref,
                      h_scr, mix_scr, s0_scr, s1_scr, p0_scr, p1_scr, *x_scratch):
    xs = _XStream(x_ref, xo_ref, x_scratch)
    xs.prologue(mod_ref, n1_ref, h_scr)
    sbuf, pbuf = (s0_scr, s1_scr), (p0_scr, p1_scr)

    def proj(c0, c1):
        return _dot(h_scr[...], win_ref[:, c0:c1])

    ret = []

    def window(qst_scr, kpad_scr, vaug_scr, ckb_scr, cvaug_scr, bias_scr, e0_scr, e1_scr):
        ebuf = (e0_scr, e1_scr)
        lane_lo = _lane_iota((TOK, LANES)) < 64
        rc, rsa, rsb = rc_ref[...], rsa_ref[...], rsb_ref[...]
        qa = proj(C_QA, C_KA)
        for g in range(4):
            q = _rope(qa[:, g * LANES:(g + 1) * LANES], rc, rsa, rsb, 16) * ATTN_SCALE
            lo = jnp.where(lane_lo, q, 0.0).astype(BF16).reshape(N_BLK, LANES, LANES)
            hi = jnp.where(lane_lo, 0.0, q).astype(BF16).reshape(N_BLK, LANES, LANES)
            qst_scr[:, g * LANES:(g + 1) * LANES, :] = lo
            qst_scr[:, (4 + g) * LANES:(5 + g) * LANES, :] = hi
        kva = proj(C_KA, C_QB)
        zpad = jnp.zeros((LANES, LANES), BF16)
        kpad_scr[0:LANES, :] = zpad
        kpad_scr[LANES + TOK:, :] = zpad
        vaug_scr[0:LANES, :LANES] = zpad
        vaug_scr[LANES + TOK:, :LANES] = zpad
        kpad_scr[LANES:LANES + TOK, :] = _rope(kva[:, :LANES], rc, rsa, rsb, 16).astype(BF16)
        vaug_scr[LANES:LANES + TOK, :LANES] = kva[:, LANES:].astype(BF16)
        vaug_scr[:, LANES:] = jnp.ones((TOK + 2 * LANES, LANES), BF16)
        ckb_scr[...] = ck_ref[...].astype(BF16)
        cvaug_scr[:, :LANES] = cv_ref[...].astype(BF16)
        cvaug_scr[:, LANES:] = jnp.ones((PAST_LEN, LANES), BF16)
        qi, kj = _row_iota((LANES, LANES)), _lane_iota((LANES, LANES))
        bias_scr[0] = jnp.full((LANES, LANES), -jnp.inf, F32)
        bias_scr[1] = jnp.where(kj >= qi, 0.0, -jnp.inf)
        bias_scr[2] = jnp.where(kj <= qi, 0.0, -jnp.inf)
        lane_lo_b = _lane_iota((LANES, LANES)) < 64

        def scores(n, b):
            q = qst_scr[n]
            sbuf[b][:, :KEYS_LOC] = _dot_nt(q, kpad_scr[_ds(n * LANES, KEYS_LOC), :])
            sbuf[b][:, KEYS_LOC:] = _dot_nt(q, ckb_scr[...])
            ret[0].scores([n], b)

        def softmax(n, b):
            if isinstance(n, int):
                i_prev, i_next = (1 if n > 0 else 0), (2 if n < N_BLK - 1 else 0)
            else:
                i_prev, i_next = jnp.where(n > 0, 1, 0), jnp.where(n < N_BLK - 1, 2, 0)
            b_prev, b_next = bias_scr[i_prev], bias_scr[i_next]
            for h in range(WIN_HEADS):
                hr = slice(h * LANES, (h + 1) * LANES)
                parts = [sbuf[b][hr, 0:LANES] + b_prev, sbuf[b][hr, LANES:2 * LANES],
                         sbuf[b][hr, 2 * LANES:KEYS_LOC] + b_next, sbuf[b][hr, KEYS_LOC:]]
                ps, extra = _softmax_tile(parts, floor=jnp.full((LANES, 1), sink_ref[h], F32))
                pbuf[b][hr, 0:LANES] = ps[0]
                pbuf[b][hr, LANES:2 * LANES] = ps[1]
                pbuf[b][hr, 2 * LANES:KEYS_LOC] = ps[2]
                pbuf[b][hr, KEYS_LOC:] = ps[3]
                ebuf[b][hr, :] = jnp.broadcast_to(extra, (LANES, LANES))
            ret[0].mask([n], b)

        def values(n, b):
            ret[0].values([n], b)
            oa = (_dot(pbuf[b][:, :KEYS_LOC], vaug_scr[_ds(n * LANES, KEYS_LOC), :])
                  + _dot(pbuf[b][:, KEYS_LOC:], cvaug_scr[...]))
            o = oa[:, :LANES] / (oa[:, LANES:] + ebuf[b][...])
            for g in range(4):
                merged = jnp.where(lane_lo_b, o[g * LANES:(g + 1) * LANES],
                                   o[(4 + g) * LANES:(5 + g) * LANES])
                mix_scr[_ds(n * LANES, LANES), M_A + g * LANES:M_A + (g + 1) * LANES] = (
                    merged.astype(BF16))

        _pipeline(N_BLK, scores, softmax, values)

    window_scratch = (
        pltpu.VMEM((N_BLK, WIN_ROWS, LANES), BF16),
        pltpu.VMEM((TOK + 2 * LANES, LANES), BF16),
        pltpu.VMEM((TOK + 2 * LANES, 2 * LANES), BF16),
        pltpu.VMEM((PAST_LEN, LANES), BF16),
        pltpu.VMEM((PAST_LEN, 2 * LANES), BF16),
        pltpu.VMEM((3, LANES, LANES), F32),
        pltpu.VMEM((WIN_ROWS, LANES), F32), pltpu.VMEM((WIN_ROWS, LANES), F32),
    )

    def retention(*scratch):
        r = _Retention(scratch, gn_ref, mix_scr)
        r.st[0] = s0_ref[...]
        r.prepare(proj, decay_ref, [(0, N_CHUNK)])
        ret.append(r)
        pl.run_scoped(window, *window_scratch)

    assert N_CHUNK == N_BLK
    pl.run_scoped(retention, *_ret_scratch(1, 1))

    def latent(qc_scr, kcat_scr, kvaug_scr, mixc_scr):
        lane_lo = _lane_iota((TOK, LANES)) < 64
        mc, msa, msb = mc_ref[...], msa_ref[...], msb_ref[...]
        qc = proj(C_QC, C_CKV)
        for h in range(MLA_HEADS):
            qc_scr[h] = _rope(qc[:, h * LANES:(h + 1) * LANES], mc, msa, msb, 8).astype(BF16)
        ckr = proj(C_CKV, D_IN_P)
        ckv = ckr[:, :LANES]
        kr = _rope(ckr[:, LANES:], mc, msa, msb, 8)
        ckv_n = ckv * lax.rsqrt(jnp.mean(ckv * ckv, axis=-1, keepdims=True) + EPS) * kvn_ref[...]
        kv = _dot(ckv_n.astype(BF16), wkvb_ref[...])
        kv_c = _dot(cckv_ref[...].astype(BF16), wkvb_ref[...])
        kr_c = ckr_ref[...]
        lane_lo_c = _lane_iota((PAST_LEN, LANES)) < 64
        for h in range(MLA_HEADS):
            kvh, kvh_c = kv[:, h * LANES:(h + 1) * LANES], kv_c[:, h * LANES:(h + 1) * LANES]
            kcat_scr[h, 0:TOK, :] = jnp.where(lane_lo, kvh, kr).astype(BF16)
            kcat_scr[h, TOK:, :] = jnp.where(lane_lo_c, kvh_c, kr_c).astype(BF16)
            kvaug_scr[h, 0:TOK, :LANES] = kvh.astype(BF16)
            kvaug_scr[h, TOK:, :LANES] = kvh_c.astype(BF16)
            kvaug_scr[h, :, LANES:] = jnp.ones((MLA_KEYS, LANES), BF16)
        lane_lo_m = _lane_iota((MLA_QB, LANES)) < 64
        n_qb = DEC_SEQ // MLA_QB

        def split(t):
            if isinstance(t, int):
                return t // n_qb, t % n_qb
            return lax.shift_right_logical(t, 2), lax.bitwise_and(t, n_qb - 1)

        def scores(t, b):
            jp, qb = split(t)
            for i in range(2):
                h = 2 * jp + i
                q = qc_scr[h, _ds(qb * MLA_QB, MLA_QB), :]
                for part in range(2):
                    r0 = (2 * i + part) * MLA_QB
                    sbuf[b][r0:r0 + MLA_QB, :] = _dot_nt(
                        q, kcat_scr[h, part * MLA_HALF:(part + 1) * MLA_HALF, :])

        def softmax(t, b):
            for i in range(2):
                for rt in range(MLA_QB // LANES):
                    ra = 2 * i * MLA_QB + rt * LANES
                    rb = ra + MLA_QB
                    ps, _ = _softmax_tile([sbuf[b][ra:ra + LANES, :], sbuf[b][rb:rb + LANES, :]],
                                          scale=MLA_SCALE)
                    pbuf[b][ra:ra + LANES, :] = ps[0]
                    pbuf[b][rb:rb + LANES, :] = ps[1]

        def values(t, b):
            jp, qb = split(t)
            outs = []
            for i in range(2):
                h = 2 * jp + i
                r0 = 2 * i * MLA_QB
                oc = (_dot(pbuf[b][r0:r0 + MLA_QB, :], kvaug_scr[h, 0:MLA_HALF, :])
                      + _dot(pbuf[b][r0 + MLA_QB:r0 + 2 * MLA_QB, :], kvaug_scr[h, MLA_HALF:, :]))
                outs.append(oc[:, :LANES] / oc[:, LANES:])
            merged = jnp.where(lane_lo_m, pltpu.roll(outs[0], 64, 1), outs[1])
            mixc_scr[jp, _ds(qb * MLA_QB, MLA_QB), :] = merged.astype(BF16)

        _pipeline(2 * n_qb, scores, softmax, values)
        for jp in range(2):
            mix_scr[:, M_C + jp * LANES:M_C + (jp + 1) * LANES] = mixc_scr[jp]

    pl.run_scoped(
        latent,
        pltpu.VMEM((MLA_HEADS, TOK, LANES), BF16),
        pltpu.VMEM((MLA_HEADS, MLA_KEYS, LANES), BF16),
        pltpu.VMEM((MLA_HEADS, MLA_KEYS, 2 * LANES), BF16),
        pltpu.VMEM((2, TOK, LANES), BF16),
    )
    xs.epilogue(mod_ref, wout_ref, mix_scr)


def _lat_mixer(layer, x, mod, n1, w_in_p, sink, decay, gn, kvn, w_kv_b, w_out_p,
               ck, cv, cckv, ckr, s0, rope_a, rope_m):
    n_tok = x.shape[0]
    n_seq = n_tok // DEC_SEQ
    seq_spec = lambda shape: pl.BlockSpec(
        (None, None) + shape, lambda i: (i, layer) + (0,) * len(shape))
    assert WIN_ROWS == 4 * MLA_QB and KEYS_WIN == MLA_HALF
    return pl.pallas_call(
        _lat_mixer_kernel,
        out_shape=jax.ShapeDtypeStruct((n_tok, D_MODEL), F32),
        grid=(n_seq,),
        in_specs=[
            pl.BlockSpec(memory_space=pl.ANY),
            pl.BlockSpec((None, N_MOD, D_MODEL), lambda i: (i, 0, 0)),
            _const_spec((1, D_MODEL)),
            _const_spec((D_MODEL, D_IN_P), layer),
            _smem_spec(),
            _smem_spec(),
            _const_spec((1, RET_HEADS * HEAD_DIM)),
            _const_spec((1, MLA_KV_RANK)),
            _const_spec((MLA_KV_RANK, MLA_HEADS * LANES), layer),
            _const_spec((D_MODEL, D_MODEL), layer),
            seq_spec((PAST_LEN, LANES)),
            seq_spec((PAST_LEN, LANES)),
            seq_spec((PAST_LEN, MLA_KV_RANK)),
            seq_spec((PAST_LEN, LANES)),
            seq_spec((2, 2, LANES, LANES)),
        ] + [_const_spec((DEC_SEQ, LANES))] * 6,
        out_specs=pl.BlockSpec(memory_space=pl.ANY),
        scratch_shapes=[
            pltpu.VMEM((TOK, D_MODEL), BF16),
            pltpu.VMEM((TOK, D_MODEL), BF16),
            pltpu.VMEM((WIN_ROWS, KEYS_WIN), F32), pltpu.VMEM((WIN_ROWS, KEYS_WIN), F32),
            pltpu.VMEM((WIN_ROWS, KEYS_WIN), BF16), pltpu.VMEM((WIN_ROWS, KEYS_WIN), BF16),
        ] + _XStream.SCRATCH,
        compiler_params=pltpu.CompilerParams(
            dimension_semantics=("arbitrary",), vmem_limit_bytes=VMEM_LIMIT),
        name="lat_mixer",
    )(x, mod, n1, w_in_p, sink, decay, gn, kvn, w_kv_b, w_out_p, ck, cv, cckv, ckr, s0,
      *rope_a, *rope_m)


def _mlp_kernel(final, x_ref, mod_ref, n2_ref, wup_ref, wdn_ref, fn_ref, o_ref):
    x = x_ref[...]
    h2 = _norm_mod(x, n2_ref[...], mod_ref[4:5, :], mod_ref[3:4, :]).astype(BF16)
    acc = None
    for c in range(D_FF // FF_CHUNK):
        cols = slice(c * FF_CHUNK, (c + 1) * FF_CHUNK)
        u = jnp.maximum(_dot(h2, wup_ref[:, cols]), 0.0)
        part = _dot((u * u).astype(BF16), wdn_ref[cols, :])
        acc = part if acc is None else acc + part
    y = x + mod_ref[5:6, :] * acc
    if final:
        y = y * lax.rsqrt(jnp.mean(y * y, axis=-1, keepdims=True) + EPS) * fn_ref[...]
    o_ref[...] = y


def _mlp(layer, x, mod, n2, w_up, w_down, final_norm, final):
    n_tok = x.shape[0]
    per_mod = n_tok // mod.shape[0] // MLP_ROWS
    return pl.pallas_call(
        functools.partial(_mlp_kernel, final),
        out_shape=jax.ShapeDtypeStruct((n_tok, D_MODEL), F32),
        grid=(n_tok // MLP_ROWS,),
        in_specs=[
            pl.BlockSpec((MLP_ROWS, D_MODEL), lambda i: (i, 0)),
            pl.BlockSpec((None, N_MOD, D_MODEL), lambda i: (i // per_mod, 0, 0)),
            _const_spec((1, D_MODEL)),
            _const_spec((D_MODEL, D_FF), layer),
            _const_spec((D_FF, D_MODEL), layer),
            _const_spec((1, D_MODEL)),
        ],
        out_specs=pl.BlockSpec((MLP_ROWS, D_MODEL), lambda i: (i, 0)),
        compiler_params=pltpu.CompilerParams(
            dimension_semantics=("arbitrary",), vmem_limit_bytes=VMEM_LIMIT),
        name="mlp",
    )(x, mod, n2, w_up, w_down, final_norm)


def kernel(x_prompt, x_sample, cache_win_k, cache_win_v, cache_mla_ckv, cache_mla_krope, state_ret,
           c, c_ctx, w_mod, b_mod, norm1, norm2, w_in, win_sink, ret_decay, ret_gn, mla_kv_norm,
           w_kv_b, w_out, w_up, w_down, final_norm):
    n_ctx, n_lat = x_prompt.shape[0], x_sample.shape[0]

    w_in_p = _in_proj_layout(w_in)
    w_out_p = _take_runs(w_out, _mix_rows(), 1).astype(BF16)
    w_kv_b16, w_up16, w_down16 = w_kv_b.astype(BF16), w_up.astype(BF16), w_down.astype(BF16)

    c_rows = jnp.zeros((16, D_MODEL), F32).at[0].set(c_ctx).at[1:1 + n_lat].set(c)
    mod = _modulation(c_rows, w_mod, b_mod).reshape(DEPTH, 16, N_MOD, D_MODEL)

    rope_a = _rope_tables(DEC_SEQ, HEAD_DIM, 0, HEAD_DIM)
    rope_m = _rope_tables(DEC_SEQ, MLA_ROPE, MLA_NOPE, LANES)

    ck = cache_win_k.reshape(n_lat, DEPTH, PAST_LEN, LANES)
    cv = cache_win_v.reshape(n_lat, DEPTH, PAST_LEN, LANES)
    ckr = jnp.pad(cache_mla_krope, ((0, 0), (0, 0), (0, 0), (MLA_NOPE, LANES - MLA_NOPE - MLA_ROPE)))
    sr = state_ret.reshape(n_lat, DEPTH, 2, 2, 2, HEAD_DIM, HEAD_DIM)
    zero = jnp.zeros_like(sr[:, :, :, :, 0])
    s0 = jnp.concatenate([jnp.concatenate([sr[:, :, :, :, 0], zero], axis=-1),
                          jnp.concatenate([zero, sr[:, :, :, :, 1]], axis=-1)], axis=-2)

    xp = x_prompt.reshape(n_ctx * SEQ, D_MODEL)
    xs = x_sample.reshape(n_lat * DEC_SEQ, D_MODEL)
    state = ()
    for l in range(DEPTH):
        last = l == DEPTH - 1
        shared = (norm1[l][None], w_in_p, win_sink[l], ret_decay[l], ret_gn[l][None],
                  mla_kv_norm[l][None], w_kv_b16, w_out_p)
        mod_ctx, mod_lat = mod[l, 0:1], mod[l, 1:1 + n_lat]
        xp, *state = _ctx_mixer(l, xp, mod_ctx[0], *shared, state)
        xp = _mlp(l, xp, mod_ctx, norm2[l][None], w_up16, w_down16, final_norm[None], last)
        xs = _lat_mixer(l, xs, mod_lat, *shared, ck, cv, cache_mla_ckv, ckr, s0, rope_a, rope_m)
        xs = _mlp(l, xs, mod_lat, norm2[l][None], w_up16, w_down16, final_norm[None], last)
    new_k, new_v, new_ckv, new_kr, new_s = state
    return (xp.reshape(n_ctx, SEQ, D_MODEL), xs.reshape(n_lat, DEC_SEQ, D_MODEL),
            new_k.reshape(n_ctx, DEPTH, SEQ, 2, HEAD_DIM), new_v.reshape(n_ctx, DEPTH, SEQ, 2, HEAD_DIM),
            new_ckv, new_kr, new_s)
```

```python
import functools

import numpy as np
import jax
import jax.numpy as jnp
from jax import lax
from jax.experimental import pallas as pl
from jax.experimental.pallas import tpu as pltpu

F32 = jnp.float32
BF16 = jnp.bfloat16

D_MODEL = 1024
DEPTH = 2
SEQ = 256
DEC_SEQ = 1024
PAST_LEN = 256
GRID_W = 64
HEAD_DIM = 64
ROPE_BASE = 10000.0
EPS = 1e-6
WIN_HEADS = 8
WINDOW = 128
ATTN_SCALE = HEAD_DIM ** -0.5
RET_HEADS = 4
RET_CHUNK = 128
RET_K_SCALE = HEAD_DIM ** -0.5
MLA_HEADS = 4
MLA_NOPE = 64
MLA_ROPE = 32
MLA_KV_RANK = 128
MLA_QK = MLA_NOPE + MLA_ROPE
MLA_SCALE = MLA_QK ** -0.5
D_IN = 2336
D_FF = 4 * D_MODEL
N_MOD = 6

LANES = 128
TOK = 1024
MLP_ROWS = 1024
FF_CHUNK = 1024
VMEM_LIMIT = 60 * 1024 * 1024

C_QA, C_KA, C_VA, C_QB, C_KB, C_VB, C_GB, C_QC, C_CKV, C_KR, D_IN_P = (
    0, 512, 640, 768, 1024, 1280, 1536, 1792, 2304, 2432, 2560)
M_A, M_B, M_C = 0, 512, 768

NT_DIMS = (((1,), (1,)), ((), ()))


def _in_proj_columns():
    idx = []
    for g in range(4):
        idx += list(range(g * 64, (g + 1) * 64)) + list(range((4 + g) * 64, (5 + g) * 64))
    idx += list(range(512, 1792))
    for h in range(MLA_HEADS):
        idx += list(range(1792 + h * MLA_QK, 1792 + (h + 1) * MLA_QK)) + [-1] * 32
    idx += list(range(2176, 2304))
    idx += [-1] * 64 + list(range(2304, 2336)) + [-1] * 32
    return np.asarray(idx, np.int32)


def _mix_rows():
    idx = []
    for g in range(4):
        idx += list(range(g * 64, (g + 1) * 64)) + list(range((4 + g) * 64, (5 + g) * 64))
    idx += list(range(512, 1024))
    return np.asarray(idx, np.int32)


def _take_runs(w, idx, axis):
    pieces, i = [], 0
    while i < len(idx):
        j = i + 1
        if idx[i] < 0:
            while j < len(idx) and idx[j] < 0:
                j += 1
            shape = list(w.shape)
            shape[axis] = j - i
            pieces.append(jnp.zeros(shape, w.dtype))
        else:
            while j < len(idx) and idx[j] == idx[j - 1] + 1:
                j += 1
            pieces.append(lax.slice_in_dim(w, int(idx[i]), int(idx[j - 1]) + 1, axis=axis))
        i = j
    return jnp.concatenate(pieces, axis=axis)


def _rope_tables(n_tokens, dim, lane0, period):
    quarter = dim // 4
    t = np.arange(n_tokens)
    row = (t // GRID_W).astype(np.float64)
    col = (t % GRID_W).astype(np.float64)
    inv_freq = ROPE_BASE ** (-np.arange(quarter, dtype=np.float64) / quarter)
    ar, ac = row[:, None] * inv_freq, col[:, None] * inv_freq
    cos = np.concatenate([np.cos(ar), np.cos(ar), np.cos(ac), np.cos(ac)], axis=-1)
    sin = np.concatenate([np.sin(ar), np.sin(ar), np.sin(ac), np.sin(ac)], axis=-1)
    first = np.tile(np.concatenate([np.ones(quarter), np.zeros(quarter)]), 2)
    c = np.ones((n_tokens, LANES))
    sa = np.zeros((n_tokens, LANES))
    sb = np.zeros((n_tokens, LANES))
    for start in range(lane0, LANES, period):
        c[:, start:start + dim] = cos
        sa[:, start:start + dim] = -sin * first
        sb[:, start:start + dim] = sin * (1.0 - first)
    return tuple(jnp.asarray(a, F32) for a in (c, sa, sb))


def _lane_iota(shape):
    return lax.broadcasted_iota(jnp.int32, shape, len(shape) - 1)


def _row_iota(shape):
    return lax.broadcasted_iota(jnp.int32, shape, len(shape) - 2)


def _ds(start, size):
    if isinstance(start, int):
        return pl.ds(start, size)
    return pl.ds(pl.multiple_of(start, LANES), size)


def _norm_mod(x, gain, scale, shift):
    ms = jnp.mean(x * x, axis=-1, keepdims=True)
    return (x * lax.rsqrt(ms + EPS) * gain) * (1.0 + scale) + shift


def _log_sigmoid(x):
    return -(jnp.maximum(-x, 0.0) + jnp.log1p(jnp.exp(-jnp.abs(x))))


def _rope(x, c, sa, sb, quarter):
    return (x * c + pltpu.roll(x, LANES - quarter, 1) * sa + pltpu.roll(x, quarter, 1) * sb)


def _dot(a, b):
    return jnp.dot(a, b, preferred_element_type=F32)


def _dot_nt(a, b):
    return lax.dot_general(a, b, NT_DIMS, preferred_element_type=F32)


def _mod_kernel(c_ref, w_ref, b_ref, o_ref):
    cv = c_ref[...]
    s = cv * jax.nn.sigmoid(cv)
    o_ref[0] = _dot(s.astype(BF16), w_ref[0].astype(BF16)) + b_ref[0]


def _modulation(c_rows, w_mod, b_mod):
    tn = 1536
    nj = (N_MOD * D_MODEL) // tn
    return pl.pallas_call(
        _mod_kernel,
        out_shape=jax.ShapeDtypeStruct((DEPTH, 16, N_MOD * D_MODEL), F32),
        grid=(DEPTH, nj),
        in_specs=[
            pl.BlockSpec((16, D_MODEL), lambda l, j: (0, 0)),
            pl.BlockSpec((1, D_MODEL, tn), lambda l, j: (l, 0, j)),
            pl.BlockSpec((1, 1, tn), lambda l, j: (l, 0, j)),
        ],
        out_specs=pl.BlockSpec((1, 16, tn), lambda l, j: (l, 0, j)),
        compiler_params=pltpu.CompilerParams(
            dimension_semantics=("arbitrary", "arbitrary"), vmem_limit_bytes=VMEM_LIMIT),
        name="modulation",
    )(c_rows, w_mod, b_mod.reshape(DEPTH, 1, N_MOD * D_MODEL))


W_IN_LANES = 256


def _proj_layout_kernel(win_ref, wout_ref, oin_ref, oout_ref):
    oin_ref[0] = _take_runs(win_ref[0], _in_proj_columns(), 0).astype(BF16)
    oout_ref[0] = _take_runs(wout_ref[0], _mix_rows(), 0).astype(BF16)


def _proj_layout(w_in_t, w_out):
    col_spec = lambda rows: pl.BlockSpec((1, rows, W_IN_LANES), lambda l, c: (l, 0, c))
    return pl.pallas_call(
        _proj_layout_kernel,
        out_shape=(jax.ShapeDtypeStruct((DEPTH, D_IN_P, D_MODEL), BF16),
                   jax.ShapeDtypeStruct((DEPTH, D_MODEL, D_MODEL), BF16)),
        grid=(DEPTH, D_MODEL // W_IN_LANES),
        in_specs=[col_spec(D_IN), col_spec(D_MODEL)],
        out_specs=(col_spec(D_IN_P), col_spec(D_MODEL)),
        compiler_params=pltpu.CompilerParams(
            dimension_semantics=("arbitrary", "arbitrary"), vmem_limit_bytes=VMEM_LIMIT),
        name="proj_layout",
    )(w_in_t, w_out)


def _pipeline(n_steps, scores, softmax, values):
    scores(0, 0)
    scores(1, 1)
    softmax(0, 0)

    def body(i, carry):
        t = 2 * i + 1
        scores(t + 1, 0)
        softmax(t, 1)
        values(t - 1, 0)
        scores(t + 2, 1)
        softmax(t + 1, 0)
        values(t, 1)
        return carry

    lax.fori_loop(0, n_steps // 2 - 1, body, 0)
    softmax(n_steps - 1, 1)
    values(n_steps - 2, 0)
    values(n_steps - 1, 1)


def _softmax_tile(parts, floor=None, scale=None):
    m = None
    for s in parts:
        pm = jnp.max(s, axis=-1, keepdims=True)
        m = pm if m is None else jnp.maximum(m, pm)
    if floor is not None:
        m = jnp.maximum(m, floor)
    if scale is None:
        ps = [jnp.exp(s - m).astype(BF16) for s in parts]
    else:
        ps = [jnp.exp((s - m) * scale).astype(BF16) for s in parts]
    extra = None if floor is None else jnp.exp(floor - m)
    return ps, extra


def _retention_tables(decay_ref, rt_scr, m2_scr):
    shape = (RET_CHUNK, LANES)
    lane_lo = _lane_iota(shape) < 64
    row = _row_iota(shape)
    row_lo = row < 64
    i = row.astype(F32)
    rel = i - _lane_iota(shape).astype(F32)
    for j in range(2):
        df0, df1 = decay_ref[0, 2 * j], decay_ref[0, 2 * j + 1]
        db0, db1 = decay_ref[1, 2 * j], decay_ref[1, 2 * j + 1]
        lgf = _log_sigmoid(jnp.where(lane_lo, df0, df1))
        lgb = _log_sigmoid(jnp.where(lane_lo, db0, db1))
        rt_scr[j, 0] = jnp.exp((i + 1.0) * lgf)
        rt_scr[j, 1] = jnp.exp((RET_CHUNK - i) * lgb)
        rt_scr[j, 2] = jnp.exp((RET_CHUNK - 1.0 - i) * lgf)
        rt_scr[j, 3] = jnp.exp(i * lgb)
        rt_scr[j, 4] = jnp.exp(RET_CHUNK * _log_sigmoid(jnp.where(row_lo, df0, df1)))
        rt_scr[j, 5] = jnp.exp(RET_CHUNK * _log_sigmoid(jnp.where(row_lo, db0, db1)))
        for hh, (df, db) in enumerate(((df0, db0), (df1, db1))):
            lf = _log_sigmoid(jnp.full(shape, df, F32))
            lb = _log_sigmoid(jnp.full(shape, db, F32))
            low, upp = rel >= 0.0, rel <= 0.0
            m = (jnp.where(low, jnp.exp(jnp.where(low, rel, 0.0) * lf), 0.0)
                 + jnp.where(upp, jnp.exp(jnp.where(upp, -rel, 0.0) * lb), 0.0))
            m2_scr[j, :, hh * RET_CHUNK:(hh + 1) * RET_CHUNK] = m


N_CHUNK = TOK // RET_CHUNK


def _ret_scratch(n_seq, chunks_per_stage):
    per_chunk = lambda dtype: pltpu.VMEM((N_CHUNK, 2, 2 * RET_CHUNK, LANES), dtype)
    stage = lambda dtype: pltpu.VMEM((2 * chunks_per_stage, RET_CHUNK, 2 * LANES), dtype)
    return [
        pltpu.VMEM((TOK, 2 * LANES), BF16),
        pltpu.VMEM((TOK, 4 * LANES), BF16),
        pltpu.VMEM((TOK, 2 * LANES), F32),
        per_chunk(BF16),
        pltpu.VMEM((TOK, 2 * LANES), BF16),
        per_chunk(BF16),
        pltpu.VMEM((TOK, 2 * LANES), F32),
        pltpu.VMEM((2, 6, RET_CHUNK, LANES), F32),
        pltpu.VMEM((2, RET_CHUNK, 2 * LANES), F32),
        per_chunk(F32),
        per_chunk(BF16),
        pltpu.VMEM((n_seq, 2, 2, LANES, LANES), F32),
        stage(F32), stage(F32), stage(BF16), stage(BF16),
    ]


class _Retention:
    def __init__(self, scratch, gn_ref, mix_scr):
        (self.q, self.qd, self.k, self.kbd, self.v, self.vbd, self.g, self.rt, self.m2, self.upd,
         self.sall, self.st, rs0, rs1, rp0, rp1) = scratch
        self.rs, self.rp = (rs0, rs1), (rp0, rp1)
        self.gn_ref, self.mix = gn_ref, mix_scr

    def prepare(self, proj, decay_ref, seq_chunks):
        _retention_tables(decay_ref, self.rt, self.m2)
        lane_lo = _lane_iota((TOK, LANES)) < 64
        per_chunk = lambda a: a.reshape(N_CHUNK, RET_CHUNK, LANES)
        q = proj(C_QB, C_KB)
        k = proj(C_KB, C_VB) * RET_K_SCALE
        v = proj(C_VB, C_GB)
        self.g[...] = proj(C_GB, C_QC)
        self.q[...] = q.astype(BF16)
        self.k[...] = k
        self.v[...] = v.astype(BF16)
        for j in range(2):
            cols = slice(j * LANES, (j + 1) * LANES)
            q3 = per_chunk(q[:, cols])
            for d in range(2):
                self.qd[:, (2 * j + d) * LANES:(2 * j + d + 1) * LANES] = (
                    (q3 * self.rt[j, d]).reshape(TOK, LANES).astype(BF16))
            for src, dst in ((k[:, cols], self.kbd), (v[:, cols], self.vbd)):
                dst[:, j, :RET_CHUNK, :] = per_chunk(jnp.where(lane_lo, src, 0.0).astype(BF16))
                dst[:, j, RET_CHUNK:, :] = per_chunk(jnp.where(lane_lo, 0.0, src).astype(BF16))

        blockdiag2 = _lane_iota((2 * RET_CHUNK, LANES)) < 64
        blockdiag2 = blockdiag2 == ((_row_iota((2 * RET_CHUNK, LANES)) & (RET_CHUNK - 1)) < 64)

        def upd_body(c, carry):
            rows = _ds(c * RET_CHUNK, RET_CHUNK)
            for j in range(2):
                cols = slice(j * LANES, (j + 1) * LANES)
                k2 = self.k[rows, cols]
                kd = jnp.concatenate([k2 * self.rt[j, 2], k2 * self.rt[j, 3]], axis=1)
                upd = _dot(kd.T.astype(BF16), self.v[rows, cols])
                self.upd[c, j] = jnp.where(blockdiag2, upd, 0.0)
            return carry

        lax.fori_loop(0, N_CHUNK, upd_body, 0, unroll=4)

        n_per_seq = seq_chunks[0][1]
        assert all(n == n_per_seq for _, n in seq_chunks)

        def scan_body(t, carry):
            for si, (first, n) in enumerate(seq_chunks):
                cf, cb = first + t, first + n - 1 - t
                for j in range(2):
                    sf, sb = self.st[si, 0, j], self.st[si, 1, j]
                    self.sall[cf, j, :RET_CHUNK, :] = sf.astype(BF16)
                    self.sall[cb, j, RET_CHUNK:, :] = sb.astype(BF16)
                    self.st[si, 0, j] = self.rt[j, 4] * sf + self.upd[cf, j, :RET_CHUNK, :]
                    self.st[si, 1, j] = self.rt[j, 5] * sb + self.upd[cb, j, RET_CHUNK:, :]
            return carry

        lax.fori_loop(0, n_per_seq, scan_body, 0)

    def scores(self, chunks, b):
        for i, c in enumerate(chunks):
            rows = _ds(c * RET_CHUNK, RET_CHUNK)
            for j in range(2):
                self.rs[b][2 * i + j] = _dot_nt(self.q[rows, j * LANES:(j + 1) * LANES], self.kbd[c, j])

    def mask(self, chunks, b):
        for i in range(len(chunks)):
            for j in range(2):
                self.rp[b][2 * i + j] = (self.rs[b][2 * i + j] * self.m2[j]).astype(BF16)

    def values(self, chunks, b):
        lane_lo = _lane_iota((RET_CHUNK, LANES)) < 64
        for i, c in enumerate(chunks):
            rows = _ds(c * RET_CHUNK, RET_CHUNK)
            for j in range(2):
                cols = slice(j * LANES, (j + 1) * LANES)
                o = (_dot(self.rp[b][2 * i + j], self.vbd[c, j])
                     + _dot(self.qd[rows, 2 * j * LANES:2 * (j + 1) * LANES], self.sall[c, j]))
                s_lo = jnp.sum(jnp.where(lane_lo, o, 0.0), axis=-1, keepdims=True)
                s_hi = jnp.sum(jnp.where(lane_lo, 0.0, o), axis=-1, keepdims=True)
                d = o - jnp.where(lane_lo, s_lo, s_hi) * (1.0 / HEAD_DIM)
                dd = d * d
                v_lo = jnp.sum(jnp.where(lane_lo, dd, 0.0), axis=-1, keepdims=True)
                v_hi = jnp.sum(jnp.where(lane_lo, 0.0, dd), axis=-1, keepdims=True)
                var = jnp.where(lane_lo, v_lo, v_hi) * (1.0 / HEAD_DIM)
                g2 = self.g[rows, cols]
                y = d * lax.rsqrt(var + EPS) * self.gn_ref[:, cols] * (g2 * jax.nn.sigmoid(g2))
                self.mix[rows, M_B + j * LANES:M_B + (j + 1) * LANES] = y.astype(BF16)


X_ROWS = 256
N_XCHUNK = TOK // X_ROWS


class _XStream:
    SCRATCH = [pltpu.VMEM((N_XCHUNK, X_ROWS, D_MODEL), F32), pltpu.VMEM((2, X_ROWS, D_MODEL), F32),
               pltpu.SemaphoreType.DMA((N_XCHUNK,)), pltpu.SemaphoreType.DMA((2,))]

    def __init__(self, x_hbm, xo_hbm, scratch):
        self.x_hbm, self.xo_hbm = x_hbm, xo_hbm
        self.xin, self.xout, self.sem_in, self.sem_out = scratch
        self.step, self.n_steps = pl.program_id(0), pl.num_programs(0)

    @staticmethod
    def _rows(step, r):
        return pl.ds(pl.multiple_of(step * TOK + r * X_ROWS, X_ROWS), X_ROWS)

    def load(self, step, r):
        return pltpu.make_async_copy(self.x_hbm.at[self._rows(step, r), :], self.xin.at[r],
                                     self.sem_in.at[r])

    def store(self, r):
        return pltpu.make_async_copy(self.xout.at[r % 2], self.xo_hbm.at[self._rows(self.step, r), :],
                                     self.sem_out.at[r % 2])

    def prologue(self, mod_ref, n1_ref, h_scr):
        @pl.when(self.step == 0)
        def _():
            for r in range(N_XCHUNK):
                self.load(self.step, r).start()

        for r in range(N_XCHUNK):
            self.load(self.step, r).wait()
            h = _norm_mod(self.xin[r], n1_ref[...], mod_ref[1:2, :], mod_ref[0:1, :])
            h_scr[r * X_ROWS:(r + 1) * X_ROWS, :] = h.astype(BF16)

    def epilogue(self, mod_ref, wout_ref, mix_scr):
        for r in range(N_XCHUNK):
            if r >= 2:
                self.store(r - 2).wait()
            else:
                @pl.when(self.step > 0)
                def _():
                    self.store(r).wait()

            y = _dot(mix_scr[r * X_ROWS:(r + 1) * X_ROWS, :], wout_ref[...])
            self.xout[r % 2] = self.xin[r] + mod_ref[2:3, :] * y
            self.store(r).start()

            @pl.when(self.step + 1 < self.n_steps)
            def _():
                self.load(self.step + 1, r).start()

        @pl.when(self.step + 1 == self.n_steps)
        def _():
            for r in range(N_XCHUNK - 2, N_XCHUNK):
                self.store(r).wait()


CTX_PER_STEP = TOK // SEQ
CTX_WROWS = WIN_HEADS * SEQ
CTX_MROWS = MLA_HEADS * SEQ


def _ctx_mixer_kernel(n_alias, x_ref, mod_ref, n1_ref, win_ref, sink_ref, decay_ref, gn_ref, kvn_ref,
                      wkvb_ref, wout_ref, *refs):
    xo_ref, ko_ref, vo_ref, ckvo_ref, kro_ref, so_ref, h_scr, mix_scr = refs[n_alias:n_alias + 8]
    xs = _XStream(x_ref, xo_ref, refs[n_alias + 8:])
    xs.prologue(mod_ref, n1_ref, h_scr)
    per_seq = lambda a: a.reshape(CTX_PER_STEP, SEQ, a.shape[-1])
    if n_alias == 0:
        for ref in (ko_ref, vo_ref, ckvo_ref, kro_ref, so_ref):
            ref[:, 1:] = jnp.zeros((ref.shape[0], ref.shape[1] - 1) + ref.shape[2:], F32)
        ko_ref, vo_ref, ckvo_ref, kro_ref, so_ref = (
            ref.at[:, 0] for ref in (ko_ref, vo_ref, ckvo_ref, kro_ref, so_ref))
    chunks_per_seq = SEQ // RET_CHUNK

    def proj(c0, c1):
        return _dot_nt(h_scr[...], win_ref[c0:c1, :])

    ret = []
    seq_chunks = lambda e: [chunks_per_seq * e + c for c in range(chunks_per_seq)]

    def attention(qst_scr, k_scr, vaug_scr, qc_scr, kcat_scr, kvaug_scr,
                  sw0, sw1, pw0, pw1, ew0, ew1, sm0, sm1, pm0, pm1):
        lane_lo = _lane_iota((TOK, LANES)) < 64
        ones = jnp.ones((TOK, LANES), BF16)
        qa = proj(C_QA, C_KA) * ATTN_SCALE
        for g in range(4):
            q = qa[:, g * LANES:(g + 1) * LANES]
            lo = jnp.where(lane_lo, q, 0.0).astype(BF16).reshape(CTX_PER_STEP, SEQ, LANES)
            hi = jnp.where(lane_lo, 0.0, q).astype(BF16).reshape(CTX_PER_STEP, SEQ, LANES)
            qst_scr[:, g * SEQ:(g + 1) * SEQ, :] = lo
            qst_scr[:, (4 + g) * SEQ:(5 + g) * SEQ, :] = hi
        kva = proj(C_KA, C_QB)
        ko_ref[...] = per_seq(kva[:, :LANES])
        vo_ref[...] = per_seq(kva[:, LANES:])
        k_scr[...] = kva[:, :LANES].astype(BF16)
        vaug_scr[:, :LANES] = kva[:, LANES:].astype(BF16)
        vaug_scr[:, LANES:] = ones
        qc = proj(C_QC, C_CKV).astype(BF16)
        for h in range(MLA_HEADS):
            qc_scr[h] = qc[:, h * LANES:(h + 1) * LANES]
        ckr = proj(C_CKV, D_IN_P)
        ckv, kr = ckr[:, :LANES], ckr[:, LANES:]
        ckv_n = ckv * lax.rsqrt(jnp.mean(ckv * ckv, axis=-1, keepdims=True) + EPS) * kvn_ref[...]
        ckvo_ref[...] = per_seq(ckv_n)
        kro_ref[...] = per_seq(kr[:, 64:64 + MLA_ROPE])
        kv = _dot(ckv_n.astype(BF16), wkvb_ref[...])
        for h in range(MLA_HEADS):
            kvh = kv[:, h * LANES:(h + 1) * LANES]
            kcat_scr[h] = jnp.where(lane_lo, kvh, kr).astype(BF16)
            kvaug_scr[h, :, :LANES] = kvh.astype(BF16)
            kvaug_scr[h, :, LANES:] = ones

        sw, pw, ew, sm, pm = (sw0, sw1), (pw0, pw1), (ew0, ew1), (sm0, sm1), (pm0, pm1)
        lane_lo_s = _lane_iota((SEQ, LANES)) < 64

        def scores(e, b):
            rows = _ds(e * SEQ, SEQ)
            sw[b][...] = _dot_nt(qst_scr[e], k_scr[rows, :])
            for h in range(MLA_HEADS):
                sm[b][h * SEQ:(h + 1) * SEQ, :] = _dot_nt(qc_scr[h, rows, :], kcat_scr[h, rows, :])
            ret[0].scores(seq_chunks(e), b)

        def softmax(e, b):
            for h in range(WIN_HEADS):
                hr = slice(h * SEQ, (h + 1) * SEQ)
                sink = jnp.full((SEQ, 1), sink_ref[h], F32)
                (p,), extra = _softmax_tile([sw[b][hr, :]], floor=sink)
                pw[b][hr, :] = p
                ew[b][hr, :] = jnp.broadcast_to(extra, (SEQ, LANES))
            for h in range(MLA_HEADS):
                hr = slice(h * SEQ, (h + 1) * SEQ)
                (p,), _ = _softmax_tile([sm[b][hr, :]], scale=MLA_SCALE)
                pm[b][hr, :] = p
            ret[0].mask(seq_chunks(e), b)

        def values(e, b):
            ret[0].values(seq_chunks(e), b)
            rows = _ds(e * SEQ, SEQ)
            oa = _dot(pw[b][...], vaug_scr[rows, :])
            o = oa[:, :LANES] / (oa[:, LANES:] + ew[b][...])
            for g in range(4):
                merged = jnp.where(lane_lo_s, o[g * SEQ:(g + 1) * SEQ], o[(4 + g) * SEQ:(5 + g) * SEQ])
                mix_scr[rows, M_A + g * LANES:M_A + (g + 1) * LANES] = merged.astype(BF16)
            for jp in range(2):
                outs = []
                for h in (2 * jp, 2 * jp + 1):
                    oc = _dot(pm[b][h * SEQ:(h + 1) * SEQ, :], kvaug_scr[h, rows, :])
                    outs.append(oc[:, :LANES] / oc[:, LANES:])
                merged = jnp.where(lane_lo_s, pltpu.roll(outs[0], 64, 1), outs[1])
                mix_scr[rows, M_C + jp * LANES:M_C + (jp + 1) * LANES] = merged.astype(BF16)

        _pipeline(CTX_PER_STEP, scores, softmax, values)

    attention_scratch = (
        pltpu.VMEM((CTX_PER_STEP, CTX_WROWS, LANES), BF16),
        pltpu.VMEM((TOK, LANES), BF16),
        pltpu.VMEM((TOK, 2 * LANES), BF16),
        pltpu.VMEM((MLA_HEADS, TOK, LANES), BF16),
        pltpu.VMEM((MLA_HEADS, TOK, LANES), BF16),
        pltpu.VMEM((MLA_HEADS, TOK, 2 * LANES), BF16),
        pltpu.VMEM((CTX_WROWS, SEQ), F32), pltpu.VMEM((CTX_WROWS, SEQ), F32),
        pltpu.VMEM((CTX_WROWS, SEQ), BF16), pltpu.VMEM((CTX_WROWS, SEQ), BF16),
        pltpu.VMEM((CTX_WROWS, LANES), F32), pltpu.VMEM((CTX_WROWS, LANES), F32),
        pltpu.VMEM((CTX_MROWS, SEQ), F32), pltpu.VMEM((CTX_MROWS, SEQ), F32),
        pltpu.VMEM((CTX_MROWS, SEQ), BF16), pltpu.VMEM((CTX_MROWS, SEQ), BF16),
    )

    def retention(*scratch):
        r = _Retention(scratch, gn_ref, mix_scr)
        r.st[...] = jnp.zeros(r.st.shape, F32)
        r.prepare(proj, decay_ref, [(chunks_per_seq * e, chunks_per_seq) for e in range(CTX_PER_STEP)])
        for e in range(CTX_PER_STEP):
            for dirn in range(2):
                for j in range(2):
                    st = r.st[e, dirn, j]
                    so_ref[e, dirn, 2 * j] = st[:64, :64]
                    so_ref[e, dirn, 2 * j + 1] = st[64:, 64:]
        ret.append(r)
        pl.run_scoped(attention, *attention_scratch)

    pl.run_scoped(retention, *_ret_scratch(CTX_PER_STEP, chunks_per_seq))
    xs.epilogue(mod_ref, wout_ref, mix_scr)


def _const_spec(shape, layer=None):
    if layer is None:
        return pl.BlockSpec(shape, lambda i: (0,) * len(shape), pipeline_mode=pl.Buffered(1))
    return pl.BlockSpec((None,) + shape, lambda i: (layer,) + (0,) * len(shape),
                        pipeline_mode=pl.Buffered(1))


def _smem_spec():
    return pl.BlockSpec(memory_space=pltpu.SMEM)


def _ctx_mixer(layer, x, mod, n1, w_in_p, sink, decay, gn, kvn, w_kv_b, w_out_p, prev_state):
    n_tok = x.shape[0]
    n_seq = n_tok // SEQ
    hbm_spec = pl.BlockSpec(memory_space=pl.ANY)
    if prev_state:
        state_spec = lambda *tail: pl.BlockSpec(
            (CTX_PER_STEP, None) + tail, lambda i: (i, layer) + (0,) * len(tail))
    else:
        assert layer == 0
        state_spec = lambda *tail: pl.BlockSpec(
            (CTX_PER_STEP, DEPTH) + tail, lambda i: (i, 0) + (0,) * len(tail))
    state_tails = [(SEQ, LANES), (SEQ, LANES), (SEQ, MLA_KV_RANK), (SEQ, MLA_ROPE),
                   (2, RET_HEADS, HEAD_DIM, HEAD_DIM)]
    n_in = 10
    return pl.pallas_call(
        functools.partial(_ctx_mixer_kernel, len(prev_state)),
        out_shape=[jax.ShapeDtypeStruct((n_tok, D_MODEL), F32)] + [
            jax.ShapeDtypeStruct((n_seq, DEPTH) + tail, F32) for tail in state_tails],
        grid=(n_tok // TOK,),
        in_specs=[
            hbm_spec,
            _const_spec((N_MOD, D_MODEL)),
            _const_spec((1, D_MODEL)),
            _const_spec((D_IN_P, D_MODEL), layer),
            _smem_spec(),
            _smem_spec(),
            _const_spec((1, RET_HEADS * HEAD_DIM)),
            _const_spec((1, MLA_KV_RANK)),
            _const_spec((MLA_KV_RANK, MLA_HEADS * LANES), layer),
            _const_spec((D_MODEL, D_MODEL), layer),
        ] + [hbm_spec] * len(prev_state),
        out_specs=[hbm_spec] + [state_spec(*tail) for tail in state_tails],
        input_output_aliases={n_in + k: 1 + k for k in range(len(prev_state))},
        scratch_shapes=[
            pltpu.VMEM((TOK, D_MODEL), BF16),
            pltpu.VMEM((TOK, D_MODEL), BF16),
        ] + _XStream.SCRATCH,
        compiler_params=pltpu.CompilerParams(
            dimension_semantics=("arbitrary",), vmem_limit_bytes=VMEM_LIMIT),
        name="ctx_mixer",
    )(x, mod, n1, w_in_p, sink, decay, gn, kvn, w_kv_b, w_out_p, *prev_state)


N_BLK = DEC_SEQ // LANES
KEYS_LOC = 3 * LANES
KEYS_WIN = KEYS_LOC + PAST_LEN
WIN_ROWS = WIN_HEADS * LANES
MLA_QB = 256
MLA_KEYS = DEC_SEQ + PAST_LEN
MLA_HALF = MLA_KEYS // 2


def _lat_mixer_kernel(x_ref, mod_ref, n1_ref, win_ref, sink_ref, decay_ref, gn_ref, kvn_ref,
                      wkvb_ref, wout_ref, ck_ref, cv_ref, cckv_ref, ckr_ref, s0_ref,
                      rc_ref, rsa_ref, rsb_ref, mc_ref, msa_ref, msb_ref,
                      xo_ref,
                      h_scr, mix_scr, s0_scr, s1_scr, p0_scr, p1_scr, *x_scratch):
    xs = _XStream(x_ref, xo_ref, x_scratch)
    xs.prologue(mod_ref, n1_ref, h_scr)
    sbuf, pbuf = (s0_scr, s1_scr), (p0_scr, p1_scr)

    def proj(c0, c1):
        return _dot_nt(h_scr[...], win_ref[c0:c1, :])

    ret = []

    def window(qst_scr, kpad_scr, vaug_scr, ckb_scr, cvaug_scr, bias_scr, e0_scr, e1_scr):
        ebuf = (e0_scr, e1_scr)
        lane_lo = _lane_iota((TOK, LANES)) < 64
        rc, rsa, rsb = rc_ref[...], rsa_ref[...], rsb_ref[...]
        qa = proj(C_QA, C_KA)
        for g in range(4):
            q = _rope(qa[:, g * LANES:(g + 1) * LANES], rc, rsa, rsb, 16) * ATTN_SCALE
            lo = jnp.where(lane_lo, q, 0.0).astype(BF16).reshape(N_BLK, LANES, LANES)
            hi = jnp.where(lane_lo, 0.0, q).astype(BF16).reshape(N_BLK, LANES, LANES)
            qst_scr[:, g * LANES:(g + 1) * LANES, :] = lo
            qst_scr[:, (4 + g) * LANES:(5 + g) * LANES, :] = hi
        kva = proj(C_KA, C_QB)
        zpad = jnp.zeros((LANES, LANES), BF16)
        kpad_scr[0:LANES, :] = zpad
        kpad_scr[LANES + TOK:, :] = zpad
        vaug_scr[0:LANES, :LANES] = zpad
        vaug_scr[LANES + TOK:, :LANES] = zpad
        kpad_scr[LANES:LANES + TOK, :] = _rope(kva[:, :LANES], rc, rsa, rsb, 16).astype(BF16)
        vaug_scr[LANES:LANES + TOK, :LANES] = kva[:, LANES:].astype(BF16)
        vaug_scr[:, LANES:] = jnp.ones((TOK + 2 * LANES, LANES), BF16)
        ckb_scr[...] = ck_ref[...].astype(BF16)
        cvaug_scr[:, :LANES] = cv_ref[...].astype(BF16)
        cvaug_scr[:, LANES:] = jnp.ones((PAST_LEN, LANES), BF16)
        qi, kj = _row_iota((LANES, LANES)), _lane_iota((LANES, LANES))
        bias_scr[0] = jnp.full((LANES, LANES), -jnp.inf, F32)
        bias_scr[1] = jnp.where(kj >= qi, 0.0, -jnp.inf)
        bias_scr[2] = jnp.where(kj <= qi, 0.0, -jnp.inf)
        lane_lo_b = _lane_iota((LANES, LANES)) < 64

        def scores(n, b):
            q = qst_scr[n]
            sbuf[b][:, :KEYS_LOC] = _dot_nt(q, kpad_scr[_ds(n * LANES, KEYS_LOC), :])
            sbuf[b][:, KEYS_LOC:] = _dot_nt(q, ckb_scr[...])
            ret[0].scores([n], b)

        def softmax(n, b):
            if isinstance(n, int):
                i_prev, i_next = (1 if n > 0 else 0), (2 if n < N_BLK - 1 else 0)
            else:
                i_prev, i_next = jnp.where(n > 0, 1, 0), jnp.where(n < N_BLK - 1, 2, 0)
            b_prev, b_next = bias_scr[i_prev], bias_scr[i_next]
            for h in range(WIN_HEADS):
                hr = slice(h * LANES, (h + 1) * LANES)
                parts = [sbuf[b][hr, 0:LANES] + b_prev, sbuf[b][hr, LANES:2 * LANES],
                         sbuf[b][hr, 2 * LANES:KEYS_LOC] + b_next, sbuf[b][hr, KEYS_LOC:]]
                ps, extra = _softmax_tile(parts, floor=jnp.full((LANES, 1), sink_ref[h], F32))
                pbuf[b][hr, 0:LANES] = ps[0]
                pbuf[b][hr, LANES:2 * LANES] = ps[1]
                pbuf[b][hr, 2 * LANES:KEYS_LOC] = ps[2]
                pbuf[b][hr, KEYS_LOC:] = ps[3]
                ebuf[b][hr, :] = jnp.broadcast_to(extra, (LANES, LANES))
            ret[0].mask([n], b)

        def values(n, b):
            ret[0].values([n], b)
            oa = (_dot(pbuf[b][:, :KEYS_LOC], vaug_scr[_ds(n * LANES, KEYS_LOC), :])
                  + _dot(pbuf[b][:, KEYS_LOC:], cvaug_scr[...]))
            o = oa[:, :LANES] / (oa[:, LANES:] + ebuf[b][...])
            for g in range(4):
                merged = jnp.where(lane_lo_b, o[g * LANES:(g + 1) * LANES],
                                   o[(4 + g) * LANES:(5 + g) * LANES])
                mix_scr[_ds(n * LANES, LANES), M_A + g * LANES:M_A + (g + 1) * LANES] = (
                    merged.astype(BF16))

        _pipeline(N_BLK, scores, softmax, values)

    window_scratch = (
        pltpu.VMEM((N_BLK, WIN_ROWS, LANES), BF16),
        pltpu.VMEM((TOK + 2 * LANES, LANES), BF16),
        pltpu.VMEM((TOK + 2 * LANES, 2 * LANES), BF16),
        pltpu.VMEM((PAST_LEN, LANES), BF16),
        pltpu.VMEM((PAST_LEN, 2 * LANES), BF16),
        pltpu.VMEM((3, LANES, LANES), F32),
        pltpu.VMEM((WIN_ROWS, LANES), F32), pltpu.VMEM((WIN_ROWS, LANES), F32),
    )

    def retention(*scratch):
        r = _Retention(scratch, gn_ref, mix_scr)
        r.st[0] = s0_ref[...]
        r.prepare(proj, decay_ref, [(0, N_CHUNK)])
        ret.append(r)
        pl.run_scoped(window, *window_scratch)

    assert N_CHUNK == N_BLK
    pl.run_scoped(retention, *_ret_scratch(1, 1))

    def latent(qc_scr, kcat_scr, kvaug_scr, mixc_scr):
        lane_lo = _lane_iota((TOK, LANES)) < 64
        mc, msa, msb = mc_ref[...], msa_ref[...], msb_ref[...]
        qc = proj(C_QC, C_CKV)
        for h in range(MLA_HEADS):
            qc_scr[h] = _rope(qc[:, h * LANES:(h + 1) * LANES], mc, msa, msb, 8).astype(BF16)
        ckr = proj(C_CKV, D_IN_P)
        ckv = ckr[:, :LANES]
        kr = _rope(ckr[:, LANES:], mc, msa, msb, 8)
        ckv_n = ckv * lax.rsqrt(jnp.mean(ckv * ckv, axis=-1, keepdims=True) + EPS) * kvn_ref[...]
        kv = _dot(ckv_n.astype(BF16), wkvb_ref[...])
        kv_c = _dot(cckv_ref[...].astype(BF16), wkvb_ref[...])
        kr_c = ckr_ref[...]
        lane_lo_c = _lane_iota((PAST_LEN, LANES)) < 64
        for h in range(MLA_HEADS):
            kvh, kvh_c = kv[:, h * LANES:(h + 1) * LANES], kv_c[:, h * LANES:(h + 1) * LANES]
            kcat_scr[h, 0:TOK, :] = jnp.where(lane_lo, kvh, kr).astype(BF16)
            kcat_scr[h, TOK:, :] = jnp.where(lane_lo_c, kvh_c, kr_c).astype(BF16)
            kvaug_scr[h, 0:TOK, :LANES] = kvh.astype(BF16)
            kvaug_scr[h, TOK:, :LANES] = kvh_c.astype(BF16)
            kvaug_scr[h, :, LANES:] = jnp.ones((MLA_KEYS, LANES), BF16)
        lane_lo_m = _lane_iota((MLA_QB, LANES)) < 64
        n_qb = DEC_SEQ // MLA_QB

        def split(t):
            if isinstance(t, int):
                return t // n_qb, t % n_qb
            return lax.shift_right_logical(t, 2), lax.bitwise_and(t, n_qb - 1)

        def scores(t, b):
            jp, qb = split(t)
            for i in range(2):
                h = 2 * jp + i
                q = qc_scr[h, _ds(qb * MLA_QB, MLA_QB), :]
                for part in range(2):
                    r0 = (2 * i + part) * MLA_QB
                    sbuf[b][r0:r0 + MLA_QB, :] = _dot_nt(
                        q, kcat_scr[h, part * MLA_HALF:(part + 1) * MLA_HALF, :])

        def softmax(t, b):
            for i in range(2):
                for rt in range(MLA_QB // LANES):
                    ra = 2 * i * MLA_QB + rt * LANES
                    rb = ra + MLA_QB
                    ps, _ = _softmax_tile([sbuf[b][ra:ra + LANES, :], sbuf[b][rb:rb + LANES, :]],
                                          scale=MLA_SCALE)
                    pbuf[b][ra:ra + LANES, :] = ps[0]
                    pbuf[b][rb:rb + LANES, :] = ps[1]

        def values(t, b):
            jp, qb = split(t)
            outs = []
            for i in range(2):
                h = 2 * jp + i
                r0 = 2 * i * MLA_QB
                oc = (_dot(pbuf[b][r0:r0 + MLA_QB, :], kvaug_scr[h, 0:MLA_HALF, :])
                      + _dot(pbuf[b][r0 + MLA_QB:r0 + 2 * MLA_QB, :], kvaug_scr[h, MLA_HALF:, :]))
                outs.append(oc[:, :LANES] / oc[:, LANES:])
            merged = jnp.where(lane_lo_m, pltpu.roll(outs[0], 64, 1), outs[1])
            mixc_scr[jp, _ds(qb * MLA_QB, MLA_QB), :] = merged.astype(BF16)

        _pipeline(2 * n_qb, scores, softmax, values)
        for jp in range(2):
            mix_scr[:, M_C + jp * LANES:M_C + (jp + 1) * LANES] = mixc_scr[jp]

    pl.run_scoped(
        latent,
        pltpu.VMEM((MLA_HEADS, TOK, LANES), BF16),
        pltpu.VMEM((MLA_HEADS, MLA_KEYS, LANES), BF16),
        pltpu.VMEM((MLA_HEADS, MLA_KEYS, 2 * LANES), BF16),
        pltpu.VMEM((2, TOK, LANES), BF16),
    )
    xs.epilogue(mod_ref, wout_ref, mix_scr)


def _lat_mixer(layer, x, mod, n1, w_in_p, sink, decay, gn, kvn, w_kv_b, w_out_p,
               ck, cv, cckv, ckr, s0, rope_a, rope_m):
    n_tok = x.shape[0]
    n_seq = n_tok // DEC_SEQ
    seq_spec = lambda shape: pl.BlockSpec(
        (None, None) + shape, lambda i: (i, layer) + (0,) * len(shape))
    assert WIN_ROWS == 4 * MLA_QB and KEYS_WIN == MLA_HALF
    return pl.pallas_call(
        _lat_mixer_kernel,
        out_shape=jax.ShapeDtypeStruct((n_tok, D_MODEL), F32),
        grid=(n_seq,),
        in_specs=[
            pl.BlockSpec(memory_space=pl.ANY),
            pl.BlockSpec((None, N_MOD, D_MODEL), lambda i: (i, 0, 0)),
            _const_spec((1, D_MODEL)),
            _const_spec((D_IN_P, D_MODEL), layer),
            _smem_spec(),
            _smem_spec(),
            _const_spec((1, RET_HEADS * HEAD_DIM)),
            _const_spec((1, MLA_KV_RANK)),
            _const_spec((MLA_KV_RANK, MLA_HEADS * LANES), layer),
            _const_spec((D_MODEL, D_MODEL), layer),
            seq_spec((PAST_LEN, LANES)),
            seq_spec((PAST_LEN, LANES)),
            seq_spec((PAST_LEN, MLA_KV_RANK)),
            seq_spec((PAST_LEN, LANES)),
            seq_spec((2, 2, LANES, LANES)),
        ] + [_const_spec((DEC_SEQ, LANES))] * 6,
        out_specs=pl.BlockSpec(memory_space=pl.ANY),
        scratch_shapes=[
            pltpu.VMEM((TOK, D_MODEL), BF16),
            pltpu.VMEM((TOK, D_MODEL), BF16),
            pltpu.VMEM((WIN_ROWS, KEYS_WIN), F32), pltpu.VMEM((WIN_ROWS, KEYS_WIN), F32),
            pltpu.VMEM((WIN_ROWS, KEYS_WIN), BF16), pltpu.VMEM((WIN_ROWS, KEYS_WIN), BF16),
        ] + _XStream.SCRATCH,
        compiler_params=pltpu.CompilerParams(
            dimension_semantics=("arbitrary",), vmem_limit_bytes=VMEM_LIMIT),
        name="lat_mixer",
    )(x, mod, n1, w_in_p, sink, decay, gn, kvn, w_kv_b, w_out_p, ck, cv, cckv, ckr, s0,
      *rope_a, *rope_m)


def _mlp_kernel(final, x_ref, mod_ref, n2_ref, wup_ref, wdn_ref, fn_ref, o_ref):
    x = x_ref[...]
    h2 = _norm_mod(x, n2_ref[...], mod_ref[4:5, :], mod_ref[3:4, :]).astype(BF16)
    acc = None
    for c in range(D_FF // FF_CHUNK):
        cols = slice(c * FF_CHUNK, (c + 1) * FF_CHUNK)
        u = jnp.maximum(_dot(h2, wup_ref[:, cols]), 0.0)
        part = _dot((u * u).astype(BF16), wdn_ref[cols, :])
        acc = part if acc is None else acc + part
    y = x + mod_ref[5:6, :] * acc
    if final:
        y = y * lax.rsqrt(jnp.mean(y * y, axis=-1, keepdims=True) + EPS) * fn_ref[...]
    o_ref[...] = y


def _mlp(layer, x, mod, n2, w_up, w_down, final_norm, final):
    n_tok = x.shape[0]
    per_mod = n_tok // mod.shape[0] // MLP_ROWS
    return pl.pallas_call(
        functools.partial(_mlp_kernel, final),
        out_shape=jax.ShapeDtypeStruct((n_tok, D_MODEL), F32),
        grid=(n_tok // MLP_ROWS,),
        in_specs=[
            pl.BlockSpec((MLP_ROWS, D_MODEL), lambda i: (i, 0)),
            pl.BlockSpec((None, N_MOD, D_MODEL), lambda i: (i // per_mod, 0, 0)),
            _const_spec((1, D_MODEL)),
            _const_spec((D_MODEL, D_FF), layer),
            _const_spec((D_FF, D_MODEL), layer),
            _const_spec((1, D_MODEL)),
        ],
        out_specs=pl.BlockSpec((MLP_ROWS, D_MODEL), lambda i: (i, 0)),
        compiler_params=pltpu.CompilerParams(
            dimension_semantics=("arbitrary",), vmem_limit_bytes=VMEM_LIMIT),
        name="mlp",
    )(x, mod, n2, w_up, w_down, final_norm)


def kernel(x_prompt, x_sample, cache_win_k, cache_win_v, cache_mla_ckv, cache_mla_krope, state_ret,
           c, c_ctx, w_mod, b_mod, norm1, norm2, w_in, win_sink, ret_decay, ret_gn, mla_kv_norm,
           w_kv_b, w_out, w_up, w_down, final_norm):
    n_ctx, n_lat = x_prompt.shape[0], x_sample.shape[0]

    w_in_p, w_out_p = _proj_layout(jnp.swapaxes(w_in, 1, 2), w_out)
    w_kv_b16, w_up16, w_down16 = w_kv_b.astype(BF16), w_up.astype(BF16), w_down.astype(BF16)

    c_rows = jnp.zeros((16, D_MODEL), F32).at[0].set(c_ctx).at[1:1 + n_lat].set(c)
    mod = _modulation(c_rows, w_mod, b_mod).reshape(DEPTH, 16, N_MOD, D_MODEL)

    rope_a = _rope_tables(DEC_SEQ, HEAD_DIM, 0, HEAD_DIM)
    rope_m = _rope_tables(DEC_SEQ, MLA_ROPE, MLA_NOPE, LANES)

    ck = cache_win_k.reshape(n_lat, DEPTH, PAST_LEN, LANES)
    cv = cache_win_v.reshape(n_lat, DEPTH, PAST_LEN, LANES)
    ckr = jnp.pad(cache_mla_krope, ((0, 0), (0, 0), (0, 0), (MLA_NOPE, LANES - MLA_NOPE - MLA_ROPE)))
    sr = state_ret.reshape(n_lat, DEPTH, 2, 2, 2, HEAD_DIM, HEAD_DIM)
    zero = jnp.zeros_like(sr[:, :, :, :, 0])
    s0 = jnp.concatenate([jnp.concatenate([sr[:, :, :, :, 0], zero], axis=-1),
                          jnp.concatenate([zero, sr[:, :, :, :, 1]], axis=-1)], axis=-2)

    xp = x_prompt.reshape(n_ctx * SEQ, D_MODEL)
    xs = x_sample.reshape(n_lat * DEC_SEQ, D_MODEL)
    state = ()
    for l in range(DEPTH):
        last = l == DEPTH - 1
        shared = (norm1[l][None], w_in_p, win_sink[l], ret_decay[l], ret_gn[l][None],
                  mla_kv_norm[l][None], w_kv_b16, w_out_p)
        mod_ctx, mod_lat = mod[l, 0:1], mod[l, 1:1 + n_lat]
        xp, *state = _ctx_mixer(l, xp, mod_ctx[0], *shared, state)
        xp = _mlp(l, xp, mod_ctx, norm2[l][None], w_up16, w_down16, final_norm[None], last)
        xs = _lat_mixer(l, xs, mod_lat, *shared, ck, cv, cache_mla_ckv, ckr, s0, rope_a, rope_m)
        xs = _mlp(l, xs, mod_lat, norm2[l][None], w_up16, w_down16, final_norm[None], last)
    new_k, new_v, new_ckv, new_kr, new_s = state
    return (xp.reshape(n_ctx, SEQ, D_MODEL), xs.reshape(n_lat, DEC_SEQ, D_MODEL),
            new_k.reshape(n_ctx, DEPTH, SEQ, 2, HEAD_DIM), new_v.reshape(n_ctx, DEPTH, SEQ, 2, HEAD_DIM),
            new_ckv, new_kr, new_s)
```

```python
import functools

import numpy as np
import jax
import jax.numpy as jnp
from jax import lax
from jax.experimental import pallas as pl
from jax.experimental.pallas import tpu as pltpu

F32 = jnp.float32
BF16 = jnp.bfloat16

D_MODEL = 1024
DEPTH = 2
SEQ = 256
DEC_SEQ = 1024
PAST_LEN = 256
GRID_W = 64
HEAD_DIM = 64
ROPE_BASE = 10000.0
EPS = 1e-6
WIN_HEADS = 8
WINDOW = 128
ATTN_SCALE = HEAD_DIM ** -0.5
RET_HEADS = 4
RET_CHUNK = 128
RET_K_SCALE = HEAD_DIM ** -0.5
MLA_HEADS = 4
MLA_NOPE = 64
MLA_ROPE = 32
MLA_V = 64
MLA_KV_RANK = 128
MLA_QK = MLA_NOPE + MLA_ROPE
MLA_SCALE = MLA_QK ** -0.5
D_IN = 2336
D_FF = 4 * D_MODEL
N_MOD = 6

LANES = 128
TOK = 1024
MLP_ROWS = 1024
FF_CHUNK = 1024
VMEM_LIMIT = 60 * 1024 * 1024

C_QA, C_KA, C_VA, C_QB, C_KB, C_VB, C_GB, C_QN, C_QR, C_CKV, C_KR, D_IN_P = (
    0, 512, 640, 768, 1024, 1280, 1536, 1792, 2048, 2176, 2304, 2432)
M_A, M_B, M_C = 0, 512, 768

NT_DIMS = (((1,), (1,)), ((), ()))


def _in_proj_columns():
    idx = []
    for g in range(4):
        idx += list(range(g * 64, (g + 1) * 64)) + list(range((4 + g) * 64, (5 + g) * 64))
    idx += list(range(512, 1792))
    for h in range(MLA_HEADS):
        idx += list(range(1792 + h * MLA_QK, 1792 + h * MLA_QK + MLA_NOPE))
    for h in range(MLA_HEADS):
        idx += list(range(1792 + h * MLA_QK + MLA_NOPE, 1792 + (h + 1) * MLA_QK))
    idx += list(range(2176, 2304))
    idx += list(range(2304, 2336)) * MLA_HEADS
    return np.asarray(idx, np.int32)


def _kv_expand_columns():
    width = MLA_NOPE + MLA_V
    nope = [list(range(h * width, h * width + MLA_NOPE)) for h in range(MLA_HEADS)]
    val = [list(range(h * width + MLA_NOPE, (h + 1) * width)) for h in range(MLA_HEADS)]
    return np.asarray(sum(nope, []) + sum(val, []), np.int32)


def _mix_rows():
    idx = []
    for g in range(4):
        idx += list(range(g * 64, (g + 1) * 64)) + list(range((4 + g) * 64, (5 + g) * 64))
    idx += list(range(512, 1024))
    return np.asarray(idx, np.int32)


def _take_runs(w, idx, axis):
    pieces, i = [], 0
    while i < len(idx):
        j = i + 1
        if idx[i] < 0:
            while j < len(idx) and idx[j] < 0:
                j += 1
            shape = list(w.shape)
            shape[axis] = j - i
            pieces.append(jnp.zeros(shape, w.dtype))
        else:
            while j < len(idx) and idx[j] == idx[j - 1] + 1:
                j += 1
            pieces.append(lax.slice_in_dim(w, int(idx[i]), int(idx[j - 1]) + 1, axis=axis))
        i = j
    return jnp.concatenate(pieces, axis=axis)


def _rope_tables(n_tokens, dim, lane0, period):
    quarter = dim // 4
    t = np.arange(n_tokens)
    row = (t // GRID_W).astype(np.float64)
    col = (t % GRID_W).astype(np.float64)
    inv_freq = ROPE_BASE ** (-np.arange(quarter, dtype=np.float64) / quarter)
    ar, ac = row[:, None] * inv_freq, col[:, None] * inv_freq
    cos = np.concatenate([np.cos(ar), np.cos(ar), np.cos(ac), np.cos(ac)], axis=-1)
    sin = np.concatenate([np.sin(ar), np.sin(ar), np.sin(ac), np.sin(ac)], axis=-1)
    first = np.tile(np.concatenate([np.ones(quarter), np.zeros(quarter)]), 2)
    c = np.ones((n_tokens, LANES))
    sa = np.zeros((n_tokens, LANES))
    sb = np.zeros((n_tokens, LANES))
    for start in range(lane0, LANES, period):
        c[:, start:start + dim] = cos
        sa[:, start:start + dim] = -sin * first
        sb[:, start:start + dim] = sin * (1.0 - first)
    return tuple(jnp.asarray(a, F32) for a in (c, sa, sb))


def _lane_iota(shape):
    return lax.broadcasted_iota(jnp.int32, shape, len(shape) - 1)


def _row_iota(shape):
    return lax.broadcasted_iota(jnp.int32, shape, len(shape) - 2)


def _ds(start, size):
    if isinstance(start, int):
        return pl.ds(start, size)
    return pl.ds(pl.multiple_of(start, LANES), size)


def _norm_mod(x, gain, scale, shift):
    ms = jnp.mean(x * x, axis=-1, keepdims=True)
    return (x * lax.rsqrt(ms + EPS) * gain) * (1.0 + scale) + shift


def _log_sigmoid(x):
    return -(jnp.maximum(-x, 0.0) + jnp.log1p(jnp.exp(-jnp.abs(x))))


def _rope(x, c, sa, sb, quarter):
    return (x * c + pltpu.roll(x, LANES - quarter, 1) * sa + pltpu.roll(x, quarter, 1) * sb)


def _dot(a, b):
    return jnp.dot(a, b, preferred_element_type=F32)


def _dot_nt(a, b):
    return lax.dot_general(a, b, NT_DIMS, preferred_element_type=F32)


def _mod_kernel(c_ref, w_ref, b_ref, o_ref):
    cv = c_ref[...]
    s = cv * jax.nn.sigmoid(cv)
    o_ref[0] = _dot(s.astype(BF16), w_ref[0].astype(BF16)) + b_ref[0]


def _modulation(c_rows, w_mod, b_mod):
    tn = 1536
    nj = (N_MOD * D_MODEL) // tn
    return pl.pallas_call(
        _mod_kernel,
        out_shape=jax.ShapeDtypeStruct((DEPTH, 16, N_MOD * D_MODEL), F32),
        grid=(DEPTH, nj),
        in_specs=[
            pl.BlockSpec((16, D_MODEL), lambda l, j: (0, 0)),
            pl.BlockSpec((1, D_MODEL, tn), lambda l, j: (l, 0, j)),
            pl.BlockSpec((1, 1, tn), lambda l, j: (l, 0, j)),
        ],
        out_specs=pl.BlockSpec((1, 16, tn), lambda l, j: (l, 0, j)),
        compiler_params=pltpu.CompilerParams(
            dimension_semantics=("arbitrary", "arbitrary"), vmem_limit_bytes=VMEM_LIMIT),
        name="modulation",
    )(c_rows, w_mod, b_mod.reshape(DEPTH, 1, N_MOD * D_MODEL))


W_IN_LANES = 256


def _proj_layout_kernel(win_ref, wout_ref, oin_ref, oout_ref):
    oin_ref[0] = _take_runs(win_ref[0], _in_proj_columns(), 0).astype(BF16)
    oout_ref[0] = _take_runs(wout_ref[0], _mix_rows(), 0).astype(BF16)


def _proj_layout(w_in_t, w_out):
    col_spec = lambda rows: pl.BlockSpec((1, rows, W_IN_LANES), lambda l, c: (l, 0, c))
    return pl.pallas_call(
        _proj_layout_kernel,
        out_shape=(jax.ShapeDtypeStruct((DEPTH, D_IN_P, D_MODEL), BF16),
                   jax.ShapeDtypeStruct((DEPTH, D_MODEL, D_MODEL), BF16)),
        grid=(DEPTH, D_MODEL // W_IN_LANES),
        in_specs=[col_spec(D_IN), col_spec(D_MODEL)],
        out_specs=(col_spec(D_IN_P), col_spec(D_MODEL)),
        compiler_params=pltpu.CompilerParams(
            dimension_semantics=("arbitrary", "arbitrary"), vmem_limit_bytes=VMEM_LIMIT),
        name="proj_layout",
    )(w_in_t, w_out)


def _pipeline(n_steps, scores, softmax, values):
    scores(0, 0)
    scores(1, 1)
    softmax(0, 0)

    def body(i, carry):
        t = 2 * i + 1
        scores(t + 1, 0)
        softmax(t, 1)
        values(t - 1, 0)
        scores(t + 2, 1)
        softmax(t + 1, 0)
        values(t, 1)
        return carry

    lax.fori_loop(0, n_steps // 2 - 1, body, 0)
    softmax(n_steps - 1, 1)
    values(n_steps - 2, 0)
    values(n_steps - 1, 1)


def _softmax_tile(parts, floor=None, scale=None):
    m = None
    for s in parts:
        pm = jnp.max(s, axis=-1, keepdims=True)
        m = pm if m is None else jnp.maximum(m, pm)
    if floor is not None:
        m = jnp.maximum(m, floor)
    if scale is None:
        ps = [jnp.exp(s - m).astype(BF16) for s in parts]
    else:
        ps = [jnp.exp((s - m) * scale).astype(BF16) for s in parts]
    extra = None if floor is None else jnp.exp(floor - m)
    return ps, extra


def _latent_queries(qn, qr, qc_scr):
    lane = _lane_iota(qr.shape)
    for h in range(MLA_HEADS):
        pair, half = divmod(h, 2)
        qn_h = jnp.where((lane < MLA_NOPE) == (half == 0), qn[:, pair * LANES:(pair + 1) * LANES], 0.0)
        qr_h = jnp.where((lane >= h * MLA_ROPE) & (lane < (h + 1) * MLA_ROPE), qr, 0.0)
        qc_scr[h, :, :LANES] = qn_h.astype(BF16)
        qc_scr[h, :, LANES:] = qr_h.astype(BF16)


def _latent_keys(kv, kr, rows, kcat_scr, kvaug_scr):
    ones = jnp.ones(kr.shape, BF16)
    for p in range(2):
        kcat_scr[p, rows, :LANES] = kv[:, p * LANES:(p + 1) * LANES].astype(BF16)
        kcat_scr[p, rows, LANES:] = kr.astype(BF16)
        kvaug_scr[p, rows, :LANES] = kv[:, (2 + p) * LANES:(3 + p) * LANES].astype(BF16)
        kvaug_scr[p, rows, LANES:] = ones


def _retention_tables(decay_ref, rt_scr, m2_scr):
    shape = (RET_CHUNK, LANES)
    lane_lo = _lane_iota(shape) < 64
    row = _row_iota(shape)
    row_lo = row < 64
    i = row.astype(F32)
    rel = i - _lane_iota(shape).astype(F32)
    for j in range(2):
        df0, df1 = decay_ref[0, 2 * j], decay_ref[0, 2 * j + 1]
        db0, db1 = decay_ref[1, 2 * j], decay_ref[1, 2 * j + 1]
        lgf = _log_sigmoid(jnp.where(lane_lo, df0, df1))
        lgb = _log_sigmoid(jnp.where(lane_lo, db0, db1))
        rt_scr[j, 0] = jnp.exp((i + 1.0) * lgf)
        rt_scr[j, 1] = jnp.exp((RET_CHUNK - i) * lgb)
        rt_scr[j, 2] = jnp.exp((RET_CHUNK - 1.0 - i) * lgf)
        rt_scr[j, 3] = jnp.exp(i * lgb)
        rt_scr[j, 4] = jnp.exp(RET_CHUNK * _log_sigmoid(jnp.where(row_lo, df0, df1)))
        rt_scr[j, 5] = jnp.exp(RET_CHUNK * _log_sigmoid(jnp.where(row_lo, db0, db1)))
        for hh, (df, db) in enumerate(((df0, db0), (df1, db1))):
            lf = _log_sigmoid(jnp.full(shape, df, F32))
            lb = _log_sigmoid(jnp.full(shape, db, F32))
            low, upp = rel >= 0.0, rel <= 0.0
            m = (jnp.where(low, jnp.exp(jnp.where(low, rel, 0.0) * lf), 0.0)
                 + jnp.where(upp, jnp.exp(jnp.where(upp, -rel, 0.0) * lb), 0.0))
            m2_scr[j, :, hh * RET_CHUNK:(hh + 1) * RET_CHUNK] = m


N_CHUNK = TOK // RET_CHUNK


def _ret_scratch(n_seq, chunks_per_stage):
    per_chunk = lambda dtype: pltpu.VMEM((N_CHUNK, 2, 2 * RET_CHUNK, LANES), dtype)
    stage = lambda dtype: pltpu.VMEM((2 * chunks_per_stage, RET_CHUNK, 2 * LANES), dtype)
    return [
        pltpu.VMEM((TOK, 2 * LANES), BF16),
        pltpu.VMEM((TOK, 4 * LANES), BF16),
        per_chunk(BF16),
        pltpu.VMEM((TOK, 2 * LANES), BF16),
        per_chunk(BF16),
        pltpu.VMEM((TOK, 2 * LANES), F32),
        pltpu.VMEM((2, 6, RET_CHUNK, LANES), F32),
        pltpu.VMEM((2, RET_CHUNK, 2 * LANES), F32),
        per_chunk(BF16),
        pltpu.VMEM((n_seq, 2, 2, LANES, LANES), F32),
        stage(F32), stage(F32), stage(BF16), stage(BF16),
    ]


class _Retention:
    def __init__(self, scratch, gn_ref, mix_scr):
        (self.q, self.qd, self.kbd, self.v, self.vbd, self.g, self.rt, self.m2,
         self.sall, self.st, rs0, rs1, rp0, rp1) = scratch
        self.rs, self.rp = (rs0, rs1), (rp0, rp1)
        self.gn_ref, self.mix = gn_ref, mix_scr

    def prepare(self, proj, decay_ref, seq_chunks):
        _retention_tables(decay_ref, self.rt, self.m2)
        lane_lo = _lane_iota((TOK, LANES)) < 64
        per_chunk = lambda a: a.reshape(N_CHUNK, RET_CHUNK, LANES)
        qkvg = proj(C_QB, C_QN)
        q = qkvg[:, :C_KB - C_QB]
        k = qkvg[:, C_KB - C_QB:C_VB - C_QB] * RET_K_SCALE
        v = qkvg[:, C_VB - C_QB:C_GB - C_QB]
        self.g[...] = qkvg[:, C_GB - C_QB:]
        self.q[...] = q.astype(BF16)
        self.v[...] = v.astype(BF16)
        for j in range(2):
            cols = slice(j * LANES, (j + 1) * LANES)
            q3 = per_chunk(q[:, cols])
            for d in range(2):
                self.qd[:, (2 * j + d) * LANES:(2 * j + d + 1) * LANES] = (
                    (q3 * self.rt[j, d]).reshape(TOK, LANES).astype(BF16))
            for src, dst in ((k[:, cols], self.kbd), (v[:, cols], self.vbd)):
                dst[:, j, :RET_CHUNK, :] = per_chunk(jnp.where(lane_lo, src, 0.0).astype(BF16))
                dst[:, j, RET_CHUNK:, :] = per_chunk(jnp.where(lane_lo, 0.0, src).astype(BF16))

        pl.run_scoped(
            functools.partial(self._states, k, seq_chunks),
            pltpu.VMEM((TOK, 2 * LANES), F32),
            pltpu.VMEM((N_CHUNK, 2, 2 * RET_CHUNK, LANES), F32))

    def _states(self, k, seq_chunks, k_scr, upd_scr):
        k_scr[...] = k
        blockdiag2 = _lane_iota((2 * RET_CHUNK, LANES)) < 64
        blockdiag2 = blockdiag2 == ((_row_iota((2 * RET_CHUNK, LANES)) & (RET_CHUNK - 1)) < 64)

        def upd_body(c, carry):
            rows = _ds(c * RET_CHUNK, RET_CHUNK)
            for j in range(2):
                cols = slice(j * LANES, (j + 1) * LANES)
                k2 = k_scr[rows, cols]
                kd = jnp.concatenate([k2 * self.rt[j, 2], k2 * self.rt[j, 3]], axis=1)
                upd = _dot(kd.T.astype(BF16), self.v[rows, cols])
                upd_scr[c, j] = jnp.where(blockdiag2, upd, 0.0)
            return carry

        lax.fori_loop(0, N_CHUNK, upd_body, 0, unroll=4)

        n_per_seq = seq_chunks[0][1]
        assert all(n == n_per_seq for _, n in seq_chunks)

        def scan_body(t, carry):
            for si, (first, n) in enumerate(seq_chunks):
                cf, cb = first + t, first + n - 1 - t
                for j in range(2):
                    sf, sb = self.st[si, 0, j], self.st[si, 1, j]
                    self.sall[cf, j, :RET_CHUNK, :] = sf.astype(BF16)
                    self.sall[cb, j, RET_CHUNK:, :] = sb.astype(BF16)
                    self.st[si, 0, j] = self.rt[j, 4] * sf + upd_scr[cf, j, :RET_CHUNK, :]
                    self.st[si, 1, j] = self.rt[j, 5] * sb + upd_scr[cb, j, RET_CHUNK:, :]
            return carry

        lax.fori_loop(0, n_per_seq, scan_body, 0)

    def scores(self, chunks, b):
        for i, c in enumerate(chunks):
            rows = _ds(c * RET_CHUNK, RET_CHUNK)
            for j in range(2):
                self.rs[b][2 * i + j] = _dot_nt(self.q[rows, j * LANES:(j + 1) * LANES], self.kbd[c, j])

    def mask(self, chunks, b):
        for i in range(len(chunks)):
            for j in range(2):
                self.rp[b][2 * i + j] = (self.rs[b][2 * i + j] * self.m2[j]).astype(BF16)

    def values(self, chunks, b):
        lane_lo = _lane_iota((RET_CHUNK, LANES)) < 64
        for i, c in enumerate(chunks):
            rows = _ds(c * RET_CHUNK, RET_CHUNK)
            for j in range(2):
                cols = slice(j * LANES, (j + 1) * LANES)
                o = (_dot(self.rp[b][2 * i + j], self.vbd[c, j])
                     + _dot(self.qd[rows, 2 * j * LANES:2 * (j + 1) * LANES], self.sall[c, j]))
                s_lo = jnp.sum(jnp.where(lane_lo, o, 0.0), axis=-1, keepdims=True)
                s_hi = jnp.sum(jnp.where(lane_lo, 0.0, o), axis=-1, keepdims=True)
                d = o - jnp.where(lane_lo, s_lo, s_hi) * (1.0 / HEAD_DIM)
                dd = d * d
                v_lo = jnp.sum(jnp.where(lane_lo, dd, 0.0), axis=-1, keepdims=True)
                v_hi = jnp.sum(jnp.where(lane_lo, 0.0, dd), axis=-1, keepdims=True)
                var = jnp.where(lane_lo, v_lo, v_hi) * (1.0 / HEAD_DIM)
                g2 = self.g[rows, cols]
                y = d * lax.rsqrt(var + EPS) * self.gn_ref[:, cols] * (g2 * jax.nn.sigmoid(g2))
                self.mix[rows, M_B + j * LANES:M_B + (j + 1) * LANES] = y.astype(BF16)


X_ROWS = 512
N_XCHUNK = TOK // X_ROWS


class _XStream:
    SCRATCH = [pltpu.VMEM((N_XCHUNK, X_ROWS, D_MODEL), F32), pltpu.VMEM((2, X_ROWS, D_MODEL), F32),
               pltpu.SemaphoreType.DMA((N_XCHUNK,)), pltpu.SemaphoreType.DMA((2,))]

    def __init__(self, x_hbm, xo_hbm, scratch):
        self.x_hbm, self.xo_hbm = x_hbm, xo_hbm
        self.xin, self.xout, self.sem_in, self.sem_out = scratch
        self.step, self.n_steps = pl.program_id(0), pl.num_programs(0)

    @staticmethod
    def _rows(step, r):
        return pl.ds(pl.multiple_of(step * TOK + r * X_ROWS, X_ROWS), X_ROWS)

    def load(self, step, r):
        return pltpu.make_async_copy(self.x_hbm.at[self._rows(step, r), :], self.xin.at[r],
                                     self.sem_in.at[r])

    def store(self, r):
        return pltpu.make_async_copy(self.xout.at[r % 2], self.xo_hbm.at[self._rows(self.step, r), :],
                                     self.sem_out.at[r % 2])

    def _norm(self, r, mod, n1_ref, h_scr):
        h = _norm_mod(self.xin[r], n1_ref[...], mod[1:2, :], mod[0:1, :])
        h_scr[r * X_ROWS:(r + 1) * X_ROWS, :] = h.astype(BF16)

    def prologue(self, mod, n1_ref, h_scr):
        @pl.when(self.step == 0)
        def _():
            for r in range(N_XCHUNK):
                self.load(self.step, r).start()

        for r in range(N_XCHUNK):
            self.load(self.step, r).wait()
            self._norm(r, mod, n1_ref, h_scr)

    def epilogue(self, mod, wout_ref, mix_scr):
        for r in range(N_XCHUNK):
            if r >= 2:
                self.store(r - 2).wait()
            else:
                @pl.when(self.step > 0)
                def _():
                    self.store(r).wait()

            y = _dot(mix_scr[r * X_ROWS:(r + 1) * X_ROWS, :], wout_ref[...])
            self.xout[r % 2] = self.xin[r] + mod[2:3, :] * y
            self.store(r).start()

            @pl.when(self.step + 1 < self.n_steps)
            def _():
                self.load(self.step + 1, r).start()

        @pl.when(self.step + 1 == self.n_steps)
        def _():
            for r in range(N_XCHUNK - 2, N_XCHUNK):
                self.store(r).wait()


FF_PIECES = D_FF // D_MODEL


class _WeightCast:
    SCRATCH = [pltpu.VMEM((D_MODEL, D_MODEL), F32), pltpu.VMEM((D_MODEL, D_MODEL), BF16),
               pltpu.SemaphoreType.DMA(()), pltpu.SemaphoreType.DMA(())]

    def __init__(self, layer, wup_hbm, wdn_hbm, oup_hbm, odn_hbm, scratch):
        self.layer, self.srcs, self.dsts = layer, (wup_hbm, wdn_hbm), (oup_hbm, odn_hbm)
        self.stage_in, self.stage_out, self.sem_in, self.sem_out = scratch
        self.step, self.n_steps = pl.program_id(0), pl.num_programs(0)

    def _piece(self, p):
        if p < FF_PIECES:
            cols = slice(p * D_MODEL, (p + 1) * D_MODEL)
            return self.srcs[0].at[self.layer, :, cols], self.dsts[0].at[:, cols]
        rows = slice((p - FF_PIECES) * D_MODEL, (p - FF_PIECES + 1) * D_MODEL)
        return self.srcs[1].at[self.layer, rows, :], self.dsts[1].at[rows, :]

    def _load(self, p):
        return pltpu.make_async_copy(self._piece(p)[0], self.stage_in, self.sem_in)

    def _store(self, p):
        return pltpu.make_async_copy(self.stage_out, self._piece(p)[1], self.sem_out)

    def fetch(self):
        for p in range(2 * FF_PIECES):
            @pl.when(self.step == p)
            def _():
                self._load(p).start()

    def convert(self):
        self._load(0).wait()

        @pl.when(self.step > 0)
        def _():
            self._store(0).wait()

        self.stage_out[...] = self.stage_in[...].astype(BF16)
        for p in range(2 * FF_PIECES):
            @pl.when(self.step == p)
            def _():
                self._store(p).start()

        @pl.when(self.step + 1 == self.n_steps)
        def _():
            self._store(0).wait()


CTX_PER_STEP = TOK // SEQ
CTX_WROWS = WIN_HEADS * SEQ
CTX_MROWS = MLA_HEADS * SEQ


def _ctx_mixer_kernel(n_alias, x_ref, mod_ref, n1_ref, win_ref, sink_ref, decay_ref, gn_ref, kvn_ref,
                      wkvb_ref, wout_ref, *refs):
    xo_ref, ko_ref, vo_ref, ckvo_ref, kro_ref, so_ref, h_scr, mix_scr = refs[n_alias:n_alias + 8]
    xs = _XStream(x_ref, xo_ref, refs[n_alias + 8:])
    xs.prologue(mod_ref, n1_ref, h_scr)
    per_seq = lambda a: a.reshape(CTX_PER_STEP, SEQ, a.shape[-1])
    if n_alias == 0:
        for ref in (ko_ref, vo_ref, ckvo_ref, kro_ref, so_ref):
            ref[:, 1:] = jnp.zeros((ref.shape[0], ref.shape[1] - 1) + ref.shape[2:], F32)
        ko_ref, vo_ref, ckvo_ref, kro_ref, so_ref = (
            ref.at[:, 0] for ref in (ko_ref, vo_ref, ckvo_ref, kro_ref, so_ref))
    chunks_per_seq = SEQ // RET_CHUNK

    def proj(c0, c1):
        return _dot_nt(h_scr[...], win_ref[c0:c1, :])

    ret = []
    seq_chunks = lambda e: [chunks_per_seq * e + c for c in range(chunks_per_seq)]

    def attention(qst_scr, k_scr, vaug_scr, qc_scr, kcat_scr, kvaug_scr,
                  sw0, sw1, pw0, pw1, ew0, ew1, sm0, sm1, pm0, pm1):
        lane_lo = _lane_iota((TOK, LANES)) < 64
        ones = jnp.ones((TOK, LANES), BF16)
        qkv = proj(C_QA, C_QB)
        for g in range(4):
            q = qkv[:, g * LANES:(g + 1) * LANES] * ATTN_SCALE
            lo = jnp.where(lane_lo, q, 0.0).astype(BF16).reshape(CTX_PER_STEP, SEQ, LANES)
            hi = jnp.where(lane_lo, 0.0, q).astype(BF16).reshape(CTX_PER_STEP, SEQ, LANES)
            qst_scr[:, g * SEQ:(g + 1) * SEQ, :] = lo
            qst_scr[:, (4 + g) * SEQ:(5 + g) * SEQ, :] = hi
        kva = qkv[:, C_KA:C_QB]
        ko_ref[...] = per_seq(kva[:, :LANES])
        vo_ref[...] = per_seq(kva[:, LANES:])
        k_scr[...] = kva[:, :LANES].astype(BF16)
        vaug_scr[:, :LANES] = kva[:, LANES:].astype(BF16)
        vaug_scr[:, LANES:] = ones
        qck = proj(C_QN, D_IN_P)
        _latent_queries(qck[:, :C_QR - C_QN], qck[:, C_QR - C_QN:C_CKV - C_QN], qc_scr)
        ckv, kr = qck[:, C_CKV - C_QN:C_KR - C_QN], qck[:, C_KR - C_QN:]
        ckv_n = ckv * lax.rsqrt(jnp.mean(ckv * ckv, axis=-1, keepdims=True) + EPS) * kvn_ref[...]
        ckvo_ref[...] = per_seq(ckv_n)
        kro_ref[...] = per_seq(kr[:, :MLA_ROPE])
        _latent_keys(_dot(ckv_n.astype(BF16), wkvb_ref[...]), kr, slice(0, TOK), kcat_scr, kvaug_scr)

        sw, pw, ew, sm, pm = (sw0, sw1), (pw0, pw1), (ew0, ew1), (sm0, sm1), (pm0, pm1)
        lane_lo_s = _lane_iota((SEQ, LANES)) < 64

        def scores(e, b):
            rows = _ds(e * SEQ, SEQ)
            sw[b][...] = _dot_nt(qst_scr[e], k_scr[rows, :])
            for h in range(MLA_HEADS):
                sm[b][h * SEQ:(h + 1) * SEQ, :] = _dot_nt(qc_scr[h, rows, :], kcat_scr[h // 2, rows, :])
            ret[0].scores(seq_chunks(e), b)

        def softmax(e, b):
            for h in range(WIN_HEADS):
                hr = slice(h * SEQ, (h + 1) * SEQ)
                sink = jnp.full((SEQ, 1), sink_ref[h], F32)
                (p,), extra = _softmax_tile([sw[b][hr, :]], floor=sink)
                pw[b][hr, :] = p
                ew[b][hr, :] = jnp.broadcast_to(extra, (SEQ, LANES))
            for h in range(MLA_HEADS):
                hr = slice(h * SEQ, (h + 1) * SEQ)
                (p,), _ = _softmax_tile([sm[b][hr, :]], scale=MLA_SCALE)
                pm[b][hr, :] = p
            ret[0].mask(seq_chunks(e), b)

        def values(e, b):
            ret[0].values(seq_chunks(e), b)
            rows = _ds(e * SEQ, SEQ)
            oa = _dot(pw[b][...], vaug_scr[rows, :])
            o = oa[:, :LANES] / (oa[:, LANES:] + ew[b][...])
            for g in range(4):
                merged = jnp.where(lane_lo_s, o[g * SEQ:(g + 1) * SEQ], o[(4 + g) * SEQ:(5 + g) * SEQ])
                mix_scr[rows, M_A + g * LANES:M_A + (g + 1) * LANES] = merged.astype(BF16)
            for jp in range(2):
                outs = []
                for h in (2 * jp, 2 * jp + 1):
                    oc = _dot(pm[b][h * SEQ:(h + 1) * SEQ, :], kvaug_scr[jp, rows, :])
                    outs.append(oc[:, :LANES] / oc[:, LANES:])
                merged = jnp.where(lane_lo_s, outs[0], outs[1])
                mix_scr[rows, M_C + jp * LANES:M_C + (jp + 1) * LANES] = merged.astype(BF16)

        _pipeline(CTX_PER_STEP, scores, softmax, values)

    attention_scratch = (
        pltpu.VMEM((CTX_PER_STEP, CTX_WROWS, LANES), BF16),
        pltpu.VMEM((TOK, LANES), BF16),
        pltpu.VMEM((TOK, 2 * LANES), BF16),
        pltpu.VMEM((MLA_HEADS, TOK, 2 * LANES), BF16),
        pltpu.VMEM((2, TOK, 2 * LANES), BF16),
        pltpu.VMEM((2, TOK, 2 * LANES), BF16),
        pltpu.VMEM((CTX_WROWS, SEQ), F32), pltpu.VMEM((CTX_WROWS, SEQ), F32),
        pltpu.VMEM((CTX_WROWS, SEQ), BF16), pltpu.VMEM((CTX_WROWS, SEQ), BF16),
        pltpu.VMEM((CTX_WROWS, LANES), F32), pltpu.VMEM((CTX_WROWS, LANES), F32),
        pltpu.VMEM((CTX_MROWS, SEQ), F32), pltpu.VMEM((CTX_MROWS, SEQ), F32),
        pltpu.VMEM((CTX_MROWS, SEQ), BF16), pltpu.VMEM((CTX_MROWS, SEQ), BF16),
    )

    def retention(*scratch):
        r = _Retention(scratch, gn_ref, mix_scr)
        r.st[...] = jnp.zeros(r.st.shape, F32)
        r.prepare(proj, decay_ref, [(chunks_per_seq * e, chunks_per_seq) for e in range(CTX_PER_STEP)])
        for e in range(CTX_PER_STEP):
            for dirn in range(2):
                for j in range(2):
                    st = r.st[e, dirn, j]
                    so_ref[e, dirn, 2 * j] = st[:64, :64]
                    so_ref[e, dirn, 2 * j + 1] = st[64:, 64:]
        ret.append(r)
        pl.run_scoped(attention, *attention_scratch)

    pl.run_scoped(retention, *_ret_scratch(CTX_PER_STEP, chunks_per_seq))
    xs.epilogue(mod_ref, wout_ref, mix_scr)


def _const_spec(shape, layer=None):
    if layer is None:
        return pl.BlockSpec(shape, lambda i: (0,) * len(shape), pipeline_mode=pl.Buffered(1))
    return pl.BlockSpec((None,) + shape, lambda i: (layer,) + (0,) * len(shape),
                        pipeline_mode=pl.Buffered(1))


def _smem_spec():
    return pl.BlockSpec(memory_space=pltpu.SMEM)


def _ctx_mixer(layer, x, mod, n1, w_in_p, sink, decay, gn, kvn, w_kv_b, w_out_p, prev_state):
    n_tok = x.shape[0]
    n_seq = n_tok // SEQ
    hbm_spec = pl.BlockSpec(memory_space=pl.ANY)
    if prev_state:
        state_spec = lambda *tail: pl.BlockSpec(
            (CTX_PER_STEP, None) + tail, lambda i: (i, layer) + (0,) * len(tail))
    else:
        assert layer == 0
        state_spec = lambda *tail: pl.BlockSpec(
            (CTX_PER_STEP, DEPTH) + tail, lambda i: (i, 0) + (0,) * len(tail))
    state_tails = [(SEQ, LANES), (SEQ, LANES), (SEQ, MLA_KV_RANK), (SEQ, MLA_ROPE),
                   (2, RET_HEADS, HEAD_DIM, HEAD_DIM)]
    n_in = 10
    return pl.pallas_call(
        functools.partial(_ctx_mixer_kernel, len(prev_state)),
        out_shape=[jax.ShapeDtypeStruct((n_tok, D_MODEL), F32)] + [
            jax.ShapeDtypeStruct((n_seq, DEPTH) + tail, F32) for tail in state_tails],
        grid=(n_tok // TOK,),
        in_specs=[
            hbm_spec,
            _const_spec((N_MOD, D_MODEL)),
            _const_spec((1, D_MODEL)),
            _const_spec((D_IN_P, D_MODEL), layer),
            _smem_spec(),
            _smem_spec(),
            _const_spec((1, RET_HEADS * HEAD_DIM)),
            _const_spec((1, MLA_KV_RANK)),
            _const_spec((MLA_KV_RANK, MLA_HEADS * LANES), layer),
            _const_spec((D_MODEL, D_MODEL), layer),
        ] + [hbm_spec] * len(prev_state),
        out_specs=[hbm_spec] + [state_spec(*tail) for tail in state_tails],
        input_output_aliases={n_in + k: 1 + k for k in range(len(prev_state))},
        scratch_shapes=[
            pltpu.VMEM((TOK, D_MODEL), BF16),
            pltpu.VMEM((TOK, D_MODEL), BF16),
        ] + _XStream.SCRATCH,
        compiler_params=pltpu.CompilerParams(
            dimension_semantics=("arbitrary",), vmem_limit_bytes=VMEM_LIMIT),
        name="ctx_mixer",
    )(x, mod, n1, w_in_p, sink, decay, gn, kvn, w_kv_b, w_out_p, *prev_state)


N_BLK = DEC_SEQ // LANES
KEYS_LOC = 3 * LANES
KEYS_WIN = KEYS_LOC + PAST_LEN
WIN_ROWS = WIN_HEADS * LANES
MLA_QB = 256
MLA_KEYS = DEC_SEQ + PAST_LEN
MLA_HALF = MLA_KEYS // 2


def _lat_mixer_kernel(layer, x_ref, mod_ref, n1_ref, win_ref, sink_ref, decay_ref, gn_ref, kvn_ref,
                      wkvb_ref, wout_ref, ck_ref, cv_ref, cckv_ref, ckr_ref, s0_ref,
                      rc_ref, rsa_ref, rsb_ref, mc_ref, msa_ref, msb_ref, wup_ref, wdn_ref,
                      xo_ref, wup16_ref, wdn16_ref,
                      h_scr, mix_scr, s0_scr, s1_scr, p0_scr, p1_scr, *dma_scratch):
    xs = _XStream(x_ref, xo_ref, dma_scratch[:len(_XStream.SCRATCH)])
    wc = _WeightCast(layer, wup_ref, wdn_ref, wup16_ref, wdn16_ref,
                     dma_scratch[len(_XStream.SCRATCH):])
    xs.prologue(mod_ref, n1_ref, h_scr)
    wc.fetch()
    sbuf, pbuf = (s0_scr, s1_scr), (p0_scr, p1_scr)

    def proj(c0, c1):
        return _dot_nt(h_scr[...], win_ref[c0:c1, :])

    ret = []

    def window(qst_scr, kpad_scr, vaug_scr, ckb_scr, cvaug_scr, bias_scr, e0_scr, e1_scr):
        ebuf = (e0_scr, e1_scr)
        lane_lo = _lane_iota((TOK, LANES)) < 64
        rc, rsa, rsb = rc_ref[...], rsa_ref[...], rsb_ref[...]
        qkv = proj(C_QA, C_QB)
        for g in range(4):
            q = _rope(qkv[:, g * LANES:(g + 1) * LANES], rc, rsa, rsb, 16) * ATTN_SCALE
            lo = jnp.where(lane_lo, q, 0.0).astype(BF16).reshape(N_BLK, LANES, LANES)
            hi = jnp.where(lane_lo, 0.0, q).astype(BF16).reshape(N_BLK, LANES, LANES)
            qst_scr[:, g * LANES:(g + 1) * LANES, :] = lo
            qst_scr[:, (4 + g) * LANES:(5 + g) * LANES, :] = hi
        kva = qkv[:, C_KA:C_QB]
        zpad = jnp.zeros((LANES, LANES), BF16)
        kpad_scr[0:LANES, :] = zpad
        kpad_scr[LANES + TOK:, :] = zpad
        vaug_scr[0:LANES, :LANES] = zpad
        vaug_scr[LANES + TOK:, :LANES] = zpad
        kpad_scr[LANES:LANES + TOK, :] = _rope(kva[:, :LANES], rc, rsa, rsb, 16).astype(BF16)
        vaug_scr[LANES:LANES + TOK, :LANES] = kva[:, LANES:].astype(BF16)
        vaug_scr[:, LANES:] = jnp.ones((TOK + 2 * LANES, LANES), BF16)
        ckb_scr[...] = ck_ref[...].astype(BF16)
        cvaug_scr[:, :LANES] = cv_ref[...].astype(BF16)
        cvaug_scr[:, LANES:] = jnp.ones((PAST_LEN, LANES), BF16)
        qi, kj = _row_iota((LANES, LANES)), _lane_iota((LANES, LANES))
        bias_scr[0] = jnp.full((LANES, LANES), -jnp.inf, F32)
        bias_scr[1] = jnp.where(kj >= qi, 0.0, -jnp.inf)
        bias_scr[2] = jnp.where(kj <= qi, 0.0, -jnp.inf)
        lane_lo_b = _lane_iota((LANES, LANES)) < 64

        def scores(n, b):
            q = qst_scr[n]
            sbuf[b][:, :KEYS_LOC] = _dot_nt(q, kpad_scr[_ds(n * LANES, KEYS_LOC), :])
            sbuf[b][:, KEYS_LOC:] = _dot_nt(q, ckb_scr[...])
            ret[0].scores([n], b)

        def softmax(n, b):
            if isinstance(n, int):
                i_prev, i_next = (1 if n > 0 else 0), (2 if n < N_BLK - 1 else 0)
            else:
                i_prev, i_next = jnp.where(n > 0, 1, 0), jnp.where(n < N_BLK - 1, 2, 0)
            b_prev, b_next = bias_scr[i_prev], bias_scr[i_next]
            for h in range(WIN_HEADS):
                hr = slice(h * LANES, (h + 1) * LANES)
                parts = [sbuf[b][hr, 0:LANES] + b_prev, sbuf[b][hr, LANES:2 * LANES],
                         sbuf[b][hr, 2 * LANES:KEYS_LOC] + b_next, sbuf[b][hr, KEYS_LOC:]]
                ps, extra = _softmax_tile(parts, floor=jnp.full((LANES, 1), sink_ref[h], F32))
                pbuf[b][hr, 0:LANES] = ps[0]
                pbuf[b][hr, LANES:2 * LANES] = ps[1]
                pbuf[b][hr, 2 * LANES:KEYS_LOC] = ps[2]
                pbuf[b][hr, KEYS_LOC:] = ps[3]
                ebuf[b][hr, :] = jnp.broadcast_to(extra, (LANES, LANES))
            ret[0].mask([n], b)

        def values(n, b):
            ret[0].values([n], b)
            oa = (_dot(pbuf[b][:, :KEYS_LOC], vaug_scr[_ds(n * LANES, KEYS_LOC), :])
                  + _dot(pbuf[b][:, KEYS_LOC:], cvaug_scr[...]))
            o = oa[:, :LANES] / (oa[:, LANES:] + ebuf[b][...])
            for g in range(4):
                merged = jnp.where(lane_lo_b, o[g * LANES:(g + 1) * LANES],
                                   o[(4 + g) * LANES:(5 + g) * LANES])
                mix_scr[_ds(n * LANES, LANES), M_A + g * LANES:M_A + (g + 1) * LANES] = (
                    merged.astype(BF16))

        _pipeline(N_BLK, scores, softmax, values)

    window_scratch = (
        pltpu.VMEM((N_BLK, WIN_ROWS, LANES), BF16),
        pltpu.VMEM((TOK + 2 * LANES, LANES), BF16),
        pltpu.VMEM((TOK + 2 * LANES, 2 * LANES), BF16),
        pltpu.VMEM((PAST_LEN, LANES), BF16),
        pltpu.VMEM((PAST_LEN, 2 * LANES), BF16),
        pltpu.VMEM((3, LANES, LANES), F32),
        pltpu.VMEM((WIN_ROWS, LANES), F32), pltpu.VMEM((WIN_ROWS, LANES), F32),
    )

    def retention(*scratch):
        r = _Retention(scratch, gn_ref, mix_scr)
        r.st[0] = s0_ref[...]
        r.prepare(proj, decay_ref, [(0, N_CHUNK)])
        ret.append(r)
        pl.run_scoped(window, *window_scratch)

    assert N_CHUNK == N_BLK
    pl.run_scoped(retention, *_ret_scratch(1, 1))

    def latent(qc_scr, kcat_scr, kvaug_scr, mixc_scr):
        mc, msa, msb = mc_ref[...], msa_ref[...], msb_ref[...]
        qck = proj(C_QN, D_IN_P)
        _latent_queries(qck[:, :C_QR - C_QN], _rope(qck[:, C_QR - C_QN:C_CKV - C_QN], mc, msa, msb, 8),
                        qc_scr)
        ckv = qck[:, C_CKV - C_QN:C_KR - C_QN]
        kr = _rope(qck[:, C_KR - C_QN:], mc, msa, msb, 8)
        ckv_n = ckv * lax.rsqrt(jnp.mean(ckv * ckv, axis=-1, keepdims=True) + EPS) * kvn_ref[...]
        _latent_keys(_dot(ckv_n.astype(BF16), wkvb_ref[...]), kr, slice(0, TOK), kcat_scr, kvaug_scr)
        _latent_keys(_dot(cckv_ref[...].astype(BF16), wkvb_ref[...]), ckr_ref[...],
                     slice(TOK, MLA_KEYS), kcat_scr, kvaug_scr)
        lane_lo_m = _lane_iota((MLA_QB, LANES)) < 64
        n_qb = DEC_SEQ // MLA_QB

        def split(t):
            if isinstance(t, int):
                return t // n_qb, t % n_qb
            return lax.shift_right_logical(t, 2), lax.bitwise_and(t, n_qb - 1)

        def scores(t, b):
            jp, qb = split(t)
            for i in range(2):
                h = 2 * jp + i
                q = qc_scr[h, _ds(qb * MLA_QB, MLA_QB), :]
                for part in range(2):
                    r0 = (2 * i + part) * MLA_QB
                    sbuf[b][r0:r0 + MLA_QB, :] = _dot_nt(
                        q, kcat_scr[jp, part * MLA_HALF:(part + 1) * MLA_HALF, :])

        def softmax(t, b):
            for i in range(2):
                for rt in range(MLA_QB // LANES):
                    ra = 2 * i * MLA_QB + rt * LANES
                    rb = ra + MLA_QB
                    ps, _ = _softmax_tile([sbuf[b][ra:ra + LANES, :], sbuf[b][rb:rb + LANES, :]],
                                          scale=MLA_SCALE)
                    pbuf[b][ra:ra + LANES, :] = ps[0]
                    pbuf[b][rb:rb + LANES, :] = ps[1]

        def values(t, b):
            jp, qb = split(t)
            outs = []
            for i in range(2):
                r0 = 2 * i * MLA_QB
                oc = (_dot(pbuf[b][r0:r0 + MLA_QB, :], kvaug_scr[jp, 0:MLA_HALF, :])
                      + _dot(pbuf[b][r0 + MLA_QB:r0 + 2 * MLA_QB, :], kvaug_scr[jp, MLA_HALF:, :]))
                outs.append(oc[:, :LANES] / oc[:, LANES:])
            merged = jnp.where(lane_lo_m, outs[0], outs[1])
            mixc_scr[jp, _ds(qb * MLA_QB, MLA_QB), :] = merged.astype(BF16)

        _pipeline(2 * n_qb, scores, softmax, values)
        for jp in range(2):
            mix_scr[:, M_C + jp * LANES:M_C + (jp + 1) * LANES] = mixc_scr[jp]

    pl.run_scoped(
        latent,
        pltpu.VMEM((MLA_HEADS, TOK, 2 * LANES), BF16),
        pltpu.VMEM((2, MLA_KEYS, 2 * LANES), BF16),
        pltpu.VMEM((2, MLA_KEYS, 2 * LANES), BF16),
        pltpu.VMEM((2, TOK, LANES), BF16),
    )
    wc.convert()
    xs.epilogue(mod_ref, wout_ref, mix_scr)


def _lat_mixer(layer, x, mod, n1, w_in_p, sink, decay, gn, kvn, w_kv_b, w_out_p,
               ck, cv, cckv, ckr, s0, rope_a, rope_m, w_up, w_down):
    n_tok = x.shape[0]
    n_seq = n_tok // DEC_SEQ
    assert n_seq == 2 * FF_PIECES
    hbm_spec = pl.BlockSpec(memory_space=pl.ANY)
    seq_spec = lambda shape: pl.BlockSpec(
        (None, None) + shape, lambda i: (i, layer) + (0,) * len(shape))
    assert WIN_ROWS == 4 * MLA_QB and KEYS_WIN == MLA_HALF
    return pl.pallas_call(
        functools.partial(_lat_mixer_kernel, layer),
        out_shape=(jax.ShapeDtypeStruct((n_tok, D_MODEL), F32),
                   jax.ShapeDtypeStruct((D_MODEL, D_FF), BF16),
                   jax.ShapeDtypeStruct((D_FF, D_MODEL), BF16)),
        grid=(n_seq,),
        in_specs=[
            hbm_spec,
            pl.BlockSpec((None, N_MOD, D_MODEL), lambda i: (i, 0, 0)),
            _const_spec((1, D_MODEL)),
            _const_spec((D_IN_P, D_MODEL), layer),
            _smem_spec(),
            _smem_spec(),
            _const_spec((1, RET_HEADS * HEAD_DIM)),
            _const_spec((1, MLA_KV_RANK)),
            _const_spec((MLA_KV_RANK, MLA_HEADS * LANES), layer),
            _const_spec((D_MODEL, D_MODEL), layer),
            seq_spec((PAST_LEN, LANES)),
            seq_spec((PAST_LEN, LANES)),
            seq_spec((PAST_LEN, MLA_KV_RANK)),
            seq_spec((PAST_LEN, LANES)),
            seq_spec((2, 2, LANES, LANES)),
        ] + [_const_spec((DEC_SEQ, LANES))] * 6 + [hbm_spec, hbm_spec],
        out_specs=(hbm_spec, hbm_spec, hbm_spec),
        scratch_shapes=[
            pltpu.VMEM((TOK, D_MODEL), BF16),
            pltpu.VMEM((TOK, D_MODEL), BF16),
            pltpu.VMEM((WIN_ROWS, KEYS_WIN), F32), pltpu.VMEM((WIN_ROWS, KEYS_WIN), F32),
            pltpu.VMEM((WIN_ROWS, KEYS_WIN), BF16), pltpu.VMEM((WIN_ROWS, KEYS_WIN), BF16),
        ] + _XStream.SCRATCH + _WeightCast.SCRATCH,
        compiler_params=pltpu.CompilerParams(
            dimension_semantics=("arbitrary",), vmem_limit_bytes=VMEM_LIMIT),
        name="lat_mixer",
    )(x, mod, n1, w_in_p, sink, decay, gn, kvn, w_kv_b, w_out_p, ck, cv, cckv, ckr, s0,
      *rope_a, *rope_m, w_up, w_down)


def _mlp_kernel(final, x_ref, mod_ref, n2_ref, wup_ref, wdn_ref, fn_ref, o_ref):
    x = x_ref[...]
    h2 = _norm_mod(x, n2_ref[...], mod_ref[4:5, :], mod_ref[3:4, :]).astype(BF16)
    acc = None
    for c in range(D_FF // FF_CHUNK):
        cols = slice(c * FF_CHUNK, (c + 1) * FF_CHUNK)
        u = jnp.maximum(_dot(h2, wup_ref[:, cols]), 0.0)
        part = _dot((u * u).astype(BF16), wdn_ref[cols, :])
        acc = part if acc is None else acc + part
    y = x + mod_ref[5:6, :] * acc
    if final:
        y = y * lax.rsqrt(jnp.mean(y * y, axis=-1, keepdims=True) + EPS) * fn_ref[...]
    o_ref[...] = y


def _mlp(x, mod, n2, w_up, w_down, final_norm, final):
    n_tok = x.shape[0]
    per_mod = n_tok // mod.shape[0] // MLP_ROWS
    return pl.pallas_call(
        functools.partial(_mlp_kernel, final),
        out_shape=jax.ShapeDtypeStruct((n_tok, D_MODEL), F32),
        grid=(n_tok // MLP_ROWS,),
        in_specs=[
            pl.BlockSpec((MLP_ROWS, D_MODEL), lambda i: (i, 0)),
            pl.BlockSpec((None, N_MOD, D_MODEL), lambda i: (i // per_mod, 0, 0)),
            _const_spec((1, D_MODEL)),
            _const_spec((D_MODEL, D_FF)),
            _const_spec((D_FF, D_MODEL)),
            _const_spec((1, D_MODEL)),
        ],
        out_specs=pl.BlockSpec((MLP_ROWS, D_MODEL), lambda i: (i, 0)),
        compiler_params=pltpu.CompilerParams(
            dimension_semantics=("arbitrary",), vmem_limit_bytes=VMEM_LIMIT),
        name="mlp",
    )(x, mod, n2, w_up, w_down, final_norm)


def kernel(x_prompt, x_sample, cache_win_k, cache_win_v, cache_mla_ckv, cache_mla_krope, state_ret,
           c, c_ctx, w_mod, b_mod, norm1, norm2, w_in, win_sink, ret_decay, ret_gn, mla_kv_norm,
           w_kv_b, w_out, w_up, w_down, final_norm):
    n_ctx, n_lat = x_prompt.shape[0], x_sample.shape[0]

    w_in_p, w_out_p = _proj_layout(jnp.swapaxes(w_in, 1, 2), w_out)
    w_kv_b16 = _take_runs(w_kv_b, _kv_expand_columns(), 2).astype(BF16)

    c_rows = jnp.zeros((16, D_MODEL), F32).at[0].set(c_ctx).at[1:1 + n_lat].set(c)
    mod = _modulation(c_rows, w_mod, b_mod).reshape(DEPTH, 16, N_MOD, D_MODEL)

    rope_a = _rope_tables(DEC_SEQ, HEAD_DIM, 0, HEAD_DIM)
    rope_m = _rope_tables(DEC_SEQ, MLA_ROPE, 0, MLA_ROPE)

    ck = cache_win_k.reshape(n_lat, DEPTH, PAST_LEN, LANES)
    cv = cache_win_v.reshape(n_lat, DEPTH, PAST_LEN, LANES)
    ckr = jnp.tile(cache_mla_krope, (1, 1, 1, MLA_HEADS))
    sr = state_ret.reshape(n_lat, DEPTH, 2, 2, 2, HEAD_DIM, HEAD_DIM)
    zero = jnp.zeros_like(sr[:, :, :, :, 0])
    s0 = jnp.concatenate([jnp.concatenate([sr[:, :, :, :, 0], zero], axis=-1),
                          jnp.concatenate([zero, sr[:, :, :, :, 1]], axis=-1)], axis=-2)

    xp = x_prompt.reshape(n_ctx * SEQ, D_MODEL)
    xs = x_sample.reshape(n_lat * DEC_SEQ, D_MODEL)
    state = ()
    for l in range(DEPTH):
        last = l == DEPTH - 1
        shared = (norm1[l][None], w_in_p, win_sink[l], ret_decay[l], ret_gn[l][None],
                  mla_kv_norm[l][None], w_kv_b16, w_out_p)
        mod_ctx, mod_lat = mod[l, 0:1], mod[l, 1:1 + n_lat]
        xs, w_up16, w_down16 = _lat_mixer(l, xs, mod_lat, *shared, ck, cv, cache_mla_ckv, ckr, s0,
                                          rope_a, rope_m, w_up, w_down)
        xp, *state = _ctx_mixer(l, xp, mod_ctx[0], *shared, state)
        xp = _mlp(xp, mod_ctx, norm2[l][None], w_up16, w_down16, final_norm[None], last)
        xs = _mlp(xs, mod_lat, norm2[l][None], w_up16, w_down16, final_norm[None], last)
    new_k, new_v, new_ckv, new_kr, new_s = state
    return (xp.reshape(n_ctx, SEQ, D_MODEL), xs.reshape(n_lat, DEC_SEQ, D_MODEL),
            new_k.reshape(n_ctx, DEPTH, SEQ, 2, HEAD_DIM), new_v.reshape(n_ctx, DEPTH, SEQ, 2, HEAD_DIM),
            new_ckv, new_kr, new_s)
```

```python
import functools

import numpy as np
import jax
import jax.numpy as jnp
from jax import lax
from jax.experimental import pallas as pl
from jax.experimental.pallas import tpu as pltpu

F32 = jnp.float32
BF16 = jnp.bfloat16

D_MODEL = 1024
DEPTH = 2
SEQ = 256
DEC_SEQ = 1024
PAST_LEN = 256
GRID_W = 64
HEAD_DIM = 64
ROPE_BASE = 10000.0
EPS = 1e-6
WIN_HEADS = 8
WINDOW = 128
ATTN_SCALE = HEAD_DIM ** -0.5
RET_HEADS = 4
RET_CHUNK = 128
RET_K_SCALE = HEAD_DIM ** -0.5
MLA_HEADS = 4
MLA_NOPE = 64
MLA_ROPE = 32
MLA_KV_RANK = 128
MLA_QK = MLA_NOPE + MLA_ROPE
MLA_SCALE = MLA_QK ** -0.5
D_IN = 2336
D_FF = 4 * D_MODEL
N_MOD = 6

LANES = 128
TOK = 1024
MLP_ROWS = 1024
FF_CHUNK = 1024
VMEM_LIMIT = 60 * 1024 * 1024

C_QA, C_KA, C_VA, C_QB, C_KB, C_VB, C_GB, C_QC, C_CKV, C_KR, D_IN_P = (
    0, 512, 640, 768, 1024, 1280, 1536, 1792, 2304, 2432, 2560)
M_A, M_B, M_C = 0, 512, 768

NT_DIMS = (((1,), (1,)), ((), ()))


def _in_proj_columns():
    idx = []
    for g in range(4):
        idx += list(range(g * 64, (g + 1) * 64)) + list(range((4 + g) * 64, (5 + g) * 64))
    idx += list(range(512, 1792))
    for h in range(MLA_HEADS):
        idx += list(range(1792 + h * MLA_QK, 1792 + (h + 1) * MLA_QK)) + [-1] * 32
    idx += list(range(2176, 2304))
    idx += [-1] * 64 + list(range(2304, 2336)) + [-1] * 32
    return np.asarray(idx, np.int32)


def _mix_rows():
    idx = []
    for g in range(4):
        idx += list(range(g * 64, (g + 1) * 64)) + list(range((4 + g) * 64, (5 + g) * 64))
    idx += list(range(512, 1024))
    return np.asarray(idx, np.int32)


def _take_runs(w, idx, axis):
    pieces, i = [], 0
    while i < len(idx):
        j = i + 1
        if idx[i] < 0:
            while j < len(idx) and idx[j] < 0:
                j += 1
            shape = list(w.shape)
            shape[axis] = j - i
            pieces.append(jnp.zeros(shape, w.dtype))
        else:
            while j < len(idx) and idx[j] == idx[j - 1] + 1:
                j += 1
            pieces.append(lax.slice_in_dim(w, int(idx[i]), int(idx[j - 1]) + 1, axis=axis))
        i = j
    return jnp.concatenate(pieces, axis=axis)


def _rope_tables(n_tokens, dim, lane0, period):
    quarter = dim // 4
    t = np.arange(n_tokens)
    row = (t // GRID_W).astype(np.float64)
    col = (t % GRID_W).astype(np.float64)
    inv_freq = ROPE_BASE ** (-np.arange(quarter, dtype=np.float64) / quarter)
    ar, ac = row[:, None] * inv_freq, col[:, None] * inv_freq
    cos = np.concatenate([np.cos(ar), np.cos(ar), np.cos(ac), np.cos(ac)], axis=-1)
    sin = np.concatenate([np.sin(ar), np.sin(ar), np.sin(ac), np.sin(ac)], axis=-1)
    first = np.tile(np.concatenate([np.ones(quarter), np.zeros(quarter)]), 2)
    c = np.ones((n_tokens, LANES))
    sa = np.zeros((n_tokens, LANES))
    sb = np.zeros((n_tokens, LANES))
    for start in range(lane0, LANES, period):
        c[:, start:start + dim] = cos
        sa[:, start:start + dim] = -sin * first
        sb[:, start:start + dim] = sin * (1.0 - first)
    return tuple(jnp.asarray(a, F32) for a in (c, sa, sb))


def _lane_iota(shape):
    return lax.broadcasted_iota(jnp.int32, shape, len(shape) - 1)


def _row_iota(shape):
    return lax.broadcasted_iota(jnp.int32, shape, len(shape) - 2)


def _ds(start, size):
    if isinstance(start, int):
        return pl.ds(start, size)
    return pl.ds(pl.multiple_of(start, LANES), size)


def _norm_mod(x, gain, scale, shift):
    ms = jnp.mean(x * x, axis=-1, keepdims=True)
    return (x * lax.rsqrt(ms + EPS) * gain) * (1.0 + scale) + shift


def _log_sigmoid(x):
    return -(jnp.maximum(-x, 0.0) + jnp.log1p(jnp.exp(-jnp.abs(x))))


def _rope(x, c, sa, sb, quarter):
    return (x * c + pltpu.roll(x, LANES - quarter, 1) * sa + pltpu.roll(x, quarter, 1) * sb)


def _dot(a, b):
    return jnp.dot(a, b, preferred_element_type=F32)


def _dot_nt(a, b):
    return lax.dot_general(a, b, NT_DIMS, preferred_element_type=F32)


def _mod_kernel(c_ref, w_ref, b_ref, o_ref):
    cv = c_ref[...]
    s = cv * jax.nn.sigmoid(cv)
    o_ref[0] = _dot(s.astype(BF16), w_ref[0].astype(BF16)) + b_ref[0]


def _modulation(c_rows, w_mod, b_mod):
    tn = 1536
    nj = (N_MOD * D_MODEL) // tn
    return pl.pallas_call(
        _mod_kernel,
        out_shape=jax.ShapeDtypeStruct((DEPTH, 16, N_MOD * D_MODEL), F32),
        grid=(DEPTH, nj),
        in_specs=[
            pl.BlockSpec((16, D_MODEL), lambda l, j: (0, 0)),
            pl.BlockSpec((1, D_MODEL, tn), lambda l, j: (l, 0, j)),
            pl.BlockSpec((1, 1, tn), lambda l, j: (l, 0, j)),
        ],
        out_specs=pl.BlockSpec((1, 16, tn), lambda l, j: (l, 0, j)),
        compiler_params=pltpu.CompilerParams(
            dimension_semantics=("arbitrary", "arbitrary"), vmem_limit_bytes=VMEM_LIMIT),
        name="modulation",
    )(c_rows, w_mod, b_mod.reshape(DEPTH, 1, N_MOD * D_MODEL))


W_IN_LANES = 256


def _proj_layout_kernel(win_ref, wout_ref, oin_ref, oout_ref):
    oin_ref[0] = _take_runs(win_ref[0], _in_proj_columns(), 0).astype(BF16)
    oout_ref[0] = _take_runs(wout_ref[0], _mix_rows(), 0).astype(BF16)


def _proj_layout(w_in_t, w_out):
    col_spec = lambda rows: pl.BlockSpec((1, rows, W_IN_LANES), lambda l, c: (l, 0, c))
    return pl.pallas_call(
        _proj_layout_kernel,
        out_shape=(jax.ShapeDtypeStruct((DEPTH, D_IN_P, D_MODEL), BF16),
                   jax.ShapeDtypeStruct((DEPTH, D_MODEL, D_MODEL), BF16)),
        grid=(DEPTH, D_MODEL // W_IN_LANES),
        in_specs=[col_spec(D_IN), col_spec(D_MODEL)],
        out_specs=(col_spec(D_IN_P), col_spec(D_MODEL)),
        compiler_params=pltpu.CompilerParams(
            dimension_semantics=("arbitrary", "arbitrary"), vmem_limit_bytes=VMEM_LIMIT),
        name="proj_layout",
    )(w_in_t, w_out)


def _pipeline(n_steps, scores, softmax, values):
    scores(0, 0)
    scores(1, 1)
    softmax(0, 0)

    def body(i, carry):
        t = 2 * i + 1
        scores(t + 1, 0)
        softmax(t, 1)
        values(t - 1, 0)
        scores(t + 2, 1)
        softmax(t + 1, 0)
        values(t, 1)
        return carry

    lax.fori_loop(0, n_steps // 2 - 1, body, 0)
    softmax(n_steps - 1, 1)
    values(n_steps - 2, 0)
    values(n_steps - 1, 1)


def _softmax_tile(parts, floor=None, scale=None):
    m = None
    for s in parts:
        pm = jnp.max(s, axis=-1, keepdims=True)
        m = pm if m is None else jnp.maximum(m, pm)
    if floor is not None:
        m = jnp.maximum(m, floor)
    if scale is None:
        ps = [jnp.exp(s - m).astype(BF16) for s in parts]
    else:
        ps = [jnp.exp((s - m) * scale).astype(BF16) for s in parts]
    extra = None if floor is None else jnp.exp(floor - m)
    return ps, extra


def _retention_tables(decay_ref, rt_scr, m2_scr):
    shape = (RET_CHUNK, LANES)
    lane_lo = _lane_iota(shape) < 64
    row = _row_iota(shape)
    row_lo = row < 64
    i = row.astype(F32)
    rel = i - _lane_iota(shape).astype(F32)
    for j in range(2):
        df0, df1 = decay_ref[0, 2 * j], decay_ref[0, 2 * j + 1]
        db0, db1 = decay_ref[1, 2 * j], decay_ref[1, 2 * j + 1]
        lgf = _log_sigmoid(jnp.where(lane_lo, df0, df1))
        lgb = _log_sigmoid(jnp.where(lane_lo, db0, db1))
        rt_scr[j, 0] = jnp.exp((i + 1.0) * lgf)
        rt_scr[j, 1] = jnp.exp((RET_CHUNK - i) * lgb)
        rt_scr[j, 2] = jnp.exp((RET_CHUNK - 1.0 - i) * lgf)
        rt_scr[j, 3] = jnp.exp(i * lgb)
        rt_scr[j, 4] = jnp.exp(RET_CHUNK * _log_sigmoid(jnp.where(row_lo, df0, df1)))
        rt_scr[j, 5] = jnp.exp(RET_CHUNK * _log_sigmoid(jnp.where(row_lo, db0, db1)))
        for hh, (df, db) in enumerate(((df0, db0), (df1, db1))):
            lf = _log_sigmoid(jnp.full(shape, df, F32))
            lb = _log_sigmoid(jnp.full(shape, db, F32))
            low, upp = rel >= 0.0, rel <= 0.0
            m = (jnp.where(low, jnp.exp(jnp.where(low, rel, 0.0) * lf), 0.0)
                 + jnp.where(upp, jnp.exp(jnp.where(upp, -rel, 0.0) * lb), 0.0))
            m2_scr[j, :, hh * RET_CHUNK:(hh + 1) * RET_CHUNK] = m


N_CHUNK = TOK // RET_CHUNK


def _ret_scratch(n_seq, chunks_per_stage):
    per_chunk = lambda dtype: pltpu.VMEM((N_CHUNK, 2, 2 * RET_CHUNK, LANES), dtype)
    stage = lambda dtype: pltpu.VMEM((2 * chunks_per_stage, RET_CHUNK, 2 * LANES), dtype)
    return [
        pltpu.VMEM((TOK, 2 * LANES), BF16),
        pltpu.VMEM((TOK, 4 * LANES), BF16),
        per_chunk(BF16),
        pltpu.VMEM((TOK, 2 * LANES), BF16),
        per_chunk(BF16),
        pltpu.VMEM((TOK, 2 * LANES), F32),
        pltpu.VMEM((2, 6, RET_CHUNK, LANES), F32),
        pltpu.VMEM((2, RET_CHUNK, 2 * LANES), F32),
        per_chunk(BF16),
        pltpu.VMEM((n_seq, 2, 2, LANES, LANES), F32),
        stage(F32), stage(F32), stage(BF16), stage(BF16),
    ]


class _Retention:
    def __init__(self, scratch, gn_ref, mix_scr):
        (self.q, self.qd, self.kbd, self.v, self.vbd, self.g, self.rt, self.m2,
         self.sall, self.st, rs0, rs1, rp0, rp1) = scratch
        self.rs, self.rp = (rs0, rs1), (rp0, rp1)
        self.gn_ref, self.mix = gn_ref, mix_scr

    def prepare(self, proj, decay_ref, seq_chunks):
        _retention_tables(decay_ref, self.rt, self.m2)
        lane_lo = _lane_iota((TOK, LANES)) < 64
        per_chunk = lambda a: a.reshape(N_CHUNK, RET_CHUNK, LANES)
        q = proj(C_QB, C_KB)
        k = proj(C_KB, C_VB) * RET_K_SCALE
        v = proj(C_VB, C_GB)
        self.g[...] = proj(C_GB, C_QC)
        self.q[...] = q.astype(BF16)
        self.v[...] = v.astype(BF16)
        for j in range(2):
            cols = slice(j * LANES, (j + 1) * LANES)
            q3 = per_chunk(q[:, cols])
            for d in range(2):
                self.qd[:, (2 * j + d) * LANES:(2 * j + d + 1) * LANES] = (
                    (q3 * self.rt[j, d]).reshape(TOK, LANES).astype(BF16))
            for src, dst in ((k[:, cols], self.kbd), (v[:, cols], self.vbd)):
                dst[:, j, :RET_CHUNK, :] = per_chunk(jnp.where(lane_lo, src, 0.0).astype(BF16))
                dst[:, j, RET_CHUNK:, :] = per_chunk(jnp.where(lane_lo, 0.0, src).astype(BF16))

        pl.run_scoped(
            functools.partial(self._states, k, seq_chunks),
            pltpu.VMEM((TOK, 2 * LANES), F32),
            pltpu.VMEM((N_CHUNK, 2, 2 * RET_CHUNK, LANES), F32))

    def _states(self, k, seq_chunks, k_scr, upd_scr):
        k_scr[...] = k
        blockdiag2 = _lane_iota((2 * RET_CHUNK, LANES)) < 64
        blockdiag2 = blockdiag2 == ((_row_iota((2 * RET_CHUNK, LANES)) & (RET_CHUNK - 1)) < 64)

        def upd_body(c, carry):
            rows = _ds(c * RET_CHUNK, RET_CHUNK)
            for j in range(2):
                cols = slice(j * LANES, (j + 1) * LANES)
                k2 = k_scr[rows, cols]
                kd = jnp.concatenate([k2 * self.rt[j, 2], k2 * self.rt[j, 3]], axis=1)
                upd = _dot(kd.T.astype(BF16), self.v[rows, cols])
                upd_scr[c, j] = jnp.where(blockdiag2, upd, 0.0)
            return carry

        lax.fori_loop(0, N_CHUNK, upd_body, 0, unroll=4)

        n_per_seq = seq_chunks[0][1]
        assert all(n == n_per_seq for _, n in seq_chunks)

        def scan_body(t, carry):
            for si, (first, n) in enumerate(seq_chunks):
                cf, cb = first + t, first + n - 1 - t
                for j in range(2):
                    sf, sb = self.st[si, 0, j], self.st[si, 1, j]
                    self.sall[cf, j, :RET_CHUNK, :] = sf.astype(BF16)
                    self.sall[cb, j, RET_CHUNK:, :] = sb.astype(BF16)
                    self.st[si, 0, j] = self.rt[j, 4] * sf + upd_scr[cf, j, :RET_CHUNK, :]
                    self.st[si, 1, j] = self.rt[j, 5] * sb + upd_scr[cb, j, RET_CHUNK:, :]
            return carry

        lax.fori_loop(0, n_per_seq, scan_body, 0)

    def scores(self, chunks, b):
        for i, c in enumerate(chunks):
            rows = _ds(c * RET_CHUNK, RET_CHUNK)
            for j in range(2):
                self.rs[b][2 * i + j] = _dot_nt(self.q[rows, j * LANES:(j + 1) * LANES], self.kbd[c, j])

    def mask(self, chunks, b):
        for i in range(len(chunks)):
            for j in range(2):
                self.rp[b][2 * i + j] = (self.rs[b][2 * i + j] * self.m2[j]).astype(BF16)

    def values(self, chunks, b):
        lane_lo = _lane_iota((RET_CHUNK, LANES)) < 64
        for i, c in enumerate(chunks):
            rows = _ds(c * RET_CHUNK, RET_CHUNK)
            for j in range(2):
                cols = slice(j * LANES, (j + 1) * LANES)
                o = (_dot(self.rp[b][2 * i + j], self.vbd[c, j])
                     + _dot(self.qd[rows, 2 * j * LANES:2 * (j + 1) * LANES], self.sall[c, j]))
                s_lo = jnp.sum(jnp.where(lane_lo, o, 0.0), axis=-1, keepdims=True)
                s_hi = jnp.sum(jnp.where(lane_lo, 0.0, o), axis=-1, keepdims=True)
                d = o - jnp.where(lane_lo, s_lo, s_hi) * (1.0 / HEAD_DIM)
                dd = d * d
                v_lo = jnp.sum(jnp.where(lane_lo, dd, 0.0), axis=-1, keepdims=True)
                v_hi = jnp.sum(jnp.where(lane_lo, 0.0, dd), axis=-1, keepdims=True)
                var = jnp.where(lane_lo, v_lo, v_hi) * (1.0 / HEAD_DIM)
                g2 = self.g[rows, cols]
                y = d * lax.rsqrt(var + EPS) * self.gn_ref[:, cols] * (g2 * jax.nn.sigmoid(g2))
                self.mix[rows, M_B + j * LANES:M_B + (j + 1) * LANES] = y.astype(BF16)


X_ROWS = 512
N_XCHUNK = TOK // X_ROWS


class _XStream:
    SCRATCH = [pltpu.VMEM((N_XCHUNK, X_ROWS, D_MODEL), F32), pltpu.VMEM((2, X_ROWS, D_MODEL), F32),
               pltpu.SemaphoreType.DMA((N_XCHUNK,)), pltpu.SemaphoreType.DMA((2,))]

    def __init__(self, x_hbm, xo_hbm, scratch):
        self.x_hbm, self.xo_hbm = x_hbm, xo_hbm
        self.xin, self.xout, self.sem_in, self.sem_out = scratch
        self.step, self.n_steps = pl.program_id(0), pl.num_programs(0)

    @staticmethod
    def _rows(step, r):
        return pl.ds(pl.multiple_of(step * TOK + r * X_ROWS, X_ROWS), X_ROWS)

    def load(self, step, r):
        return pltpu.make_async_copy(self.x_hbm.at[self._rows(step, r), :], self.xin.at[r],
                                     self.sem_in.at[r])

    def store(self, r):
        return pltpu.make_async_copy(self.xout.at[r % 2], self.xo_hbm.at[self._rows(self.step, r), :],
                                     self.sem_out.at[r % 2])

    def prologue(self, mod_ref, n1_ref, h_scr):
        @pl.when(self.step == 0)
        def _():
            for r in range(N_XCHUNK):
                self.load(self.step, r).start()

        for r in range(N_XCHUNK):
            self.load(self.step, r).wait()
            h = _norm_mod(self.xin[r], n1_ref[...], mod_ref[1:2, :], mod_ref[0:1, :])
            h_scr[r * X_ROWS:(r + 1) * X_ROWS, :] = h.astype(BF16)

    def epilogue(self, mod_ref, wout_ref, mix_scr):
        for r in range(N_XCHUNK):
            if r >= 2:
                self.store(r - 2).wait()
            else:
                @pl.when(self.step > 0)
                def _():
                    self.store(r).wait()

            y = _dot(mix_scr[r * X_ROWS:(r + 1) * X_ROWS, :], wout_ref[...])
            self.xout[r % 2] = self.xin[r] + mod_ref[2:3, :] * y
            self.store(r).start()

            @pl.when(self.step + 1 < self.n_steps)
            def _():
                self.load(self.step + 1, r).start()

        @pl.when(self.step + 1 == self.n_steps)
        def _():
            for r in range(N_XCHUNK - 2, N_XCHUNK):
                self.store(r).wait()


FF_PIECES = D_FF // D_MODEL


class _WeightCast:
    SCRATCH = [pltpu.VMEM((D_MODEL, D_MODEL), F32), pltpu.VMEM((D_MODEL, D_MODEL), BF16),
               pltpu.SemaphoreType.DMA(()), pltpu.SemaphoreType.DMA(())]

    def __init__(self, layer, wup_hbm, wdn_hbm, oup_hbm, odn_hbm, scratch):
        self.layer, self.srcs, self.dsts = layer, (wup_hbm, wdn_hbm), (oup_hbm, odn_hbm)
        self.stage_in, self.stage_out, self.sem_in, self.sem_out = scratch
        self.step, self.n_steps = pl.program_id(0), pl.num_programs(0)

    def _piece(self, p):
        if p < FF_PIECES:
            cols = slice(p * D_MODEL, (p + 1) * D_MODEL)
            return self.srcs[0].at[self.layer, :, cols], self.dsts[0].at[:, cols]
        rows = slice((p - FF_PIECES) * D_MODEL, (p - FF_PIECES + 1) * D_MODEL)
        return self.srcs[1].at[self.layer, rows, :], self.dsts[1].at[rows, :]

    def _load(self, p):
        return pltpu.make_async_copy(self._piece(p)[0], self.stage_in, self.sem_in)

    def _store(self, p):
        return pltpu.make_async_copy(self.stage_out, self._piece(p)[1], self.sem_out)

    def fetch(self):
        for p in range(2 * FF_PIECES):
            @pl.when(self.step == p)
            def _():
                self._load(p).start()

    def convert(self):
        self._load(0).wait()

        @pl.when(self.step > 0)
        def _():
            self._store(0).wait()

        self.stage_out[...] = self.stage_in[...].astype(BF16)
        for p in range(2 * FF_PIECES):
            @pl.when(self.step == p)
            def _():
                self._store(p).start()

        @pl.when(self.step + 1 == self.n_steps)
        def _():
            self._store(0).wait()


CTX_PER_STEP = TOK // SEQ
CTX_WROWS = WIN_HEADS * SEQ
CTX_MROWS = MLA_HEADS * SEQ


def _ctx_mixer_kernel(n_alias, x_ref, mod_ref, n1_ref, win_ref, sink_ref, decay_ref, gn_ref, kvn_ref,
                      wkvb_ref, wout_ref, *refs):
    xo_ref, ko_ref, vo_ref, ckvo_ref, kro_ref, so_ref, h_scr, mix_scr = refs[n_alias:n_alias + 8]
    xs = _XStream(x_ref, xo_ref, refs[n_alias + 8:])
    xs.prologue(mod_ref, n1_ref, h_scr)
    per_seq = lambda a: a.reshape(CTX_PER_STEP, SEQ, a.shape[-1])
    if n_alias == 0:
        for ref in (ko_ref, vo_ref, ckvo_ref, kro_ref, so_ref):
            ref[:, 1:] = jnp.zeros((ref.shape[0], ref.shape[1] - 1) + ref.shape[2:], F32)
        ko_ref, vo_ref, ckvo_ref, kro_ref, so_ref = (
            ref.at[:, 0] for ref in (ko_ref, vo_ref, ckvo_ref, kro_ref, so_ref))
    chunks_per_seq = SEQ // RET_CHUNK

    def proj(c0, c1):
        return _dot_nt(h_scr[...], win_ref[c0:c1, :])

    ret = []
    seq_chunks = lambda e: [chunks_per_seq * e + c for c in range(chunks_per_seq)]

    def attention(qst_scr, k_scr, vaug_scr, qc_scr, kcat_scr, kvaug_scr,
                  sw0, sw1, pw0, pw1, ew0, ew1, sm0, sm1, pm0, pm1):
        lane_lo = _lane_iota((TOK, LANES)) < 64
        ones = jnp.ones((TOK, LANES), BF16)
        qa = proj(C_QA, C_KA) * ATTN_SCALE
        for g in range(4):
            q = qa[:, g * LANES:(g + 1) * LANES]
            lo = jnp.where(lane_lo, q, 0.0).astype(BF16).reshape(CTX_PER_STEP, SEQ, LANES)
            hi = jnp.where(lane_lo, 0.0, q).astype(BF16).reshape(CTX_PER_STEP, SEQ, LANES)
            qst_scr[:, g * SEQ:(g + 1) * SEQ, :] = lo
            qst_scr[:, (4 + g) * SEQ:(5 + g) * SEQ, :] = hi
        kva = proj(C_KA, C_QB)
        ko_ref[...] = per_seq(kva[:, :LANES])
        vo_ref[...] = per_seq(kva[:, LANES:])
        k_scr[...] = kva[:, :LANES].astype(BF16)
        vaug_scr[:, :LANES] = kva[:, LANES:].astype(BF16)
        vaug_scr[:, LANES:] = ones
        qc = proj(C_QC, C_CKV).astype(BF16)
        for h in range(MLA_HEADS):
            qc_scr[h] = qc[:, h * LANES:(h + 1) * LANES]
        ckr = proj(C_CKV, D_IN_P)
        ckv, kr = ckr[:, :LANES], ckr[:, LANES:]
        ckv_n = ckv * lax.rsqrt(jnp.mean(ckv * ckv, axis=-1, keepdims=True) + EPS) * kvn_ref[...]
        ckvo_ref[...] = per_seq(ckv_n)
        kro_ref[...] = per_seq(kr[:, 64:64 + MLA_ROPE])
        kv = _dot(ckv_n.astype(BF16), wkvb_ref[...])
        for h in range(MLA_HEADS):
            kvh = kv[:, h * LANES:(h + 1) * LANES]
            kcat_scr[h] = jnp.where(lane_lo, kvh, kr).astype(BF16)
            kvaug_scr[h, :, :LANES] = kvh.astype(BF16)
            kvaug_scr[h, :, LANES:] = ones

        sw, pw, ew, sm, pm = (sw0, sw1), (pw0, pw1), (ew0, ew1), (sm0, sm1), (pm0, pm1)
        lane_lo_s = _lane_iota((SEQ, LANES)) < 64

        def scores(e, b):
            rows = _ds(e * SEQ, SEQ)
            sw[b][...] = _dot_nt(qst_scr[e], k_scr[rows, :])
            for h in range(MLA_HEADS):
                sm[b][h * SEQ:(h + 1) * SEQ, :] = _dot_nt(qc_scr[h, rows, :], kcat_scr[h, rows, :])
            ret[0].scores(seq_chunks(e), b)

        def softmax(e, b):
            for h in range(WIN_HEADS):
                hr = slice(h * SEQ, (h + 1) * SEQ)
                sink = jnp.full((SEQ, 1), sink_ref[h], F32)
                (p,), extra = _softmax_tile([sw[b][hr, :]], floor=sink)
                pw[b][hr, :] = p
                ew[b][hr, :] = jnp.broadcast_to(extra, (SEQ, LANES))
            for h in range(MLA_HEADS):
                hr = slice(h * SEQ, (h + 1) * SEQ)
                (p,), _ = _softmax_tile([sm[b][hr, :]], scale=MLA_SCALE)
                pm[b][hr, :] = p
            ret[0].mask(seq_chunks(e), b)

        def values(e, b):
            ret[0].values(seq_chunks(e), b)
            rows = _ds(e * SEQ, SEQ)
            oa = _dot(pw[b][...], vaug_scr[rows, :])
            o = oa[:, :LANES] / (oa[:, LANES:] + ew[b][...])
            for g in range(4):
                merged = jnp.where(lane_lo_s, o[g * SEQ:(g + 1) * SEQ], o[(4 + g) * SEQ:(5 + g) * SEQ])
                mix_scr[rows, M_A + g * LANES:M_A + (g + 1) * LANES] = merged.astype(BF16)
            for jp in range(2):
                outs = []
                for h in (2 * jp, 2 * jp + 1):
                    oc = _dot(pm[b][h * SEQ:(h + 1) * SEQ, :], kvaug_scr[h, rows, :])
                    outs.append(oc[:, :LANES] / oc[:, LANES:])
                merged = jnp.where(lane_lo_s, pltpu.roll(outs[0], 64, 1), outs[1])
                mix_scr[rows, M_C + jp * LANES:M_C + (jp + 1) * LANES] = merged.astype(BF16)

        _pipeline(CTX_PER_STEP, scores, softmax, values)

    attention_scratch = (
        pltpu.VMEM((CTX_PER_STEP, CTX_WROWS, LANES), BF16),
        pltpu.VMEM((TOK, LANES), BF16),
        pltpu.VMEM((TOK, 2 * LANES), BF16),
        pltpu.VMEM((MLA_HEADS, TOK, LANES), BF16),
        pltpu.VMEM((MLA_HEADS, TOK, LANES), BF16),
        pltpu.VMEM((MLA_HEADS, TOK, 2 * LANES), BF16),
        pltpu.VMEM((CTX_WROWS, SEQ), F32), pltpu.VMEM((CTX_WROWS, SEQ), F32),
        pltpu.VMEM((CTX_WROWS, SEQ), BF16), pltpu.VMEM((CTX_WROWS, SEQ), BF16),
        pltpu.VMEM((CTX_WROWS, LANES), F32), pltpu.VMEM((CTX_WROWS, LANES), F32),
        pltpu.VMEM((CTX_MROWS, SEQ), F32), pltpu.VMEM((CTX_MROWS, SEQ), F32),
        pltpu.VMEM((CTX_MROWS, SEQ), BF16), pltpu.VMEM((CTX_MROWS, SEQ), BF16),
    )

    def retention(*scratch):
        r = _Retention(scratch, gn_ref, mix_scr)
        r.st[...] = jnp.zeros(r.st.shape, F32)
        r.prepare(proj, decay_ref, [(chunks_per_seq * e, chunks_per_seq) for e in range(CTX_PER_STEP)])
        for e in range(CTX_PER_STEP):
            for dirn in range(2):
                for j in range(2):
                    st = r.st[e, dirn, j]
                    so_ref[e, dirn, 2 * j] = st[:64, :64]
                    so_ref[e, dirn, 2 * j + 1] = st[64:, 64:]
        ret.append(r)
        pl.run_scoped(attention, *attention_scratch)

    pl.run_scoped(retention, *_ret_scratch(CTX_PER_STEP, chunks_per_seq))
    xs.epilogue(mod_ref, wout_ref, mix_scr)


def _const_spec(shape, layer=None):
    if layer is None:
        return pl.BlockSpec(shape, lambda i: (0,) * len(shape), pipeline_mode=pl.Buffered(1))
    return pl.BlockSpec((None,) + shape, lambda i: (layer,) + (0,) * len(shape),
                        pipeline_mode=pl.Buffered(1))


def _smem_spec():
    return pl.BlockSpec(memory_space=pltpu.SMEM)


def _ctx_mixer(layer, x, mod, n1, w_in_p, sink, decay, gn, kvn, w_kv_b, w_out_p, prev_state):
    n_tok = x.shape[0]
    n_seq = n_tok // SEQ
    hbm_spec = pl.BlockSpec(memory_space=pl.ANY)
    if prev_state:
        state_spec = lambda *tail: pl.BlockSpec(
            (CTX_PER_STEP, None) + tail, lambda i: (i, layer) + (0,) * len(tail))
    else:
        assert layer == 0
        state_spec = lambda *tail: pl.BlockSpec(
            (CTX_PER_STEP, DEPTH) + tail, lambda i: (i, 0) + (0,) * len(tail))
    state_tails = [(SEQ, LANES), (SEQ, LANES), (SEQ, MLA_KV_RANK), (SEQ, MLA_ROPE),
                   (2, RET_HEADS, HEAD_DIM, HEAD_DIM)]
    n_in = 10
    return pl.pallas_call(
        functools.partial(_ctx_mixer_kernel, len(prev_state)),
        out_shape=[jax.ShapeDtypeStruct((n_tok, D_MODEL), F32)] + [
            jax.ShapeDtypeStruct((n_seq, DEPTH) + tail, F32) for tail in state_tails],
        grid=(n_tok // TOK,),
        in_specs=[
            hbm_spec,
            _const_spec((N_MOD, D_MODEL)),
            _const_spec((1, D_MODEL)),
            _const_spec((D_IN_P, D_MODEL), layer),
            _smem_spec(),
            _smem_spec(),
            _const_spec((1, RET_HEADS * HEAD_DIM)),
            _const_spec((1, MLA_KV_RANK)),
            _const_spec((MLA_KV_RANK, MLA_HEADS * LANES), layer),
            _const_spec((D_MODEL, D_MODEL), layer),
        ] + [hbm_spec] * len(prev_state),
        out_specs=[hbm_spec] + [state_spec(*tail) for tail in state_tails],
        input_output_aliases={n_in + k: 1 + k for k in range(len(prev_state))},
        scratch_shapes=[
            pltpu.VMEM((TOK, D_MODEL), BF16),
            pltpu.VMEM((TOK, D_MODEL), BF16),
        ] + _XStream.SCRATCH,
        compiler_params=pltpu.CompilerParams(
            dimension_semantics=("arbitrary",), vmem_limit_bytes=VMEM_LIMIT),
        name="ctx_mixer",
    )(x, mod, n1, w_in_p, sink, decay, gn, kvn, w_kv_b, w_out_p, *prev_state)


N_BLK = DEC_SEQ // LANES
KEYS_LOC = 3 * LANES
KEYS_WIN = KEYS_LOC + PAST_LEN
WIN_ROWS = WIN_HEADS * LANES
MLA_QB = 256
MLA_KEYS = DEC_SEQ + PAST_LEN
MLA_HALF = MLA_KEYS // 2


def _lat_mixer_kernel(layer, x_ref, mod_ref, n1_ref, win_ref, sink_ref, decay_ref, gn_ref, kvn_ref,
                      wkvb_ref, wout_ref, ck_ref, cv_ref, cckv_ref, ckr_ref, s0_ref,
                      rc_ref, rsa_ref, rsb_ref, mc_ref, msa_ref, msb_ref, wup_ref, wdn_ref,
                      xo_ref, wup16_ref, wdn16_ref,
                      h_scr, mix_scr, s0_scr, s1_scr, p0_scr, p1_scr, *dma_scratch):
    xs = _XStream(x_ref, xo_ref, dma_scratch[:len(_XStream.SCRATCH)])
    wc = _WeightCast(layer, wup_ref, wdn_ref, wup16_ref, wdn16_ref,
                     dma_scratch[len(_XStream.SCRATCH):])
    xs.prologue(mod_ref, n1_ref, h_scr)
    wc.fetch()
    sbuf, pbuf = (s0_scr, s1_scr), (p0_scr, p1_scr)

    def proj(c0, c1):
        return _dot_nt(h_scr[...], win_ref[c0:c1, :])

    ret = []

    def window(qst_scr, kpad_scr, vaug_scr, ckb_scr, cvaug_scr, bias_scr, e0_scr, e1_scr):
        ebuf = (e0_scr, e1_scr)
        lane_lo = _lane_iota((TOK, LANES)) < 64
        rc, rsa, rsb = rc_ref[...], rsa_ref[...], rsb_ref[...]
        qa = proj(C_QA, C_KA)
        for g in range(4):
            q = _rope(qa[:, g * LANES:(g + 1) * LANES], rc, rsa, rsb, 16) * ATTN_SCALE
            lo = jnp.where(lane_lo, q, 0.0).astype(BF16).reshape(N_BLK, LANES, LANES)
            hi = jnp.where(lane_lo, 0.0, q).astype(BF16).reshape(N_BLK, LANES, LANES)
            qst_scr[:, g * LANES:(g + 1) * LANES, :] = lo
            qst_scr[:, (4 + g) * LANES:(5 + g) * LANES, :] = hi
        kva = proj(C_KA, C_QB)
        zpad = jnp.zeros((LANES, LANES), BF16)
        kpad_scr[0:LANES, :] = zpad
        kpad_scr[LANES + TOK:, :] = zpad
        vaug_scr[0:LANES, :LANES] = zpad
        vaug_scr[LANES + TOK:, :LANES] = zpad
        kpad_scr[LANES:LANES + TOK, :] = _rope(kva[:, :LANES], rc, rsa, rsb, 16).astype(BF16)
        vaug_scr[LANES:LANES + TOK, :LANES] = kva[:, LANES:].astype(BF16)
        vaug_scr[:, LANES:] = jnp.ones((TOK + 2 * LANES, LANES), BF16)
        ckb_scr[...] = ck_ref[...].astype(BF16)
        cvaug_scr[:, :LANES] = cv_ref[...].astype(BF16)
        cvaug_scr[:, LANES:] = jnp.ones((PAST_LEN, LANES), BF16)
        qi, kj = _row_iota((LANES, LANES)), _lane_iota((LANES, LANES))
        bias_scr[0] = jnp.full((LANES, LANES), -jnp.inf, F32)
        bias_scr[1] = jnp.where(kj >= qi, 0.0, -jnp.inf)
        bias_scr[2] = jnp.where(kj <= qi, 0.0, -jnp.inf)
        lane_lo_b = _lane_iota((LANES, LANES)) < 64

        def scores(n, b):
            q = qst_scr[n]
            sbuf[b][:, :KEYS_LOC] = _dot_nt(q, kpad_scr[_ds(n * LANES, KEYS_LOC), :])
            sbuf[b][:, KEYS_LOC:] = _dot_nt(q, ckb_scr[...])
            ret[0].scores([n], b)

        def softmax(n, b):
            if isinstance(n, int):
                i_prev, i_next = (1 if n > 0 else 0), (2 if n < N_BLK - 1 else 0)
            else:
                i_prev, i_next = jnp.where(n > 0, 1, 0), jnp.where(n < N_BLK - 1, 2, 0)
            b_prev, b_next = bias_scr[i_prev], bias_scr[i_next]
            for h in range(WIN_HEADS):
                hr = slice(h * LANES, (h + 1) * LANES)
                parts = [sbuf[b][hr, 0:LANES] + b_prev, sbuf[b][hr, LANES:2 * LANES],
                         sbuf[b][hr, 2 * LANES:KEYS_LOC] + b_next, sbuf[b][hr, KEYS_LOC:]]
                ps, extra = _softmax_tile(parts, floor=jnp.full((LANES, 1), sink_ref[h], F32))
                pbuf[b][hr, 0:LANES] = ps[0]
                pbuf[b][hr, LANES:2 * LANES] = ps[1]
                pbuf[b][hr, 2 * LANES:KEYS_LOC] = ps[2]
                pbuf[b][hr, KEYS_LOC:] = ps[3]
                ebuf[b][hr, :] = jnp.broadcast_to(extra, (LANES, LANES))
            ret[0].mask([n], b)

        def values(n, b):
            ret[0].values([n], b)
            oa = (_dot(pbuf[b][:, :KEYS_LOC], vaug_scr[_ds(n * LANES, KEYS_LOC), :])
                  + _dot(pbuf[b][:, KEYS_LOC:], cvaug_scr[...]))
            o = oa[:, :LANES] / (oa[:, LANES:] + ebuf[b][...])
            for g in range(4):
                merged = jnp.where(lane_lo_b, o[g * LANES:(g + 1) * LANES],
                                   o[(4 + g) * LANES:(5 + g) * LANES])
                mix_scr[_ds(n * LANES, LANES), M_A + g * LANES:M_A + (g + 1) * LANES] = (
                    merged.astype(BF16))

        _pipeline(N_BLK, scores, softmax, values)

    window_scratch = (
        pltpu.VMEM((N_BLK, WIN_ROWS, LANES), BF16),
        pltpu.VMEM((TOK + 2 * LANES, LANES), BF16),
        pltpu.VMEM((TOK + 2 * LANES, 2 * LANES), BF16),
        pltpu.VMEM((PAST_LEN, LANES), BF16),
        pltpu.VMEM((PAST_LEN, 2 * LANES), BF16),
        pltpu.VMEM((3, LANES, LANES), F32),
        pltpu.VMEM((WIN_ROWS, LANES), F32), pltpu.VMEM((WIN_ROWS, LANES), F32),
    )

    def retention(*scratch):
        r = _Retention(scratch, gn_ref, mix_scr)
        r.st[0] = s0_ref[...]
        r.prepare(proj, decay_ref, [(0, N_CHUNK)])
        ret.append(r)
        pl.run_scoped(window, *window_scratch)

    assert N_CHUNK == N_BLK
    pl.run_scoped(retention, *_ret_scratch(1, 1))

    def latent(qc_scr, kcat_scr, kvaug_scr, mixc_scr):
        lane_lo = _lane_iota((TOK, LANES)) < 64
        mc, msa, msb = mc_ref[...], msa_ref[...], msb_ref[...]
        qc = proj(C_QC, C_CKV)
        for h in range(MLA_HEADS):
            qc_scr[h] = _rope(qc[:, h * LANES:(h + 1) * LANES], mc, msa, msb, 8).astype(BF16)
        ckr = proj(C_CKV, D_IN_P)
        ckv = ckr[:, :LANES]
        kr = _rope(ckr[:, LANES:], mc, msa, msb, 8)
        ckv_n = ckv * lax.rsqrt(jnp.mean(ckv * ckv, axis=-1, keepdims=True) + EPS) * kvn_ref[...]
        kv = _dot(ckv_n.astype(BF16), wkvb_ref[...])
        kv_c = _dot(cckv_ref[...].astype(BF16), wkvb_ref[...])
        kr_c = ckr_ref[...]
        lane_lo_c = _lane_iota((PAST_LEN, LANES)) < 64
        for h in range(MLA_HEADS):
            kvh, kvh_c = kv[:, h * LANES:(h + 1) * LANES], kv_c[:, h * LANES:(h + 1) * LANES]
            kcat_scr[h, 0:TOK, :] = jnp.where(lane_lo, kvh, kr).astype(BF16)
            kcat_scr[h, TOK:, :] = jnp.where(lane_lo_c, kvh_c, kr_c).astype(BF16)
            kvaug_scr[h, 0:TOK, :LANES] = kvh.astype(BF16)
            kvaug_scr[h, TOK:, :LANES] = kvh_c.astype(BF16)
            kvaug_scr[h, :, LANES:] = jnp.ones((MLA_KEYS, LANES), BF16)
        lane_lo_m = _lane_iota((MLA_QB, LANES)) < 64
        n_qb = DEC_SEQ // MLA_QB

        def split(t):
            if isinstance(t, int):
                return t // n_qb, t % n_qb
            return lax.shift_right_logical(t, 2), lax.bitwise_and(t, n_qb - 1)

        def scores(t, b):
            jp, qb = split(t)
            for i in range(2):
                h = 2 * jp + i
                q = qc_scr[h, _ds(qb * MLA_QB, MLA_QB), :]
                for part in range(2):
                    r0 = (2 * i + part) * MLA_QB
                    sbuf[b][r0:r0 + MLA_QB, :] = _dot_nt(
                        q, kcat_scr[h, part * MLA_HALF:(part + 1) * MLA_HALF, :])

        def softmax(t, b):
            for i in range(2):
                for rt in range(MLA_QB // LANES):
                    ra = 2 * i * MLA_QB + rt * LANES
                    rb = ra + MLA_QB
                    ps, _ = _softmax_tile([sbuf[b][ra:ra + LANES, :], sbuf[b][rb:rb + LANES, :]],
                                          scale=MLA_SCALE)
                    pbuf[b][ra:ra + LANES, :] = ps[0]
                    pbuf[b][rb:rb + LANES, :] = ps[1]

        def values(t, b):
            jp, qb = split(t)
            outs = []
            for i in range(2):
                h = 2 * jp + i
                r0 = 2 * i * MLA_QB
                oc = (_dot(pbuf[b][r0:r0 + MLA_QB, :], kvaug_scr[h, 0:MLA_HALF, :])
                      + _dot(pbuf[b][r0 + MLA_QB:r0 + 2 * MLA_QB, :], kvaug_scr[h, MLA_HALF:, :]))
                outs.append(oc[:, :LANES] / oc[:, LANES:])
            merged = jnp.where(lane_lo_m, pltpu.roll(outs[0], 64, 1), outs[1])
            mixc_scr[jp, _ds(qb * MLA_QB, MLA_QB), :] = merged.astype(BF16)

        _pipeline(2 * n_qb, scores, softmax, values)
        for jp in range(2):
            mix_scr[:, M_C + jp * LANES:M_C + (jp + 1) * LANES] = mixc_scr[jp]

    pl.run_scoped(
        latent,
        pltpu.VMEM((MLA_HEADS, TOK, LANES), BF16),
        pltpu.VMEM((MLA_HEADS, MLA_KEYS, LANES), BF16),
        pltpu.VMEM((MLA_HEADS, MLA_KEYS, 2 * LANES), BF16),
        pltpu.VMEM((2, TOK, LANES), BF16),
    )
    wc.convert()
    xs.epilogue(mod_ref, wout_ref, mix_scr)


def _lat_mixer(layer, x, mod, n1, w_in_p, sink, decay, gn, kvn, w_kv_b, w_out_p,
               ck, cv, cckv, ckr, s0, rope_a, rope_m, w_up, w_down):
    n_tok = x.shape[0]
    n_seq = n_tok // DEC_SEQ
    assert n_seq == 2 * FF_PIECES
    hbm_spec = pl.BlockSpec(memory_space=pl.ANY)
    seq_spec = lambda shape: pl.BlockSpec(
        (None, None) + shape, lambda i: (i, layer) + (0,) * len(shape))
    assert WIN_ROWS == 4 * MLA_QB and KEYS_WIN == MLA_HALF
    return pl.pallas_call(
        functools.partial(_lat_mixer_kernel, layer),
        out_shape=(jax.ShapeDtypeStruct((n_tok, D_MODEL), F32),
                   jax.ShapeDtypeStruct((D_MODEL, D_FF), BF16),
                   jax.ShapeDtypeStruct((D_FF, D_MODEL), BF16)),
        grid=(n_seq,),
        in_specs=[
            hbm_spec,
            pl.BlockSpec((None, N_MOD, D_MODEL), lambda i: (i, 0, 0)),
            _const_spec((1, D_MODEL)),
            _const_spec((D_IN_P, D_MODEL), layer),
            _smem_spec(),
            _smem_spec(),
            _const_spec((1, RET_HEADS * HEAD_DIM)),
            _const_spec((1, MLA_KV_RANK)),
            _const_spec((MLA_KV_RANK, MLA_HEADS * LANES), layer),
            _const_spec((D_MODEL, D_MODEL), layer),
            seq_spec((PAST_LEN, LANES)),
            seq_spec((PAST_LEN, LANES)),
            seq_spec((PAST_LEN, MLA_KV_RANK)),
            seq_spec((PAST_LEN, LANES)),
            seq_spec((2, 2, LANES, LANES)),
        ] + [_const_spec((DEC_SEQ, LANES))] * 6 + [hbm_spec, hbm_spec],
        out_specs=(hbm_spec, hbm_spec, hbm_spec),
        scratch_shapes=[
            pltpu.VMEM((TOK, D_MODEL), BF16),
            pltpu.VMEM((TOK, D_MODEL), BF16),
            pltpu.VMEM((WIN_ROWS, KEYS_WIN), F32), pltpu.VMEM((WIN_ROWS, KEYS_WIN), F32),
            pltpu.VMEM((WIN_ROWS, KEYS_WIN), BF16), pltpu.VMEM((WIN_ROWS, KEYS_WIN), BF16),
        ] + _XStream.SCRATCH + _WeightCast.SCRATCH,
        compiler_params=pltpu.CompilerParams(
            dimension_semantics=("arbitrary",), vmem_limit_bytes=VMEM_LIMIT),
        name="lat_mixer",
    )(x, mod, n1, w_in_p, sink, decay, gn, kvn, w_kv_b, w_out_p, ck, cv, cckv, ckr, s0,
      *rope_a, *rope_m, w_up, w_down)


def _mlp_kernel(final, x_ref, mod_ref, n2_ref, wup_ref, wdn_ref, fn_ref, o_ref):
    x = x_ref[...]
    h2 = _norm_mod(x, n2_ref[...], mod_ref[4:5, :], mod_ref[3:4, :]).astype(BF16)
    acc = None
    for c in range(D_FF // FF_CHUNK):
        cols = slice(c * FF_CHUNK, (c + 1) * FF_CHUNK)
        u = jnp.maximum(_dot(h2, wup_ref[:, cols]), 0.0)
        part = _dot((u * u).astype(BF16), wdn_ref[cols, :])
        acc = part if acc is None else acc + part
    y = x + mod_ref[5:6, :] * acc
    if final:
        y = y * lax.rsqrt(jnp.mean(y * y, axis=-1, keepdims=True) + EPS) * fn_ref[...]
    o_ref[...] = y


def _mlp(x, mod, n2, w_up, w_down, final_norm, final):
    n_tok = x.shape[0]
    per_mod = n_tok // mod.shape[0] // MLP_ROWS
    return pl.pallas_call(
        functools.partial(_mlp_kernel, final),
        out_shape=jax.ShapeDtypeStruct((n_tok, D_MODEL), F32),
        grid=(n_tok // MLP_ROWS,),
        in_specs=[
            pl.BlockSpec((MLP_ROWS, D_MODEL), lambda i: (i, 0)),
            pl.BlockSpec((None, N_MOD, D_MODEL), lambda i: (i // per_mod, 0, 0)),
            _const_spec((1, D_MODEL)),
            _const_spec((D_MODEL, D_FF)),
            _const_spec((D_FF, D_MODEL)),
            _const_spec((1, D_MODEL)),
        ],
        out_specs=pl.BlockSpec((MLP_ROWS, D_MODEL), lambda i: (i, 0)),
        compiler_params=pltpu.CompilerParams(
            dimension_semantics=("arbitrary",), vmem_limit_bytes=VMEM_LIMIT),
        name="mlp",
    )(x, mod, n2, w_up, w_down, final_norm)


def kernel(x_prompt, x_sample, cache_win_k, cache_win_v, cache_mla_ckv, cache_mla_krope, state_ret,
           c, c_ctx, w_mod, b_mod, norm1, norm2, w_in, win_sink, ret_decay, ret_gn, mla_kv_norm,
           w_kv_b, w_out, w_up, w_down, final_norm):
    n_ctx, n_lat = x_prompt.shape[0], x_sample.shape[0]

    w_in_p, w_out_p = _proj_layout(jnp.swapaxes(w_in, 1, 2), w_out)
    w_kv_b16 = w_kv_b.astype(BF16)

    c_rows = jnp.zeros((16, D_MODEL), F32).at[0].set(c_ctx).at[1:1 + n_lat].set(c)
    mod = _modulation(c_rows, w_mod, b_mod).reshape(DEPTH, 16, N_MOD, D_MODEL)

    rope_a = _rope_tables(DEC_SEQ, HEAD_DIM, 0, HEAD_DIM)
    rope_m = _rope_tables(DEC_SEQ, MLA_ROPE, MLA_NOPE, LANES)

    ck = cache_win_k.reshape(n_lat, DEPTH, PAST_LEN, LANES)
    cv = cache_win_v.reshape(n_lat, DEPTH, PAST_LEN, LANES)
    ckr = jnp.pad(cache_mla_krope, ((0, 0), (0, 0), (0, 0), (MLA_NOPE, LANES - MLA_NOPE - MLA_ROPE)))
    sr = state_ret.reshape(n_lat, DEPTH, 2, 2, 2, HEAD_DIM, HEAD_DIM)
    zero = jnp.zeros_like(sr[:, :, :, :, 0])
    s0 = jnp.concatenate([jnp.concatenate([sr[:, :, :, :, 0], zero], axis=-1),
                          jnp.concatenate([zero, sr[:, :, :, :, 1]], axis=-1)], axis=-2)

    xp = x_prompt.reshape(n_ctx * SEQ, D_MODEL)
    xs = x_sample.reshape(n_lat * DEC_SEQ, D_MODEL)
    state = ()
    for l in range(DEPTH):
        last = l == DEPTH - 1
        shared = (norm1[l][None], w_in_p, win_sink[l], ret_decay[l], ret_gn[l][None],
                  mla_kv_norm[l][None], w_kv_b16, w_out_p)
        mod_ctx, mod_lat = mod[l, 0:1], mod[l, 1:1 + n_lat]
        xp, *state = _ctx_mixer(l, xp, mod_ctx[0], *shared, state)
        xs, w_up16, w_down16 = _lat_mixer(l, xs, mod_lat, *shared, ck, cv, cache_mla_ckv, ckr, s0,
                                          rope_a, rope_m, w_up, w_down)
        xp = _mlp(xp, mod_ctx, norm2[l][None], w_up16, w_down16, final_norm[None], last)
        xs = _mlp(xs, mod_lat, norm2[l][None], w_up16, w_down16, final_norm[None], last)
    new_k, new_v, new_ckv, new_kr, new_s = state
    return (xp.reshape(n_ctx, SEQ, D_MODEL), xs.reshape(n_lat, DEC_SEQ, D_MODEL),
            new_k.reshape(n_ctx, DEPTH, SEQ, 2, HEAD_DIM), new_v.reshape(n_ctx, DEPTH, SEQ, 2, HEAD_DIM),
            new_ckv, new_kr, new_s)
```

```python
import functools

import numpy as np
import jax
import jax.numpy as jnp
from jax import lax
from jax.experimental import pallas as pl
from jax.experimental.pallas import tpu as pltpu

F32 = jnp.float32
BF16 = jnp.bfloat16

D_MODEL = 1024
DEPTH = 2
SEQ = 256
DEC_SEQ = 1024
PAST_LEN = 256
GRID_W = 64
HEAD_DIM = 64
ROPE_BASE = 10000.0
EPS = 1e-6
WIN_HEADS = 8
WINDOW = 128
ATTN_SCALE = HEAD_DIM ** -0.5
RET_HEADS = 4
RET_CHUNK = 128
RET_K_SCALE = HEAD_DIM ** -0.5
MLA_HEADS = 4
MLA_NOPE = 64
MLA_ROPE = 32
MLA_KV_RANK = 128
MLA_QK = MLA_NOPE + MLA_ROPE
MLA_SCALE = MLA_QK ** -0.5
D_IN = 2336
D_FF = 4 * D_MODEL
N_MOD = 6

LANES = 128
TOK = 1024
MLP_ROWS = 1024
FF_CHUNK = 1024
VMEM_LIMIT = 60 * 1024 * 1024

C_QA, C_KA, C_VA, C_QB, C_KB, C_VB, C_GB, C_QC, C_CKV, C_KR, D_IN_P = (
    0, 512, 640, 768, 1024, 1280, 1536, 1792, 2304, 2432, 2560)
M_A, M_B, M_C = 0, 512, 768

NT_DIMS = (((1,), (1,)), ((), ()))


def _in_proj_columns():
    idx = []
    for g in range(4):
        idx += list(range(g * 64, (g + 1) * 64)) + list(range((4 + g) * 64, (5 + g) * 64))
    idx += list(range(512, 1792))
    for h in range(MLA_HEADS):
        idx += list(range(1792 + h * MLA_QK, 1792 + (h + 1) * MLA_QK)) + [-1] * 32
    idx += list(range(2176, 2304))
    idx += [-1] * 64 + list(range(2304, 2336)) + [-1] * 32
    return np.asarray(idx, np.int32)


def _mix_rows():
    idx = []
    for g in range(4):
        idx += list(range(g * 64, (g + 1) * 64)) + list(range((4 + g) * 64, (5 + g) * 64))
    idx += list(range(512, 1024))
    return np.asarray(idx, np.int32)


def _take_runs(w, idx, axis):
    pieces, i = [], 0
    while i < len(idx):
        j = i + 1
        if idx[i] < 0:
            while j < len(idx) and idx[j] < 0:
                j += 1
            shape = list(w.shape)
            shape[axis] = j - i
            pieces.append(jnp.zeros(shape, w.dtype))
        else:
            while j < len(idx) and idx[j] == idx[j - 1] + 1:
                j += 1
            pieces.append(lax.slice_in_dim(w, int(idx[i]), int(idx[j - 1]) + 1, axis=axis))
        i = j
    return jnp.concatenate(pieces, axis=axis)


def _rope_tables(n_tokens, dim, lane0, period):
    quarter = dim // 4
    t = np.arange(n_tokens)
    row = (t // GRID_W).astype(np.float64)
    col = (t % GRID_W).astype(np.float64)
    inv_freq = ROPE_BASE ** (-np.arange(quarter, dtype=np.float64) / quarter)
    ar, ac = row[:, None] * inv_freq, col[:, None] * inv_freq
    cos = np.concatenate([np.cos(ar), np.cos(ar), np.cos(ac), np.cos(ac)], axis=-1)
    sin = np.concatenate([np.sin(ar), np.sin(ar), np.sin(ac), np.sin(ac)], axis=-1)
    first = np.tile(np.concatenate([np.ones(quarter), np.zeros(quarter)]), 2)
    c = np.ones((n_tokens, LANES))
    sa = np.zeros((n_tokens, LANES))
    sb = np.zeros((n_tokens, LANES))
    for start in range(lane0, LANES, period):
        c[:, start:start + dim] = cos
        sa[:, start:start + dim] = -sin * first
        sb[:, start:start + dim] = sin * (1.0 - first)
    return tuple(jnp.asarray(a, F32) for a in (c, sa, sb))


def _lane_iota(shape):
    return lax.broadcasted_iota(jnp.int32, shape, len(shape) - 1)


def _row_iota(shape):
    return lax.broadcasted_iota(jnp.int32, shape, len(shape) - 2)


def _ds(start, size):
    if isinstance(start, int):
        return pl.ds(start, size)
    return pl.ds(pl.multiple_of(start, LANES), size)


def _norm_mod(x, gain, scale, shift):
    ms = jnp.mean(x * x, axis=-1, keepdims=True)
    return (x * lax.rsqrt(ms + EPS) * gain) * (1.0 + scale) + shift


def _log_sigmoid(x):
    return -(jnp.maximum(-x, 0.0) + jnp.log1p(jnp.exp(-jnp.abs(x))))


def _rope(x, c, sa, sb, quarter):
    return (x * c + pltpu.roll(x, LANES - quarter, 1) * sa + pltpu.roll(x, quarter, 1) * sb)


def _dot(a, b):
    return jnp.dot(a, b, preferred_element_type=F32)


def _dot_nt(a, b):
    return lax.dot_general(a, b, NT_DIMS, preferred_element_type=F32)


def _mod_kernel(c_ref, w_ref, b_ref, o_ref):
    cv = c_ref[...]
    s = cv * jax.nn.sigmoid(cv)
    o_ref[0] = _dot(s.astype(BF16), w_ref[0].astype(BF16)) + b_ref[0]


def _modulation(c_rows, w_mod, b_mod):
    tn = 1536
    nj = (N_MOD * D_MODEL) // tn
    return pl.pallas_call(
        _mod_kernel,
        out_shape=jax.ShapeDtypeStruct((DEPTH, 16, N_MOD * D_MODEL), F32),
        grid=(DEPTH, nj),
        in_specs=[
            pl.BlockSpec((16, D_MODEL), lambda l, j: (0, 0)),
            pl.BlockSpec((1, D_MODEL, tn), lambda l, j: (l, 0, j)),
            pl.BlockSpec((1, 1, tn), lambda l, j: (l, 0, j)),
        ],
        out_specs=pl.BlockSpec((1, 16, tn), lambda l, j: (l, 0, j)),
        compiler_params=pltpu.CompilerParams(
            dimension_semantics=("arbitrary", "arbitrary"), vmem_limit_bytes=VMEM_LIMIT),
        name="modulation",
    )(c_rows, w_mod, b_mod.reshape(DEPTH, 1, N_MOD * D_MODEL))


W_IN_LANES = 256


def _proj_layout_kernel(win_ref, wout_ref, oin_ref, oout_ref):
    oin_ref[0] = _take_runs(win_ref[0], _in_proj_columns(), 0).astype(BF16)
    oout_ref[0] = _take_runs(wout_ref[0], _mix_rows(), 0).astype(BF16)


def _proj_layout(w_in_t, w_out):
    col_spec = lambda rows: pl.BlockSpec((1, rows, W_IN_LANES), lambda l, c: (l, 0, c))
    return pl.pallas_call(
        _proj_layout_kernel,
        out_shape=(jax.ShapeDtypeStruct((DEPTH, D_IN_P, D_MODEL), BF16),
                   jax.ShapeDtypeStruct((DEPTH, D_MODEL, D_MODEL), BF16)),
        grid=(DEPTH, D_MODEL // W_IN_LANES),
        in_specs=[col_spec(D_IN), col_spec(D_MODEL)],
        out_specs=(col_spec(D_IN_P), col_spec(D_MODEL)),
        compiler_params=pltpu.CompilerParams(
            dimension_semantics=("arbitrary", "arbitrary"), vmem_limit_bytes=VMEM_LIMIT),
        name="proj_layout",
    )(w_in_t, w_out)


def _pipeline(n_steps, scores, softmax, values):
    scores(0, 0)
    scores(1, 1)
    softmax(0, 0)

    def body(i, carry):
        t = 2 * i + 1
        scores(t + 1, 0)
        softmax(t, 1)
        values(t - 1, 0)
        scores(t + 2, 1)
        softmax(t + 1, 0)
        values(t, 1)
        return carry

    lax.fori_loop(0, n_steps // 2 - 1, body, 0)
    softmax(n_steps - 1, 1)
    values(n_steps - 2, 0)
    values(n_steps - 1, 1)


def _softmax_tile(parts, floor=None, scale=None):
    m = None
    for s in parts:
        pm = jnp.max(s, axis=-1, keepdims=True)
        m = pm if m is None else jnp.maximum(m, pm)
    if floor is not None:
        m = jnp.maximum(m, floor)
    if scale is None:
        ps = [jnp.exp(s - m).astype(BF16) for s in parts]
    else:
        ps = [jnp.exp((s - m) * scale).astype(BF16) for s in parts]
    extra = None if floor is None else jnp.exp(floor - m)
    return ps, extra


def _retention_tables(decay_ref, rt_scr, m2_scr):
    shape = (RET_CHUNK, LANES)
    lane_lo = _lane_iota(shape) < 64
    row = _row_iota(shape)
    row_lo = row < 64
    i = row.astype(F32)
    rel = i - _lane_iota(shape).astype(F32)
    for j in range(2):
        df0, df1 = decay_ref[0, 2 * j], decay_ref[0, 2 * j + 1]
        db0, db1 = decay_ref[1, 2 * j], decay_ref[1, 2 * j + 1]
        lgf = _log_sigmoid(jnp.where(lane_lo, df0, df1))
        lgb = _log_sigmoid(jnp.where(lane_lo, db0, db1))
        rt_scr[j, 0] = jnp.exp((i + 1.0) * lgf)
        rt_scr[j, 1] = jnp.exp((RET_CHUNK - i) * lgb)
        rt_scr[j, 2] = jnp.exp((RET_CHUNK - 1.0 - i) * lgf)
        rt_scr[j, 3] = jnp.exp(i * lgb)
        rt_scr[j, 4] = jnp.exp(RET_CHUNK * _log_sigmoid(jnp.where(row_lo, df0, df1)))
        rt_scr[j, 5] = jnp.exp(RET_CHUNK * _log_sigmoid(jnp.where(row_lo, db0, db1)))
        for hh, (df, db) in enumerate(((df0, db0), (df1, db1))):
            lf = _log_sigmoid(jnp.full(shape, df, F32))
            lb = _log_sigmoid(jnp.full(shape, db, F32))
            low, upp = rel >= 0.0, rel <= 0.0
            m = (jnp.where(low, jnp.exp(jnp.where(low, rel, 0.0) * lf), 0.0)
                 + jnp.where(upp, jnp.exp(jnp.where(upp, -rel, 0.0) * lb), 0.0))
            m2_scr[j, :, hh * RET_CHUNK:(hh + 1) * RET_CHUNK] = m


N_CHUNK = TOK // RET_CHUNK


def _ret_scratch(n_seq, chunks_per_stage):
    per_chunk = lambda dtype: pltpu.VMEM((N_CHUNK, 2, 2 * RET_CHUNK, LANES), dtype)
    stage = lambda dtype: pltpu.VMEM((2 * chunks_per_stage, RET_CHUNK, 2 * LANES), dtype)
    return [
        pltpu.VMEM((TOK, 2 * LANES), BF16),
        pltpu.VMEM((TOK, 4 * LANES), BF16),
        per_chunk(BF16),
        pltpu.VMEM((TOK, 2 * LANES), BF16),
        per_chunk(BF16),
        pltpu.VMEM((TOK, 2 * LANES), F32),
        pltpu.VMEM((2, 6, RET_CHUNK, LANES), F32),
        pltpu.VMEM((2, RET_CHUNK, 2 * LANES), F32),
        per_chunk(BF16),
        pltpu.VMEM((n_seq, 2, 2, LANES, LANES), F32),
        stage(F32), stage(F32), stage(BF16), stage(BF16),
    ]


class _Retention:
    def __init__(self, scratch, gn_ref, mix_scr):
        (self.q, self.qd, self.kbd, self.v, self.vbd, self.g, self.rt, self.m2,
         self.sall, self.st, rs0, rs1, rp0, rp1) = scratch
        self.rs, self.rp = (rs0, rs1), (rp0, rp1)
        self.gn_ref, self.mix = gn_ref, mix_scr

    def prepare(self, proj, decay_ref, seq_chunks):
        _retention_tables(decay_ref, self.rt, self.m2)
        lane_lo = _lane_iota((TOK, LANES)) < 64
        per_chunk = lambda a: a.reshape(N_CHUNK, RET_CHUNK, LANES)
        qkvg = proj(C_QB, C_QC)
        q = qkvg[:, :C_KB - C_QB]
        k = qkvg[:, C_KB - C_QB:C_VB - C_QB] * RET_K_SCALE
        v = qkvg[:, C_VB - C_QB:C_GB - C_QB]
        self.g[...] = qkvg[:, C_GB - C_QB:]
        self.q[...] = q.astype(BF16)
        self.v[...] = v.astype(BF16)
        for j in range(2):
            cols = slice(j * LANES, (j + 1) * LANES)
            q3 = per_chunk(q[:, cols])
            for d in range(2):
                self.qd[:, (2 * j + d) * LANES:(2 * j + d + 1) * LANES] = (
                    (q3 * self.rt[j, d]).reshape(TOK, LANES).astype(BF16))
            for src, dst in ((k[:, cols], self.kbd), (v[:, cols], self.vbd)):
                dst[:, j, :RET_CHUNK, :] = per_chunk(jnp.where(lane_lo, src, 0.0).astype(BF16))
                dst[:, j, RET_CHUNK:, :] = per_chunk(jnp.where(lane_lo, 0.0, src).astype(BF16))

        pl.run_scoped(
            functools.partial(self._states, k, seq_chunks),
            pltpu.VMEM((TOK, 2 * LANES), F32),
            pltpu.VMEM((N_CHUNK, 2, 2 * RET_CHUNK, LANES), F32))

    def _states(self, k, seq_chunks, k_scr, upd_scr):
        k_scr[...] = k
        blockdiag2 = _lane_iota((2 * RET_CHUNK, LANES)) < 64
        blockdiag2 = blockdiag2 == ((_row_iota((2 * RET_CHUNK, LANES)) & (RET_CHUNK - 1)) < 64)

        def upd_body(c, carry):
            rows = _ds(c * RET_CHUNK, RET_CHUNK)
            for j in range(2):
                cols = slice(j * LANES, (j + 1) * LANES)
                k2 = k_scr[rows, cols]
                kd = jnp.concatenate([k2 * self.rt[j, 2], k2 * self.rt[j, 3]], axis=1)
                upd = _dot(kd.T.astype(BF16), self.v[rows, cols])
                upd_scr[c, j] = jnp.where(blockdiag2, upd, 0.0)
            return carry

        lax.fori_loop(0, N_CHUNK, upd_body, 0, unroll=4)

        n_per_seq = seq_chunks[0][1]
        assert all(n == n_per_seq for _, n in seq_chunks)

        def scan_body(t, carry):
            for si, (first, n) in enumerate(seq_chunks):
                cf, cb = first + t, first + n - 1 - t
                for j in range(2):
                    sf, sb = self.st[si, 0, j], self.st[si, 1, j]
                    self.sall[cf, j, :RET_CHUNK, :] = sf.astype(BF16)
                    self.sall[cb, j, RET_CHUNK:, :] = sb.astype(BF16)
                    self.st[si, 0, j] = self.rt[j, 4] * sf + upd_scr[cf, j, :RET_CHUNK, :]
                    self.st[si, 1, j] = self.rt[j, 5] * sb + upd_scr[cb, j, RET_CHUNK:, :]
            return carry

        lax.fori_loop(0, n_per_seq, scan_body, 0)

    def scores(self, chunks, b):
        for i, c in enumerate(chunks):
            rows = _ds(c * RET_CHUNK, RET_CHUNK)
            for j in range(2):
                self.rs[b][2 * i + j] = _dot_nt(self.q[rows, j * LANES:(j + 1) * LANES], self.kbd[c, j])

    def mask(self, chunks, b):
        for i in range(len(chunks)):
            for j in range(2):
                self.rp[b][2 * i + j] = (self.rs[b][2 * i + j] * self.m2[j]).astype(BF16)

    def values(self, chunks, b):
        lane_lo = _lane_iota((RET_CHUNK, LANES)) < 64
        for i, c in enumerate(chunks):
            rows = _ds(c * RET_CHUNK, RET_CHUNK)
            for j in range(2):
                cols = slice(j * LANES, (j + 1) * LANES)
                o = (_dot(self.rp[b][2 * i + j], self.vbd[c, j])
                     + _dot(self.qd[rows, 2 * j * LANES:2 * (j + 1) * LANES], self.sall[c, j]))
                s_lo = jnp.sum(jnp.where(lane_lo, o, 0.0), axis=-1, keepdims=True)
                s_hi = jnp.sum(jnp.where(lane_lo, 0.0, o), axis=-1, keepdims=True)
                d = o - jnp.where(lane_lo, s_lo, s_hi) * (1.0 / HEAD_DIM)
                dd = d * d
                v_lo = jnp.sum(jnp.where(lane_lo, dd, 0.0), axis=-1, keepdims=True)
                v_hi = jnp.sum(jnp.where(lane_lo, 0.0, dd), axis=-1, keepdims=True)
                var = jnp.where(lane_lo, v_lo, v_hi) * (1.0 / HEAD_DIM)
                g2 = self.g[rows, cols]
                y = d * lax.rsqrt(var + EPS) * self.gn_ref[:, cols] * (g2 * jax.nn.sigmoid(g2))
                self.mix[rows, M_B + j * LANES:M_B + (j + 1) * LANES] = y.astype(BF16)


X_ROWS = 256
N_XCHUNK = TOK // X_ROWS


class _XStream:
    SCRATCH = [pltpu.VMEM((N_XCHUNK, X_ROWS, D_MODEL), F32), pltpu.VMEM((2, X_ROWS, D_MODEL), F32),
               pltpu.SemaphoreType.DMA((N_XCHUNK,)), pltpu.SemaphoreType.DMA((2,))]

    def __init__(self, x_hbm, xo_hbm, scratch):
        self.x_hbm, self.xo_hbm = x_hbm, xo_hbm
        self.xin, self.xout, self.sem_in, self.sem_out = scratch
        self.step, self.n_steps = pl.program_id(0), pl.num_programs(0)

    @staticmethod
    def _rows(step, r):
        return pl.ds(pl.multiple_of(step * TOK + r * X_ROWS, X_ROWS), X_ROWS)

    def load(self, step, r):
        return pltpu.make_async_copy(self.x_hbm.at[self._rows(step, r), :], self.xin.at[r],
                                     self.sem_in.at[r])

    def store(self, r):
        return pltpu.make_async_copy(self.xout.at[r % 2], self.xo_hbm.at[self._rows(self.step, r), :],
                                     self.sem_out.at[r % 2])

    def prologue(self, mod_ref, n1_ref, h_scr):
        @pl.when(self.step == 0)
        def _():
            for r in range(N_XCHUNK):
                self.load(self.step, r).start()

        for r in range(N_XCHUNK):
            self.load(self.step, r).wait()
            h = _norm_mod(self.xin[r], n1_ref[...], mod_ref[1:2, :], mod_ref[0:1, :])
            h_scr[r * X_ROWS:(r + 1) * X_ROWS, :] = h.astype(BF16)

    def epilogue(self, mod_ref, wout_ref, mix_scr):
        for r in range(N_XCHUNK):
            if r >= 2:
                self.store(r - 2).wait()
            else:
                @pl.when(self.step > 0)
                def _():
                    self.store(r).wait()

            y = _dot(mix_scr[r * X_ROWS:(r + 1) * X_ROWS, :], wout_ref[...])
            self.xout[r % 2] = self.xin[r] + mod_ref[2:3, :] * y
            self.store(r).start()

            @pl.when(self.step + 1 < self.n_steps)
            def _():
                self.load(self.step + 1, r).start()

        @pl.when(self.step + 1 == self.n_steps)
        def _():
            for r in range(N_XCHUNK - 2, N_XCHUNK):
                self.store(r).wait()


FF_PIECES = D_FF // D_MODEL


class _WeightCast:
    SCRATCH = [pltpu.VMEM((D_MODEL, D_MODEL), F32), pltpu.VMEM((D_MODEL, D_MODEL), BF16),
               pltpu.SemaphoreType.DMA(()), pltpu.SemaphoreType.DMA(())]

    def __init__(self, layer, wup_hbm, wdn_hbm, oup_hbm, odn_hbm, scratch):
        self.layer, self.srcs, self.dsts = layer, (wup_hbm, wdn_hbm), (oup_hbm, odn_hbm)
        self.stage_in, self.stage_out, self.sem_in, self.sem_out = scratch
        self.step, self.n_steps = pl.program_id(0), pl.num_programs(0)

    def _piece(self, p):
        if p < FF_PIECES:
            cols = slice(p * D_MODEL, (p + 1) * D_MODEL)
            return self.srcs[0].at[self.layer, :, cols], self.dsts[0].at[:, cols]
        rows = slice((p - FF_PIECES) * D_MODEL, (p - FF_PIECES + 1) * D_MODEL)
        return self.srcs[1].at[self.layer, rows, :], self.dsts[1].at[rows, :]

    def _load(self, p):
        return pltpu.make_async_copy(self._piece(p)[0], self.stage_in, self.sem_in)

    def _store(self, p):
        return pltpu.make_async_copy(self.stage_out, self._piece(p)[1], self.sem_out)

    def fetch(self):
        for p in range(2 * FF_PIECES):
            @pl.when(self.step == p)
            def _():
                self._load(p).start()

    def convert(self):
        self._load(0).wait()

        @pl.when(self.step > 0)
        def _():
            self._store(0).wait()

        self.stage_out[...] = self.stage_in[...].astype(BF16)
        for p in range(2 * FF_PIECES):
            @pl.when(self.step == p)
            def _():
                self._store(p).start()

        @pl.when(self.step + 1 == self.n_steps)
        def _():
            self._store(0).wait()


CTX_PER_STEP = TOK // SEQ
CTX_WROWS = WIN_HEADS * SEQ
CTX_MROWS = MLA_HEADS * SEQ


def _ctx_mixer_kernel(n_alias, x_ref, mod_ref, n1_ref, win_ref, sink_ref, decay_ref, gn_ref, kvn_ref,
                      wkvb_ref, wout_ref, *refs):
    xo_ref, ko_ref, vo_ref, ckvo_ref, kro_ref, so_ref, h_scr, mix_scr = refs[n_alias:n_alias + 8]
    xs = _XStream(x_ref, xo_ref, refs[n_alias + 8:])
    xs.prologue(mod_ref, n1_ref, h_scr)
    per_seq = lambda a: a.reshape(CTX_PER_STEP, SEQ, a.shape[-1])
    if n_alias == 0:
        for ref in (ko_ref, vo_ref, ckvo_ref, kro_ref, so_ref):
            ref[:, 1:] = jnp.zeros((ref.shape[0], ref.shape[1] - 1) + ref.shape[2:], F32)
        ko_ref, vo_ref, ckvo_ref, kro_ref, so_ref = (
            ref.at[:, 0] for ref in (ko_ref, vo_ref, ckvo_ref, kro_ref, so_ref))
    chunks_per_seq = SEQ // RET_CHUNK

    def proj(c0, c1):
        return _dot_nt(h_scr[...], win_ref[c0:c1, :])

    ret = []
    seq_chunks = lambda e: [chunks_per_seq * e + c for c in range(chunks_per_seq)]

    def attention(qst_scr, k_scr, vaug_scr, qc_scr, kcat_scr, kvaug_scr,
                  sw0, sw1, pw0, pw1, ew0, ew1, sm0, sm1, pm0, pm1):
        lane_lo = _lane_iota((TOK, LANES)) < 64
        ones = jnp.ones((TOK, LANES), BF16)
        qkv = proj(C_QA, C_QB)
        for g in range(4):
            q = qkv[:, g * LANES:(g + 1) * LANES] * ATTN_SCALE
            lo = jnp.where(lane_lo, q, 0.0).astype(BF16).reshape(CTX_PER_STEP, SEQ, LANES)
            hi = jnp.where(lane_lo, 0.0, q).astype(BF16).reshape(CTX_PER_STEP, SEQ, LANES)
            qst_scr[:, g * SEQ:(g + 1) * SEQ, :] = lo
            qst_scr[:, (4 + g) * SEQ:(5 + g) * SEQ, :] = hi
        kva = qkv[:, C_KA:C_QB]
        ko_ref[...] = per_seq(kva[:, :LANES])
        vo_ref[...] = per_seq(kva[:, LANES:])
        k_scr[...] = kva[:, :LANES].astype(BF16)
        vaug_scr[:, :LANES] = kva[:, LANES:].astype(BF16)
        vaug_scr[:, LANES:] = ones
        qck = proj(C_QC, D_IN_P)
        for h in range(MLA_HEADS):
            qc_scr[h] = qck[:, h * LANES:(h + 1) * LANES].astype(BF16)
        ckv, kr = qck[:, C_CKV - C_QC:C_KR - C_QC], qck[:, C_KR - C_QC:]
        ckv_n = ckv * lax.rsqrt(jnp.mean(ckv * ckv, axis=-1, keepdims=True) + EPS) * kvn_ref[...]
        ckvo_ref[...] = per_seq(ckv_n)
        kro_ref[...] = per_seq(kr[:, 64:64 + MLA_ROPE])
        kv = _dot(ckv_n.astype(BF16), wkvb_ref[...])
        for h in range(MLA_HEADS):
            kvh = kv[:, h * LANES:(h + 1) * LANES]
            kcat_scr[h] = jnp.where(lane_lo, kvh, kr).astype(BF16)
            kvaug_scr[h, :, :LANES] = kvh.astype(BF16)
            kvaug_scr[h, :, LANES:] = ones

        sw, pw, ew, sm, pm = (sw0, sw1), (pw0, pw1), (ew0, ew1), (sm0, sm1), (pm0, pm1)
        lane_lo_s = _lane_iota((SEQ, LANES)) < 64

        def scores(e, b):
            rows = _ds(e * SEQ, SEQ)
            sw[b][...] = _dot_nt(qst_scr[e], k_scr[rows, :])
            for h in range(MLA_HEADS):
                sm[b][h * SEQ:(h + 1) * SEQ, :] = _dot_nt(qc_scr[h, rows, :], kcat_scr[h, rows, :])
            ret[0].scores(seq_chunks(e), b)

        def softmax(e, b):
            for h in range(WIN_HEADS):
                hr = slice(h * SEQ, (h + 1) * SEQ)
                sink = jnp.full((SEQ, 1), sink_ref[h], F32)
                (p,), extra = _softmax_tile([sw[b][hr, :]], floor=sink)
                pw[b][hr, :] = p
                ew[b][hr, :] = jnp.broadcast_to(extra, (SEQ, LANES))
            for h in range(MLA_HEADS):
                hr = slice(h * SEQ, (h + 1) * SEQ)
                (p,), _ = _softmax_tile([sm[b][hr, :]], scale=MLA_SCALE)
                pm[b][hr, :] = p
            ret[0].mask(seq_chunks(e), b)

        def values(e, b):
            ret[0].values(seq_chunks(e), b)
            rows = _ds(e * SEQ, SEQ)
            oa = _dot(pw[b][...], vaug_scr[rows, :])
            o = oa[:, :LANES] / (oa[:, LANES:] + ew[b][...])
            for g in range(4):
                merged = jnp.where(lane_lo_s, o[g * SEQ:(g + 1) * SEQ], o[(4 + g) * SEQ:(5 + g) * SEQ])
                mix_scr[rows, M_A + g * LANES:M_A + (g + 1) * LANES] = merged.astype(BF16)
            for jp in range(2):
                outs = []
                for h in (2 * jp, 2 * jp + 1):
                    oc = _dot(pm[b][h * SEQ:(h + 1) * SEQ, :], kvaug_scr[h, rows, :])
                    outs.append(oc[:, :LANES] / oc[:, LANES:])
                merged = jnp.where(lane_lo_s, pltpu.roll(outs[0], 64, 1), outs[1])
                mix_scr[rows, M_C + jp * LANES:M_C + (jp + 1) * LANES] = merged.astype(BF16)

        _pipeline(CTX_PER_STEP, scores, softmax, values)

    attention_scratch = (
        pltpu.VMEM((CTX_PER_STEP, CTX_WROWS, LANES), BF16),
        pltpu.VMEM((TOK, LANES), BF16),
        pltpu.VMEM((TOK, 2 * LANES), BF16),
        pltpu.VMEM((MLA_HEADS, TOK, LANES), BF16),
        pltpu.VMEM((MLA_HEADS, TOK, LANES), BF16),
        pltpu.VMEM((MLA_HEADS, TOK, 2 * LANES), BF16),
        pltpu.VMEM((CTX_WROWS, SEQ), F32), pltpu.VMEM((CTX_WROWS, SEQ), F32),
        pltpu.VMEM((CTX_WROWS, SEQ), BF16), pltpu.VMEM((CTX_WROWS, SEQ), BF16),
        pltpu.VMEM((CTX_WROWS, LANES), F32), pltpu.VMEM((CTX_WROWS, LANES), F32),
        pltpu.VMEM((CTX_MROWS, SEQ), F32), pltpu.VMEM((CTX_MROWS, SEQ), F32),
        pltpu.VMEM((CTX_MROWS, SEQ), BF16), pltpu.VMEM((CTX_MROWS, SEQ), BF16),
    )

    def retention(*scratch):
        r = _Retention(scratch, gn_ref, mix_scr)
        r.st[...] = jnp.zeros(r.st.shape, F32)
        r.prepare(proj, decay_ref, [(chunks_per_seq * e, chunks_per_seq) for e in range(CTX_PER_STEP)])
        for e in range(CTX_PER_STEP):
            for dirn in range(2):
                for j in range(2):
                    st = r.st[e, dirn, j]
                    so_ref[e, dirn, 2 * j] = st[:64, :64]
                    so_ref[e, dirn, 2 * j + 1] = st[64:, 64:]
        ret.append(r)
        pl.run_scoped(attention, *attention_scratch)

    pl.run_scoped(retention, *_ret_scratch(CTX_PER_STEP, chunks_per_seq))
    xs.epilogue(mod_ref, wout_ref, mix_scr)


def _const_spec(shape, layer=None):
    if layer is None:
        return pl.BlockSpec(shape, lambda i: (0,) * len(shape), pipeline_mode=pl.Buffered(1))
    return pl.BlockSpec((None,) + shape, lambda i: (layer,) + (0,) * len(shape),
                        pipeline_mode=pl.Buffered(1))


def _smem_spec():
    return pl.BlockSpec(memory_space=pltpu.SMEM)


def _ctx_mixer(layer, x, mod, n1, w_in_p, sink, decay, gn, kvn, w_kv_b, w_out_p, prev_state):
    n_tok = x.shape[0]
    n_seq = n_tok // SEQ
    hbm_spec = pl.BlockSpec(memory_space=pl.ANY)
    if prev_state:
        state_spec = lambda *tail: pl.BlockSpec(
            (CTX_PER_STEP, None) + tail, lambda i: (i, layer) + (0,) * len(tail))
    else:
        assert layer == 0
        state_spec = lambda *tail: pl.BlockSpec(
            (CTX_PER_STEP, DEPTH) + tail, lambda i: (i, 0) + (0,) * len(tail))
    state_tails = [(SEQ, LANES), (SEQ, LANES), (SEQ, MLA_KV_RANK), (SEQ, MLA_ROPE),
                   (2, RET_HEADS, HEAD_DIM, HEAD_DIM)]
    n_in = 10
    return pl.pallas_call(
        functools.partial(_ctx_mixer_kernel, len(prev_state)),
        out_shape=[jax.ShapeDtypeStruct((n_tok, D_MODEL), F32)] + [
            jax.ShapeDtypeStruct((n_seq, DEPTH) + tail, F32) for tail in state_tails],
        grid=(n_tok // TOK,),
        in_specs=[
            hbm_spec,
            _const_spec((N_MOD, D_MODEL)),
            _const_spec((1, D_MODEL)),
            _const_spec((D_IN_P, D_MODEL), layer),
            _smem_spec(),
            _smem_spec(),
            _const_spec((1, RET_HEADS * HEAD_DIM)),
            _const_spec((1, MLA_KV_RANK)),
            _const_spec((MLA_KV_RANK, MLA_HEADS * LANES), layer),
            _const_spec((D_MODEL, D_MODEL), layer),
        ] + [hbm_spec] * len(prev_state),
        out_specs=[hbm_spec] + [state_spec(*tail) for tail in state_tails],
        input_output_aliases={n_in + k: 1 + k for k in range(len(prev_state))},
        scratch_shapes=[
            pltpu.VMEM((TOK, D_MODEL), BF16),
            pltpu.VMEM((TOK, D_MODEL), BF16),
        ] + _XStream.SCRATCH,
        compiler_params=pltpu.CompilerParams(
            dimension_semantics=("arbitrary",), vmem_limit_bytes=VMEM_LIMIT),
        name="ctx_mixer",
    )(x, mod, n1, w_in_p, sink, decay, gn, kvn, w_kv_b, w_out_p, *prev_state)


N_BLK = DEC_SEQ // LANES
KEYS_LOC = 3 * LANES
KEYS_WIN = KEYS_LOC + PAST_LEN
WIN_ROWS = WIN_HEADS * LANES
MLA_QB = 256
MLA_KEYS = DEC_SEQ + PAST_LEN
MLA_HALF = MLA_KEYS // 2


def _lat_mixer_kernel(layer, x_ref, mod_ref, n1_ref, win_ref, sink_ref, decay_ref, gn_ref, kvn_ref,
                      wkvb_ref, wout_ref, ck_ref, cv_ref, cckv_ref, ckr_ref, s0_ref,
                      rc_ref, rsa_ref, rsb_ref, mc_ref, msa_ref, msb_ref, wup_ref, wdn_ref,
                      xo_ref, wup16_ref, wdn16_ref,
                      h_scr, mix_scr, s0_scr, s1_scr, p0_scr, p1_scr, *dma_scratch):
    xs = _XStream(x_ref, xo_ref, dma_scratch[:len(_XStream.SCRATCH)])
    wc = _WeightCast(layer, wup_ref, wdn_ref, wup16_ref, wdn16_ref,
                     dma_scratch[len(_XStream.SCRATCH):])
    xs.prologue(mod_ref, n1_ref, h_scr)
    wc.fetch()
    sbuf, pbuf = (s0_scr, s1_scr), (p0_scr, p1_scr)

    def proj(c0, c1):
        return _dot_nt(h_scr[...], win_ref[c0:c1, :])

    ret = []

    def window(qst_scr, kpad_scr, vaug_scr, ckb_scr, cvaug_scr, bias_scr, e0_scr, e1_scr):
        ebuf = (e0_scr, e1_scr)
        lane_lo = _lane_iota((TOK, LANES)) < 64
        rc, rsa, rsb = rc_ref[...], rsa_ref[...], rsb_ref[...]
        qkv = proj(C_QA, C_QB)
        for g in range(4):
            q = _rope(qkv[:, g * LANES:(g + 1) * LANES], rc, rsa, rsb, 16) * ATTN_SCALE
            lo = jnp.where(lane_lo, q, 0.0).astype(BF16).reshape(N_BLK, LANES, LANES)
            hi = jnp.where(lane_lo, 0.0, q).astype(BF16).reshape(N_BLK, LANES, LANES)
            qst_scr[:, g * LANES:(g + 1) * LANES, :] = lo
            qst_scr[:, (4 + g) * LANES:(5 + g) * LANES, :] = hi
        kva = qkv[:, C_KA:C_QB]
        zpad = jnp.zeros((LANES, LANES), BF16)
        kpad_scr[0:LANES, :] = zpad
        kpad_scr[LANES + TOK:, :] = zpad
        vaug_scr[0:LANES, :LANES] = zpad
        vaug_scr[LANES + TOK:, :LANES] = zpad
        kpad_scr[LANES:LANES + TOK, :] = _rope(kva[:, :LANES], rc, rsa, rsb, 16).astype(BF16)
        vaug_scr[LANES:LANES + TOK, :LANES] = kva[:, LANES:].astype(BF16)
        vaug_scr[:, LANES:] = jnp.ones((TOK + 2 * LANES, LANES), BF16)
        ckb_scr[...] = ck_ref[...].astype(BF16)
        cvaug_scr[:, :LANES] = cv_ref[...].astype(BF16)
        cvaug_scr[:, LANES:] = jnp.ones((PAST_LEN, LANES), BF16)
        qi, kj = _row_iota((LANES, LANES)), _lane_iota((LANES, LANES))
        bias_scr[0] = jnp.full((LANES, LANES), -jnp.inf, F32)
        bias_scr[1] = jnp.where(kj >= qi, 0.0, -jnp.inf)
        bias_scr[2] = jnp.where(kj <= qi, 0.0, -jnp.inf)
        lane_lo_b = _lane_iota((LANES, LANES)) < 64

        def scores(n, b):
            q = qst_scr[n]
            sbuf[b][:, :KEYS_LOC] = _dot_nt(q, kpad_scr[_ds(n * LANES, KEYS_LOC), :])
            sbuf[b][:, KEYS_LOC:] = _dot_nt(q, ckb_scr[...])
            ret[0].scores([n], b)

        def softmax(n, b):
            if isinstance(n, int):
                i_prev, i_next = (1 if n > 0 else 0), (2 if n < N_BLK - 1 else 0)
            else:
                i_prev, i_next = jnp.where(n > 0, 1, 0), jnp.where(n < N_BLK - 1, 2, 0)
            b_prev, b_next = bias_scr[i_prev], bias_scr[i_next]
            for h in range(WIN_HEADS):
                hr = slice(h * LANES, (h + 1) * LANES)
                parts = [sbuf[b][hr, 0:LANES] + b_prev, sbuf[b][hr, LANES:2 * LANES],
                         sbuf[b][hr, 2 * LANES:KEYS_LOC] + b_next, sbuf[b][hr, KEYS_LOC:]]
                ps, extra = _softmax_tile(parts, floor=jnp.full((LANES, 1), sink_ref[h], F32))
                pbuf[b][hr, 0:LANES] = ps[0]
                pbuf[b][hr, LANES:2 * LANES] = ps[1]
                pbuf[b][hr, 2 * LANES:KEYS_LOC] = ps[2]
                pbuf[b][hr, KEYS_LOC:] = ps[3]
                ebuf[b][hr, :] = jnp.broadcast_to(extra, (LANES, LANES))
            ret[0].mask([n], b)

        def values(n, b):
            ret[0].values([n], b)
            oa = (_dot(pbuf[b][:, :KEYS_LOC], vaug_scr[_ds(n * LANES, KEYS_LOC), :])
                  + _dot(pbuf[b][:, KEYS_LOC:], cvaug_scr[...]))
            o = oa[:, :LANES] / (oa[:, LANES:] + ebuf[b][...])
            for g in range(4):
                merged = jnp.where(lane_lo_b, o[g * LANES:(g + 1) * LANES],
                                   o[(4 + g) * LANES:(5 + g) * LANES])
                mix_scr[_ds(n * LANES, LANES), M_A + g * LANES:M_A + (g + 1) * LANES] = (
                    merged.astype(BF16))

        _pipeline(N_BLK, scores, softmax, values)

    window_scratch = (
        pltpu.VMEM((N_BLK, WIN_ROWS, LANES), BF16),
        pltpu.VMEM((TOK + 2 * LANES, LANES), BF16),
        pltpu.VMEM((TOK + 2 * LANES, 2 * LANES), BF16),
        pltpu.VMEM((PAST_LEN, LANES), BF16),
        pltpu.VMEM((PAST_LEN, 2 * LANES), BF16),
        pltpu.VMEM((3, LANES, LANES), F32),
        pltpu.VMEM((WIN_ROWS, LANES), F32), pltpu.VMEM((WIN_ROWS, LANES), F32),
    )

    def retention(*scratch):
        r = _Retention(scratch, gn_ref, mix_scr)
        r.st[0] = s0_ref[...]
        r.prepare(proj, decay_ref, [(0, N_CHUNK)])
        ret.append(r)
        pl.run_scoped(window, *window_scratch)

    assert N_CHUNK == N_BLK
    pl.run_scoped(retention, *_ret_scratch(1, 1))

    def latent(qc_scr, kcat_scr, kvaug_scr, mixc_scr):
        lane_lo = _lane_iota((TOK, LANES)) < 64
        mc, msa, msb = mc_ref[...], msa_ref[...], msb_ref[...]
        qck = proj(C_QC, D_IN_P)
        for h in range(MLA_HEADS):
            qc_scr[h] = _rope(qck[:, h * LANES:(h + 1) * LANES], mc, msa, msb, 8).astype(BF16)
        ckv = qck[:, C_CKV - C_QC:C_KR - C_QC]
        kr = _rope(qck[:, C_KR - C_QC:], mc, msa, msb, 8)
        ckv_n = ckv * lax.rsqrt(jnp.mean(ckv * ckv, axis=-1, keepdims=True) + EPS) * kvn_ref[...]
        kv = _dot(ckv_n.astype(BF16), wkvb_ref[...])
        kv_c = _dot(cckv_ref[...].astype(BF16), wkvb_ref[...])
        kr_c = ckr_ref[...]
        lane_lo_c = _lane_iota((PAST_LEN, LANES)) < 64
        for h in range(MLA_HEADS):
            kvh, kvh_c = kv[:, h * LANES:(h + 1) * LANES], kv_c[:, h * LANES:(h + 1) * LANES]
            kcat_scr[h, 0:TOK, :] = jnp.where(lane_lo, kvh, kr).astype(BF16)
            kcat_scr[h, TOK:, :] = jnp.where(lane_lo_c, kvh_c, kr_c).astype(BF16)
            kvaug_scr[h, 0:TOK, :LANES] = kvh.astype(BF16)
            kvaug_scr[h, TOK:, :LANES] = kvh_c.astype(BF16)
            kvaug_scr[h, :, LANES:] = jnp.ones((MLA_KEYS, LANES), BF16)
        lane_lo_m = _lane_iota((MLA_QB, LANES)) < 64
        n_qb = DEC_SEQ // MLA_QB

        def split(t):
            if isinstance(t, int):
                return t // n_qb, t % n_qb
            return lax.shift_right_logical(t, 2), lax.bitwise_and(t, n_qb - 1)

        def scores(t, b):
            jp, qb = split(t)
            for i in range(2):
                h = 2 * jp + i
                q = qc_scr[h, _ds(qb * MLA_QB, MLA_QB), :]
                for part in range(2):
                    r0 = (2 * i + part) * MLA_QB
                    sbuf[b][r0:r0 + MLA_QB, :] = _dot_nt(
                        q, kcat_scr[h, part * MLA_HALF:(part + 1) * MLA_HALF, :])

        def softmax(t, b):
            for i in range(2):
                for rt in range(MLA_QB // LANES):
                    ra = 2 * i * MLA_QB + rt * LANES
                    rb = ra + MLA_QB
                    ps, _ = _softmax_tile([sbuf[b][ra:ra + LANES, :], sbuf[b][rb:rb + LANES, :]],
                                          scale=MLA_SCALE)
                    pbuf[b][ra:ra + LANES, :] = ps[0]
                    pbuf[b][rb:rb + LANES, :] = ps[1]

        def values(t, b):
            jp, qb = split(t)
            outs = []
            for i in range(2):
                h = 2 * jp + i
                r0 = 2 * i * MLA_QB
                oc = (_dot(pbuf[b][r0:r0 + MLA_QB, :], kvaug_scr[h, 0:MLA_HALF, :])
                      + _dot(pbuf[b][r0 + MLA_QB:r0 + 2 * MLA_QB, :], kvaug_scr[h, MLA_HALF:, :]))
                outs.append(oc[:, :LANES] / oc[:, LANES:])
            merged = jnp.where(lane_lo_m, pltpu.roll(outs[0], 64, 1), outs[1])
            mixc_scr[jp, _ds(qb * MLA_QB, MLA_QB), :] = merged.astype(BF16)

        _pipeline(2 * n_qb, scores, softmax, values)
        for jp in range(2):
            mix_scr[:, M_C + jp * LANES:M_C + (jp + 1) * LANES] = mixc_scr[jp]

    pl.run_scoped(
        latent,
        pltpu.VMEM((MLA_HEADS, TOK, LANES), BF16),
        pltpu.VMEM((MLA_HEADS, MLA_KEYS, LANES), BF16),
        pltpu.VMEM((MLA_HEADS, MLA_KEYS, 2 * LANES), BF16),
        pltpu.VMEM((2, TOK, LANES), BF16),
    )
    wc.convert()
    xs.epilogue(mod_ref, wout_ref, mix_scr)


def _lat_mixer(layer, x, mod, n1, w_in_p, sink, decay, gn, kvn, w_kv_b, w_out_p,
               ck, cv, cckv, ckr, s0, rope_a, rope_m, w_up, w_down):
    n_tok = x.shape[0]
    n_seq = n_tok // DEC_SEQ
    assert n_seq == 2 * FF_PIECES
    hbm_spec = pl.BlockSpec(memory_space=pl.ANY)
    seq_spec = lambda shape: pl.BlockSpec(
        (None, None) + shape, lambda i: (i, layer) + (0,) * len(shape))
    assert WIN_ROWS == 4 * MLA_QB and KEYS_WIN == MLA_HALF
    return pl.pallas_call(
        functools.partial(_lat_mixer_kernel, layer),
        out_shape=(jax.ShapeDtypeStruct((n_tok, D_MODEL), F32),
                   jax.ShapeDtypeStruct((D_MODEL, D_FF), BF16),
                   jax.ShapeDtypeStruct((D_FF, D_MODEL), BF16)),
        grid=(n_seq,),
        in_specs=[
            hbm_spec,
            pl.BlockSpec((None, N_MOD, D_MODEL), lambda i: (i, 0, 0)),
            _const_spec((1, D_MODEL)),
            _const_spec((D_IN_P, D_MODEL), layer),
            _smem_spec(),
            _smem_spec(),
            _const_spec((1, RET_HEADS * HEAD_DIM)),
            _const_spec((1, MLA_KV_RANK)),
            _const_spec((MLA_KV_RANK, MLA_HEADS * LANES), layer),
            _const_spec((D_MODEL, D_MODEL), layer),
            seq_spec((PAST_LEN, LANES)),
            seq_spec((PAST_LEN, LANES)),
            seq_spec((PAST_LEN, MLA_KV_RANK)),
            seq_spec((PAST_LEN, LANES)),
            seq_spec((2, 2, LANES, LANES)),
        ] + [_const_spec((DEC_SEQ, LANES))] * 6 + [hbm_spec, hbm_spec],
        out_specs=(hbm_spec, hbm_spec, hbm_spec),
        scratch_shapes=[
            pltpu.VMEM((TOK, D_MODEL), BF16),
            pltpu.VMEM((TOK, D_MODEL), BF16),
            pltpu.VMEM((WIN_ROWS, KEYS_WIN), F32), pltpu.VMEM((WIN_ROWS, KEYS_WIN), F32),
            pltpu.VMEM((WIN_ROWS, KEYS_WIN), BF16), pltpu.VMEM((WIN_ROWS, KEYS_WIN), BF16),
        ] + _XStream.SCRATCH + _WeightCast.SCRATCH,
        compiler_params=pltpu.CompilerParams(
            dimension_semantics=("arbitrary",), vmem_limit_bytes=VMEM_LIMIT),
        name="lat_mixer",
    )(x, mod, n1, w_in_p, sink, decay, gn, kvn, w_kv_b, w_out_p, ck, cv, cckv, ckr, s0,
      *rope_a, *rope_m, w_up, w_down)


def _mlp_kernel(final, x_ref, mod_ref, n2_ref, wup_ref, wdn_ref, fn_ref, o_ref, wup_scr, wdn_scr, sem):
    n_chunks = D_FF // FF_CHUNK
    chunk = lambda c: slice(c * FF_CHUNK, (c + 1) * FF_CHUNK)
    up_copy = lambda c: pltpu.make_async_copy(wup_ref.at[:, chunk(c)], wup_scr.at[:, chunk(c)], sem.at[c])
    dn_copy = lambda c: pltpu.make_async_copy(wdn_ref.at[chunk(c), :], wdn_scr.at[chunk(c), :],
                                              sem.at[n_chunks + c])

    def body(first):
        if first:
            for c in range(n_chunks):
                up_copy(c).start()
                dn_copy(c).start()
        x = x_ref[...]
        h2 = _norm_mod(x, n2_ref[...], mod_ref[4:5, :], mod_ref[3:4, :]).astype(BF16)
        acc = None
        for c in range(n_chunks):
            if first:
                up_copy(c).wait()
            u = jnp.maximum(_dot(h2, wup_scr[:, chunk(c)]), 0.0)
            if first:
                dn_copy(c).wait()
            part = _dot((u * u).astype(BF16), wdn_scr[chunk(c), :])
            acc = part if acc is None else acc + part
        y = x + mod_ref[5:6, :] * acc
        if final:
            y = y * lax.rsqrt(jnp.mean(y * y, axis=-1, keepdims=True) + EPS) * fn_ref[...]
        o_ref[...] = y

    pl.when(pl.program_id(0) == 0)(functools.partial(body, True))
    pl.when(pl.program_id(0) > 0)(functools.partial(body, False))


def _mlp(x, mod, n2, w_up, w_down, final_norm, final):
    n_tok = x.shape[0]
    per_mod = n_tok // mod.shape[0] // MLP_ROWS
    hbm_spec = pl.BlockSpec(memory_space=pl.ANY)
    return pl.pallas_call(
        functools.partial(_mlp_kernel, final),
        out_shape=jax.ShapeDtypeStruct((n_tok, D_MODEL), F32),
        grid=(n_tok // MLP_ROWS,),
        in_specs=[
            pl.BlockSpec((MLP_ROWS, D_MODEL), lambda i: (i, 0)),
            pl.BlockSpec((None, N_MOD, D_MODEL), lambda i: (i // per_mod, 0, 0)),
            _const_spec((1, D_MODEL)),
            hbm_spec,
            hbm_spec,
            _const_spec((1, D_MODEL)),
        ],
        out_specs=pl.BlockSpec((MLP_ROWS, D_MODEL), lambda i: (i, 0)),
        scratch_shapes=[
            pltpu.VMEM((D_MODEL, D_FF), BF16),
            pltpu.VMEM((D_FF, D_MODEL), BF16),
            pltpu.SemaphoreType.DMA((2 * (D_FF // FF_CHUNK),)),
        ],
        compiler_params=pltpu.CompilerParams(
            dimension_semantics=("arbitrary",), vmem_limit_bytes=VMEM_LIMIT),
        name="mlp",
    )(x, mod, n2, w_up, w_down, final_norm)


def kernel(x_prompt, x_sample, cache_win_k, cache_win_v, cache_mla_ckv, cache_mla_krope, state_ret,
           c, c_ctx, w_mod, b_mod, norm1, norm2, w_in, win_sink, ret_decay, ret_gn, mla_kv_norm,
           w_kv_b, w_out, w_up, w_down, final_norm):
    n_ctx, n_lat = x_prompt.shape[0], x_sample.shape[0]

    w_in_p, w_out_p = _proj_layout(jnp.swapaxes(w_in, 1, 2), w_out)
    w_kv_b16 = w_kv_b.astype(BF16)

    c_rows = jnp.zeros((16, D_MODEL), F32).at[0].set(c_ctx).at[1:1 + n_lat].set(c)
    mod = _modulation(c_rows, w_mod, b_mod).reshape(DEPTH, 16, N_MOD, D_MODEL)

    rope_a = _rope_tables(DEC_SEQ, HEAD_DIM, 0, HEAD_DIM)
    rope_m = _rope_tables(DEC_SEQ, MLA_ROPE, MLA_NOPE, LANES)

    ck = cache_win_k.reshape(n_lat, DEPTH, PAST_LEN, LANES)
    cv = cache_win_v.reshape(n_lat, DEPTH, PAST_LEN, LANES)
    ckr = jnp.pad(cache_mla_krope, ((0, 0), (0, 0), (0, 0), (MLA_NOPE, LANES - MLA_NOPE - MLA_ROPE)))
    sr = state_ret.reshape(n_lat, DEPTH, 2, 2, 2, HEAD_DIM, HEAD_DIM)
    zero = jnp.zeros_like(sr[:, :, :, :, 0])
    s0 = jnp.concatenate([jnp.concatenate([sr[:, :, :, :, 0], zero], axis=-1),
                          jnp.concatenate([zero, sr[:, :, :, :, 1]], axis=-1)], axis=-2)

    xp = x_prompt.reshape(n_ctx * SEQ, D_MODEL)
    xs = x_sample.reshape(n_lat * DEC_SEQ, D_MODEL)
    state = ()
    for l in range(DEPTH):
        last = l == DEPTH - 1
        shared = (norm1[l][None], w_in_p, win_sink[l], ret_decay[l], ret_gn[l][None],
                  mla_kv_norm[l][None], w_kv_b16, w_out_p)
        mod_ctx, mod_lat = mod[l, 0:1], mod[l, 1:1 + n_lat]
        xp, *state = _ctx_mixer(l, xp, mod_ctx[0], *shared, state)
        xs, w_up16, w_down16 = _lat_mixer(l, xs, mod_lat, *shared, ck, cv, cache_mla_ckv, ckr, s0,
                                          rope_a, rope_m, w_up, w_down)
        xp = _mlp(xp, mod_ctx, norm2[l][None], w_up16, w_down16, final_norm[None], last)
        xs = _mlp(xs, mod_lat, norm2[l][None], w_up16, w_down16, final_norm[None], last)
    new_k, new_v, new_ckv, new_kr, new_s = state
    return (xp.reshape(n_ctx, SEQ, D_MODEL), xs.reshape(n_lat, DEC_SEQ, D_MODEL),
            new_k.reshape(n_ctx, DEPTH, SEQ, 2, HEAD_DIM), new_v.reshape(n_ctx, DEPTH, SEQ, 2, HEAD_DIM),
            new_ckv, new_kr, new_s)
```

```python
import functools

import numpy as np
import jax
import jax.numpy as jnp
from jax import lax
from jax.experimental import pallas as pl
from jax.experimental.pallas import tpu as pltpu

F32 = jnp.float32
BF16 = jnp.bfloat16

D_MODEL = 1024
DEPTH = 2
SEQ = 256
DEC_SEQ = 1024
PAST_LEN = 256
GRID_W = 64
HEAD_DIM = 64
ROPE_BASE = 10000.0
EPS = 1e-6
WIN_HEADS = 8
WINDOW = 128
ATTN_SCALE = HEAD_DIM ** -0.5
RET_HEADS = 4
RET_CHUNK = 128
RET_K_SCALE = HEAD_DIM ** -0.5
MLA_HEADS = 4
MLA_NOPE = 64
MLA_ROPE = 32
MLA_KV_RANK = 128
MLA_QK = MLA_NOPE + MLA_ROPE
MLA_SCALE = MLA_QK ** -0.5
D_IN = 2336
D_FF = 4 * D_MODEL
N_MOD = 6

LANES = 128
TOK = 1024
MLP_ROWS = 1024
FF_CHUNK = 1024
VMEM_LIMIT = 60 * 1024 * 1024

C_QA, C_KA, C_VA, C_QB, C_KB, C_VB, C_GB, C_QC, C_CKV, C_KR, D_IN_P = (
    0, 512, 640, 768, 1024, 1280, 1536, 1792, 2304, 2432, 2560)
M_A, M_B, M_C = 0, 512, 768

NT_DIMS = (((1,), (1,)), ((), ()))


def _in_proj_columns():
    idx = []
    for g in range(4):
        idx += list(range(g * 64, (g + 1) * 64)) + list(range((4 + g) * 64, (5 + g) * 64))
    idx += list(range(512, 1792))
    for h in range(MLA_HEADS):
        idx += list(range(1792 + h * MLA_QK, 1792 + (h + 1) * MLA_QK)) + [-1] * 32
    idx += list(range(2176, 2304))
    idx += [-1] * 64 + list(range(2304, 2336)) + [-1] * 32
    return np.asarray(idx, np.int32)


def _mix_rows():
    idx = []
    for g in range(4):
        idx += list(range(g * 64, (g + 1) * 64)) + list(range((4 + g) * 64, (5 + g) * 64))
    idx += list(range(512, 1024))
    return np.asarray(idx, np.int32)


def _take_runs(w, idx, axis):
    pieces, i = [], 0
    while i < len(idx):
        j = i + 1
        if idx[i] < 0:
            while j < len(idx) and idx[j] < 0:
                j += 1
            shape = list(w.shape)
            shape[axis] = j - i
            pieces.append(jnp.zeros(shape, w.dtype))
        else:
            while j < len(idx) and idx[j] == idx[j - 1] + 1:
                j += 1
            pieces.append(lax.slice_in_dim(w, int(idx[i]), int(idx[j - 1]) + 1, axis=axis))
        i = j
    return jnp.concatenate(pieces, axis=axis)


def _rope_tables(n_tokens, dim, lane0, period):
    quarter = dim // 4
    t = np.arange(n_tokens)
    row = (t // GRID_W).astype(np.float64)
    col = (t % GRID_W).astype(np.float64)
    inv_freq = ROPE_BASE ** (-np.arange(quarter, dtype=np.float64) / quarter)
    ar, ac = row[:, None] * inv_freq, col[:, None] * inv_freq
    cos = np.concatenate([np.cos(ar), np.cos(ar), np.cos(ac), np.cos(ac)], axis=-1)
    sin = np.concatenate([np.sin(ar), np.sin(ar), np.sin(ac), np.sin(ac)], axis=-1)
    first = np.tile(np.concatenate([np.ones(quarter), np.zeros(quarter)]), 2)
    c = np.ones((n_tokens, LANES))
    sa = np.zeros((n_tokens, LANES))
    sb = np.zeros((n_tokens, LANES))
    for start in range(lane0, LANES, period):
        c[:, start:start + dim] = cos
        sa[:, start:start + dim] = -sin * first
        sb[:, start:start + dim] = sin * (1.0 - first)
    return tuple(jnp.asarray(a, F32) for a in (c, sa, sb))


def _lane_iota(shape):
    return lax.broadcasted_iota(jnp.int32, shape, len(shape) - 1)


def _row_iota(shape):
    return lax.broadcasted_iota(jnp.int32, shape, len(shape) - 2)


def _ds(start, size):
    if isinstance(start, int):
        return pl.ds(start, size)
    return pl.ds(pl.multiple_of(start, LANES), size)


def _norm_mod(x, gain, scale, shift):
    ms = jnp.mean(x * x, axis=-1, keepdims=True)
    return (x * lax.rsqrt(ms + EPS) * gain) * (1.0 + scale) + shift


def _log_sigmoid(x):
    return -(jnp.maximum(-x, 0.0) + jnp.log1p(jnp.exp(-jnp.abs(x))))


def _rope(x, c, sa, sb, quarter):
    return (x * c + pltpu.roll(x, LANES - quarter, 1) * sa + pltpu.roll(x, quarter, 1) * sb)


def _dot(a, b):
    return jnp.dot(a, b, preferred_element_type=F32)


def _dot_nt(a, b):
    return lax.dot_general(a, b, NT_DIMS, preferred_element_type=F32)


def _mod_kernel(c_ref, w_ref, b_ref, o_ref):
    cv = c_ref[...]
    s = cv * jax.nn.sigmoid(cv)
    o_ref[0] = _dot(s.astype(BF16), w_ref[0].astype(BF16)) + b_ref[0]


def _modulation(c_rows, w_mod, b_mod):
    tn = 1536
    nj = (N_MOD * D_MODEL) // tn
    return pl.pallas_call(
        _mod_kernel,
        out_shape=jax.ShapeDtypeStruct((DEPTH, 16, N_MOD * D_MODEL), F32),
        grid=(DEPTH, nj),
        in_specs=[
            pl.BlockSpec((16, D_MODEL), lambda l, j: (0, 0)),
            pl.BlockSpec((1, D_MODEL, tn), lambda l, j: (l, 0, j)),
            pl.BlockSpec((1, 1, tn), lambda l, j: (l, 0, j)),
        ],
        out_specs=pl.BlockSpec((1, 16, tn), lambda l, j: (l, 0, j)),
        compiler_params=pltpu.CompilerParams(
            dimension_semantics=("arbitrary", "arbitrary"), vmem_limit_bytes=VMEM_LIMIT),
        name="modulation",
    )(c_rows, w_mod, b_mod.reshape(DEPTH, 1, N_MOD * D_MODEL))


W_IN_LANES = 256


def _proj_layout_kernel(win_ref, wout_ref, oin_ref, oout_ref):
    oin_ref[0] = _take_runs(win_ref[0], _in_proj_columns(), 0).astype(BF16)
    oout_ref[0] = _take_runs(wout_ref[0], _mix_rows(), 0).astype(BF16)


def _proj_layout(w_in_t, w_out):
    col_spec = lambda rows: pl.BlockSpec((1, rows, W_IN_LANES), lambda l, c: (l, 0, c))
    return pl.pallas_call(
        _proj_layout_kernel,
        out_shape=(jax.ShapeDtypeStruct((DEPTH, D_IN_P, D_MODEL), BF16),
                   jax.ShapeDtypeStruct((DEPTH, D_MODEL, D_MODEL), BF16)),
        grid=(DEPTH, D_MODEL // W_IN_LANES),
        in_specs=[col_spec(D_IN), col_spec(D_MODEL)],
        out_specs=(col_spec(D_IN_P), col_spec(D_MODEL)),
        compiler_params=pltpu.CompilerParams(
            dimension_semantics=("arbitrary", "arbitrary"), vmem_limit_bytes=VMEM_LIMIT),
        name="proj_layout",
    )(w_in_t, w_out)


def _pipeline(n_steps, scores, softmax, values):
    scores(0, 0)
    scores(1, 1)
    softmax(0, 0)

    def body(i, carry):
        t = 2 * i + 1
        scores(t + 1, 0)
        softmax(t, 1)
        values(t - 1, 0)
        scores(t + 2, 1)
        softmax(t + 1, 0)
        values(t, 1)
        return carry

    lax.fori_loop(0, n_steps // 2 - 1, body, 0)
    softmax(n_steps - 1, 1)
    values(n_steps - 2, 0)
    values(n_steps - 1, 1)


def _softmax_tile(parts, floor=None, scale=None):
    m = None
    for s in parts:
        pm = jnp.max(s, axis=-1, keepdims=True)
        m = pm if m is None else jnp.maximum(m, pm)
    if floor is not None:
        m = jnp.maximum(m, floor)
    if scale is None:
        ps = [jnp.exp(s - m).astype(BF16) for s in parts]
    else:
        ps = [jnp.exp((s - m) * scale).astype(BF16) for s in parts]
    extra = None if floor is None else jnp.exp(floor - m)
    return ps, extra


def _retention_tables(decay_ref, rt_scr, m2_scr):
    shape = (RET_CHUNK, LANES)
    lane_lo = _lane_iota(shape) < 64
    row = _row_iota(shape)
    row_lo = row < 64
    i = row.astype(F32)
    rel = i - _lane_iota(shape).astype(F32)
    for j in range(2):
        df0, df1 = decay_ref[0, 2 * j], decay_ref[0, 2 * j + 1]
        db0, db1 = decay_ref[1, 2 * j], decay_ref[1, 2 * j + 1]
        lgf = _log_sigmoid(jnp.where(lane_lo, df0, df1))
        lgb = _log_sigmoid(jnp.where(lane_lo, db0, db1))
        rt_scr[j, 0] = jnp.exp((i + 1.0) * lgf)
        rt_scr[j, 1] = jnp.exp((RET_CHUNK - i) * lgb)
        rt_scr[j, 2] = jnp.exp((RET_CHUNK - 1.0 - i) * lgf)
        rt_scr[j, 3] = jnp.exp(i * lgb)
        rt_scr[j, 4] = jnp.exp(RET_CHUNK * _log_sigmoid(jnp.where(row_lo, df0, df1)))
        rt_scr[j, 5] = jnp.exp(RET_CHUNK * _log_sigmoid(jnp.where(row_lo, db0, db1)))
        for hh, (df, db) in enumerate(((df0, db0), (df1, db1))):
            lf = _log_sigmoid(jnp.full(shape, df, F32))
            lb = _log_sigmoid(jnp.full(shape, db, F32))
            low, upp = rel >= 0.0, rel <= 0.0
            m = (jnp.where(low, jnp.exp(jnp.where(low, rel, 0.0) * lf), 0.0)
                 + jnp.where(upp, jnp.exp(jnp.where(upp, -rel, 0.0) * lb), 0.0))
            m2_scr[j, :, hh * RET_CHUNK:(hh + 1) * RET_CHUNK] = m


N_CHUNK = TOK // RET_CHUNK


def _ret_scratch(n_seq, chunks_per_stage):
    per_chunk = lambda dtype: pltpu.VMEM((N_CHUNK, 2, 2 * RET_CHUNK, LANES), dtype)
    stage = lambda dtype: pltpu.VMEM((2 * chunks_per_stage, RET_CHUNK, 2 * LANES), dtype)
    return [
        pltpu.VMEM((TOK, 2 * LANES), BF16),
        pltpu.VMEM((TOK, 4 * LANES), BF16),
        per_chunk(BF16),
        pltpu.VMEM((TOK, 2 * LANES), BF16),
        per_chunk(BF16),
        pltpu.VMEM((TOK, 2 * LANES), F32),
        pltpu.VMEM((2, 6, RET_CHUNK, LANES), F32),
        pltpu.VMEM((2, RET_CHUNK, 2 * LANES), F32),
        per_chunk(BF16),
        pltpu.VMEM((n_seq, 2, 2, LANES, LANES), F32),
        stage(F32), stage(F32), stage(BF16), stage(BF16),
    ]


class _Retention:
    def __init__(self, scratch, gn_ref, mix_scr):
        (self.q, self.qd, self.kbd, self.v, self.vbd, self.g, self.rt, self.m2,
         self.sall, self.st, rs0, rs1, rp0, rp1) = scratch
        self.rs, self.rp = (rs0, rs1), (rp0, rp1)
        self.gn_ref, self.mix = gn_ref, mix_scr

    def prepare(self, proj, decay_ref, seq_chunks):
        _retention_tables(decay_ref, self.rt, self.m2)
        lane_lo = _lane_iota((TOK, LANES)) < 64
        per_chunk = lambda a: a.reshape(N_CHUNK, RET_CHUNK, LANES)
        qkvg = proj(C_QB, C_QC)
        q = qkvg[:, :C_KB - C_QB]
        k = qkvg[:, C_KB - C_QB:C_VB - C_QB] * RET_K_SCALE
        v = qkvg[:, C_VB - C_QB:C_GB - C_QB]
        self.g[...] = qkvg[:, C_GB - C_QB:]
        self.q[...] = q.astype(BF16)
        self.v[...] = v.astype(BF16)
        for j in range(2):
            cols = slice(j * LANES, (j + 1) * LANES)
            q3 = per_chunk(q[:, cols])
            for d in range(2):
                self.qd[:, (2 * j + d) * LANES:(2 * j + d + 1) * LANES] = (
                    (q3 * self.rt[j, d]).reshape(TOK, LANES).astype(BF16))
            for src, dst in ((k[:, cols], self.kbd), (v[:, cols], self.vbd)):
                dst[:, j, :RET_CHUNK, :] = per_chunk(jnp.where(lane_lo, src, 0.0).astype(BF16))
                dst[:, j, RET_CHUNK:, :] = per_chunk(jnp.where(lane_lo, 0.0, src).astype(BF16))

        pl.run_scoped(
            functools.partial(self._states, k, seq_chunks),
            pltpu.VMEM((TOK, 2 * LANES), F32),
            pltpu.VMEM((N_CHUNK, 2, 2 * RET_CHUNK, LANES), F32))

    def _states(self, k, seq_chunks, k_scr, upd_scr):
        k_scr[...] = k
        blockdiag2 = _lane_iota((2 * RET_CHUNK, LANES)) < 64
        blockdiag2 = blockdiag2 == ((_row_iota((2 * RET_CHUNK, LANES)) & (RET_CHUNK - 1)) < 64)

        def upd_body(c, carry):
            rows = _ds(c * RET_CHUNK, RET_CHUNK)
            for j in range(2):
                cols = slice(j * LANES, (j + 1) * LANES)
                k2 = k_scr[rows, cols]
                kd = jnp.concatenate([k2 * self.rt[j, 2], k2 * self.rt[j, 3]], axis=1)
                upd = _dot(kd.T.astype(BF16), self.v[rows, cols])
                upd_scr[c, j] = jnp.where(blockdiag2, upd, 0.0)
            return carry

        lax.fori_loop(0, N_CHUNK, upd_body, 0, unroll=4)

        n_per_seq = seq_chunks[0][1]
        assert all(n == n_per_seq for _, n in seq_chunks)

        def scan_body(t, carry):
            for si, (first, n) in enumerate(seq_chunks):
                cf, cb = first + t, first + n - 1 - t
                for j in range(2):
                    sf, sb = self.st[si, 0, j], self.st[si, 1, j]
                    self.sall[cf, j, :RET_CHUNK, :] = sf.astype(BF16)
                    self.sall[cb, j, RET_CHUNK:, :] = sb.astype(BF16)
                    self.st[si, 0, j] = self.rt[j, 4] * sf + upd_scr[cf, j, :RET_CHUNK, :]
                    self.st[si, 1, j] = self.rt[j, 5] * sb + upd_scr[cb, j, RET_CHUNK:, :]
            return carry

        lax.fori_loop(0, n_per_seq, scan_body, 0)

    def scores(self, chunks, b):
        for i, c in enumerate(chunks):
            rows = _ds(c * RET_CHUNK, RET_CHUNK)
            for j in range(2):
                self.rs[b][2 * i + j] = _dot_nt(self.q[rows, j * LANES:(j + 1) * LANES], self.kbd[c, j])

    def mask(self, chunks, b):
        for i in range(len(chunks)):
            for j in range(2):
                self.rp[b][2 * i + j] = (self.rs[b][2 * i + j] * self.m2[j]).astype(BF16)

    def values(self, chunks, b):
        lane_lo = _lane_iota((RET_CHUNK, LANES)) < 64
        for i, c in enumerate(chunks):
            rows = _ds(c * RET_CHUNK, RET_CHUNK)
            for j in range(2):
                cols = slice(j * LANES, (j + 1) * LANES)
                o = (_dot(self.rp[b][2 * i + j], self.vbd[c, j])
                     + _dot(self.qd[rows, 2 * j * LANES:2 * (j + 1) * LANES], self.sall[c, j]))
                s_lo = jnp.sum(jnp.where(lane_lo, o, 0.0), axis=-1, keepdims=True)
                s_hi = jnp.sum(jnp.where(lane_lo, 0.0, o), axis=-1, keepdims=True)
                d = o - jnp.where(lane_lo, s_lo, s_hi) * (1.0 / HEAD_DIM)
                dd = d * d
                v_lo = jnp.sum(jnp.where(lane_lo, dd, 0.0), axis=-1, keepdims=True)
                v_hi = jnp.sum(jnp.where(lane_lo, 0.0, dd), axis=-1, keepdims=True)
                var = jnp.where(lane_lo, v_lo, v_hi) * (1.0 / HEAD_DIM)
                g2 = self.g[rows, cols]
                y = d * lax.rsqrt(var + EPS) * self.gn_ref[:, cols] * (g2 * jax.nn.sigmoid(g2))
                self.mix[rows, M_B + j * LANES:M_B + (j + 1) * LANES] = y.astype(BF16)


X_ROWS = 256
N_XCHUNK = TOK // X_ROWS


class _XStream:
    SCRATCH = [pltpu.VMEM((N_XCHUNK, X_ROWS, D_MODEL), F32), pltpu.VMEM((2, X_ROWS, D_MODEL), F32),
               pltpu.SemaphoreType.DMA((N_XCHUNK,)), pltpu.SemaphoreType.DMA((2,))]

    def __init__(self, x_hbm, xo_hbm, scratch):
        self.x_hbm, self.xo_hbm = x_hbm, xo_hbm
        self.xin, self.xout, self.sem_in, self.sem_out = scratch
        self.step, self.n_steps = pl.program_id(0), pl.num_programs(0)

    @staticmethod
    def _rows(step, r):
        return pl.ds(pl.multiple_of(step * TOK + r * X_ROWS, X_ROWS), X_ROWS)

    def load(self, step, r):
        return pltpu.make_async_copy(self.x_hbm.at[self._rows(step, r), :], self.xin.at[r],
                                     self.sem_in.at[r])

    def store(self, r):
        return pltpu.make_async_copy(self.xout.at[r % 2], self.xo_hbm.at[self._rows(self.step, r), :],
                                     self.sem_out.at[r % 2])

    def prologue(self, mod_ref, n1_ref, h_scr):
        @pl.when(self.step == 0)
        def _():
            for r in range(N_XCHUNK):
                self.load(self.step, r).start()

        for r in range(N_XCHUNK):
            self.load(self.step, r).wait()
            h = _norm_mod(self.xin[r], n1_ref[...], mod_ref[1:2, :], mod_ref[0:1, :])
            h_scr[r * X_ROWS:(r + 1) * X_ROWS, :] = h.astype(BF16)

    def epilogue(self, mod_ref, wout_ref, mix_scr):
        for r in range(N_XCHUNK):
            if r >= 2:
                self.store(r - 2).wait()
            else:
                @pl.when(self.step > 0)
                def _():
                    self.store(r).wait()

            y = _dot(mix_scr[r * X_ROWS:(r + 1) * X_ROWS, :], wout_ref[...])
            self.xout[r % 2] = self.xin[r] + mod_ref[2:3, :] * y
            self.store(r).start()

            @pl.when(self.step + 1 < self.n_steps)
            def _():
                self.load(self.step + 1, r).start()

        @pl.when(self.step + 1 == self.n_steps)
        def _():
            for r in range(N_XCHUNK - 2, N_XCHUNK):
                self.store(r).wait()


FF_PIECES = D_FF // D_MODEL


class _WeightCast:
    SCRATCH = [pltpu.VMEM((D_MODEL, D_MODEL), F32), pltpu.VMEM((D_MODEL, D_MODEL), BF16),
               pltpu.SemaphoreType.DMA(()), pltpu.SemaphoreType.DMA(())]

    def __init__(self, layer, wup_hbm, wdn_hbm, oup_hbm, odn_hbm, scratch):
        self.layer, self.srcs, self.dsts = layer, (wup_hbm, wdn_hbm), (oup_hbm, odn_hbm)
        self.stage_in, self.stage_out, self.sem_in, self.sem_out = scratch
        self.step, self.n_steps = pl.program_id(0), pl.num_programs(0)

    def _piece(self, p):
        if p < FF_PIECES:
            cols = slice(p * D_MODEL, (p + 1) * D_MODEL)
            return self.srcs[0].at[self.layer, :, cols], self.dsts[0].at[:, cols]
        rows = slice((p - FF_PIECES) * D_MODEL, (p - FF_PIECES + 1) * D_MODEL)
        return self.srcs[1].at[self.layer, rows, :], self.dsts[1].at[rows, :]

    def _load(self, p):
        return pltpu.make_async_copy(self._piece(p)[0], self.stage_in, self.sem_in)

    def _store(self, p):
        return pltpu.make_async_copy(self.stage_out, self._piece(p)[1], self.sem_out)

    def fetch(self):
        for p in range(2 * FF_PIECES):
            @pl.when(self.step == p)
            def _():
                self._load(p).start()

    def convert(self):
        self._load(0).wait()

        @pl.when(self.step > 0)
        def _():
            self._store(0).wait()

        self.stage_out[...] = self.stage_in[...].astype(BF16)
        for p in range(2 * FF_PIECES):
            @pl.when(self.step == p)
            def _():
                self._store(p).start()

        @pl.when(self.step + 1 == self.n_steps)
        def _():
            self._store(0).wait()


CTX_PER_STEP = TOK // SEQ
CTX_WROWS = WIN_HEADS * SEQ
CTX_MROWS = MLA_HEADS * SEQ


def _ctx_mixer_kernel(n_alias, x_ref, mod_ref, n1_ref, win_ref, sink_ref, decay_ref, gn_ref, kvn_ref,
                      wkvb_ref, wout_ref, *refs):
    xo_ref, ko_ref, vo_ref, ckvo_ref, kro_ref, so_ref, h_scr, mix_scr = refs[n_alias:n_alias + 8]
    xs = _XStream(x_ref, xo_ref, refs[n_alias + 8:])
    xs.prologue(mod_ref, n1_ref, h_scr)
    per_seq = lambda a: a.reshape(CTX_PER_STEP, SEQ, a.shape[-1])
    if n_alias == 0:
        for ref in (ko_ref, vo_ref, ckvo_ref, kro_ref, so_ref):
            ref[:, 1:] = jnp.zeros((ref.shape[0], ref.shape[1] - 1) + ref.shape[2:], F32)
        ko_ref, vo_ref, ckvo_ref, kro_ref, so_ref = (
            ref.at[:, 0] for ref in (ko_ref, vo_ref, ckvo_ref, kro_ref, so_ref))
    chunks_per_seq = SEQ // RET_CHUNK

    def proj(c0, c1):
        return _dot_nt(h_scr[...], win_ref[c0:c1, :])

    ret = []
    seq_chunks = lambda e: [chunks_per_seq * e + c for c in range(chunks_per_seq)]

    def attention(qst_scr, k_scr, vaug_scr, qc_scr, kcat_scr, kvaug_scr,
                  sw0, sw1, pw0, pw1, ew0, ew1, sm0, sm1, pm0, pm1):
        lane_lo = _lane_iota((TOK, LANES)) < 64
        ones = jnp.ones((TOK, LANES), BF16)
        qkv = proj(C_QA, C_QB)
        for g in range(4):
            q = qkv[:, g * LANES:(g + 1) * LANES] * ATTN_SCALE
            lo = jnp.where(lane_lo, q, 0.0).astype(BF16).reshape(CTX_PER_STEP, SEQ, LANES)
            hi = jnp.where(lane_lo, 0.0, q).astype(BF16).reshape(CTX_PER_STEP, SEQ, LANES)
            qst_scr[:, g * SEQ:(g + 1) * SEQ, :] = lo
            qst_scr[:, (4 + g) * SEQ:(5 + g) * SEQ, :] = hi
        kva = qkv[:, C_KA:C_QB]
        ko_ref[...] = per_seq(kva[:, :LANES])
        vo_ref[...] = per_seq(kva[:, LANES:])
        k_scr[...] = kva[:, :LANES].astype(BF16)
        vaug_scr[:, :LANES] = kva[:, LANES:].astype(BF16)
        vaug_scr[:, LANES:] = ones
        qck = proj(C_QC, D_IN_P)
        for h in range(MLA_HEADS):
            qc_scr[h] = qck[:, h * LANES:(h + 1) * LANES].astype(BF16)
        ckv, kr = qck[:, C_CKV - C_QC:C_KR - C_QC], qck[:, C_KR - C_QC:]
        ckv_n = ckv * lax.rsqrt(jnp.mean(ckv * ckv, axis=-1, keepdims=True) + EPS) * kvn_ref[...]
        ckvo_ref[...] = per_seq(ckv_n)
        kro_ref[...] = per_seq(kr[:, 64:64 + MLA_ROPE])
        kv = _dot(ckv_n.astype(BF16), wkvb_ref[...])
        for h in range(MLA_HEADS):
            kvh = kv[:, h * LANES:(h + 1) * LANES]
            kcat_scr[h] = jnp.where(lane_lo, kvh, kr).astype(BF16)
            kvaug_scr[h, :, :LANES] = kvh.astype(BF16)
            kvaug_scr[h, :, LANES:] = ones

        sw, pw, ew, sm, pm = (sw0, sw1), (pw0, pw1), (ew0, ew1), (sm0, sm1), (pm0, pm1)
        lane_lo_s = _lane_iota((SEQ, LANES)) < 64

        def scores(e, b):
            rows = _ds(e * SEQ, SEQ)
            sw[b][...] = _dot_nt(qst_scr[e], k_scr[rows, :])
            for h in range(MLA_HEADS):
                sm[b][h * SEQ:(h + 1) * SEQ, :] = _dot_nt(qc_scr[h, rows, :], kcat_scr[h, rows, :])
            ret[0].scores(seq_chunks(e), b)

        def softmax(e, b):
            for h in range(WIN_HEADS):
                hr = slice(h * SEQ, (h + 1) * SEQ)
                sink = jnp.full((SEQ, 1), sink_ref[h], F32)
                (p,), extra = _softmax_tile([sw[b][hr, :]], floor=sink)
                pw[b][hr, :] = p
                ew[b][hr, :] = jnp.broadcast_to(extra, (SEQ, LANES))
            for h in range(MLA_HEADS):
                hr = slice(h * SEQ, (h + 1) * SEQ)
                (p,), _ = _softmax_tile([sm[b][hr, :]], scale=MLA_SCALE)
                pm[b][hr, :] = p
            ret[0].mask(seq_chunks(e), b)

        def values(e, b):
            ret[0].values(seq_chunks(e), b)
            rows = _ds(e * SEQ, SEQ)
            oa = _dot(pw[b][...], vaug_scr[rows, :])
            o = oa[:, :LANES] / (oa[:, LANES:] + ew[b][...])
            for g in range(4):
                merged = jnp.where(lane_lo_s, o[g * SEQ:(g + 1) * SEQ], o[(4 + g) * SEQ:(5 + g) * SEQ])
                mix_scr[rows, M_A + g * LANES:M_A + (g + 1) * LANES] = merged.astype(BF16)
            for jp in range(2):
                outs = []
                for h in (2 * jp, 2 * jp + 1):
                    oc = _dot(pm[b][h * SEQ:(h + 1) * SEQ, :], kvaug_scr[h, rows, :])
                    outs.append(oc[:, :LANES] / oc[:, LANES:])
                merged = jnp.where(lane_lo_s, pltpu.roll(outs[0], 64, 1), outs[1])
                mix_scr[rows, M_C + jp * LANES:M_C + (jp + 1) * LANES] = merged.astype(BF16)

        _pipeline(CTX_PER_STEP, scores, softmax, values)

    attention_scratch = (
        pltpu.VMEM((CTX_PER_STEP, CTX_WROWS, LANES), BF16),
        pltpu.VMEM((TOK, LANES), BF16),
        pltpu.VMEM((TOK, 2 * LANES), BF16),
        pltpu.VMEM((MLA_HEADS, TOK, LANES), BF16),
        pltpu.VMEM((MLA_HEADS, TOK, LANES), BF16),
        pltpu.VMEM((MLA_HEADS, TOK, 2 * LANES), BF16),
        pltpu.VMEM((CTX_WROWS, SEQ), F32), pltpu.VMEM((CTX_WROWS, SEQ), F32),
        pltpu.VMEM((CTX_WROWS, SEQ), BF16), pltpu.VMEM((CTX_WROWS, SEQ), BF16),
        pltpu.VMEM((CTX_WROWS, LANES), F32), pltpu.VMEM((CTX_WROWS, LANES), F32),
        pltpu.VMEM((CTX_MROWS, SEQ), F32), pltpu.VMEM((CTX_MROWS, SEQ), F32),
        pltpu.VMEM((CTX_MROWS, SEQ), BF16), pltpu.VMEM((CTX_MROWS, SEQ), BF16),
    )

    def retention(*scratch):
        r = _Retention(scratch, gn_ref, mix_scr)
        r.st[...] = jnp.zeros(r.st.shape, F32)
        r.prepare(proj, decay_ref, [(chunks_per_seq * e, chunks_per_seq) for e in range(CTX_PER_STEP)])
        for e in range(CTX_PER_STEP):
            for dirn in range(2):
                for j in range(2):
                    st = r.st[e, dirn, j]
                    so_ref[e, dirn, 2 * j] = st[:64, :64]
                    so_ref[e, dirn, 2 * j + 1] = st[64:, 64:]
        ret.append(r)
        pl.run_scoped(attention, *attention_scratch)

    pl.run_scoped(retention, *_ret_scratch(CTX_PER_STEP, chunks_per_seq))
    xs.epilogue(mod_ref, wout_ref, mix_scr)


def _const_spec(shape, layer=None):
    if layer is None:
        return pl.BlockSpec(shape, lambda i: (0,) * len(shape), pipeline_mode=pl.Buffered(1))
    return pl.BlockSpec((None,) + shape, lambda i: (layer,) + (0,) * len(shape),
                        pipeline_mode=pl.Buffered(1))


def _smem_spec():
    return pl.BlockSpec(memory_space=pltpu.SMEM)


def _ctx_mixer(layer, x, mod, n1, w_in_p, sink, decay, gn, kvn, w_kv_b, w_out_p, prev_state):
    n_tok = x.shape[0]
    n_seq = n_tok // SEQ
    hbm_spec = pl.BlockSpec(memory_space=pl.ANY)
    if prev_state:
        state_spec = lambda *tail: pl.BlockSpec(
            (CTX_PER_STEP, None) + tail, lambda i: (i, layer) + (0,) * len(tail))
    else:
        assert layer == 0
        state_spec = lambda *tail: pl.BlockSpec(
            (CTX_PER_STEP, DEPTH) + tail, lambda i: (i, 0) + (0,) * len(tail))
    state_tails = [(SEQ, LANES), (SEQ, LANES), (SEQ, MLA_KV_RANK), (SEQ, MLA_ROPE),
                   (2, RET_HEADS, HEAD_DIM, HEAD_DIM)]
    n_in = 10
    return pl.pallas_call(
        functools.partial(_ctx_mixer_kernel, len(prev_state)),
        out_shape=[jax.ShapeDtypeStruct((n_tok, D_MODEL), F32)] + [
            jax.ShapeDtypeStruct((n_seq, DEPTH) + tail, F32) for tail in state_tails],
        grid=(n_tok // TOK,),
        in_specs=[
            hbm_spec,
            _const_spec((N_MOD, D_MODEL)),
            _const_spec((1, D_MODEL)),
            _const_spec((D_IN_P, D_MODEL), layer),
            _smem_spec(),
            _smem_spec(),
            _const_spec((1, RET_HEADS * HEAD_DIM)),
            _const_spec((1, MLA_KV_RANK)),
            _const_spec((MLA_KV_RANK, MLA_HEADS * LANES), layer),
            _const_spec((D_MODEL, D_MODEL), layer),
        ] + [hbm_spec] * len(prev_state),
        out_specs=[hbm_spec] + [state_spec(*tail) for tail in state_tails],
        input_output_aliases={n_in + k: 1 + k for k in range(len(prev_state))},
        scratch_shapes=[
            pltpu.VMEM((TOK, D_MODEL), BF16),
            pltpu.VMEM((TOK, D_MODEL), BF16),
        ] + _XStream.SCRATCH,
        compiler_params=pltpu.CompilerParams(
            dimension_semantics=("arbitrary",), vmem_limit_bytes=VMEM_LIMIT),
        name="ctx_mixer",
    )(x, mod, n1, w_in_p, sink, decay, gn, kvn, w_kv_b, w_out_p, *prev_state)


N_BLK = DEC_SEQ // LANES
KEYS_LOC = 3 * LANES
KEYS_WIN = KEYS_LOC + PAST_LEN
WIN_ROWS = WIN_HEADS * LANES
MLA_QB = 256
MLA_KEYS = DEC_SEQ + PAST_LEN
MLA_HALF = MLA_KEYS // 2


def _lat_mixer_kernel(layer, x_ref, mod_ref, n1_ref, win_ref, sink_ref, decay_ref, gn_ref, kvn_ref,
                      wkvb_ref, wout_ref, ck_ref, cv_ref, cckv_ref, ckr_ref, s0_ref,
                      rc_ref, rsa_ref, rsb_ref, mc_ref, msa_ref, msb_ref, wup_ref, wdn_ref,
                      xo_ref, wup16_ref, wdn16_ref,
                      h_scr, mix_scr, s0_scr, s1_scr, p0_scr, p1_scr, *dma_scratch):
    xs = _XStream(x_ref, xo_ref, dma_scratch[:len(_XStream.SCRATCH)])
    wc = _WeightCast(layer, wup_ref, wdn_ref, wup16_ref, wdn16_ref,
                     dma_scratch[len(_XStream.SCRATCH):])
    xs.prologue(mod_ref, n1_ref, h_scr)
    wc.fetch()
    sbuf, pbuf = (s0_scr, s1_scr), (p0_scr, p1_scr)

    def proj(c0, c1):
        return _dot_nt(h_scr[...], win_ref[c0:c1, :])

    ret = []

    def window(qst_scr, kpad_scr, vaug_scr, ckb_scr, cvaug_scr, bias_scr, e0_scr, e1_scr):
        ebuf = (e0_scr, e1_scr)
        lane_lo = _lane_iota((TOK, LANES)) < 64
        rc, rsa, rsb = rc_ref[...], rsa_ref[...], rsb_ref[...]
        qkv = proj(C_QA, C_QB)
        for g in range(4):
            q = _rope(qkv[:, g * LANES:(g + 1) * LANES], rc, rsa, rsb, 16) * ATTN_SCALE
            lo = jnp.where(lane_lo, q, 0.0).astype(BF16).reshape(N_BLK, LANES, LANES)
            hi = jnp.where(lane_lo, 0.0, q).astype(BF16).reshape(N_BLK, LANES, LANES)
            qst_scr[:, g * LANES:(g + 1) * LANES, :] = lo
            qst_scr[:, (4 + g) * LANES:(5 + g) * LANES, :] = hi
        kva = qkv[:, C_KA:C_QB]
        zpad = jnp.zeros((LANES, LANES), BF16)
        kpad_scr[0:LANES, :] = zpad
        kpad_scr[LANES + TOK:, :] = zpad
        vaug_scr[0:LANES, :LANES] = zpad
        vaug_scr[LANES + TOK:, :LANES] = zpad
        kpad_scr[LANES:LANES + TOK, :] = _rope(kva[:, :LANES], rc, rsa, rsb, 16).astype(BF16)
        vaug_scr[LANES:LANES + TOK, :LANES] = kva[:, LANES:].astype(BF16)
        vaug_scr[:, LANES:] = jnp.ones((TOK + 2 * LANES, LANES), BF16)
        ckb_scr[...] = ck_ref[...].astype(BF16)
        cvaug_scr[:, :LANES] = cv_ref[...].astype(BF16)
        cvaug_scr[:, LANES:] = jnp.ones((PAST_LEN, LANES), BF16)
        qi, kj = _row_iota((LANES, LANES)), _lane_iota((LANES, LANES))
        bias_scr[0] = jnp.full((LANES, LANES), -jnp.inf, F32)
        bias_scr[1] = jnp.where(kj >= qi, 0.0, -jnp.inf)
        bias_scr[2] = jnp.where(kj <= qi, 0.0, -jnp.inf)
        lane_lo_b = _lane_iota((LANES, LANES)) < 64

        def scores(n, b):
            q = qst_scr[n]
            sbuf[b][:, :KEYS_LOC] = _dot_nt(q, kpad_scr[_ds(n * LANES, KEYS_LOC), :])
            sbuf[b][:, KEYS_LOC:] = _dot_nt(q, ckb_scr[...])
            ret[0].scores([n], b)

        def softmax(n, b):
            if isinstance(n, int):
                i_prev, i_next = (1 if n > 0 else 0), (2 if n < N_BLK - 1 else 0)
            else:
                i_prev, i_next = jnp.where(n > 0, 1, 0), jnp.where(n < N_BLK - 1, 2, 0)
            b_prev, b_next = bias_scr[i_prev], bias_scr[i_next]
            for h in range(WIN_HEADS):
                hr = slice(h * LANES, (h + 1) * LANES)
                parts = [sbuf[b][hr, 0:LANES] + b_prev, sbuf[b][hr, LANES:2 * LANES],
                         sbuf[b][hr, 2 * LANES:KEYS_LOC] + b_next, sbuf[b][hr, KEYS_LOC:]]
                ps, extra = _softmax_tile(parts, floor=jnp.full((LANES, 1), sink_ref[h], F32))
                pbuf[b][hr, 0:LANES] = ps[0]
                pbuf[b][hr, LANES:2 * LANES] = ps[1]
                pbuf[b][hr, 2 * LANES:KEYS_LOC] = ps[2]
                pbuf[b][hr, KEYS_LOC:] = ps[3]
                ebuf[b][hr, :] = jnp.broadcast_to(extra, (LANES, LANES))
            ret[0].mask([n], b)

        def values(n, b):
            ret[0].values([n], b)
            oa = (_dot(pbuf[b][:, :KEYS_LOC], vaug_scr[_ds(n * LANES, KEYS_LOC), :])
                  + _dot(pbuf[b][:, KEYS_LOC:], cvaug_scr[...]))
            o = oa[:, :LANES] / (oa[:, LANES:] + ebuf[b][...])
            for g in range(4):
                merged = jnp.where(lane_lo_b, o[g * LANES:(g + 1) * LANES],
                                   o[(4 + g) * LANES:(5 + g) * LANES])
                mix_scr[_ds(n * LANES, LANES), M_A + g * LANES:M_A + (g + 1) * LANES] = (
                    merged.astype(BF16))

        _pipeline(N_BLK, scores, softmax, values)

    window_scratch = (
        pltpu.VMEM((N_BLK, WIN_ROWS, LANES), BF16),
        pltpu.VMEM((TOK + 2 * LANES, LANES), BF16),
        pltpu.VMEM((TOK + 2 * LANES, 2 * LANES), BF16),
        pltpu.VMEM((PAST_LEN, LANES), BF16),
        pltpu.VMEM((PAST_LEN, 2 * LANES), BF16),
        pltpu.VMEM((3, LANES, LANES), F32),
        pltpu.VMEM((WIN_ROWS, LANES), F32), pltpu.VMEM((WIN_ROWS, LANES), F32),
    )

    def retention(*scratch):
        r = _Retention(scratch, gn_ref, mix_scr)
        r.st[0] = s0_ref[...]
        r.prepare(proj, decay_ref, [(0, N_CHUNK)])
        ret.append(r)
        pl.run_scoped(window, *window_scratch)

    assert N_CHUNK == N_BLK
    pl.run_scoped(retention, *_ret_scratch(1, 1))

    def latent(qc_scr, kcat_scr, kvaug_scr, mixc_scr):
        lane_lo = _lane_iota((TOK, LANES)) < 64
        mc, msa, msb = mc_ref[...], msa_ref[...], msb_ref[...]
        qck = proj(C_QC, D_IN_P)
        for h in range(MLA_HEADS):
            qc_scr[h] = _rope(qck[:, h * LANES:(h + 1) * LANES], mc, msa, msb, 8).astype(BF16)
        ckv = qck[:, C_CKV - C_QC:C_KR - C_QC]
        kr = _rope(qck[:, C_KR - C_QC:], mc, msa, msb, 8)
        ckv_n = ckv * lax.rsqrt(jnp.mean(ckv * ckv, axis=-1, keepdims=True) + EPS) * kvn_ref[...]
        kv = _dot(ckv_n.astype(BF16), wkvb_ref[...])
        kv_c = _dot(cckv_ref[...].astype(BF16), wkvb_ref[...])
        kr_c = ckr_ref[...]
        lane_lo_c = _lane_iota((PAST_LEN, LANES)) < 64
        for h in range(MLA_HEADS):
            kvh, kvh_c = kv[:, h * LANES:(h + 1) * LANES], kv_c[:, h * LANES:(h + 1) * LANES]
            kcat_scr[h, 0:TOK, :] = jnp.where(lane_lo, kvh, kr).astype(BF16)
            kcat_scr[h, TOK:, :] = jnp.where(lane_lo_c, kvh_c, kr_c).astype(BF16)
            kvaug_scr[h, 0:TOK, :LANES] = kvh.astype(BF16)
            kvaug_scr[h, TOK:, :LANES] = kvh_c.astype(BF16)
            kvaug_scr[h, :, LANES:] = jnp.ones((MLA_KEYS, LANES), BF16)
        lane_lo_m = _lane_iota((MLA_QB, LANES)) < 64
        n_qb = DEC_SEQ // MLA_QB

        def split(t):
            if isinstance(t, int):
                return t // n_qb, t % n_qb
            return lax.shift_right_logical(t, 2), lax.bitwise_and(t, n_qb - 1)

        def scores(t, b):
            jp, qb = split(t)
            for i in range(2):
                h = 2 * jp + i
                q = qc_scr[h, _ds(qb * MLA_QB, MLA_QB), :]
                for part in range(2):
                    r0 = (2 * i + part) * MLA_QB
                    sbuf[b][r0:r0 + MLA_QB, :] = _dot_nt(
                        q, kcat_scr[h, part * MLA_HALF:(part + 1) * MLA_HALF, :])

        def softmax(t, b):
            for i in range(2):
                for rt in range(MLA_QB // LANES):
                    ra = 2 * i * MLA_QB + rt * LANES
                    rb = ra + MLA_QB
                    ps, _ = _softmax_tile([sbuf[b][ra:ra + LANES, :], sbuf[b][rb:rb + LANES, :]],
                                          scale=MLA_SCALE)
                    pbuf[b][ra:ra + LANES, :] = ps[0]
                    pbuf[b][rb:rb + LANES, :] = ps[1]

        def values(t, b):
            jp, qb = split(t)
            outs = []
            for i in range(2):
                h = 2 * jp + i
                r0 = 2 * i * MLA_QB
                oc = (_dot(pbuf[b][r0:r0 + MLA_QB, :], kvaug_scr[h, 0:MLA_HALF, :])
                      + _dot(pbuf[b][r0 + MLA_QB:r0 + 2 * MLA_QB, :], kvaug_scr[h, MLA_HALF:, :]))
                outs.append(oc[:, :LANES] / oc[:, LANES:])
            merged = jnp.where(lane_lo_m, pltpu.roll(outs[0], 64, 1), outs[1])
            mixc_scr[jp, _ds(qb * MLA_QB, MLA_QB), :] = merged.astype(BF16)

        _pipeline(2 * n_qb, scores, softmax, values)
        for jp in range(2):
            mix_scr[:, M_C + jp * LANES:M_C + (jp + 1) * LANES] = mixc_scr[jp]

    pl.run_scoped(
        latent,
        pltpu.VMEM((MLA_HEADS, TOK, LANES), BF16),
        pltpu.VMEM((MLA_HEADS, MLA_KEYS, LANES), BF16),
        pltpu.VMEM((MLA_HEADS, MLA_KEYS, 2 * LANES), BF16),
        pltpu.VMEM((2, TOK, LANES), BF16),
    )
    wc.convert()
    xs.epilogue(mod_ref, wout_ref, mix_scr)


def _lat_mixer(layer, x, mod, n1, w_in_p, sink, decay, gn, kvn, w_kv_b, w_out_p,
               ck, cv, cckv, ckr, s0, rope_a, rope_m, w_up, w_down):
    n_tok = x.shape[0]
    n_seq = n_tok // DEC_SEQ
    assert n_seq == 2 * FF_PIECES
    hbm_spec = pl.BlockSpec(memory_space=pl.ANY)
    seq_spec = lambda shape: pl.BlockSpec(
        (None, None) + shape, lambda i: (i, layer) + (0,) * len(shape))
    assert WIN_ROWS == 4 * MLA_QB and KEYS_WIN == MLA_HALF
    return pl.pallas_call(
        functools.partial(_lat_mixer_kernel, layer),
        out_shape=(jax.ShapeDtypeStruct((n_tok, D_MODEL), F32),
                   jax.ShapeDtypeStruct((D_MODEL, D_FF), BF16),
                   jax.ShapeDtypeStruct((D_FF, D_MODEL), BF16)),
        grid=(n_seq,),
        in_specs=[
            hbm_spec,
            pl.BlockSpec((None, N_MOD, D_MODEL), lambda i: (i, 0, 0)),
            _const_spec((1, D_MODEL)),
            _const_spec((D_IN_P, D_MODEL), layer),
            _smem_spec(),
            _smem_spec(),
            _const_spec((1, RET_HEADS * HEAD_DIM)),
            _const_spec((1, MLA_KV_RANK)),
            _const_spec((MLA_KV_RANK, MLA_HEADS * LANES), layer),
            _const_spec((D_MODEL, D_MODEL), layer),
            seq_spec((PAST_LEN, LANES)),
            seq_spec((PAST_LEN, LANES)),
            seq_spec((PAST_LEN, MLA_KV_RANK)),
            seq_spec((PAST_LEN, LANES)),
            seq_spec((2, 2, LANES, LANES)),
        ] + [_const_spec((DEC_SEQ, LANES))] * 6 + [hbm_spec, hbm_spec],
        out_specs=(hbm_spec, hbm_spec, hbm_spec),
        scratch_shapes=[
            pltpu.VMEM((TOK, D_MODEL), BF16),
            pltpu.VMEM((TOK, D_MODEL), BF16),
            pltpu.VMEM((WIN_ROWS, KEYS_WIN), F32), pltpu.VMEM((WIN_ROWS, KEYS_WIN), F32),
            pltpu.VMEM((WIN_ROWS, KEYS_WIN), BF16), pltpu.VMEM((WIN_ROWS, KEYS_WIN), BF16),
        ] + _XStream.SCRATCH + _WeightCast.SCRATCH,
        compiler_params=pltpu.CompilerParams(
            dimension_semantics=("arbitrary",), vmem_limit_bytes=VMEM_LIMIT),
        name="lat_mixer",
    )(x, mod, n1, w_in_p, sink, decay, gn, kvn, w_kv_b, w_out_p, ck, cv, cckv, ckr, s0,
      *rope_a, *rope_m, w_up, w_down)


def _mlp_kernel(final, x_ref, mod_ref, n2_ref, wup_ref, wdn_ref, fn_ref, o_ref):
    x = x_ref[...]
    h2 = _norm_mod(x, n2_ref[...], mod_ref[4:5, :], mod_ref[3:4, :]).astype(BF16)
    acc = None
    for c in range(D_FF // FF_CHUNK):
        cols = slice(c * FF_CHUNK, (c + 1) * FF_CHUNK)
        u = jnp.maximum(_dot(h2, wup_ref[:, cols]), 0.0)
        part = _dot((u * u).astype(BF16), wdn_ref[cols, :])
        acc = part if acc is None else acc + part
    y = x + mod_ref[5:6, :] * acc
    if final:
        y = y * lax.rsqrt(jnp.mean(y * y, axis=-1, keepdims=True) + EPS) * fn_ref[...]
    o_ref[...] = y


def _mlp(x, mod, n2, w_up, w_down, final_norm, final):
    n_tok = x.shape[0]
    per_mod = n_tok // mod.shape[0] // MLP_ROWS
    return pl.pallas_call(
        functools.partial(_mlp_kernel, final),
        out_shape=jax.ShapeDtypeStruct((n_tok, D_MODEL), F32),
        grid=(n_tok // MLP_ROWS,),
        in_specs=[
            pl.BlockSpec((MLP_ROWS, D_MODEL), lambda i: (i, 0)),
            pl.BlockSpec((None, N_MOD, D_MODEL), lambda i: (i // per_mod, 0, 0)),
            _const_spec((1, D_MODEL)),
            _const_spec((D_MODEL, D_FF)),
            _const_spec((D_FF, D_MODEL)),
            _const_spec((1, D_MODEL)),
        ],
        out_specs=pl.BlockSpec((MLP_ROWS, D_MODEL), lambda i: (i, 0)),
        compiler_params=pltpu.CompilerParams(
            dimension_semantics=("arbitrary",), vmem_limit_bytes=VMEM_LIMIT),
        name="mlp",
    )(x, mod, n2, w_up, w_down, final_norm)


def kernel(x_prompt, x_sample, cache_win_k, cache_win_v, cache_mla_ckv, cache_mla_krope, state_ret,
           c, c_ctx, w_mod, b_mod, norm1, norm2, w_in, win_sink, ret_decay, ret_gn, mla_kv_norm,
           w_kv_b, w_out, w_up, w_down, final_norm):
    n_ctx, n_lat = x_prompt.shape[0], x_sample.shape[0]

    w_in_p, w_out_p = _proj_layout(jnp.swapaxes(w_in, 1, 2), w_out)
    w_kv_b16 = w_kv_b.astype(BF16)

    c_rows = jnp.zeros((16, D_MODEL), F32).at[0].set(c_ctx).at[1:1 + n_lat].set(c)
    mod = _modulation(c_rows, w_mod, b_mod).reshape(DEPTH, 16, N_MOD, D_MODEL)

    rope_a = _rope_tables(DEC_SEQ, HEAD_DIM, 0, HEAD_DIM)
    rope_m = _rope_tables(DEC_SEQ, MLA_ROPE, MLA_NOPE, LANES)

    ck = cache_win_k.reshape(n_lat, DEPTH, PAST_LEN, LANES)
    cv = cache_win_v.reshape(n_lat, DEPTH, PAST_LEN, LANES)
    ckr = jnp.pad(cache_mla_krope, ((0, 0), (0, 0), (0, 0), (MLA_NOPE, LANES - MLA_NOPE - MLA_ROPE)))
    sr = state_ret.reshape(n_lat, DEPTH, 2, 2, 2, HEAD_DIM, HEAD_DIM)
    zero = jnp.zeros_like(sr[:, :, :, :, 0])
    s0 = jnp.concatenate([jnp.concatenate([sr[:, :, :, :, 0], zero], axis=-1),
                          jnp.concatenate([zero, sr[:, :, :, :, 1]], axis=-1)], axis=-2)

    xp = x_prompt.reshape(n_ctx * SEQ, D_MODEL)
    xs = x_sample.reshape(n_lat * DEC_SEQ, D_MODEL)
    state = ()
    for l in range(DEPTH):
        last = l == DEPTH - 1
        shared = (norm1[l][None], w_in_p, win_sink[l], ret_decay[l], ret_gn[l][None],
                  mla_kv_norm[l][None], w_kv_b16, w_out_p)
        mod_ctx, mod_lat = mod[l, 0:1], mod[l, 1:1 + n_lat]
        xp, *state = _ctx_mixer(l, xp, mod_ctx[0], *shared, state)
        xs, w_up16, w_down16 = _lat_mixer(l, xs, mod_lat, *shared, ck, cv, cache_mla_ckv, ckr, s0,
                                          rope_a, rope_m, w_up, w_down)
        xp = _mlp(xp, mod_ctx, norm2[l][None], w_up16, w_down16, final_norm[None], last)
        xs = _mlp(xs, mod_lat, norm2[l][None], w_up16, w_down16, final_norm[None], last)
    new_k, new_v, new_ckv, new_kr, new_s = state
    return (xp.reshape(n_ctx, SEQ, D_MODEL), xs.reshape(n_lat, DEC_SEQ, D_MODEL),
            new_k.reshape(n_ctx, DEPTH, SEQ, 2, HEAD_DIM), new_v.reshape(n_ctx, DEPTH, SEQ, 2, HEAD_DIM),
            new_ckv, new_kr, new_s)
```

```python
import functools

import numpy as np
import jax
import jax.numpy as jnp
from jax import lax
from jax.experimental import pallas as pl
from jax.experimental.pallas import tpu as pltpu

F32 = jnp.float32
BF16 = jnp.bfloat16

D_MODEL = 1024
DEPTH = 2
SEQ = 256
DEC_SEQ = 1024
PAST_LEN = 256
GRID_W = 64
HEAD_DIM = 64
ROPE_BASE = 10000.0
EPS = 1e-6
WIN_HEADS = 8
WINDOW = 128
ATTN_SCALE = HEAD_DIM ** -0.5
RET_HEADS = 4
RET_CHUNK = 128
RET_K_SCALE = HEAD_DIM ** -0.5
MLA_HEADS = 4
MLA_NOPE = 64
MLA_ROPE = 32
MLA_KV_RANK = 128
MLA_QK = MLA_NOPE + MLA_ROPE
MLA_SCALE = MLA_QK ** -0.5
D_IN = 2336
D_FF = 4 * D_MODEL
N_MOD = 6

LANES = 128
TOK = 1024
MLP_ROWS = 1024
FF_CHUNK = 1024
VMEM_LIMIT = 60 * 1024 * 1024

C_QA, C_KA, C_VA, C_QB, C_KB, C_VB, C_GB, C_QC, C_CKV, C_KR, D_IN_P = (
    0, 512, 640, 768, 1024, 1280, 1536, 1792, 2304, 2432, 2560)
M_A, M_B, M_C = 0, 512, 768

NT_DIMS = (((1,), (1,)), ((), ()))


def _in_proj_columns():
    idx = []
    for g in range(4):
        idx += list(range(g * 64, (g + 1) * 64)) + list(range((4 + g) * 64, (5 + g) * 64))
    idx += list(range(512, 1792))
    for h in range(MLA_HEADS):
        idx += list(range(1792 + h * MLA_QK, 1792 + (h + 1) * MLA_QK)) + [-1] * 32
    idx += list(range(2176, 2304))
    idx += [-1] * 64 + list(range(2304, 2336)) + [-1] * 32
    return np.asarray(idx, np.int32)


def _mix_rows():
    idx = []
    for g in range(4):
        idx += list(range(g * 64, (g + 1) * 64)) + list(range((4 + g) * 64, (5 + g) * 64))
    idx += list(range(512, 1024))
    return np.asarray(idx, np.int32)


def _take_runs(w, idx, axis):
    pieces, i = [], 0
    while i < len(idx):
        j = i + 1
        if idx[i] < 0:
            while j < len(idx) and idx[j] < 0:
                j += 1
            shape = list(w.shape)
            shape[axis] = j - i
            pieces.append(jnp.zeros(shape, w.dtype))
        else:
            while j < len(idx) and idx[j] == idx[j - 1] + 1:
                j += 1
            pieces.append(lax.slice_in_dim(w, int(idx[i]), int(idx[j - 1]) + 1, axis=axis))
        i = j
    return jnp.concatenate(pieces, axis=axis)


def _rope_tables(n_tokens, dim, lane0, period):
    quarter = dim // 4
    t = np.arange(n_tokens)
    row = (t // GRID_W).astype(np.float64)
    col = (t % GRID_W).astype(np.float64)
    inv_freq = ROPE_BASE ** (-np.arange(quarter, dtype=np.float64) / quarter)
    ar, ac = row[:, None] * inv_freq, col[:, None] * inv_freq
    cos = np.concatenate([np.cos(ar), np.cos(ar), np.cos(ac), np.cos(ac)], axis=-1)
    sin = np.concatenate([np.sin(ar), np.sin(ar), np.sin(ac), np.sin(ac)], axis=-1)
    first = np.tile(np.concatenate([np.ones(quarter), np.zeros(quarter)]), 2)
    c = np.ones((n_tokens, LANES))
    sa = np.zeros((n_tokens, LANES))
    sb = np.zeros((n_tokens, LANES))
    for start in range(lane0, LANES, period):
        c[:, start:start + dim] = cos
        sa[:, start:start + dim] = -sin * first
        sb[:, start:start + dim] = sin * (1.0 - first)
    return tuple(jnp.asarray(a, F32) for a in (c, sa, sb))


def _lane_iota(shape):
    return lax.broadcasted_iota(jnp.int32, shape, len(shape) - 1)


def _row_iota(shape):
    return lax.broadcasted_iota(jnp.int32, shape, len(shape) - 2)


def _ds(start, size):
    if isinstance(start, int):
        return pl.ds(start, size)
    return pl.ds(pl.multiple_of(start, LANES), size)


def _norm_mod(x, gain, scale, shift):
    ms = jnp.mean(x * x, axis=-1, keepdims=True)
    return (x * lax.rsqrt(ms + EPS) * gain) * (1.0 + scale) + shift


def _log_sigmoid(x):
    return -(jnp.maximum(-x, 0.0) + jnp.log1p(jnp.exp(-jnp.abs(x))))


def _rope(x, c, sa, sb, quarter):
    return (x * c + pltpu.roll(x, LANES - quarter, 1) * sa + pltpu.roll(x, quarter, 1) * sb)


def _dot(a, b):
    return jnp.dot(a, b, preferred_element_type=F32)


def _dot_nt(a, b):
    return lax.dot_general(a, b, NT_DIMS, preferred_element_type=F32)


def _mod_kernel(c_ref, w_ref, b_ref, o_ref):
    cv = c_ref[...]
    s = cv * jax.nn.sigmoid(cv)
    o_ref[0] = _dot(s.astype(BF16), w_ref[0].astype(BF16)) + b_ref[0]


def _modulation(c_rows, w_mod, b_mod):
    tn = 3072
    nj = (N_MOD * D_MODEL) // tn
    return pl.pallas_call(
        _mod_kernel,
        out_shape=jax.ShapeDtypeStruct((DEPTH, 16, N_MOD * D_MODEL), F32),
        grid=(DEPTH, nj),
        in_specs=[
            pl.BlockSpec((16, D_MODEL), lambda l, j: (0, 0)),
            pl.BlockSpec((1, D_MODEL, tn), lambda l, j: (l, 0, j)),
            pl.BlockSpec((1, 1, tn), lambda l, j: (l, 0, j)),
        ],
        out_specs=pl.BlockSpec((1, 16, tn), lambda l, j: (l, 0, j)),
        compiler_params=pltpu.CompilerParams(
            dimension_semantics=("arbitrary", "arbitrary"), vmem_limit_bytes=VMEM_LIMIT),
        name="modulation",
    )(c_rows, w_mod, b_mod.reshape(DEPTH, 1, N_MOD * D_MODEL))


W_IN_LANES = 512


def _proj_layout_kernel(win_ref, wout_ref, oin_ref, oout_ref):
    oin_ref[0] = _take_runs(win_ref[0], _in_proj_columns(), 0).astype(BF16)
    oout_ref[0] = _take_runs(wout_ref[0], _mix_rows(), 0).astype(BF16)


def _proj_layout(w_in_t, w_out):
    col_spec = lambda rows: pl.BlockSpec((1, rows, W_IN_LANES), lambda l, c: (l, 0, c))
    return pl.pallas_call(
        _proj_layout_kernel,
        out_shape=(jax.ShapeDtypeStruct((DEPTH, D_IN_P, D_MODEL), BF16),
                   jax.ShapeDtypeStruct((DEPTH, D_MODEL, D_MODEL), BF16)),
        grid=(DEPTH, D_MODEL // W_IN_LANES),
        in_specs=[col_spec(D_IN), col_spec(D_MODEL)],
        out_specs=(col_spec(D_IN_P), col_spec(D_MODEL)),
        compiler_params=pltpu.CompilerParams(
            dimension_semantics=("arbitrary", "arbitrary"), vmem_limit_bytes=VMEM_LIMIT),
        name="proj_layout",
    )(w_in_t, w_out)


def _pipeline(n_steps, scores, softmax, values):
    scores(0, 0)
    scores(1, 1)
    softmax(0, 0)

    def body(i, carry):
        t = 2 * i + 1
        scores(t + 1, 0)
        softmax(t, 1)
        values(t - 1, 0)
        scores(t + 2, 1)
        softmax(t + 1, 0)
        values(t, 1)
        return carry

    lax.fori_loop(0, n_steps // 2 - 1, body, 0)
    softmax(n_steps - 1, 1)
    values(n_steps - 2, 0)
    values(n_steps - 1, 1)


def _softmax_tile(parts, floor=None, scale=None):
    m = None
    for s in parts:
        pm = jnp.max(s, axis=-1, keepdims=True)
        m = pm if m is None else jnp.maximum(m, pm)
    if floor is not None:
        m = jnp.maximum(m, floor)
    if scale is None:
        ps = [jnp.exp(s - m).astype(BF16) for s in parts]
    else:
        ps = [jnp.exp((s - m) * scale).astype(BF16) for s in parts]
    extra = None if floor is None else jnp.exp(floor - m)
    return ps, extra


def _retention_tables(decay_ref, rt_scr, m2_scr):
    shape = (RET_CHUNK, LANES)
    lane_lo = _lane_iota(shape) < 64
    row = _row_iota(shape)
    row_lo = row < 64
    i = row.astype(F32)
    rel = i - _lane_iota(shape).astype(F32)
    for j in range(2):
        df0, df1 = decay_ref[0, 2 * j], decay_ref[0, 2 * j + 1]
        db0, db1 = decay_ref[1, 2 * j], decay_ref[1, 2 * j + 1]
        lgf = _log_sigmoid(jnp.where(lane_lo, df0, df1))
        lgb = _log_sigmoid(jnp.where(lane_lo, db0, db1))
        rt_scr[j, 0] = jnp.exp((i + 1.0) * lgf)
        rt_scr[j, 1] = jnp.exp((RET_CHUNK - i) * lgb)
        rt_scr[j, 2] = jnp.exp((RET_CHUNK - 1.0 - i) * lgf)
        rt_scr[j, 3] = jnp.exp(i * lgb)
        rt_scr[j, 4] = jnp.exp(RET_CHUNK * _log_sigmoid(jnp.where(row_lo, df0, df1)))
        rt_scr[j, 5] = jnp.exp(RET_CHUNK * _log_sigmoid(jnp.where(row_lo, db0, db1)))
        for hh, (df, db) in enumerate(((df0, db0), (df1, db1))):
            lf = _log_sigmoid(jnp.full(shape, df, F32))
            lb = _log_sigmoid(jnp.full(shape, db, F32))
            low, upp = rel >= 0.0, rel <= 0.0
            m = (jnp.where(low, jnp.exp(jnp.where(low, rel, 0.0) * lf), 0.0)
                 + jnp.where(upp, jnp.exp(jnp.where(upp, -rel, 0.0) * lb), 0.0))
            m2_scr[j, :, hh * RET_CHUNK:(hh + 1) * RET_CHUNK] = m


N_CHUNK = TOK // RET_CHUNK


def _ret_scratch(n_seq, chunks_per_stage):
    per_chunk = lambda dtype: pltpu.VMEM((N_CHUNK, 2, 2 * RET_CHUNK, LANES), dtype)
    stage = lambda dtype: pltpu.VMEM((2 * chunks_per_stage, RET_CHUNK, 2 * LANES), dtype)
    return [
        pltpu.VMEM((TOK, 2 * LANES), BF16),
        pltpu.VMEM((TOK, 4 * LANES), BF16),
        pltpu.VMEM((TOK, 2 * LANES), F32),
        per_chunk(BF16),
        pltpu.VMEM((TOK, 2 * LANES), BF16),
        per_chunk(BF16),
        pltpu.VMEM((TOK, 2 * LANES), F32),
        pltpu.VMEM((2, 6, RET_CHUNK, LANES), F32),
        pltpu.VMEM((2, RET_CHUNK, 2 * LANES), F32),
        per_chunk(F32),
        per_chunk(BF16),
        pltpu.VMEM((n_seq, 2, 2, LANES, LANES), F32),
        stage(F32), stage(F32), stage(BF16), stage(BF16),
    ]


class _Retention:
    def __init__(self, scratch, gn_ref, mix_scr):
        (self.q, self.qd, self.k, self.kbd, self.v, self.vbd, self.g, self.rt, self.m2, self.upd,
         self.sall, self.st, rs0, rs1, rp0, rp1) = scratch
        self.rs, self.rp = (rs0, rs1), (rp0, rp1)
        self.gn_ref, self.mix = gn_ref, mix_scr

    def prepare(self, proj, decay_ref, seq_chunks):
        _retention_tables(decay_ref, self.rt, self.m2)
        lane_lo = _lane_iota((TOK, LANES)) < 64
        per_chunk = lambda a: a.reshape(N_CHUNK, RET_CHUNK, LANES)
        q = proj(C_QB, C_KB)
        k = proj(C_KB, C_VB) * RET_K_SCALE
        v = proj(C_VB, C_GB)
        self.g[...] = proj(C_GB, C_QC)
        self.q[...] = q.astype(BF16)
        self.k[...] = k
        self.v[...] = v.astype(BF16)
        for j in range(2):
            cols = slice(j * LANES, (j + 1) * LANES)
            q3 = per_chunk(q[:, cols])
            for d in range(2):
                self.qd[:, (2 * j + d) * LANES:(2 * j + d + 1) * LANES] = (
                    (q3 * self.rt[j, d]).reshape(TOK, LANES).astype(BF16))
            for src, dst in ((k[:, cols], self.kbd), (v[:, cols], self.vbd)):
                dst[:, j, :RET_CHUNK, :] = per_chunk(jnp.where(lane_lo, src, 0.0).astype(BF16))
                dst[:, j, RET_CHUNK:, :] = per_chunk(jnp.where(lane_lo, 0.0, src).astype(BF16))

        blockdiag2 = _lane_iota((2 * RET_CHUNK, LANES)) < 64
        blockdiag2 = blockdiag2 == ((_row_iota((2 * RET_CHUNK, LANES)) & (RET_CHUNK - 1)) < 64)

        def upd_body(c, carry):
            rows = _ds(c * RET_CHUNK, RET_CHUNK)
            for j in range(2):
                cols = slice(j * LANES, (j + 1) * LANES)
                k2 = self.k[rows, cols]
                kd = jnp.concatenate([k2 * self.rt[j, 2], k2 * self.rt[j, 3]], axis=1)
                upd = _dot(kd.T.astype(BF16), self.v[rows, cols])
                self.upd[c, j] = jnp.where(blockdiag2, upd, 0.0)
            return carry

        lax.fori_loop(0, N_CHUNK, upd_body, 0, unroll=4)

        n_per_seq = seq_chunks[0][1]
        assert all(n == n_per_seq for _, n in seq_chunks)

        def scan_body(t, carry):
            for si, (first, n) in enumerate(seq_chunks):
                cf, cb = first + t, first + n - 1 - t
                for j in range(2):
                    sf, sb = self.st[si, 0, j], self.st[si, 1, j]
                    self.sall[cf, j, :RET_CHUNK, :] = sf.astype(BF16)
                    self.sall[cb, j, RET_CHUNK:, :] = sb.astype(BF16)
                    self.st[si, 0, j] = self.rt[j, 4] * sf + self.upd[cf, j, :RET_CHUNK, :]
                    self.st[si, 1, j] = self.rt[j, 5] * sb + self.upd[cb, j, RET_CHUNK:, :]
            return carry

        lax.fori_loop(0, n_per_seq, scan_body, 0)

    def scores(self, chunks, b):
        for i, c in enumerate(chunks):
            rows = _ds(c * RET_CHUNK, RET_CHUNK)
            for j in range(2):
                self.rs[b][2 * i + j] = _dot_nt(self.q[rows, j * LANES:(j + 1) * LANES], self.kbd[c, j])

    def mask(self, chunks, b):
        for i in range(len(chunks)):
            for j in range(2):
                self.rp[b][2 * i + j] = (self.rs[b][2 * i + j] * self.m2[j]).astype(BF16)

    def values(self, chunks, b):
        lane_lo = _lane_iota((RET_CHUNK, LANES)) < 64
        for i, c in enumerate(chunks):
            rows = _ds(c * RET_CHUNK, RET_CHUNK)
            for j in range(2):
                cols = slice(j * LANES, (j + 1) * LANES)
                o = (_dot(self.rp[b][2 * i + j], self.vbd[c, j])
                     + _dot(self.qd[rows, 2 * j * LANES:2 * (j + 1) * LANES], self.sall[c, j]))
                s_lo = jnp.sum(jnp.where(lane_lo, o, 0.0), axis=-1, keepdims=True)
                s_hi = jnp.sum(jnp.where(lane_lo, 0.0, o), axis=-1, keepdims=True)
                d = o - jnp.where(lane_lo, s_lo, s_hi) * (1.0 / HEAD_DIM)
                dd = d * d
                v_lo = jnp.sum(jnp.where(lane_lo, dd, 0.0), axis=-1, keepdims=True)
                v_hi = jnp.sum(jnp.where(lane_lo, 0.0, dd), axis=-1, keepdims=True)
                var = jnp.where(lane_lo, v_lo, v_hi) * (1.0 / HEAD_DIM)
                g2 = self.g[rows, cols]
                y = d * lax.rsqrt(var + EPS) * self.gn_ref[:, cols] * (g2 * jax.nn.sigmoid(g2))
                self.mix[rows, M_B + j * LANES:M_B + (j + 1) * LANES] = y.astype(BF16)


X_ROWS = 256
N_XCHUNK = TOK // X_ROWS


class _XStream:
    SCRATCH = [pltpu.VMEM((N_XCHUNK, X_ROWS, D_MODEL), F32), pltpu.VMEM((2, X_ROWS, D_MODEL), F32),
               pltpu.SemaphoreType.DMA((N_XCHUNK,)), pltpu.SemaphoreType.DMA((2,))]

    def __init__(self, x_hbm, xo_hbm, scratch):
        self.x_hbm, self.xo_hbm = x_hbm, xo_hbm
        self.xin, self.xout, self.sem_in, self.sem_out = scratch
        self.step, self.n_steps = pl.program_id(0), pl.num_programs(0)

    @staticmethod
    def _rows(step, r):
        return pl.ds(pl.multiple_of(step * TOK + r * X_ROWS, X_ROWS), X_ROWS)

    def load(self, step, r):
        return pltpu.make_async_copy(self.x_hbm.at[self._rows(step, r), :], self.xin.at[r],
                                     self.sem_in.at[r])

    def store(self, r):
        return pltpu.make_async_copy(self.xout.at[r % 2], self.xo_hbm.at[self._rows(self.step, r), :],
                                     self.sem_out.at[r % 2])

    def prologue(self, mod_ref, n1_ref, h_scr):
        @pl.when(self.step == 0)
        def _():
            for r in range(N_XCHUNK):
                self.load(self.step, r).start()

        for r in range(N_XCHUNK):
            self.load(self.step, r).wait()
            h = _norm_mod(self.xin[r], n1_ref[...], mod_ref[1:2, :], mod_ref[0:1, :])
            h_scr[r * X_ROWS:(r + 1) * X_ROWS, :] = h.astype(BF16)

    def epilogue(self, mod_ref, wout_ref, mix_scr):
        for r in range(N_XCHUNK):
            if r >= 2:
                self.store(r - 2).wait()
            else:
                @pl.when(self.step > 0)
                def _():
                    self.store(r).wait()

            y = _dot(mix_scr[r * X_ROWS:(r + 1) * X_ROWS, :], wout_ref[...])
            self.xout[r % 2] = self.xin[r] + mod_ref[2:3, :] * y
            self.store(r).start()

            @pl.when(self.step + 1 < self.n_steps)
            def _():
                self.load(self.step + 1, r).start()

        @pl.when(self.step + 1 == self.n_steps)
        def _():
            for r in range(N_XCHUNK - 2, N_XCHUNK):
                self.store(r).wait()


FF_PIECES = D_FF // D_MODEL


class _WeightCast:
    SCRATCH = [pltpu.VMEM((D_MODEL, D_MODEL), F32), pltpu.VMEM((D_MODEL, D_MODEL), BF16),
               pltpu.SemaphoreType.DMA(()), pltpu.SemaphoreType.DMA(())]

    def __init__(self, layer, wup_hbm, wdn_hbm, oup_hbm, odn_hbm, scratch):
        self.layer, self.srcs, self.dsts = layer, (wup_hbm, wdn_hbm), (oup_hbm, odn_hbm)
        self.stage_in, self.stage_out, self.sem_in, self.sem_out = scratch
        self.step, self.n_steps = pl.program_id(0), pl.num_programs(0)

    def _piece(self, p):
        if p < FF_PIECES:
            cols = slice(p * D_MODEL, (p + 1) * D_MODEL)
            return self.srcs[0].at[self.layer, :, cols], self.dsts[0].at[:, cols]
        rows = slice((p - FF_PIECES) * D_MODEL, (p - FF_PIECES + 1) * D_MODEL)
        return self.srcs[1].at[self.layer, rows, :], self.dsts[1].at[rows, :]

    def _load(self, p):
        return pltpu.make_async_copy(self._piece(p)[0], self.stage_in, self.sem_in)

    def _store(self, p):
        return pltpu.make_async_copy(self.stage_out, self._piece(p)[1], self.sem_out)

    def fetch(self):
        for p in range(2 * FF_PIECES):
            @pl.when(self.step == p)
            def _():
                self._load(p).start()

    def convert(self):
        self._load(0).wait()

        @pl.when(self.step > 0)
        def _():
            self._store(0).wait()

        self.stage_out[...] = self.stage_in[...].astype(BF16)
        for p in range(2 * FF_PIECES):
            @pl.when(self.step == p)
            def _():
                self._store(p).start()

        @pl.when(self.step + 1 == self.n_steps)
        def _():
            self._store(0).wait()


CTX_PER_STEP = TOK // SEQ
CTX_WROWS = WIN_HEADS * SEQ
CTX_MROWS = MLA_HEADS * SEQ


def _ctx_mixer_kernel(n_alias, x_ref, mod_ref, n1_ref, win_ref, sink_ref, decay_ref, gn_ref, kvn_ref,
                      wkvb_ref, wout_ref, *refs):
    xo_ref, ko_ref, vo_ref, ckvo_ref, kro_ref, so_ref, h_scr, mix_scr = refs[n_alias:n_alias + 8]
    xs = _XStream(x_ref, xo_ref, refs[n_alias + 8:])
    xs.prologue(mod_ref, n1_ref, h_scr)
    per_seq = lambda a: a.reshape(CTX_PER_STEP, SEQ, a.shape[-1])
    if n_alias == 0:
        for ref in (ko_ref, vo_ref, ckvo_ref, kro_ref, so_ref):
            ref[:, 1:] = jnp.zeros((ref.shape[0], ref.shape[1] - 1) + ref.shape[2:], F32)
        ko_ref, vo_ref, ckvo_ref, kro_ref, so_ref = (
            ref.at[:, 0] for ref in (ko_ref, vo_ref, ckvo_ref, kro_ref, so_ref))
    chunks_per_seq = SEQ // RET_CHUNK

    def proj(c0, c1):
        return _dot_nt(h_scr[...], win_ref[c0:c1, :])

    ret = []
    seq_chunks = lambda e: [chunks_per_seq * e + c for c in range(chunks_per_seq)]

    def attention(qst_scr, k_scr, vaug_scr, qc_scr, kcat_scr, kvaug_scr,
                  sw0, sw1, pw0, pw1, ew0, ew1, sm0, sm1, pm0, pm1):
        lane_lo = _lane_iota((TOK, LANES)) < 64
        ones = jnp.ones((TOK, LANES), BF16)
        qa = proj(C_QA, C_KA) * ATTN_SCALE
        for g in range(4):
            q = qa[:, g * LANES:(g + 1) * LANES]
            lo = jnp.where(lane_lo, q, 0.0).astype(BF16).reshape(CTX_PER_STEP, SEQ, LANES)
            hi = jnp.where(lane_lo, 0.0, q).astype(BF16).reshape(CTX_PER_STEP, SEQ, LANES)
            qst_scr[:, g * SEQ:(g + 1) * SEQ, :] = lo
            qst_scr[:, (4 + g) * SEQ:(5 + g) * SEQ, :] = hi
        kva = proj(C_KA, C_QB)
        ko_ref[...] = per_seq(kva[:, :LANES])
        vo_ref[...] = per_seq(kva[:, LANES:])
        k_scr[...] = kva[:, :LANES].astype(BF16)
        vaug_scr[:, :LANES] = kva[:, LANES:].astype(BF16)
        vaug_scr[:, LANES:] = ones
        qc = proj(C_QC, C_CKV).astype(BF16)
        for h in range(MLA_HEADS):
            qc_scr[h] = qc[:, h * LANES:(h + 1) * LANES]
        ckr = proj(C_CKV, D_IN_P)
        ckv, kr = ckr[:, :LANES], ckr[:, LANES:]
        ckv_n = ckv * lax.rsqrt(jnp.mean(ckv * ckv, axis=-1, keepdims=True) + EPS) * kvn_ref[...]
        ckvo_ref[...] = per_seq(ckv_n)
        kro_ref[...] = per_seq(kr[:, 64:64 + MLA_ROPE])
        kv = _dot(ckv_n.astype(BF16), wkvb_ref[...])
        for h in range(MLA_HEADS):
            kvh = kv[:, h * LANES:(h + 1) * LANES]
            kcat_scr[h] = jnp.where(lane_lo, kvh, kr).astype(BF16)
            kvaug_scr[h, :, :LANES] = kvh.astype(BF16)
            kvaug_scr[h, :, LANES:] = ones

        sw, pw, ew, sm, pm = (sw0, sw1), (pw0, pw1), (ew0, ew1), (sm0, sm1), (pm0, pm1)
        lane_lo_s = _lane_iota((SEQ, LANES)) < 64

        def scores(e, b):
            rows = _ds(e * SEQ, SEQ)
            sw[b][...] = _dot_nt(qst_scr[e], k_scr[rows, :])
            for h in range(MLA_HEADS):
                sm[b][h * SEQ:(h + 1) * SEQ, :] = _dot_nt(qc_scr[h, rows, :], kcat_scr[h, rows, :])
            ret[0].scores(seq_chunks(e), b)

        def softmax(e, b):
            for h in range(WIN_HEADS):
                hr = slice(h * SEQ, (h + 1) * SEQ)
                sink = jnp.full((SEQ, 1), sink_ref[h], F32)
                (p,), extra = _softmax_tile([sw[b][hr, :]], floor=sink)
                pw[b][hr, :] = p
                ew[b][hr, :] = jnp.broadcast_to(extra, (SEQ, LANES))
            for h in range(MLA_HEADS):
                hr = slice(h * SEQ, (h + 1) * SEQ)
                (p,), _ = _softmax_tile([sm[b][hr, :]], scale=MLA_SCALE)
                pm[b][hr, :] = p
            ret[0].mask(seq_chunks(e), b)

        def values(e, b):
            ret[0].values(seq_chunks(e), b)
            rows = _ds(e * SEQ, SEQ)
            oa = _dot(pw[b][...], vaug_scr[rows, :])
            o = oa[:, :LANES] / (oa[:, LANES:] + ew[b][...])
            for g in range(4):
                merged = jnp.where(lane_lo_s, o[g * SEQ:(g + 1) * SEQ], o[(4 + g) * SEQ:(5 + g) * SEQ])
                mix_scr[rows, M_A + g * LANES:M_A + (g + 1) * LANES] = merged.astype(BF16)
            for jp in range(2):
                outs = []
                for h in (2 * jp, 2 * jp + 1):
                    oc = _dot(pm[b][h * SEQ:(h + 1) * SEQ, :], kvaug_scr[h, rows, :])
                    outs.append(oc[:, :LANES] / oc[:, LANES:])
                merged = jnp.where(lane_lo_s, pltpu.roll(outs[0], 64, 1), outs[1])
                mix_scr[rows, M_C + jp * LANES:M_C + (jp + 1) * LANES] = merged.astype(BF16)

        _pipeline(CTX_PER_STEP, scores, softmax, values)

    attention_scratch = (
        pltpu.VMEM((CTX_PER_STEP, CTX_WROWS, LANES), BF16),
        pltpu.VMEM((TOK, LANES), BF16),
        pltpu.VMEM((TOK, 2 * LANES), BF16),
        pltpu.VMEM((MLA_HEADS, TOK, LANES), BF16),
        pltpu.VMEM((MLA_HEADS, TOK, LANES), BF16),
        pltpu.VMEM((MLA_HEADS, TOK, 2 * LANES), BF16),
        pltpu.VMEM((CTX_WROWS, SEQ), F32), pltpu.VMEM((CTX_WROWS, SEQ), F32),
        pltpu.VMEM((CTX_WROWS, SEQ), BF16), pltpu.VMEM((CTX_WROWS, SEQ), BF16),
        pltpu.VMEM((CTX_WROWS, LANES), F32), pltpu.VMEM((CTX_WROWS, LANES), F32),
        pltpu.VMEM((CTX_MROWS, SEQ), F32), pltpu.VMEM((CTX_MROWS, SEQ), F32),
        pltpu.VMEM((CTX_MROWS, SEQ), BF16), pltpu.VMEM((CTX_MROWS, SEQ), BF16),
    )

    def retention(*scratch):
        r = _Retention(scratch, gn_ref, mix_scr)
        r.st[...] = jnp.zeros(r.st.shape, F32)
        r.prepare(proj, decay_ref, [(chunks_per_seq * e, chunks_per_seq) for e in range(CTX_PER_STEP)])
        for e in range(CTX_PER_STEP):
            for dirn in range(2):
                for j in range(2):
                    st = r.st[e, dirn, j]
                    so_ref[e, dirn, 2 * j] = st[:64, :64]
                    so_ref[e, dirn, 2 * j + 1] = st[64:, 64:]
        ret.append(r)
        pl.run_scoped(attention, *attention_scratch)

    pl.run_scoped(retention, *_ret_scratch(CTX_PER_STEP, chunks_per_seq))
    xs.epilogue(mod_ref, wout_ref, mix_scr)


def _const_spec(shape, layer=None):
    if layer is None:
        return pl.BlockSpec(shape, lambda i: (0,) * len(shape), pipeline_mode=pl.Buffered(1))
    return pl.BlockSpec((None,) + shape, lambda i: (layer,) + (0,) * len(shape),
                        pipeline_mode=pl.Buffered(1))


def _smem_spec():
    return pl.BlockSpec(memory_space=pltpu.SMEM)


def _ctx_mixer(layer, x, mod, n1, w_in_p, sink, decay, gn, kvn, w_kv_b, w_out_p, prev_state):
    n_tok = x.shape[0]
    n_seq = n_tok // SEQ
    hbm_spec = pl.BlockSpec(memory_space=pl.ANY)
    if prev_state:
        state_spec = lambda *tail: pl.BlockSpec(
            (CTX_PER_STEP, None) + tail, lambda i: (i, layer) + (0,) * len(tail))
    else:
        assert layer == 0
        state_spec = lambda *tail: pl.BlockSpec(
            (CTX_PER_STEP, DEPTH) + tail, lambda i: (i, 0) + (0,) * len(tail))
    state_tails = [(SEQ, LANES), (SEQ, LANES), (SEQ, MLA_KV_RANK), (SEQ, MLA_ROPE),
                   (2, RET_HEADS, HEAD_DIM, HEAD_DIM)]
    n_in = 10
    return pl.pallas_call(
        functools.partial(_ctx_mixer_kernel, len(prev_state)),
        out_shape=[jax.ShapeDtypeStruct((n_tok, D_MODEL), F32)] + [
            jax.ShapeDtypeStruct((n_seq, DEPTH) + tail, F32) for tail in state_tails],
        grid=(n_tok // TOK,),
        in_specs=[
            hbm_spec,
            _const_spec((N_MOD, D_MODEL)),
            _const_spec((1, D_MODEL)),
            _const_spec((D_IN_P, D_MODEL), layer),
            _smem_spec(),
            _smem_spec(),
            _const_spec((1, RET_HEADS * HEAD_DIM)),
            _const_spec((1, MLA_KV_RANK)),
            _const_spec((MLA_KV_RANK, MLA_HEADS * LANES), layer),
            _const_spec((D_MODEL, D_MODEL), layer),
        ] + [hbm_spec] * len(prev_state),
        out_specs=[hbm_spec] + [state_spec(*tail) for tail in state_tails],
        input_output_aliases={n_in + k: 1 + k for k in range(len(prev_state))},
        scratch_shapes=[
            pltpu.VMEM((TOK, D_MODEL), BF16),
            pltpu.VMEM((TOK, D_MODEL), BF16),
        ] + _XStream.SCRATCH,
        compiler_params=pltpu.CompilerParams(
            dimension_semantics=("arbitrary",), vmem_limit_bytes=VMEM_LIMIT),
        name="ctx_mixer",
    )(x, mod, n1, w_in_p, sink, decay, gn, kvn, w_kv_b, w_out_p, *prev_state)


N_BLK = DEC_SEQ // LANES
KEYS_LOC = 3 * LANES
KEYS_WIN = KEYS_LOC + PAST_LEN
WIN_ROWS = WIN_HEADS * LANES
MLA_QB = 256
MLA_KEYS = DEC_SEQ + PAST_LEN
MLA_HALF = MLA_KEYS // 2


def _lat_mixer_kernel(layer, x_ref, mod_ref, n1_ref, win_ref, sink_ref, decay_ref, gn_ref, kvn_ref,
                      wkvb_ref, wout_ref, ck_ref, cv_ref, cckv_ref, ckr_ref, s0_ref,
                      rc_ref, rsa_ref, rsb_ref, mc_ref, msa_ref, msb_ref, wup_ref, wdn_ref,
                      xo_ref, wup16_ref, wdn16_ref,
                      h_scr, mix_scr, s0_scr, s1_scr, p0_scr, p1_scr, *dma_scratch):
    xs = _XStream(x_ref, xo_ref, dma_scratch[:len(_XStream.SCRATCH)])
    wc = _WeightCast(layer, wup_ref, wdn_ref, wup16_ref, wdn16_ref,
                     dma_scratch[len(_XStream.SCRATCH):])
    xs.prologue(mod_ref, n1_ref, h_scr)
    wc.fetch()
    sbuf, pbuf = (s0_scr, s1_scr), (p0_scr, p1_scr)

    def proj(c0, c1):
        return _dot_nt(h_scr[...], win_ref[c0:c1, :])

    ret = []

    def window(qst_scr, kpad_scr, vaug_scr, ckb_scr, cvaug_scr, bias_scr, e0_scr, e1_scr):
        ebuf = (e0_scr, e1_scr)
        lane_lo = _lane_iota((TOK, LANES)) < 64
        rc, rsa, rsb = rc_ref[...], rsa_ref[...], rsb_ref[...]
        qa = proj(C_QA, C_KA)
        for g in range(4):
            q = _rope(qa[:, g * LANES:(g + 1) * LANES], rc, rsa, rsb, 16) * ATTN_SCALE
            lo = jnp.where(lane_lo, q, 0.0).astype(BF16).reshape(N_BLK, LANES, LANES)
            hi = jnp.where(lane_lo, 0.0, q).astype(BF16).reshape(N_BLK, LANES, LANES)
            qst_scr[:, g * LANES:(g + 1) * LANES, :] = lo
            qst_scr[:, (4 + g) * LANES:(5 + g) * LANES, :] = hi
        kva = proj(C_KA, C_QB)
        zpad = jnp.zeros((LANES, LANES), BF16)
        kpad_scr[0:LANES, :] = zpad
        kpad_scr[LANES + TOK:, :] = zpad
        vaug_scr[0:LANES, :LANES] = zpad
        vaug_scr[LANES + TOK:, :LANES] = zpad
        kpad_scr[LANES:LANES + TOK, :] = _rope(kva[:, :LANES], rc, rsa, rsb, 16).astype(BF16)
        vaug_scr[LANES:LANES + TOK, :LANES] = kva[:, LANES:].astype(BF16)
        vaug_scr[:, LANES:] = jnp.ones((TOK + 2 * LANES, LANES), BF16)
        ckb_scr[...] = ck_ref[...].astype(BF16)
        cvaug_scr[:, :LANES] = cv_ref[...].astype(BF16)
        cvaug_scr[:, LANES:] = jnp.ones((PAST_LEN, LANES), BF16)
        qi, kj = _row_iota((LANES, LANES)), _lane_iota((LANES, LANES))
        bias_scr[0] = jnp.full((LANES, LANES), -jnp.inf, F32)
        bias_scr[1] = jnp.where(kj >= qi, 0.0, -jnp.inf)
        bias_scr[2] = jnp.where(kj <= qi, 0.0, -jnp.inf)
        lane_lo_b = _lane_iota((LANES, LANES)) < 64

        def scores(n, b):
            q = qst_scr[n]
            sbuf[b][:, :KEYS_LOC] = _dot_nt(q, kpad_scr[_ds(n * LANES, KEYS_LOC), :])
            sbuf[b][:, KEYS_LOC:] = _dot_nt(q, ckb_scr[...])
            ret[0].scores([n], b)

        def softmax(n, b):
            if isinstance(n, int):
                i_prev, i_next = (1 if n > 0 else 0), (2 if n < N_BLK - 1 else 0)
            else:
                i_prev, i_next = jnp.where(n > 0, 1, 0), jnp.where(n < N_BLK - 1, 2, 0)
            b_prev, b_next = bias_scr[i_prev], bias_scr[i_next]
            for h in range(WIN_HEADS):
                hr = slice(h * LANES, (h + 1) * LANES)
                parts = [sbuf[b][hr, 0:LANES] + b_prev, sbuf[b][hr, LANES:2 * LANES],
                         sbuf[b][hr, 2 * LANES:KEYS_LOC] + b_next, sbuf[b][hr, KEYS_LOC:]]
                ps, extra = _softmax_tile(parts, floor=jnp.full((LANES, 1), sink_ref[h], F32))
                pbuf[b][hr, 0:LANES] = ps[0]
                pbuf[b][hr, LANES:2 * LANES] = ps[1]
                pbuf[b][hr, 2 * LANES:KEYS_LOC] = ps[2]
                pbuf[b][hr, KEYS_LOC:] = ps[3]
                ebuf[b][hr, :] = jnp.broadcast_to(extra, (LANES, LANES))
            ret[0].mask([n], b)

        def values(n, b):
            ret[0].values([n], b)
            oa = (_dot(pbuf[b][:, :KEYS_LOC], vaug_scr[_ds(n * LANES, KEYS_LOC), :])
                  + _dot(pbuf[b][:, KEYS_LOC:], cvaug_scr[...]))
            o = oa[:, :LANES] / (oa[:, LANES:] + ebuf[b][...])
            for g in range(4):
                merged = jnp.where(lane_lo_b, o[g * LANES:(g + 1) * LANES],
                                   o[(4 + g) * LANES:(5 + g) * LANES])
                mix_scr[_ds(n * LANES, LANES), M_A + g * LANES:M_A + (g + 1) * LANES] = (
                    merged.astype(BF16))

        _pipeline(N_BLK, scores, softmax, values)

    window_scratch = (
        pltpu.VMEM((N_BLK, WIN_ROWS, LANES), BF16),
        pltpu.VMEM((TOK + 2 * LANES, LANES), BF16),
        pltpu.VMEM((TOK + 2 * LANES, 2 * LANES), BF16),
        pltpu.VMEM((PAST_LEN, LANES), BF16),
        pltpu.VMEM((PAST_LEN, 2 * LANES), BF16),
        pltpu.VMEM((3, LANES, LANES), F32),
        pltpu.VMEM((WIN_ROWS, LANES), F32), pltpu.VMEM((WIN_ROWS, LANES), F32),
    )

    def retention(*scratch):
        r = _Retention(scratch, gn_ref, mix_scr)
        r.st[0] = s0_ref[...]
        r.prepare(proj, decay_ref, [(0, N_CHUNK)])
        ret.append(r)
        pl.run_scoped(window, *window_scratch)

    assert N_CHUNK == N_BLK
    pl.run_scoped(retention, *_ret_scratch(1, 1))

    def latent(qc_scr, kcat_scr, kvaug_scr, mixc_scr):
        lane_lo = _lane_iota((TOK, LANES)) < 64
        mc, msa, msb = mc_ref[...], msa_ref[...], msb_ref[...]
        qc = proj(C_QC, C_CKV)
        for h in range(MLA_HEADS):
            qc_scr[h] = _rope(qc[:, h * LANES:(h + 1) * LANES], mc, msa, msb, 8).astype(BF16)
        ckr = proj(C_CKV, D_IN_P)
        ckv = ckr[:, :LANES]
        kr = _rope(ckr[:, LANES:], mc, msa, msb, 8)
        ckv_n = ckv * lax.rsqrt(jnp.mean(ckv * ckv, axis=-1, keepdims=True) + EPS) * kvn_ref[...]
        kv = _dot(ckv_n.astype(BF16), wkvb_ref[...])
        kv_c = _dot(cckv_ref[...].astype(BF16), wkvb_ref[...])
        kr_c = ckr_ref[...]
        lane_lo_c = _lane_iota((PAST_LEN, LANES)) < 64
        for h in range(MLA_HEADS):
            kvh, kvh_c = kv[:, h * LANES:(h + 1) * LANES], kv_c[:, h * LANES:(h + 1) * LANES]
            kcat_scr[h, 0:TOK, :] = jnp.where(lane_lo, kvh, kr).astype(BF16)
            kcat_scr[h, TOK:, :] = jnp.where(lane_lo_c, kvh_c, kr_c).astype(BF16)
            kvaug_scr[h, 0:TOK, :LANES] = kvh.astype(BF16)
            kvaug_scr[h, TOK:, :LANES] = kvh_c.astype(BF16)
            kvaug_scr[h, :, LANES:] = jnp.ones((MLA_KEYS, LANES), BF16)
        lane_lo_m = _lane_iota((MLA_QB, LANES)) < 64
        n_qb = DEC_SEQ // MLA_QB

        def split(t):
            if isinstance(t, int):
                return t // n_qb, t % n_qb
            return lax.shift_right_logical(t, 2), lax.bitwise_and(t, n_qb - 1)

        def scores(t, b):
            jp, qb = split(t)
            for i in range(2):
                h = 2 * jp + i
                q = qc_scr[h, _ds(qb * MLA_QB, MLA_QB), :]
                for part in range(2):
                    r0 = (2 * i + part) * MLA_QB
                    sbuf[b][r0:r0 + MLA_QB, :] = _dot_nt(
                        q, kcat_scr[h, part * MLA_HALF:(part + 1) * MLA_HALF, :])

        def softmax(t, b):
            for i in range(2):
                for rt in range(MLA_QB // LANES):
                    ra = 2 * i * MLA_QB + rt * LANES
                    rb = ra + MLA_QB
                    ps, _ = _softmax_tile([sbuf[b][ra:ra + LANES, :], sbuf[b][rb:rb + LANES, :]],
                                          scale=MLA_SCALE)
                    pbuf[b][ra:ra + LANES, :] = ps[0]
                    pbuf[b][rb:rb + LANES, :] = ps[1]

        def values(t, b):
            jp, qb = split(t)
            outs = []
            for i in range(2):
                h = 2 * jp + i
                r0 = 2 * i * MLA_QB
                oc = (_dot(pbuf[b][r0:r0 + MLA_QB, :], kvaug_scr[h, 0:MLA_HALF, :])
                      + _dot(pbuf[b][r0 + MLA_QB:r0 + 2 * MLA_QB, :], kvaug_scr[h, MLA_HALF:, :]))
                outs.append(oc[:, :LANES] / oc[:, LANES:])
            merged = jnp.where(lane_lo_m, pltpu.roll(outs[0], 64, 1), outs[1])
            mixc_scr[jp, _ds(qb * MLA_QB, MLA_QB), :] = merged.astype(BF16)

        _pipeline(2 * n_qb, scores, softmax, values)
        for jp in range(2):
            mix_scr[:, M_C + jp * LANES:M_C + (jp + 1) * LANES] = mixc_scr[jp]

    pl.run_scoped(
        latent,
        pltpu.VMEM((MLA_HEADS, TOK, LANES), BF16),
        pltpu.VMEM((MLA_HEADS, MLA_KEYS, LANES), BF16),
        pltpu.VMEM((MLA_HEADS, MLA_KEYS, 2 * LANES), BF16),
        pltpu.VMEM((2, TOK, LANES), BF16),
    )
    wc.convert()
    xs.epilogue(mod_ref, wout_ref, mix_scr)


def _lat_mixer(layer, x, mod, n1, w_in_p, sink, decay, gn, kvn, w_kv_b, w_out_p,
               ck, cv, cckv, ckr, s0, rope_a, rope_m, w_up, w_down):
    n_tok = x.shape[0]
    n_seq = n_tok // DEC_SEQ
    assert n_seq == 2 * FF_PIECES
    hbm_spec = pl.BlockSpec(memory_space=pl.ANY)
    seq_spec = lambda shape: pl.BlockSpec(
        (None, None) + shape, lambda i: (i, layer) + (0,) * len(shape))
    assert WIN_ROWS == 4 * MLA_QB and KEYS_WIN == MLA_HALF
    return pl.pallas_call(
        functools.partial(_lat_mixer_kernel, layer),
        out_shape=(jax.ShapeDtypeStruct((n_tok, D_MODEL), F32),
                   jax.ShapeDtypeStruct((D_MODEL, D_FF), BF16),
                   jax.ShapeDtypeStruct((D_FF, D_MODEL), BF16)),
        grid=(n_seq,),
        in_specs=[
            hbm_spec,
            pl.BlockSpec((None, N_MOD, D_MODEL), lambda i: (i, 0, 0)),
            _const_spec((1, D_MODEL)),
            _const_spec((D_IN_P, D_MODEL), layer),
            _smem_spec(),
            _smem_spec(),
            _const_spec((1, RET_HEADS * HEAD_DIM)),
            _const_spec((1, MLA_KV_RANK)),
            _const_spec((MLA_KV_RANK, MLA_HEADS * LANES), layer),
            _const_spec((D_MODEL, D_MODEL), layer),
            seq_spec((PAST_LEN, LANES)),
            seq_spec((PAST_LEN, LANES)),
            seq_spec((PAST_LEN, MLA_KV_RANK)),
            seq_spec((PAST_LEN, LANES)),
            seq_spec((2, 2, LANES, LANES)),
        ] + [_const_spec((DEC_SEQ, LANES))] * 6 + [hbm_spec, hbm_spec],
        out_specs=(hbm_spec, hbm_spec, hbm_spec),
        scratch_shapes=[
            pltpu.VMEM((TOK, D_MODEL), BF16),
            pltpu.VMEM((TOK, D_MODEL), BF16),
            pltpu.VMEM((WIN_ROWS, KEYS_WIN), F32), pltpu.VMEM((WIN_ROWS, KEYS_WIN), F32),
            pltpu.VMEM((WIN_ROWS, KEYS_WIN), BF16), pltpu.VMEM((WIN_ROWS, KEYS_WIN), BF16),
        ] + _XStream.SCRATCH + _WeightCast.SCRATCH,
        compiler_params=pltpu.CompilerParams(
            dimension_semantics=("arbitrary",), vmem_limit_bytes=VMEM_LIMIT),
        name="lat_mixer",
    )(x, mod, n1, w_in_p, sink, decay, gn, kvn, w_kv_b, w_out_p, ck, cv, cckv, ckr, s0,
      *rope_a, *rope_m, w_up, w_down)


def _mlp_kernel(final, x_ref, mod_ref, n2_ref, wup_ref, wdn_ref, fn_ref, o_ref):
    x = x_ref[...]
    h2 = _norm_mod(x, n2_ref[...], mod_ref[4:5, :], mod_ref[3:4, :]).astype(BF16)
    acc = None
    for c in range(D_FF // FF_CHUNK):
        cols = slice(c * FF_CHUNK, (c + 1) * FF_CHUNK)
        u = jnp.maximum(_dot(h2, wup_ref[:, cols]), 0.0)
        part = _dot((u * u).astype(BF16), wdn_ref[cols, :])
        acc = part if acc is None else acc + part
    y = x + mod_ref[5:6, :] * acc
    if final:
        y = y * lax.rsqrt(jnp.mean(y * y, axis=-1, keepdims=True) + EPS) * fn_ref[...]
    o_ref[...] = y


def _mlp(x, mod, n2, w_up, w_down, final_norm, final):
    n_tok = x.shape[0]
    per_mod = n_tok // mod.shape[0] // MLP_ROWS
    return pl.pallas_call(
        functools.partial(_mlp_kernel, final),
        out_shape=jax.ShapeDtypeStruct((n_tok, D_MODEL), F32),
        grid=(n_tok // MLP_ROWS,),
        in_specs=[
            pl.BlockSpec((MLP_ROWS, D_MODEL), lambda i: (i, 0)),
            pl.BlockSpec((None, N_MOD, D_MODEL), lambda i: (i // per_mod, 0, 0)),
            _const_spec((1, D_MODEL)),
            _const_spec((D_MODEL, D_FF)),
            _const_spec((D_FF, D_MODEL)),
            _const_spec((1, D_MODEL)),
        ],
        out_specs=pl.BlockSpec((MLP_ROWS, D_MODEL), lambda i: (i, 0)),
        compiler_params=pltpu.CompilerParams(
            dimension_semantics=("arbitrary",), vmem_limit_bytes=VMEM_LIMIT),
        name="mlp",
    )(x, mod, n2, w_up, w_down, final_norm)


def kernel(x_prompt, x_sample, cache_win_k, cache_win_v, cache_mla_ckv, cache_mla_krope, state_ret,
           c, c_ctx, w_mod, b_mod, norm1, norm2, w_in, win_sink, ret_decay, ret_gn, mla_kv_norm,
           w_kv_b, w_out, w_up, w_down, final_norm):
    n_ctx, n_lat = x_prompt.shape[0], x_sample.shape[0]

    w_in_p, w_out_p = _proj_layout(jnp.swapaxes(w_in, 1, 2), w_out)
    w_kv_b16 = w_kv_b.astype(BF16)

    c_rows = jnp.zeros((16, D_MODEL), F32).at[0].set(c_ctx).at[1:1 + n_lat].set(c)
    mod = _modulation(c_rows, w_mod, b_mod).reshape(DEPTH, 16, N_MOD, D_MODEL)

    rope_a = _rope_tables(DEC_SEQ, HEAD_DIM, 0, HEAD_DIM)
    rope_m = _rope_tables(DEC_SEQ, MLA_ROPE, MLA_NOPE, LANES)

    ck = cache_win_k.reshape(n_lat, DEPTH, PAST_LEN, LANES)
    cv = cache_win_v.reshape(n_lat, DEPTH, PAST_LEN, LANES)
    ckr = jnp.pad(cache_mla_krope, ((0, 0), (0, 0), (0, 0), (MLA_NOPE, LANES - MLA_NOPE - MLA_ROPE)))
    sr = state_ret.reshape(n_lat, DEPTH, 2, 2, 2, HEAD_DIM, HEAD_DIM)
    zero = jnp.zeros_like(sr[:, :, :, :, 0])
    s0 = jnp.concatenate([jnp.concatenate([sr[:, :, :, :, 0], zero], axis=-1),
                          jnp.concatenate([zero, sr[:, :, :, :, 1]], axis=-1)], axis=-2)

    xp = x_prompt.reshape(n_ctx * SEQ, D_MODEL)
    xs = x_sample.reshape(n_lat * DEC_SEQ, D_MODEL)
    state = ()
    for l in range(DEPTH):
        last = l == DEPTH - 1
        shared = (norm1[l][None], w_in_p, win_sink[l], ret_decay[l], ret_gn[l][None],
                  mla_kv_norm[l][None], w_kv_b16, w_out_p)
        mod_ctx, mod_lat = mod[l, 0:1], mod[l, 1:1 + n_lat]
        xs, w_up16, w_down16 = _lat_mixer(l, xs, mod_lat, *shared, ck, cv, cache_mla_ckv, ckr, s0,
                                          rope_a, rope_m, w_up, w_down)
        xp, *state = _ctx_mixer(l, xp, mod_ctx[0], *shared, state)
        xp = _mlp(xp, mod_ctx, norm2[l][None], w_up16, w_down16, final_norm[None], last)
        xs = _mlp(xs, mod_lat, norm2[l][None], w_up16, w_down16, final_norm[None], last)
    new_k, new_v, new_ckv, new_kr, new_s = state
    return (xp.reshape(n_ctx, SEQ, D_MODEL), xs.reshape(n_lat, DEC_SEQ, D_MODEL),
            new_k.reshape(n_ctx, DEPTH, SEQ, 2, HEAD_DIM), new_v.reshape(n_ctx, DEPTH, SEQ, 2, HEAD_DIM),
            new_ckv, new_kr, new_s)
```

```python
import functools

import numpy as np
import jax
import jax.numpy as jnp
from jax import lax
from jax.experimental import pallas as pl
from jax.experimental.pallas import tpu as pltpu

F32 = jnp.float32
BF16 = jnp.bfloat16

D_MODEL = 1024
DEPTH = 2
SEQ = 256
DEC_SEQ = 1024
PAST_LEN = 256
GRID_W = 64
HEAD_DIM = 64
ROPE_BASE = 10000.0
EPS = 1e-6
WIN_HEADS = 8
WINDOW = 128
ATTN_SCALE = HEAD_DIM ** -0.5
RET_HEADS = 4
RET_CHUNK = 128
RET_K_SCALE = HEAD_DIM ** -0.5
MLA_HEADS = 4
MLA_NOPE = 64
MLA_ROPE = 32
MLA_KV_RANK = 128
MLA_QK = MLA_NOPE + MLA_ROPE
MLA_SCALE = MLA_QK ** -0.5
D_IN = 2336
D_FF = 4 * D_MODEL
N_MOD = 6

LANES = 128
TOK = 1024
MLP_ROWS = 1024
FF_CHUNK = 1024
VMEM_LIMIT = 60 * 1024 * 1024

C_QA, C_KA, C_VA, C_QB, C_KB, C_VB, C_GB, C_QC, C_CKV, C_KR, D_IN_P = (
    0, 512, 640, 768, 1024, 1280, 1536, 1792, 2304, 2432, 2560)
M_A, M_B, M_C = 0, 512, 768

NT_DIMS = (((1,), (1,)), ((), ()))


def _in_proj_columns():
    idx = []
    for g in range(4):
        idx += list(range(g * 64, (g + 1) * 64)) + list(range((4 + g) * 64, (5 + g) * 64))
    idx += list(range(512, 1792))
    for h in range(MLA_HEADS):
        idx += list(range(1792 + h * MLA_QK, 1792 + (h + 1) * MLA_QK)) + [-1] * 32
    idx += list(range(2176, 2304))
    idx += [-1] * 64 + list(range(2304, 2336)) + [-1] * 32
    return np.asarray(idx, np.int32)


def _mix_rows():
    idx = []
    for g in range(4):
        idx += list(range(g * 64, (g + 1) * 64)) + list(range((4 + g) * 64, (5 + g) * 64))
    idx += list(range(512, 1024))
    return np.asarray(idx, np.int32)


def _take_runs(w, idx, axis):
    pieces, i = [], 0
    while i < len(idx):
        j = i + 1
        if idx[i] < 0:
            while j < len(idx) and idx[j] < 0:
                j += 1
            shape = list(w.shape)
            shape[axis] = j - i
            pieces.append(jnp.zeros(shape, w.dtype))
        else:
            while j < len(idx) and idx[j] == idx[j - 1] + 1:
                j += 1
            pieces.append(lax.slice_in_dim(w, int(idx[i]), int(idx[j - 1]) + 1, axis=axis))
        i = j
    return jnp.concatenate(pieces, axis=axis)


def _rope_tables(n_tokens, dim, lane0, period):
    quarter = dim // 4
    t = np.arange(n_tokens)
    row = (t // GRID_W).astype(np.float64)
    col = (t % GRID_W).astype(np.float64)
    inv_freq = ROPE_BASE ** (-np.arange(quarter, dtype=np.float64) / quarter)
    ar, ac = row[:, None] * inv_freq, col[:, None] * inv_freq
    cos = np.concatenate([np.cos(ar), np.cos(ar), np.cos(ac), np.cos(ac)], axis=-1)
    sin = np.concatenate([np.sin(ar), np.sin(ar), np.sin(ac), np.sin(ac)], axis=-1)
    first = np.tile(np.concatenate([np.ones(quarter), np.zeros(quarter)]), 2)
    c = np.ones((n_tokens, LANES))
    sa = np.zeros((n_tokens, LANES))
    sb = np.zeros((n_tokens, LANES))
    for start in range(lane0, LANES, period):
        c[:, start:start + dim] = cos
        sa[:, start:start + dim] = -sin * first
        sb[:, start:start + dim] = sin * (1.0 - first)
    return tuple(jnp.asarray(a, F32) for a in (c, sa, sb))


def _lane_iota(shape):
    return lax.broadcasted_iota(jnp.int32, shape, len(shape) - 1)


def _row_iota(shape):
    return lax.broadcasted_iota(jnp.int32, shape, len(shape) - 2)


def _ds(start, size):
    if isinstance(start, int):
        return pl.ds(start, size)
    return pl.ds(pl.multiple_of(start, LANES), size)


def _norm_mod(x, gain, scale, shift):
    ms = jnp.mean(x * x, axis=-1, keepdims=True)
    return (x * lax.rsqrt(ms + EPS) * gain) * (1.0 + scale) + shift


def _log_sigmoid(x):
    return -(jnp.maximum(-x, 0.0) + jnp.log1p(jnp.exp(-jnp.abs(x))))


def _rope(x, c, sa, sb, quarter):
    return (x * c + pltpu.roll(x, LANES - quarter, 1) * sa + pltpu.roll(x, quarter, 1) * sb)


def _dot(a, b):
    return jnp.dot(a, b, preferred_element_type=F32)


def _dot_nt(a, b):
    return lax.dot_general(a, b, NT_DIMS, preferred_element_type=F32)


def _mod_kernel(c_ref, w_ref, b_ref, o_ref):
    cv = c_ref[...]
    s = cv * jax.nn.sigmoid(cv)
    o_ref[0] = _dot(s.astype(BF16), w_ref[0].astype(BF16)) + b_ref[0]


def _modulation(c_rows, w_mod, b_mod):
    tn = 3072
    nj = (N_MOD * D_MODEL) // tn
    return pl.pallas_call(
        _mod_kernel,
        out_shape=jax.ShapeDtypeStruct((DEPTH, 16, N_MOD * D_MODEL), F32),
        grid=(DEPTH, nj),
        in_specs=[
            pl.BlockSpec((16, D_MODEL), lambda l, j: (0, 0)),
            pl.BlockSpec((1, D_MODEL, tn), lambda l, j: (l, 0, j)),
            pl.BlockSpec((1, 1, tn), lambda l, j: (l, 0, j)),
        ],
        out_specs=pl.BlockSpec((1, 16, tn), lambda l, j: (l, 0, j)),
        compiler_params=pltpu.CompilerParams(
            dimension_semantics=("arbitrary", "arbitrary"), vmem_limit_bytes=VMEM_LIMIT),
        name="modulation",
    )(c_rows, w_mod, b_mod.reshape(DEPTH, 1, N_MOD * D_MODEL))


W_IN_LANES = 512


def _proj_layout_kernel(win_ref, wout_ref, oin_ref, oout_ref):
    oin_ref[0] = _take_runs(win_ref[0], _in_proj_columns(), 0).astype(BF16)
    oout_ref[0] = _take_runs(wout_ref[0], _mix_rows(), 0).astype(BF16)


def _proj_layout(w_in_t, w_out):
    col_spec = lambda rows: pl.BlockSpec((1, rows, W_IN_LANES), lambda l, c: (l, 0, c))
    return pl.pallas_call(
        _proj_layout_kernel,
        out_shape=(jax.ShapeDtypeStruct((DEPTH, D_IN_P, D_MODEL), BF16),
                   jax.ShapeDtypeStruct((DEPTH, D_MODEL, D_MODEL), BF16)),
        grid=(DEPTH, D_MODEL // W_IN_LANES),
        in_specs=[col_spec(D_IN), col_spec(D_MODEL)],
        out_specs=(col_spec(D_IN_P), col_spec(D_MODEL)),
        compiler_params=pltpu.CompilerParams(
            dimension_semantics=("arbitrary", "arbitrary"), vmem_limit_bytes=VMEM_LIMIT),
        name="proj_layout",
    )(w_in_t, w_out)


def _pipeline(n_steps, scores, softmax, values):
    scores(0, 0)
    scores(1, 1)
    softmax(0, 0)

    def body(i, carry):
        t = 2 * i + 1
        scores(t + 1, 0)
        softmax(t, 1)
        values(t - 1, 0)
        scores(t + 2, 1)
        softmax(t + 1, 0)
        values(t, 1)
        return carry

    lax.fori_loop(0, n_steps // 2 - 1, body, 0)
    softmax(n_steps - 1, 1)
    values(n_steps - 2, 0)
    values(n_steps - 1, 1)


def _softmax_tile(parts, floor=None, scale=None):
    m = None
    for s in parts:
        pm = jnp.max(s, axis=-1, keepdims=True)
        m = pm if m is None else jnp.maximum(m, pm)
    if floor is not None:
        m = jnp.maximum(m, floor)
    if scale is None:
        ps = [jnp.exp(s - m).astype(BF16) for s in parts]
    else:
        ps = [jnp.exp((s - m) * scale).astype(BF16) for s in parts]
    extra = None if floor is None else jnp.exp(floor - m)
    return ps, extra


def _retention_tables(decay_ref, rt_scr, m2_scr):
    shape = (RET_CHUNK, LANES)
    lane_lo = _lane_iota(shape) < 64
    row = _row_iota(shape)
    row_lo = row < 64
    i = row.astype(F32)
    rel = i - _lane_iota(shape).astype(F32)
    for j in range(2):
        df0, df1 = decay_ref[0, 2 * j], decay_ref[0, 2 * j + 1]
        db0, db1 = decay_ref[1, 2 * j], decay_ref[1, 2 * j + 1]
        lgf = _log_sigmoid(jnp.where(lane_lo, df0, df1))
        lgb = _log_sigmoid(jnp.where(lane_lo, db0, db1))
        rt_scr[j, 0] = jnp.exp((i + 1.0) * lgf)
        rt_scr[j, 1] = jnp.exp((RET_CHUNK - i) * lgb)
        rt_scr[j, 2] = jnp.exp((RET_CHUNK - 1.0 - i) * lgf)
        rt_scr[j, 3] = jnp.exp(i * lgb)
        rt_scr[j, 4] = jnp.exp(RET_CHUNK * _log_sigmoid(jnp.where(row_lo, df0, df1)))
        rt_scr[j, 5] = jnp.exp(RET_CHUNK * _log_sigmoid(jnp.where(row_lo, db0, db1)))
        for hh, (df, db) in enumerate(((df0, db0), (df1, db1))):
            lf = _log_sigmoid(jnp.full(shape, df, F32))
            lb = _log_sigmoid(jnp.full(shape, db, F32))
            low, upp = rel >= 0.0, rel <= 0.0
            m = (jnp.where(low, jnp.exp(jnp.where(low, rel, 0.0) * lf), 0.0)
                 + jnp.where(upp, jnp.exp(jnp.where(upp, -rel, 0.0) * lb), 0.0))
            m2_scr[j, :, hh * RET_CHUNK:(hh + 1) * RET_CHUNK] = m


N_CHUNK = TOK // RET_CHUNK


def _ret_scratch(n_seq, chunks_per_stage):
    per_chunk = lambda dtype: pltpu.VMEM((N_CHUNK, 2, 2 * RET_CHUNK, LANES), dtype)
    stage = lambda dtype: pltpu.VMEM((2 * chunks_per_stage, RET_CHUNK, 2 * LANES), dtype)
    return [
        pltpu.VMEM((TOK, 2 * LANES), BF16),
        pltpu.VMEM((TOK, 4 * LANES), BF16),
        pltpu.VMEM((TOK, 2 * LANES), F32),
        per_chunk(BF16),
        pltpu.VMEM((TOK, 2 * LANES), BF16),
        per_chunk(BF16),
        pltpu.VMEM((TOK, 2 * LANES), F32),
        pltpu.VMEM((2, 6, RET_CHUNK, LANES), F32),
        pltpu.VMEM((2, RET_CHUNK, 2 * LANES), F32),
        per_chunk(F32),
        per_chunk(BF16),
        pltpu.VMEM((n_seq, 2, 2, LANES, LANES), F32),
        stage(F32), stage(F32), stage(BF16), stage(BF16),
    ]


class _Retention:
    def __init__(self, scratch, gn_ref, mix_scr):
        (self.q, self.qd, self.k, self.kbd, self.v, self.vbd, self.g, self.rt, self.m2, self.upd,
         self.sall, self.st, rs0, rs1, rp0, rp1) = scratch
        self.rs, self.rp = (rs0, rs1), (rp0, rp1)
        self.gn_ref, self.mix = gn_ref, mix_scr

    def prepare(self, proj, decay_ref, seq_chunks):
        _retention_tables(decay_ref, self.rt, self.m2)
        lane_lo = _lane_iota((TOK, LANES)) < 64
        per_chunk = lambda a: a.reshape(N_CHUNK, RET_CHUNK, LANES)
        q = proj(C_QB, C_KB)
        k = proj(C_KB, C_VB) * RET_K_SCALE
        v = proj(C_VB, C_GB)
        self.g[...] = proj(C_GB, C_QC)
        self.q[...] = q.astype(BF16)
        self.k[...] = k
        self.v[...] = v.astype(BF16)
        for j in range(2):
            cols = slice(j * LANES, (j + 1) * LANES)
            q3 = per_chunk(q[:, cols])
            for d in range(2):
                self.qd[:, (2 * j + d) * LANES:(2 * j + d + 1) * LANES] = (
                    (q3 * self.rt[j, d]).reshape(TOK, LANES).astype(BF16))
            for src, dst in ((k[:, cols], self.kbd), (v[:, cols], self.vbd)):
                dst[:, j, :RET_CHUNK, :] = per_chunk(jnp.where(lane_lo, src, 0.0).astype(BF16))
                dst[:, j, RET_CHUNK:, :] = per_chunk(jnp.where(lane_lo, 0.0, src).astype(BF16))

        blockdiag2 = _lane_iota((2 * RET_CHUNK, LANES)) < 64
        blockdiag2 = blockdiag2 == ((_row_iota((2 * RET_CHUNK, LANES)) & (RET_CHUNK - 1)) < 64)

        def upd_body(c, carry):
            rows = _ds(c * RET_CHUNK, RET_CHUNK)
            for j in range(2):
                cols = slice(j * LANES, (j + 1) * LANES)
                k2 = self.k[rows, cols]
                kd = jnp.concatenate([k2 * self.rt[j, 2], k2 * self.rt[j, 3]], axis=1)
                upd = _dot(kd.T.astype(BF16), self.v[rows, cols])
                self.upd[c, j] = jnp.where(blockdiag2, upd, 0.0)
            return carry

        lax.fori_loop(0, N_CHUNK, upd_body, 0, unroll=4)

        n_per_seq = seq_chunks[0][1]
        assert all(n == n_per_seq for _, n in seq_chunks)

        def scan_body(t, carry):
            for si, (first, n) in enumerate(seq_chunks):
                cf, cb = first + t, first + n - 1 - t
                for j in range(2):
                    sf, sb = self.st[si, 0, j], self.st[si, 1, j]
                    self.sall[cf, j, :RET_CHUNK, :] = sf.astype(BF16)
                    self.sall[cb, j, RET_CHUNK:, :] = sb.astype(BF16)
                    self.st[si, 0, j] = self.rt[j, 4] * sf + self.upd[cf, j, :RET_CHUNK, :]
                    self.st[si, 1, j] = self.rt[j, 5] * sb + self.upd[cb, j, RET_CHUNK:, :]
            return carry

        lax.fori_loop(0, n_per_seq, scan_body, 0)

    def scores(self, chunks, b):
        for i, c in enumerate(chunks):
            rows = _ds(c * RET_CHUNK, RET_CHUNK)
            for j in range(2):
                self.rs[b][2 * i + j] = _dot_nt(self.q[rows, j * LANES:(j + 1) * LANES], self.kbd[c, j])

    def mask(self, chunks, b):
        for i in range(len(chunks)):
            for j in range(2):
                self.rp[b][2 * i + j] = (self.rs[b][2 * i + j] * self.m2[j]).astype(BF16)

    def values(self, chunks, b):
        lane_lo = _lane_iota((RET_CHUNK, LANES)) < 64
        for i, c in enumerate(chunks):
            rows = _ds(c * RET_CHUNK, RET_CHUNK)
            for j in range(2):
                cols = slice(j * LANES, (j + 1) * LANES)
                o = (_dot(self.rp[b][2 * i + j], self.vbd[c, j])
                     + _dot(self.qd[rows, 2 * j * LANES:2 * (j + 1) * LANES], self.sall[c, j]))
                s_lo = jnp.sum(jnp.where(lane_lo, o, 0.0), axis=-1, keepdims=True)
                s_hi = jnp.sum(jnp.where(lane_lo, 0.0, o), axis=-1, keepdims=True)
                d = o - jnp.where(lane_lo, s_lo, s_hi) * (1.0 / HEAD_DIM)
                dd = d * d
                v_lo = jnp.sum(jnp.where(lane_lo, dd, 0.0), axis=-1, keepdims=True)
                v_hi = jnp.sum(jnp.where(lane_lo, 0.0, dd), axis=-1, keepdims=True)
                var = jnp.where(lane_lo, v_lo, v_hi) * (1.0 / HEAD_DIM)
                g2 = self.g[rows, cols]
                y = d * lax.rsqrt(var + EPS) * self.gn_ref[:, cols] * (g2 * jax.nn.sigmoid(g2))
                self.mix[rows, M_B + j * LANES:M_B + (j + 1) * LANES] = y.astype(BF16)


X_ROWS = 256
N_XCHUNK = TOK // X_ROWS


class _XStream:
    SCRATCH = [pltpu.VMEM((N_XCHUNK, X_ROWS, D_MODEL), F32), pltpu.VMEM((2, X_ROWS, D_MODEL), F32),
               pltpu.SemaphoreType.DMA((N_XCHUNK,)), pltpu.SemaphoreType.DMA((2,))]

    def __init__(self, x_hbm, xo_hbm, scratch):
        self.x_hbm, self.xo_hbm = x_hbm, xo_hbm
        self.xin, self.xout, self.sem_in, self.sem_out = scratch
        self.step, self.n_steps = pl.program_id(0), pl.num_programs(0)

    @staticmethod
    def _rows(step, r):
        return pl.ds(pl.multiple_of(step * TOK + r * X_ROWS, X_ROWS), X_ROWS)

    def load(self, step, r):
        return pltpu.make_async_copy(self.x_hbm.at[self._rows(step, r), :], self.xin.at[r],
                                     self.sem_in.at[r])

    def store(self, r):
        return pltpu.make_async_copy(self.xout.at[r % 2], self.xo_hbm.at[self._rows(self.step, r), :],
                                     self.sem_out.at[r % 2])

    def prologue(self, mod_ref, n1_ref, h_scr):
        @pl.when(self.step == 0)
        def _():
            for r in range(N_XCHUNK):
                self.load(self.step, r).start()

        for r in range(N_XCHUNK):
            self.load(self.step, r).wait()
            h = _norm_mod(self.xin[r], n1_ref[...], mod_ref[1:2, :], mod_ref[0:1, :])
            h_scr[r * X_ROWS:(r + 1) * X_ROWS, :] = h.astype(BF16)

    def epilogue(self, mod_ref, wout_ref, mix_scr):
        for r in range(N_XCHUNK):
            if r >= 2:
                self.store(r - 2).wait()
            else:
                @pl.when(self.step > 0)
                def _():
                    self.store(r).wait()

            y = _dot(mix_scr[r * X_ROWS:(r + 1) * X_ROWS, :], wout_ref[...])
            self.xout[r % 2] = self.xin[r] + mod_ref[2:3, :] * y
            self.store(r).start()

            @pl.when(self.step + 1 < self.n_steps)
            def _():
                self.load(self.step + 1, r).start()

        @pl.when(self.step + 1 == self.n_steps)
        def _():
            for r in range(N_XCHUNK - 2, N_XCHUNK):
                self.store(r).wait()


FF_PIECES = D_FF // D_MODEL
WEIGHT_DMA_PRIORITY = 1


class _WeightCast:
    SCRATCH = [pltpu.VMEM((D_MODEL, D_MODEL), F32), pltpu.VMEM((D_MODEL, D_MODEL), BF16),
               pltpu.SemaphoreType.DMA(()), pltpu.SemaphoreType.DMA(())]

    def __init__(self, layer, wup_hbm, wdn_hbm, oup_hbm, odn_hbm, scratch):
        self.layer, self.srcs, self.dsts = layer, (wup_hbm, wdn_hbm), (oup_hbm, odn_hbm)
        self.stage_in, self.stage_out, self.sem_in, self.sem_out = scratch
        self.step, self.n_steps = pl.program_id(0), pl.num_programs(0)

    def _piece(self, p):
        if p < FF_PIECES:
            cols = slice(p * D_MODEL, (p + 1) * D_MODEL)
            return self.srcs[0].at[self.layer, :, cols], self.dsts[0].at[:, cols]
        rows = slice((p - FF_PIECES) * D_MODEL, (p - FF_PIECES + 1) * D_MODEL)
        return self.srcs[1].at[self.layer, rows, :], self.dsts[1].at[rows, :]

    def _load(self, p):
        return pltpu.make_async_copy(self._piece(p)[0], self.stage_in, self.sem_in)

    def _store(self, p):
        return pltpu.make_async_copy(self.stage_out, self._piece(p)[1], self.sem_out)

    def fetch(self):
        for p in range(2 * FF_PIECES):
            @pl.when(self.step == p)
            def _():
                self._load(p).start(priority=WEIGHT_DMA_PRIORITY)

    def convert(self):
        self._load(0).wait()

        @pl.when(self.step > 0)
        def _():
            self._store(0).wait()

        self.stage_out[...] = self.stage_in[...].astype(BF16)
        for p in range(2 * FF_PIECES):
            @pl.when(self.step == p)
            def _():
                self._store(p).start(priority=WEIGHT_DMA_PRIORITY)

        @pl.when(self.step + 1 == self.n_steps)
        def _():
            self._store(0).wait()


CTX_PER_STEP = TOK // SEQ
CTX_WROWS = WIN_HEADS * SEQ
CTX_MROWS = MLA_HEADS * SEQ


def _ctx_mixer_kernel(n_alias, x_ref, mod_ref, n1_ref, win_ref, sink_ref, decay_ref, gn_ref, kvn_ref,
                      wkvb_ref, wout_ref, *refs):
    xo_ref, ko_ref, vo_ref, ckvo_ref, kro_ref, so_ref, h_scr, mix_scr = refs[n_alias:n_alias + 8]
    xs = _XStream(x_ref, xo_ref, refs[n_alias + 8:])
    xs.prologue(mod_ref, n1_ref, h_scr)
    per_seq = lambda a: a.reshape(CTX_PER_STEP, SEQ, a.shape[-1])
    if n_alias == 0:
        for ref in (ko_ref, vo_ref, ckvo_ref, kro_ref, so_ref):
            ref[:, 1:] = jnp.zeros((ref.shape[0], ref.shape[1] - 1) + ref.shape[2:], F32)
        ko_ref, vo_ref, ckvo_ref, kro_ref, so_ref = (
            ref.at[:, 0] for ref in (ko_ref, vo_ref, ckvo_ref, kro_ref, so_ref))
    chunks_per_seq = SEQ // RET_CHUNK

    def proj(c0, c1):
        return _dot_nt(h_scr[...], win_ref[c0:c1, :])

    ret = []
    seq_chunks = lambda e: [chunks_per_seq * e + c for c in range(chunks_per_seq)]

    def attention(qst_scr, k_scr, vaug_scr, qc_scr, kcat_scr, kvaug_scr,
                  sw0, sw1, pw0, pw1, ew0, ew1, sm0, sm1, pm0, pm1):
        lane_lo = _lane_iota((TOK, LANES)) < 64
        ones = jnp.ones((TOK, LANES), BF16)
        qa = proj(C_QA, C_KA) * ATTN_SCALE
        for g in range(4):
            q = qa[:, g * LANES:(g + 1) * LANES]
            lo = jnp.where(lane_lo, q, 0.0).astype(BF16).reshape(CTX_PER_STEP, SEQ, LANES)
            hi = jnp.where(lane_lo, 0.0, q).astype(BF16).reshape(CTX_PER_STEP, SEQ, LANES)
            qst_scr[:, g * SEQ:(g + 1) * SEQ, :] = lo
            qst_scr[:, (4 + g) * SEQ:(5 + g) * SEQ, :] = hi
        kva = proj(C_KA, C_QB)
        ko_ref[...] = per_seq(kva[:, :LANES])
        vo_ref[...] = per_seq(kva[:, LANES:])
        k_scr[...] = kva[:, :LANES].astype(BF16)
        vaug_scr[:, :LANES] = kva[:, LANES:].astype(BF16)
        vaug_scr[:, LANES:] = ones
        qc = proj(C_QC, C_CKV).astype(BF16)
        for h in range(MLA_HEADS):
            qc_scr[h] = qc[:, h * LANES:(h + 1) * LANES]
        ckr = proj(C_CKV, D_IN_P)
        ckv, kr = ckr[:, :LANES], ckr[:, LANES:]
        ckv_n = ckv * lax.rsqrt(jnp.mean(ckv * ckv, axis=-1, keepdims=True) + EPS) * kvn_ref[...]
        ckvo_ref[...] = per_seq(ckv_n)
        kro_ref[...] = per_seq(kr[:, 64:64 + MLA_ROPE])
        kv = _dot(ckv_n.astype(BF16), wkvb_ref[...])
        for h in range(MLA_HEADS):
            kvh = kv[:, h * LANES:(h + 1) * LANES]
            kcat_scr[h] = jnp.where(lane_lo, kvh, kr).astype(BF16)
            kvaug_scr[h, :, :LANES] = kvh.astype(BF16)
            kvaug_scr[h, :, LANES:] = ones

        sw, pw, ew, sm, pm = (sw0, sw1), (pw0, pw1), (ew0, ew1), (sm0, sm1), (pm0, pm1)
        lane_lo_s = _lane_iota((SEQ, LANES)) < 64

        def scores(e, b):
            rows = _ds(e * SEQ, SEQ)
            sw[b][...] = _dot_nt(qst_scr[e], k_scr[rows, :])
            for h in range(MLA_HEADS):
                sm[b][h * SEQ:(h + 1) * SEQ, :] = _dot_nt(qc_scr[h, rows, :], kcat_scr[h, rows, :])
            ret[0].scores(seq_chunks(e), b)

        def softmax(e, b):
            for h in range(WIN_HEADS):
                hr = slice(h * SEQ, (h + 1) * SEQ)
                sink = jnp.full((SEQ, 1), sink_ref[h], F32)
                (p,), extra = _softmax_tile([sw[b][hr, :]], floor=sink)
                pw[b][hr, :] = p
                ew[b][hr, :] = jnp.broadcast_to(extra, (SEQ, LANES))
            for h in range(MLA_HEADS):
                hr = slice(h * SEQ, (h + 1) * SEQ)
                (p,), _ = _softmax_tile([sm[b][hr, :]], scale=MLA_SCALE)
                pm[b][hr, :] = p
            ret[0].mask(seq_chunks(e), b)

        def values(e, b):
            ret[0].values(seq_chunks(e), b)
            rows = _ds(e * SEQ, SEQ)
            oa = _dot(pw[b][...], vaug_scr[rows, :])
            o = oa[:, :LANES] / (oa[:, LANES:] + ew[b][...])
            for g in range(4):
                merged = jnp.where(lane_lo_s, o[g * SEQ:(g + 1) * SEQ], o[(4 + g) * SEQ:(5 + g) * SEQ])
                mix_scr[rows, M_A + g * LANES:M_A + (g + 1) * LANES] = merged.astype(BF16)
            for jp in range(2):
                outs = []
                for h in (2 * jp, 2 * jp + 1):
                    oc = _dot(pm[b][h * SEQ:(h + 1) * SEQ, :], kvaug_scr[h, rows, :])
                    outs.append(oc[:, :LANES] / oc[:, LANES:])
                merged = jnp.where(lane_lo_s, pltpu.roll(outs[0], 64, 1), outs[1])
                mix_scr[rows, M_C + jp * LANES:M_C + (jp + 1) * LANES] = merged.astype(BF16)

        _pipeline(CTX_PER_STEP, scores, softmax, values)

    attention_scratch = (
        pltpu.VMEM((CTX_PER_STEP, CTX_WROWS, LANES), BF16),
        pltpu.VMEM((TOK, LANES), BF16),
        pltpu.VMEM((TOK, 2 * LANES), BF16),
        pltpu.VMEM((MLA_HEADS, TOK, LANES), BF16),
        pltpu.VMEM((MLA_HEADS, TOK, LANES), BF16),
        pltpu.VMEM((MLA_HEADS, TOK, 2 * LANES), BF16),
        pltpu.VMEM((CTX_WROWS, SEQ), F32), pltpu.VMEM((CTX_WROWS, SEQ), F32),
        pltpu.VMEM((CTX_WROWS, SEQ), BF16), pltpu.VMEM((CTX_WROWS, SEQ), BF16),
        pltpu.VMEM((CTX_WROWS, LANES), F32), pltpu.VMEM((CTX_WROWS, LANES), F32),
        pltpu.VMEM((CTX_MROWS, SEQ), F32), pltpu.VMEM((CTX_MROWS, SEQ), F32),
        pltpu.VMEM((CTX_MROWS, SEQ), BF16), pltpu.VMEM((CTX_MROWS, SEQ), BF16),
    )

    def retention(*scratch):
        r = _Retention(scratch, gn_ref, mix_scr)
        r.st[...] = jnp.zeros(r.st.shape, F32)
        r.prepare(proj, decay_ref, [(chunks_per_seq * e, chunks_per_seq) for e in range(CTX_PER_STEP)])
        for e in range(CTX_PER_STEP):
            for dirn in range(2):
                for j in range(2):
                    st = r.st[e, dirn, j]
                    so_ref[e, dirn, 2 * j] = st[:64, :64]
                    so_ref[e, dirn, 2 * j + 1] = st[64:, 64:]
        ret.append(r)
        pl.run_scoped(attention, *attention_scratch)

    pl.run_scoped(retention, *_ret_scratch(CTX_PER_STEP, chunks_per_seq))
    xs.epilogue(mod_ref, wout_ref, mix_scr)


def _const_spec(shape, layer=None):
    if layer is None:
        return pl.BlockSpec(shape, lambda i: (0,) * len(shape), pipeline_mode=pl.Buffered(1))
    return pl.BlockSpec((None,) + shape, lambda i: (layer,) + (0,) * len(shape),
                        pipeline_mode=pl.Buffered(1))


def _smem_spec():
    return pl.BlockSpec(memory_space=pltpu.SMEM)


def _ctx_mixer(layer, x, mod, n1, w_in_p, sink, decay, gn, kvn, w_kv_b, w_out_p, prev_state):
    n_tok = x.shape[0]
    n_seq = n_tok // SEQ
    hbm_spec = pl.BlockSpec(memory_space=pl.ANY)
    if prev_state:
        state_spec = lambda *tail: pl.BlockSpec(
            (CTX_PER_STEP, None) + tail, lambda i: (i, layer) + (0,) * len(tail))
    else:
        assert layer == 0
        state_spec = lambda *tail: pl.BlockSpec(
            (CTX_PER_STEP, DEPTH) + tail, lambda i: (i, 0) + (0,) * len(tail))
    state_tails = [(SEQ, LANES), (SEQ, LANES), (SEQ, MLA_KV_RANK), (SEQ, MLA_ROPE),
                   (2, RET_HEADS, HEAD_DIM, HEAD_DIM)]
    n_in = 10
    return pl.pallas_call(
        functools.partial(_ctx_mixer_kernel, len(prev_state)),
        out_shape=[jax.ShapeDtypeStruct((n_tok, D_MODEL), F32)] + [
            jax.ShapeDtypeStruct((n_seq, DEPTH) + tail, F32) for tail in state_tails],
        grid=(n_tok // TOK,),
        in_specs=[
            hbm_spec,
            _const_spec((N_MOD, D_MODEL)),
            _const_spec((1, D_MODEL)),
            _const_spec((D_IN_P, D_MODEL), layer),
            _smem_spec(),
            _smem_spec(),
            _const_spec((1, RET_HEADS * HEAD_DIM)),
            _const_spec((1, MLA_KV_RANK)),
            _const_spec((MLA_KV_RANK, MLA_HEADS * LANES), layer),
            _const_spec((D_MODEL, D_MODEL), layer),
        ] + [hbm_spec] * len(prev_state),
        out_specs=[hbm_spec] + [state_spec(*tail) for tail in state_tails],
        input_output_aliases={n_in + k: 1 + k for k in range(len(prev_state))},
        scratch_shapes=[
            pltpu.VMEM((TOK, D_MODEL), BF16),
            pltpu.VMEM((TOK, D_MODEL), BF16),
        ] + _XStream.SCRATCH,
        compiler_params=pltpu.CompilerParams(
            dimension_semantics=("arbitrary",), vmem_limit_bytes=VMEM_LIMIT),
        name="ctx_mixer",
    )(x, mod, n1, w_in_p, sink, decay, gn, kvn, w_kv_b, w_out_p, *prev_state)


N_BLK = DEC_SEQ // LANES
KEYS_LOC = 3 * LANES
KEYS_WIN = KEYS_LOC + PAST_LEN
WIN_ROWS = WIN_HEADS * LANES
MLA_QB = 256
MLA_KEYS = DEC_SEQ + PAST_LEN
MLA_HALF = MLA_KEYS // 2


def _lat_mixer_kernel(layer, x_ref, mod_ref, n1_ref, win_ref, sink_ref, decay_ref, gn_ref, kvn_ref,
                      wkvb_ref, wout_ref, ck_ref, cv_ref, cckv_ref, ckr_ref, s0_ref,
                      rc_ref, rsa_ref, rsb_ref, mc_ref, msa_ref, msb_ref, wup_ref, wdn_ref,
                      xo_ref, wup16_ref, wdn16_ref,
                      h_scr, mix_scr, s0_scr, s1_scr, p0_scr, p1_scr, *dma_scratch):
    xs = _XStream(x_ref, xo_ref, dma_scratch[:len(_XStream.SCRATCH)])
    wc = _WeightCast(layer, wup_ref, wdn_ref, wup16_ref, wdn16_ref,
                     dma_scratch[len(_XStream.SCRATCH):])
    xs.prologue(mod_ref, n1_ref, h_scr)
    wc.fetch()
    sbuf, pbuf = (s0_scr, s1_scr), (p0_scr, p1_scr)

    def proj(c0, c1):
        return _dot_nt(h_scr[...], win_ref[c0:c1, :])

    ret = []

    def window(qst_scr, kpad_scr, vaug_scr, ckb_scr, cvaug_scr, bias_scr, e0_scr, e1_scr):
        ebuf = (e0_scr, e1_scr)
        lane_lo = _lane_iota((TOK, LANES)) < 64
        rc, rsa, rsb = rc_ref[...], rsa_ref[...], rsb_ref[...]
        qa = proj(C_QA, C_KA)
        for g in range(4):
            q = _rope(qa[:, g * LANES:(g + 1) * LANES], rc, rsa, rsb, 16) * ATTN_SCALE
            lo = jnp.where(lane_lo, q, 0.0).astype(BF16).reshape(N_BLK, LANES, LANES)
            hi = jnp.where(lane_lo, 0.0, q).astype(BF16).reshape(N_BLK, LANES, LANES)
            qst_scr[:, g * LANES:(g + 1) * LANES, :] = lo
            qst_scr[:, (4 + g) * LANES:(5 + g) * LANES, :] = hi
        kva = proj(C_KA, C_QB)
        zpad = jnp.zeros((LANES, LANES), BF16)
        kpad_scr[0:LANES, :] = zpad
        kpad_scr[LANES + TOK:, :] = zpad
        vaug_scr[0:LANES, :LANES] = zpad
        vaug_scr[LANES + TOK:, :LANES] = zpad
        kpad_scr[LANES:LANES + TOK, :] = _rope(kva[:, :LANES], rc, rsa, rsb, 16).astype(BF16)
        vaug_scr[LANES:LANES + TOK, :LANES] = kva[:, LANES:].astype(BF16)
        vaug_scr[:, LANES:] = jnp.ones((TOK + 2 * LANES, LANES), BF16)
        ckb_scr[...] = ck_ref[...].astype(BF16)
        cvaug_scr[:, :LANES] = cv_ref[...].astype(BF16)
        cvaug_scr[:, LANES:] = jnp.ones((PAST_LEN, LANES), BF16)
        qi, kj = _row_iota((LANES, LANES)), _lane_iota((LANES, LANES))
        bias_scr[0] = jnp.full((LANES, LANES), -jnp.inf, F32)
        bias_scr[1] = jnp.where(kj >= qi, 0.0, -jnp.inf)
        bias_scr[2] = jnp.where(kj <= qi, 0.0, -jnp.inf)
        lane_lo_b = _lane_iota((LANES, LANES)) < 64

        def scores(n, b):
            q = qst_scr[n]
            sbuf[b][:, :KEYS_LOC] = _dot_nt(q, kpad_scr[_ds(n * LANES, KEYS_LOC), :])
            sbuf[b][:, KEYS_LOC:] = _dot_nt(q, ckb_scr[...])
            ret[0].scores([n], b)

        def softmax(n, b):
            if isinstance(n, int):
                i_prev, i_next = (1 if n > 0 else 0), (2 if n < N_BLK - 1 else 0)
            else:
                i_prev, i_next = jnp.where(n > 0, 1, 0), jnp.where(n < N_BLK - 1, 2, 0)
            b_prev, b_next = bias_scr[i_prev], bias_scr[i_next]
            for h in range(WIN_HEADS):
                hr = slice(h * LANES, (h + 1) * LANES)
                parts = [sbuf[b][hr, 0:LANES] + b_prev, sbuf[b][hr, LANES:2 * LANES],
                         sbuf[b][hr, 2 * LANES:KEYS_LOC] + b_next, sbuf[b][hr, KEYS_LOC:]]
                ps, extra = _softmax_tile(parts, floor=jnp.full((LANES, 1), sink_ref[h], F32))
                pbuf[b][hr, 0:LANES] = ps[0]
                pbuf[b][hr, LANES:2 * LANES] = ps[1]
                pbuf[b][hr, 2 * LANES:KEYS_LOC] = ps[2]
                pbuf[b][hr, KEYS_LOC:] = ps[3]
                ebuf[b][hr, :] = jnp.broadcast_to(extra, (LANES, LANES))
            ret[0].mask([n], b)

        def values(n, b):
            ret[0].values([n], b)
            oa = (_dot(pbuf[b][:, :KEYS_LOC], vaug_scr[_ds(n * LANES, KEYS_LOC), :])
                  + _dot(pbuf[b][:, KEYS_LOC:], cvaug_scr[...]))
            o = oa[:, :LANES] / (oa[:, LANES:] + ebuf[b][...])
            for g in range(4):
                merged = jnp.where(lane_lo_b, o[g * LANES:(g + 1) * LANES],
                                   o[(4 + g) * LANES:(5 + g) * LANES])
                mix_scr[_ds(n * LANES, LANES), M_A + g * LANES:M_A + (g + 1) * LANES] = (
                    merged.astype(BF16))

        _pipeline(N_BLK, scores, softmax, values)

    window_scratch = (
        pltpu.VMEM((N_BLK, WIN_ROWS, LANES), BF16),
        pltpu.VMEM((TOK + 2 * LANES, LANES), BF16),
        pltpu.VMEM((TOK + 2 * LANES, 2 * LANES), BF16),
        pltpu.VMEM((PAST_LEN, LANES), BF16),
        pltpu.VMEM((PAST_LEN, 2 * LANES), BF16),
        pltpu.VMEM((3, LANES, LANES), F32),
        pltpu.VMEM((WIN_ROWS, LANES), F32), pltpu.VMEM((WIN_ROWS, LANES), F32),
    )

    def retention(*scratch):
        r = _Retention(scratch, gn_ref, mix_scr)
        r.st[0] = s0_ref[...]
        r.prepare(proj, decay_ref, [(0, N_CHUNK)])
        ret.append(r)
        pl.run_scoped(window, *window_scratch)

    assert N_CHUNK == N_BLK
    pl.run_scoped(retention, *_ret_scratch(1, 1))

    def latent(qc_scr, kcat_scr, kvaug_scr, mixc_scr):
        lane_lo = _lane_iota((TOK, LANES)) < 64
        mc, msa, msb = mc_ref[...], msa_ref[...], msb_ref[...]
        qc = proj(C_QC, C_CKV)
        for h in range(MLA_HEADS):
            qc_scr[h] = _rope(qc[:, h * LANES:(h + 1) * LANES], mc, msa, msb, 8).astype(BF16)
        ckr = proj(C_CKV, D_IN_P)
        ckv = ckr[:, :LANES]
        kr = _rope(ckr[:, LANES:], mc, msa, msb, 8)
        ckv_n = ckv * lax.rsqrt(jnp.mean(ckv * ckv, axis=-1, keepdims=True) + EPS) * kvn_ref[...]
        kv = _dot(ckv_n.astype(BF16), wkvb_ref[...])
        kv_c = _dot(cckv_ref[...].astype(BF16), wkvb_ref[...])
        kr_c = ckr_ref[...]
        lane_lo_c = _lane_iota((PAST_LEN, LANES)) < 64
        for h in range(MLA_HEADS):
            kvh, kvh_c = kv[:, h * LANES:(h + 1) * LANES], kv_c[:, h * LANES:(h + 1) * LANES]
            kcat_scr[h, 0:TOK, :] = jnp.where(lane_lo, kvh, kr).astype(BF16)
            kcat_scr[h, TOK:, :] = jnp.where(lane_lo_c, kvh_c, kr_c).astype(BF16)
            kvaug_scr[h, 0:TOK, :LANES] = kvh.astype(BF16)
            kvaug_scr[h, TOK:, :LANES] = kvh_c.astype(BF16)
            kvaug_scr[h, :, LANES:] = jnp.ones((MLA_KEYS, LANES), BF16)
        lane_lo_m = _lane_iota((MLA_QB, LANES)) < 64
        n_qb = DEC_SEQ // MLA_QB

        def split(t):
            if isinstance(t, int):
                return t // n_qb, t % n_qb
            return lax.shift_right_logical(t, 2), lax.bitwise_and(t, n_qb - 1)

        def scores(t, b):
            jp, qb = split(t)
            for i in range(2):
                h = 2 * jp + i
                q = qc_scr[h, _ds(qb * MLA_QB, MLA_QB), :]
                for part in range(2):
                    r0 = (2 * i + part) * MLA_QB
                    sbuf[b][r0:r0 + MLA_QB, :] = _dot_nt(
                        q, kcat_scr[h, part * MLA_HALF:(part + 1) * MLA_HALF, :])

        def softmax(t, b):
            for i in range(2):
                for rt in range(MLA_QB // LANES):
                    ra = 2 * i * MLA_QB + rt * LANES
                    rb = ra + MLA_QB
                    ps, _ = _softmax_tile([sbuf[b][ra:ra + LANES, :], sbuf[b][rb:rb + LANES, :]],
                                          scale=MLA_SCALE)
                    pbuf[b][ra:ra + LANES, :] = ps[0]
                    pbuf[b][rb:rb + LANES, :] = ps[1]

        def values(t, b):
            jp, qb = split(t)
            outs = []
            for i in range(2):
                h = 2 * jp + i
                r0 = 2 * i * MLA_QB
                oc = (_dot(pbuf[b][r0:r0 + MLA_QB, :], kvaug_scr[h, 0:MLA_HALF, :])
                      + _dot(pbuf[b][r0 + MLA_QB:r0 + 2 * MLA_QB, :], kvaug_scr[h, MLA_HALF:, :]))
                outs.append(oc[:, :LANES] / oc[:, LANES:])
            merged = jnp.where(lane_lo_m, pltpu.roll(outs[0], 64, 1), outs[1])
            mixc_scr[jp, _ds(qb * MLA_QB, MLA_QB), :] = merged.astype(BF16)

        _pipeline(2 * n_qb, scores, softmax, values)
        for jp in range(2):
            mix_scr[:, M_C + jp * LANES:M_C + (jp + 1) * LANES] = mixc_scr[jp]

    pl.run_scoped(
        latent,
        pltpu.VMEM((MLA_HEADS, TOK, LANES), BF16),
        pltpu.VMEM((MLA_HEADS, MLA_KEYS, LANES), BF16),
        pltpu.VMEM((MLA_HEADS, MLA_KEYS, 2 * LANES), BF16),
        pltpu.VMEM((2, TOK, LANES), BF16),
    )
    wc.convert()
    xs.epilogue(mod_ref, wout_ref, mix_scr)


def _lat_mixer(layer, x, mod, n1, w_in_p, sink, decay, gn, kvn, w_kv_b, w_out_p,
               ck, cv, cckv, ckr, s0, rope_a, rope_m, w_up, w_down):
    n_tok = x.shape[0]
    n_seq = n_tok // DEC_SEQ
    assert n_seq == 2 * FF_PIECES
    hbm_spec = pl.BlockSpec(memory_space=pl.ANY)
    seq_spec = lambda shape: pl.BlockSpec(
        (None, None) + shape, lambda i: (i, layer) + (0,) * len(shape))
    assert WIN_ROWS == 4 * MLA_QB and KEYS_WIN == MLA_HALF
    return pl.pallas_call(
        functools.partial(_lat_mixer_kernel, layer),
        out_shape=(jax.ShapeDtypeStruct((n_tok, D_MODEL), F32),
                   jax.ShapeDtypeStruct((D_MODEL, D_FF), BF16),
                   jax.ShapeDtypeStruct((D_FF, D_MODEL), BF16)),
        grid=(n_seq,),
        in_specs=[
            hbm_spec,
            pl.BlockSpec((None, N_MOD, D_MODEL), lambda i: (i, 0, 0)),
            _const_spec((1, D_MODEL)),
            _const_spec((D_IN_P, D_MODEL), layer),
            _smem_spec(),
            _smem_spec(),
            _const_spec((1, RET_HEADS * HEAD_DIM)),
            _const_spec((1, MLA_KV_RANK)),
            _const_spec((MLA_KV_RANK, MLA_HEADS * LANES), layer),
            _const_spec((D_MODEL, D_MODEL), layer),
            seq_spec((PAST_LEN, LANES)),
            seq_spec((PAST_LEN, LANES)),
            seq_spec((PAST_LEN, MLA_KV_RANK)),
            seq_spec((PAST_LEN, LANES)),
            seq_spec((2, 2, LANES, LANES)),
        ] + [_const_spec((DEC_SEQ, LANES))] * 6 + [hbm_spec, hbm_spec],
        out_specs=(hbm_spec, hbm_spec, hbm_spec),
        scratch_shapes=[
            pltpu.VMEM((TOK, D_MODEL), BF16),
            pltpu.VMEM((TOK, D_MODEL), BF16),
            pltpu.VMEM((WIN_ROWS, KEYS_WIN), F32), pltpu.VMEM((WIN_ROWS, KEYS_WIN), F32),
            pltpu.VMEM((WIN_ROWS, KEYS_WIN), BF16), pltpu.VMEM((WIN_ROWS, KEYS_WIN), BF16),
        ] + _XStream.SCRATCH + _WeightCast.SCRATCH,
        compiler_params=pltpu.CompilerParams(
            dimension_semantics=("arbitrary",), vmem_limit_bytes=VMEM_LIMIT),
        name="lat_mixer",
    )(x, mod, n1, w_in_p, sink, decay, gn, kvn, w_kv_b, w_out_p, ck, cv, cckv, ckr, s0,
      *rope_a, *rope_m, w_up, w_down)


def _mlp_kernel(final, x_ref, mod_ref, n2_ref, wup_ref, wdn_ref, fn_ref, o_ref):
    x = x_ref[...]
    h2 = _norm_mod(x, n2_ref[...], mod_ref[4:5, :], mod_ref[3:4, :]).astype(BF16)
    acc = None
    for c in range(D_FF // FF_CHUNK):
        cols = slice(c * FF_CHUNK, (c + 1) * FF_CHUNK)
        u = jnp.maximum(_dot(h2, wup_ref[:, cols]), 0.0)
        part = _dot((u * u).astype(BF16), wdn_ref[cols, :])
        acc = part if acc is None else acc + part
    y = x + mod_ref[5:6, :] * acc
    if final:
        y = y * lax.rsqrt(jnp.mean(y * y, axis=-1, keepdims=True) + EPS) * fn_ref[...]
    o_ref[...] = y


def _mlp(x, mod, n2, w_up, w_down, final_norm, final):
    n_tok = x.shape[0]
    per_mod = n_tok // mod.shape[0] // MLP_ROWS
    return pl.pallas_call(
        functools.partial(_mlp_kernel, final),
        out_shape=jax.ShapeDtypeStruct((n_tok, D_MODEL), F32),
        grid=(n_tok // MLP_ROWS,),
        in_specs=[
            pl.BlockSpec((MLP_ROWS, D_MODEL), lambda i: (i, 0)),
            pl.BlockSpec((None, N_MOD, D_MODEL), lambda i: (i // per_mod, 0, 0)),
            _const_spec((1, D_MODEL)),
            _const_spec((D_MODEL, D_FF)),
            _const_spec((D_FF, D_MODEL)),
            _const_spec((1, D_MODEL)),
        ],
        out_specs=pl.BlockSpec((MLP_ROWS, D_MODEL), lambda i: (i, 0)),
        compiler_params=pltpu.CompilerParams(
            dimension_semantics=("arbitrary",), vmem_limit_bytes=VMEM_LIMIT),
        name="mlp",
    )(x, mod, n2, w_up, w_down, final_norm)


def kernel(x_prompt, x_sample, cache_win_k, cache_win_v, cache_mla_ckv, cache_mla_krope, state_ret,
           c, c_ctx, w_mod, b_mod, norm1, norm2, w_in, win_sink, ret_decay, ret_gn, mla_kv_norm,
           w_kv_b, w_out, w_up, w_down, final_norm):
    n_ctx, n_lat = x_prompt.shape[0], x_sample.shape[0]

    w_in_p, w_out_p = _proj_layout(jnp.swapaxes(w_in, 1, 2), w_out)
    w_kv_b16 = w_kv_b.astype(BF16)

    c_rows = jnp.zeros((16, D_MODEL), F32).at[0].set(c_ctx).at[1:1 + n_lat].set(c)
    mod = _modulation(c_rows, w_mod, b_mod).reshape(DEPTH, 16, N_MOD, D_MODEL)

    rope_a = _rope_tables(DEC_SEQ, HEAD_DIM, 0, HEAD_DIM)
    rope_m = _rope_tables(DEC_SEQ, MLA_ROPE, MLA_NOPE, LANES)

    ck = cache_win_k.reshape(n_lat, DEPTH, PAST_LEN, LANES)
    cv = cache_win_v.reshape(n_lat, DEPTH, PAST_LEN, LANES)
    ckr = jnp.pad(cache_mla_krope, ((0, 0), (0, 0), (0, 0), (MLA_NOPE, LANES - MLA_NOPE - MLA_ROPE)))
    sr = state_ret.reshape(n_lat, DEPTH, 2, 2, 2, HEAD_DIM, HEAD_DIM)
    zero = jnp.zeros_like(sr[:, :, :, :, 0])
    s0 = jnp.concatenate([jnp.concatenate([sr[:, :, :, :, 0], zero], axis=-1),
                          jnp.concatenate([zero, sr[:, :, :, :, 1]], axis=-1)], axis=-2)

    xp = x_prompt.reshape(n_ctx * SEQ, D_MODEL)
    xs = x_sample.reshape(n_lat * DEC_SEQ, D_MODEL)
    state = ()
    for l in range(DEPTH):
        last = l == DEPTH - 1
        shared = (norm1[l][None], w_in_p, win_sink[l], ret_decay[l], ret_gn[l][None],
                  mla_kv_norm[l][None], w_kv_b16, w_out_p)
        mod_ctx, mod_lat = mod[l, 0:1], mod[l, 1:1 + n_lat]
        xs, w_up16, w_down16 = _lat_mixer(l, xs, mod_lat, *shared, ck, cv, cache_mla_ckv, ckr, s0,
                                          rope_a, rope_m, w_up, w_down)
        xp, *state = _ctx_mixer(l, xp, mod_ctx[0], *shared, state)
        xp = _mlp(xp, mod_ctx, norm2[l][None], w_up16, w_down16, final_norm[None], last)
        xs = _mlp(xs, mod_lat, norm2[l][None], w_up16, w_down16, final_norm[None], last)
    new_k, new_v, new_ckv, new_kr, new_s = state
    return (xp.reshape(n_ctx, SEQ, D_MODEL), xs.reshape(n_lat, DEC_SEQ, D_MODEL),
            new_k.reshape(n_ctx, DEPTH, SEQ, 2, HEAD_DIM), new_v.reshape(n_ctx, DEPTH, SEQ, 2, HEAD_DIM),
            new_ckv, new_kr, new_s)
```

```python
import functools

import numpy as np
import jax
import jax.numpy as jnp
from jax import lax
from jax.experimental import pallas as pl
from jax.experimental.pallas import tpu as pltpu

F32 = jnp.float32
BF16 = jnp.bfloat16

D_MODEL = 1024
DEPTH = 2
SEQ = 256
DEC_SEQ = 1024
PAST_LEN = 256
GRID_W = 64
HEAD_DIM = 64
ROPE_BASE = 10000.0
EPS = 1e-6
WIN_HEADS = 8
WINDOW = 128
ATTN_SCALE = HEAD_DIM ** -0.5
RET_HEADS = 4
RET_CHUNK = 128
RET_K_SCALE = HEAD_DIM ** -0.5
MLA_HEADS = 4
MLA_NOPE = 64
MLA_ROPE = 32
MLA_KV_RANK = 128
MLA_QK = MLA_NOPE + MLA_ROPE
MLA_SCALE = MLA_QK ** -0.5
D_IN = 2336
D_FF = 4 * D_MODEL
N_MOD = 6

LANES = 128
TOK = 1024
MLP_ROWS = 1024
FF_CHUNK = 1024
VMEM_LIMIT = 60 * 1024 * 1024

C_QA, C_KA, C_VA, C_QB, C_KB, C_VB, C_GB, C_QC, C_CKV, C_KR, D_IN_P = (
    0, 512, 640, 768, 1024, 1280, 1536, 1792, 2304, 2432, 2560)
M_A, M_B, M_C = 0, 512, 768

NT_DIMS = (((1,), (1,)), ((), ()))


def _in_proj_columns():
    idx = []
    for g in range(4):
        idx += list(range(g * 64, (g + 1) * 64)) + list(range((4 + g) * 64, (5 + g) * 64))
    idx += list(range(512, 1792))
    for h in range(MLA_HEADS):
        idx += list(range(1792 + h * MLA_QK, 1792 + (h + 1) * MLA_QK)) + [-1] * 32
    idx += list(range(2176, 2304))
    idx += [-1] * 64 + list(range(2304, 2336)) + [-1] * 32
    return np.asarray(idx, np.int32)


def _mix_rows():
    idx = []
    for g in range(4):
        idx += list(range(g * 64, (g + 1) * 64)) + list(range((4 + g) * 64, (5 + g) * 64))
    idx += list(range(512, 1024))
    return np.asarray(idx, np.int32)


def _take_runs(w, idx, axis):
    pieces, i = [], 0
    while i < len(idx):
        j = i + 1
        if idx[i] < 0:
            while j < len(idx) and idx[j] < 0:
                j += 1
            shape = list(w.shape)
            shape[axis] = j - i
            pieces.append(jnp.zeros(shape, w.dtype))
        else:
            while j < len(idx) and idx[j] == idx[j - 1] + 1:
                j += 1
            pieces.append(lax.slice_in_dim(w, int(idx[i]), int(idx[j - 1]) + 1, axis=axis))
        i = j
    return jnp.concatenate(pieces, axis=axis)


def _rope_tables(n_tokens, dim, lane0, period):
    quarter = dim // 4
    t = np.arange(n_tokens)
    row = (t // GRID_W).astype(np.float64)
    col = (t % GRID_W).astype(np.float64)
    inv_freq = ROPE_BASE ** (-np.arange(quarter, dtype=np.float64) / quarter)
    ar, ac = row[:, None] * inv_freq, col[:, None] * inv_freq
    cos = np.concatenate([np.cos(ar), np.cos(ar), np.cos(ac), np.cos(ac)], axis=-1)
    sin = np.concatenate([np.sin(ar), np.sin(ar), np.sin(ac), np.sin(ac)], axis=-1)
    first = np.tile(np.concatenate([np.ones(quarter), np.zeros(quarter)]), 2)
    c = np.ones((n_tokens, LANES))
    sa = np.zeros((n_tokens, LANES))
    sb = np.zeros((n_tokens, LANES))
    for start in range(lane0, LANES, period):
        c[:, start:start + dim] = cos
        sa[:, start:start + dim] = -sin * first
        sb[:, start:start + dim] = sin * (1.0 - first)
    return tuple(jnp.asarray(a, F32) for a in (c, sa, sb))


def _lane_iota(shape):
    return lax.broadcasted_iota(jnp.int32, shape, len(shape) - 1)


def _row_iota(shape):
    return lax.broadcasted_iota(jnp.int32, shape, len(shape) - 2)


def _ds(start, size):
    if isinstance(start, int):
        return pl.ds(start, size)
    return pl.ds(pl.multiple_of(start, LANES), size)


def _norm_mod(x, gain, scale, shift):
    ms = jnp.mean(x * x, axis=-1, keepdims=True)
    return (x * lax.rsqrt(ms + EPS) * gain) * (1.0 + scale) + shift


def _log_sigmoid(x):
    return -(jnp.maximum(-x, 0.0) + jnp.log1p(jnp.exp(-jnp.abs(x))))


def _rope(x, c, sa, sb, quarter):
    return (x * c + pltpu.roll(x, LANES - quarter, 1) * sa + pltpu.roll(x, quarter, 1) * sb)


def _dot(a, b):
    return jnp.dot(a, b, preferred_element_type=F32)


def _dot_nt(a, b):
    return lax.dot_general(a, b, NT_DIMS, preferred_element_type=F32)


def _mod_kernel(c_ref, w_ref, b_ref, o_ref):
    cv = c_ref[...]
    s = cv * jax.nn.sigmoid(cv)
    o_ref[0] = _dot(s.astype(BF16), w_ref[0].astype(BF16)) + b_ref[0]


def _modulation(c_rows, w_mod, b_mod):
    tn = 3072
    nj = (N_MOD * D_MODEL) // tn
    return pl.pallas_call(
        _mod_kernel,
        out_shape=jax.ShapeDtypeStruct((DEPTH, 16, N_MOD * D_MODEL), F32),
        grid=(DEPTH, nj),
        in_specs=[
            pl.BlockSpec((16, D_MODEL), lambda l, j: (0, 0)),
            pl.BlockSpec((1, D_MODEL, tn), lambda l, j: (l, 0, j)),
            pl.BlockSpec((1, 1, tn), lambda l, j: (l, 0, j)),
        ],
        out_specs=pl.BlockSpec((1, 16, tn), lambda l, j: (l, 0, j)),
        compiler_params=pltpu.CompilerParams(
            dimension_semantics=("arbitrary", "arbitrary"), vmem_limit_bytes=VMEM_LIMIT),
        name="modulation",
    )(c_rows, w_mod, b_mod.reshape(DEPTH, 1, N_MOD * D_MODEL))


W_IN_LANES = 512


def _proj_layout_kernel(win_ref, wout_ref, oin_ref, oout_ref):
    oin_ref[0] = _take_runs(win_ref[0], _in_proj_columns(), 0).astype(BF16)
    oout_ref[0] = _take_runs(wout_ref[0], _mix_rows(), 0).astype(BF16)


def _proj_layout(w_in_t, w_out):
    col_spec = lambda rows: pl.BlockSpec((1, rows, W_IN_LANES), lambda l, c: (l, 0, c))
    return pl.pallas_call(
        _proj_layout_kernel,
        out_shape=(jax.ShapeDtypeStruct((DEPTH, D_IN_P, D_MODEL), BF16),
                   jax.ShapeDtypeStruct((DEPTH, D_MODEL, D_MODEL), BF16)),
        grid=(DEPTH, D_MODEL // W_IN_LANES),
        in_specs=[col_spec(D_IN), col_spec(D_MODEL)],
        out_specs=(col_spec(D_IN_P), col_spec(D_MODEL)),
        compiler_params=pltpu.CompilerParams(
            dimension_semantics=("arbitrary", "arbitrary"), vmem_limit_bytes=VMEM_LIMIT),
        name="proj_layout",
    )(w_in_t, w_out)


def _pipeline(n_steps, scores, softmax, values):
    scores(0, 0)
    scores(1, 1)
    softmax(0, 0)

    def body(i, carry):
        t = 2 * i + 1
        scores(t + 1, 0)
        softmax(t, 1)
        values(t - 1, 0)
        scores(t + 2, 1)
        softmax(t + 1, 0)
        values(t, 1)
        return carry

    lax.fori_loop(0, n_steps // 2 - 1, body, 0)
    softmax(n_steps - 1, 1)
    values(n_steps - 2, 0)
    values(n_steps - 1, 1)


def _softmax_tile(parts, floor=None, scale=None):
    m = None
    for s in parts:
        pm = jnp.max(s, axis=-1, keepdims=True)
        m = pm if m is None else jnp.maximum(m, pm)
    if floor is not None:
        m = jnp.maximum(m, floor)
    if scale is None:
        ps = [jnp.exp(s - m).astype(BF16) for s in parts]
    else:
        ps = [jnp.exp((s - m) * scale).astype(BF16) for s in parts]
    extra = None if floor is None else jnp.exp(floor - m)
    return ps, extra


def _retention_tables(decay_ref, rt_scr, m2_scr):
    shape = (RET_CHUNK, LANES)
    lane_lo = _lane_iota(shape) < 64
    row = _row_iota(shape)
    row_lo = row < 64
    i = row.astype(F32)
    rel = i - _lane_iota(shape).astype(F32)
    for j in range(2):
        df0, df1 = decay_ref[0, 2 * j], decay_ref[0, 2 * j + 1]
        db0, db1 = decay_ref[1, 2 * j], decay_ref[1, 2 * j + 1]
        lgf = _log_sigmoid(jnp.where(lane_lo, df0, df1))
        lgb = _log_sigmoid(jnp.where(lane_lo, db0, db1))
        rt_scr[j, 0] = jnp.exp((i + 1.0) * lgf)
        rt_scr[j, 1] = jnp.exp((RET_CHUNK - i) * lgb)
        rt_scr[j, 2] = jnp.exp((RET_CHUNK - 1.0 - i) * lgf)
        rt_scr[j, 3] = jnp.exp(i * lgb)
        rt_scr[j, 4] = jnp.exp(RET_CHUNK * _log_sigmoid(jnp.where(row_lo, df0, df1)))
        rt_scr[j, 5] = jnp.exp(RET_CHUNK * _log_sigmoid(jnp.where(row_lo, db0, db1)))
        for hh, (df, db) in enumerate(((df0, db0), (df1, db1))):
            lf = _log_sigmoid(jnp.full(shape, df, F32))
            lb = _log_sigmoid(jnp.full(shape, db, F32))
            low, upp = rel >= 0.0, rel <= 0.0
            m = (jnp.where(low, jnp.exp(jnp.where(low, rel, 0.0) * lf), 0.0)
                 + jnp.where(upp, jnp.exp(jnp.where(upp, -rel, 0.0) * lb), 0.0))
            m2_scr[j, :, hh * RET_CHUNK:(hh + 1) * RET_CHUNK] = m


N_CHUNK = TOK // RET_CHUNK


def _ret_scratch(n_seq, chunks_per_stage):
    per_chunk = lambda dtype: pltpu.VMEM((N_CHUNK, 2, 2 * RET_CHUNK, LANES), dtype)
    stage = lambda dtype: pltpu.VMEM((2 * chunks_per_stage, RET_CHUNK, 2 * LANES), dtype)
    return [
        pltpu.VMEM((TOK, 2 * LANES), BF16),
        pltpu.VMEM((TOK, 4 * LANES), BF16),
        pltpu.VMEM((TOK, 2 * LANES), F32),
        per_chunk(BF16),
        pltpu.VMEM((TOK, 2 * LANES), BF16),
        per_chunk(BF16),
        pltpu.VMEM((TOK, 2 * LANES), F32),
        pltpu.VMEM((2, 6, RET_CHUNK, LANES), F32),
        pltpu.VMEM((2, RET_CHUNK, 2 * LANES), F32),
        per_chunk(F32),
        per_chunk(BF16),
        pltpu.VMEM((n_seq, 2, 2, LANES, LANES), F32),
        stage(F32), stage(F32), stage(BF16), stage(BF16),
    ]


class _Retention:
    def __init__(self, scratch, gn_ref, mix_scr):
        (self.q, self.qd, self.k, self.kbd, self.v, self.vbd, self.g, self.rt, self.m2, self.upd,
         self.sall, self.st, rs0, rs1, rp0, rp1) = scratch
        self.rs, self.rp = (rs0, rs1), (rp0, rp1)
        self.gn_ref, self.mix = gn_ref, mix_scr

    def prepare(self, proj, decay_ref, seq_chunks):
        _retention_tables(decay_ref, self.rt, self.m2)
        lane_lo = _lane_iota((TOK, LANES)) < 64
        per_chunk = lambda a: a.reshape(N_CHUNK, RET_CHUNK, LANES)
        q = proj(C_QB, C_KB)
        k = proj(C_KB, C_VB) * RET_K_SCALE
        v = proj(C_VB, C_GB)
        self.g[...] = proj(C_GB, C_QC)
        self.q[...] = q.astype(BF16)
        self.k[...] = k
        self.v[...] = v.astype(BF16)
        for j in range(2):
            cols = slice(j * LANES, (j + 1) * LANES)
            q3 = per_chunk(q[:, cols])
            for d in range(2):
                self.qd[:, (2 * j + d) * LANES:(2 * j + d + 1) * LANES] = (
                    (q3 * self.rt[j, d]).reshape(TOK, LANES).astype(BF16))
            for src, dst in ((k[:, cols], self.kbd), (v[:, cols], self.vbd)):
                dst[:, j, :RET_CHUNK, :] = per_chunk(jnp.where(lane_lo, src, 0.0).astype(BF16))
                dst[:, j, RET_CHUNK:, :] = per_chunk(jnp.where(lane_lo, 0.0, src).astype(BF16))

        blockdiag2 = _lane_iota((2 * RET_CHUNK, LANES)) < 64
        blockdiag2 = blockdiag2 == ((_row_iota((2 * RET_CHUNK, LANES)) & (RET_CHUNK - 1)) < 64)

        def upd_body(c, carry):
            rows = _ds(c * RET_CHUNK, RET_CHUNK)
            for j in range(2):
                cols = slice(j * LANES, (j + 1) * LANES)
                k2 = self.k[rows, cols]
                kd = jnp.concatenate([k2 * self.rt[j, 2], k2 * self.rt[j, 3]], axis=1)
                upd = _dot(kd.T.astype(BF16), self.v[rows, cols])
                self.upd[c, j] = jnp.where(blockdiag2, upd, 0.0)
            return carry

        lax.fori_loop(0, N_CHUNK, upd_body, 0, unroll=4)

        n_per_seq = seq_chunks[0][1]
        assert all(n == n_per_seq for _, n in seq_chunks)

        def scan_body(t, carry):
            for si, (first, n) in enumerate(seq_chunks):
                cf, cb = first + t, first + n - 1 - t
                for j in range(2):
                    sf, sb = self.st[si, 0, j], self.st[si, 1, j]
                    self.sall[cf, j, :RET_CHUNK, :] = sf.astype(BF16)
                    self.sall[cb, j, RET_CHUNK:, :] = sb.astype(BF16)
                    self.st[si, 0, j] = self.rt[j, 4] * sf + self.upd[cf, j, :RET_CHUNK, :]
                    self.st[si, 1, j] = self.rt[j, 5] * sb + self.upd[cb, j, RET_CHUNK:, :]
            return carry

        lax.fori_loop(0, n_per_seq, scan_body, 0)

    def scores(self, chunks, b):
        for i, c in enumerate(chunks):
            rows = _ds(c * RET_CHUNK, RET_CHUNK)
            for j in range(2):
                self.rs[b][2 * i + j] = _dot_nt(self.q[rows, j * LANES:(j + 1) * LANES], self.kbd[c, j])

    def mask(self, chunks, b):
        for i in range(len(chunks)):
            for j in range(2):
                self.rp[b][2 * i + j] = (self.rs[b][2 * i + j] * self.m2[j]).astype(BF16)

    def values(self, chunks, b):
        lane_lo = _lane_iota((RET_CHUNK, LANES)) < 64
        for i, c in enumerate(chunks):
            rows = _ds(c * RET_CHUNK, RET_CHUNK)
            for j in range(2):
                cols = slice(j * LANES, (j + 1) * LANES)
                o = (_dot(self.rp[b][2 * i + j], self.vbd[c, j])
                     + _dot(self.qd[rows, 2 * j * LANES:2 * (j + 1) * LANES], self.sall[c, j]))
                s_lo = jnp.sum(jnp.where(lane_lo, o, 0.0), axis=-1, keepdims=True)
                s_hi = jnp.sum(jnp.where(lane_lo, 0.0, o), axis=-1, keepdims=True)
                d = o - jnp.where(lane_lo, s_lo, s_hi) * (1.0 / HEAD_DIM)
                dd = d * d
                v_lo = jnp.sum(jnp.where(lane_lo, dd, 0.0), axis=-1, keepdims=True)
                v_hi = jnp.sum(jnp.where(lane_lo, 0.0, dd), axis=-1, keepdims=True)
                var = jnp.where(lane_lo, v_lo, v_hi) * (1.0 / HEAD_DIM)
                g2 = self.g[rows, cols]
                y = d * lax.rsqrt(var + EPS) * self.gn_ref[:, cols] * (g2 * jax.nn.sigmoid(g2))
                self.mix[rows, M_B + j * LANES:M_B + (j + 1) * LANES] = y.astype(BF16)


X_ROWS = 256
N_XCHUNK = TOK // X_ROWS


class _XStream:
    SCRATCH = [pltpu.VMEM((N_XCHUNK, X_ROWS, D_MODEL), F32), pltpu.VMEM((2, X_ROWS, D_MODEL), F32),
               pltpu.SemaphoreType.DMA((N_XCHUNK,)), pltpu.SemaphoreType.DMA((2,))]

    def __init__(self, x_hbm, xo_hbm, scratch):
        self.x_hbm, self.xo_hbm = x_hbm, xo_hbm
        self.xin, self.xout, self.sem_in, self.sem_out = scratch
        self.step, self.n_steps = pl.program_id(0), pl.num_programs(0)

    @staticmethod
    def _rows(step, r):
        return pl.ds(pl.multiple_of(step * TOK + r * X_ROWS, X_ROWS), X_ROWS)

    def load(self, step, r):
        return pltpu.make_async_copy(self.x_hbm.at[self._rows(step, r), :], self.xin.at[r],
                                     self.sem_in.at[r])

    def store(self, r):
        return pltpu.make_async_copy(self.xout.at[r % 2], self.xo_hbm.at[self._rows(self.step, r), :],
                                     self.sem_out.at[r % 2])

    def prologue(self, mod_ref, n1_ref, h_scr):
        @pl.when(self.step == 0)
        def _():
            for r in range(N_XCHUNK):
                self.load(self.step, r).start()

        for r in range(N_XCHUNK):
            self.load(self.step, r).wait()
            h = _norm_mod(self.xin[r], n1_ref[...], mod_ref[1:2, :], mod_ref[0:1, :])
            h_scr[r * X_ROWS:(r + 1) * X_ROWS, :] = h.astype(BF16)

    def epilogue(self, mod_ref, wout_ref, mix_scr):
        for r in range(N_XCHUNK):
            if r >= 2:
                self.store(r - 2).wait()
            else:
                @pl.when(self.step > 0)
                def _():
                    self.store(r).wait()

            y = _dot(mix_scr[r * X_ROWS:(r + 1) * X_ROWS, :], wout_ref[...])
            self.xout[r % 2] = self.xin[r] + mod_ref[2:3, :] * y
            self.store(r).start(priority=1)

            @pl.when(self.step + 1 < self.n_steps)
            def _():
                self.load(self.step + 1, r).start()

        @pl.when(self.step + 1 == self.n_steps)
        def _():
            for r in range(N_XCHUNK - 2, N_XCHUNK):
                self.store(r).wait()


FF_PIECES = D_FF // D_MODEL
WEIGHT_DMA_PRIORITY = 1


class _WeightCast:
    SCRATCH = [pltpu.VMEM((D_MODEL, D_MODEL), F32), pltpu.VMEM((D_MODEL, D_MODEL), BF16),
               pltpu.SemaphoreType.DMA(()), pltpu.SemaphoreType.DMA(())]

    def __init__(self, layer, wup_hbm, wdn_hbm, oup_hbm, odn_hbm, scratch):
        self.layer, self.srcs, self.dsts = layer, (wup_hbm, wdn_hbm), (oup_hbm, odn_hbm)
        self.stage_in, self.stage_out, self.sem_in, self.sem_out = scratch
        self.step, self.n_steps = pl.program_id(0), pl.num_programs(0)

    def _piece(self, p):
        if p < FF_PIECES:
            cols = slice(p * D_MODEL, (p + 1) * D_MODEL)
            return self.srcs[0].at[self.layer, :, cols], self.dsts[0].at[:, cols]
        rows = slice((p - FF_PIECES) * D_MODEL, (p - FF_PIECES + 1) * D_MODEL)
        return self.srcs[1].at[self.layer, rows, :], self.dsts[1].at[rows, :]

    def _load(self, p):
        return pltpu.make_async_copy(self._piece(p)[0], self.stage_in, self.sem_in)

    def _store(self, p):
        return pltpu.make_async_copy(self.stage_out, self._piece(p)[1], self.sem_out)

    def fetch(self):
        for p in range(2 * FF_PIECES):
            @pl.when(self.step == p)
            def _():
                self._load(p).start(priority=WEIGHT_DMA_PRIORITY)

    def convert(self):
        self._load(0).wait()

        @pl.when(self.step > 0)
        def _():
            self._store(0).wait()

        self.stage_out[...] = self.stage_in[...].astype(BF16)
        for p in range(2 * FF_PIECES):
            @pl.when(self.step == p)
            def _():
                self._store(p).start(priority=WEIGHT_DMA_PRIORITY)

        @pl.when(self.step + 1 == self.n_steps)
        def _():
            self._store(0).wait()


CTX_PER_STEP = TOK // SEQ
CTX_WROWS = WIN_HEADS * SEQ
CTX_MROWS = MLA_HEADS * SEQ


def _ctx_mixer_kernel(n_alias, x_ref, mod_ref, n1_ref, win_ref, sink_ref, decay_ref, gn_ref, kvn_ref,
                      wkvb_ref, wout_ref, *refs):
    xo_ref, ko_ref, vo_ref, ckvo_ref, kro_ref, so_ref, h_scr, mix_scr = refs[n_alias:n_alias + 8]
    xs = _XStream(x_ref, xo_ref, refs[n_alias + 8:])
    xs.prologue(mod_ref, n1_ref, h_scr)
    per_seq = lambda a: a.reshape(CTX_PER_STEP, SEQ, a.shape[-1])
    if n_alias == 0:
        for ref in (ko_ref, vo_ref, ckvo_ref, kro_ref, so_ref):
            ref[:, 1:] = jnp.zeros((ref.shape[0], ref.shape[1] - 1) + ref.shape[2:], F32)
        ko_ref, vo_ref, ckvo_ref, kro_ref, so_ref = (
            ref.at[:, 0] for ref in (ko_ref, vo_ref, ckvo_ref, kro_ref, so_ref))
    chunks_per_seq = SEQ // RET_CHUNK

    def proj(c0, c1):
        return _dot_nt(h_scr[...], win_ref[c0:c1, :])

    ret = []
    seq_chunks = lambda e: [chunks_per_seq * e + c for c in range(chunks_per_seq)]

    def attention(qst_scr, k_scr, vaug_scr, qc_scr, kcat_scr, kvaug_scr,
                  sw0, sw1, pw0, pw1, ew0, ew1, sm0, sm1, pm0, pm1):
        lane_lo = _lane_iota((TOK, LANES)) < 64
        ones = jnp.ones((TOK, LANES), BF16)
        qa = proj(C_QA, C_KA) * ATTN_SCALE
        for g in range(4):
            q = qa[:, g * LANES:(g + 1) * LANES]
            lo = jnp.where(lane_lo, q, 0.0).astype(BF16).reshape(CTX_PER_STEP, SEQ, LANES)
            hi = jnp.where(lane_lo, 0.0, q).astype(BF16).reshape(CTX_PER_STEP, SEQ, LANES)
            qst_scr[:, g * SEQ:(g + 1) * SEQ, :] = lo
            qst_scr[:, (4 + g) * SEQ:(5 + g) * SEQ, :] = hi
        kva = proj(C_KA, C_QB)
        ko_ref[...] = per_seq(kva[:, :LANES])
        vo_ref[...] = per_seq(kva[:, LANES:])
        k_scr[...] = kva[:, :LANES].astype(BF16)
        vaug_scr[:, :LANES] = kva[:, LANES:].astype(BF16)
        vaug_scr[:, LANES:] = ones
        qc = proj(C_QC, C_CKV).astype(BF16)
        for h in range(MLA_HEADS):
            qc_scr[h] = qc[:, h * LANES:(h + 1) * LANES]
        ckr = proj(C_CKV, D_IN_P)
        ckv, kr = ckr[:, :LANES], ckr[:, LANES:]
        ckv_n = ckv * lax.rsqrt(jnp.mean(ckv * ckv, axis=-1, keepdims=True) + EPS) * kvn_ref[...]
        ckvo_ref[...] = per_seq(ckv_n)
        kro_ref[...] = per_seq(kr[:, 64:64 + MLA_ROPE])
        kv = _dot(ckv_n.astype(BF16), wkvb_ref[...])
        for h in range(MLA_HEADS):
            kvh = kv[:, h * LANES:(h + 1) * LANES]
            kcat_scr[h] = jnp.where(lane_lo, kvh, kr).astype(BF16)
            kvaug_scr[h, :, :LANES] = kvh.astype(BF16)
            kvaug_scr[h, :, LANES:] = ones

        sw, pw, ew, sm, pm = (sw0, sw1), (pw0, pw1), (ew0, ew1), (sm0, sm1), (pm0, pm1)
        lane_lo_s = _lane_iota((SEQ, LANES)) < 64

        def scores(e, b):
            rows = _ds(e * SEQ, SEQ)
            sw[b][...] = _dot_nt(qst_scr[e], k_scr[rows, :])
            for h in range(MLA_HEADS):
                sm[b][h * SEQ:(h + 1) * SEQ, :] = _dot_nt(qc_scr[h, rows, :], kcat_scr[h, rows, :])
            ret[0].scores(seq_chunks(e), b)

        def softmax(e, b):
            for h in range(WIN_HEADS):
                hr = slice(h * SEQ, (h + 1) * SEQ)
                sink = jnp.full((SEQ, 1), sink_ref[h], F32)
                (p,), extra = _softmax_tile([sw[b][hr, :]], floor=sink)
                pw[b][hr, :] = p
                ew[b][hr, :] = jnp.broadcast_to(extra, (SEQ, LANES))
            for h in range(MLA_HEADS):
                hr = slice(h * SEQ, (h + 1) * SEQ)
                (p,), _ = _softmax_tile([sm[b][hr, :]], scale=MLA_SCALE)
                pm[b][hr, :] = p
            ret[0].mask(seq_chunks(e), b)

        def values(e, b):
            ret[0].values(seq_chunks(e), b)
            rows = _ds(e * SEQ, SEQ)
            oa = _dot(pw[b][...], vaug_scr[rows, :])
            o = oa[:, :LANES] / (oa[:, LANES:] + ew[b][...])
            for g in range(4):
                merged = jnp.where(lane_lo_s, o[g * SEQ:(g + 1) * SEQ], o[(4 + g) * SEQ:(5 + g) * SEQ])
                mix_scr[rows, M_A + g * LANES:M_A + (g + 1) * LANES] = merged.astype(BF16)
            for jp in range(2):
                outs = []
                for h in (2 * jp, 2 * jp + 1):
                    oc = _dot(pm[b][h * SEQ:(h + 1) * SEQ, :], kvaug_scr[h, rows, :])
                    outs.append(oc[:, :LANES] / oc[:, LANES:])
                merged = jnp.where(lane_lo_s, pltpu.roll(outs[0], 64, 1), outs[1])
                mix_scr[rows, M_C + jp * LANES:M_C + (jp + 1) * LANES] = merged.astype(BF16)

        _pipeline(CTX_PER_STEP, scores, softmax, values)

    attention_scratch = (
        pltpu.VMEM((CTX_PER_STEP, CTX_WROWS, LANES), BF16),
        pltpu.VMEM((TOK, LANES), BF16),
        pltpu.VMEM((TOK, 2 * LANES), BF16),
        pltpu.VMEM((MLA_HEADS, TOK, LANES), BF16),
        pltpu.VMEM((MLA_HEADS, TOK, LANES), BF16),
        pltpu.VMEM((MLA_HEADS, TOK, 2 * LANES), BF16),
        pltpu.VMEM((CTX_WROWS, SEQ), F32), pltpu.VMEM((CTX_WROWS, SEQ), F32),
        pltpu.VMEM((CTX_WROWS, SEQ), BF16), pltpu.VMEM((CTX_WROWS, SEQ), BF16),
        pltpu.VMEM((CTX_WROWS, LANES), F32), pltpu.VMEM((CTX_WROWS, LANES), F32),
        pltpu.VMEM((CTX_MROWS, SEQ), F32), pltpu.VMEM((CTX_MROWS, SEQ), F32),
        pltpu.VMEM((CTX_MROWS, SEQ), BF16), pltpu.VMEM((CTX_MROWS, SEQ), BF16),
    )

    def retention(*scratch):
        r = _Retention(scratch, gn_ref, mix_scr)
        r.st[...] = jnp.zeros(r.st.shape, F32)
        r.prepare(proj, decay_ref, [(chunks_per_seq * e, chunks_per_seq) for e in range(CTX_PER_STEP)])
        for e in range(CTX_PER_STEP):
            for dirn in range(2):
                for j in range(2):
                    st = r.st[e, dirn, j]
                    so_ref[e, dirn, 2 * j] = st[:64, :64]
                    so_ref[e, dirn, 2 * j + 1] = st[64:, 64:]
        ret.append(r)
        pl.run_scoped(attention, *attention_scratch)

    pl.run_scoped(retention, *_ret_scratch(CTX_PER_STEP, chunks_per_seq))
    xs.epilogue(mod_ref, wout_ref, mix_scr)


def _const_spec(shape, layer=None):
    if layer is None:
        return pl.BlockSpec(shape, lambda i: (0,) * len(shape), pipeline_mode=pl.Buffered(1))
    return pl.BlockSpec((None,) + shape, lambda i: (layer,) + (0,) * len(shape),
                        pipeline_mode=pl.Buffered(1))


def _smem_spec():
    return pl.BlockSpec(memory_space=pltpu.SMEM)


def _ctx_mixer(layer, x, mod, n1, w_in_p, sink, decay, gn, kvn, w_kv_b, w_out_p, prev_state):
    n_tok = x.shape[0]
    n_seq = n_tok // SEQ
    hbm_spec = pl.BlockSpec(memory_space=pl.ANY)
    if prev_state:
        state_spec = lambda *tail: pl.BlockSpec(
            (CTX_PER_STEP, None) + tail, lambda i: (i, layer) + (0,) * len(tail))
    else:
        assert layer == 0
        state_spec = lambda *tail: pl.BlockSpec(
            (CTX_PER_STEP, DEPTH) + tail, lambda i: (i, 0) + (0,) * len(tail))
    state_tails = [(SEQ, LANES), (SEQ, LANES), (SEQ, MLA_KV_RANK), (SEQ, MLA_ROPE),
                   (2, RET_HEADS, HEAD_DIM, HEAD_DIM)]
    n_in = 10
    return pl.pallas_call(
        functools.partial(_ctx_mixer_kernel, len(prev_state)),
        out_shape=[jax.ShapeDtypeStruct((n_tok, D_MODEL), F32)] + [
            jax.ShapeDtypeStruct((n_seq, DEPTH) + tail, F32) for tail in state_tails],
        grid=(n_tok // TOK,),
        in_specs=[
            hbm_spec,
            _const_spec((N_MOD, D_MODEL)),
            _const_spec((1, D_MODEL)),
            _const_spec((D_IN_P, D_MODEL), layer),
            _smem_spec(),
            _smem_spec(),
            _const_spec((1, RET_HEADS * HEAD_DIM)),
            _const_spec((1, MLA_KV_RANK)),
            _const_spec((MLA_KV_RANK, MLA_HEADS * LANES), layer),
            _const_spec((D_MODEL, D_MODEL), layer),
        ] + [hbm_spec] * len(prev_state),
        out_specs=[hbm_spec] + [state_spec(*tail) for tail in state_tails],
        input_output_aliases={n_in + k: 1 + k for k in range(len(prev_state))},
        scratch_shapes=[
            pltpu.VMEM((TOK, D_MODEL), BF16),
            pltpu.VMEM((TOK, D_MODEL), BF16),
        ] + _XStream.SCRATCH,
        compiler_params=pltpu.CompilerParams(
            dimension_semantics=("arbitrary",), vmem_limit_bytes=VMEM_LIMIT),
        name="ctx_mixer",
    )(x, mod, n1, w_in_p, sink, decay, gn, kvn, w_kv_b, w_out_p, *prev_state)


N_BLK = DEC_SEQ // LANES
KEYS_LOC = 3 * LANES
KEYS_WIN = KEYS_LOC + PAST_LEN
WIN_ROWS = WIN_HEADS * LANES
MLA_QB = 256
MLA_KEYS = DEC_SEQ + PAST_LEN
MLA_HALF = MLA_KEYS // 2


def _lat_mixer_kernel(layer, x_ref, mod_ref, n1_ref, win_ref, sink_ref, decay_ref, gn_ref, kvn_ref,
                      wkvb_ref, wout_ref, ck_ref, cv_ref, cckv_ref, ckr_ref, s0_ref,
                      rc_ref, rsa_ref, rsb_ref, mc_ref, msa_ref, msb_ref, wup_ref, wdn_ref,
                      xo_ref, wup16_ref, wdn16_ref,
                      h_scr, mix_scr, s0_scr, s1_scr, p0_scr, p1_scr, *dma_scratch):
    xs = _XStream(x_ref, xo_ref, dma_scratch[:len(_XStream.SCRATCH)])
    wc = _WeightCast(layer, wup_ref, wdn_ref, wup16_ref, wdn16_ref,
                     dma_scratch[len(_XStream.SCRATCH):])
    xs.prologue(mod_ref, n1_ref, h_scr)
    wc.fetch()
    sbuf, pbuf = (s0_scr, s1_scr), (p0_scr, p1_scr)

    def proj(c0, c1):
        return _dot_nt(h_scr[...], win_ref[c0:c1, :])

    ret = []

    def window(qst_scr, kpad_scr, vaug_scr, ckb_scr, cvaug_scr, bias_scr, e0_scr, e1_scr):
        ebuf = (e0_scr, e1_scr)
        lane_lo = _lane_iota((TOK, LANES)) < 64
        rc, rsa, rsb = rc_ref[...], rsa_ref[...], rsb_ref[...]
        qa = proj(C_QA, C_KA)
        for g in range(4):
            q = _rope(qa[:, g * LANES:(g + 1) * LANES], rc, rsa, rsb, 16) * ATTN_SCALE
            lo = jnp.where(lane_lo, q, 0.0).astype(BF16).reshape(N_BLK, LANES, LANES)
            hi = jnp.where(lane_lo, 0.0, q).astype(BF16).reshape(N_BLK, LANES, LANES)
            qst_scr[:, g * LANES:(g + 1) * LANES, :] = lo
            qst_scr[:, (4 + g) * LANES:(5 + g) * LANES, :] = hi
        kva = proj(C_KA, C_QB)
        zpad = jnp.zeros((LANES, LANES), BF16)
        kpad_scr[0:LANES, :] = zpad
        kpad_scr[LANES + TOK:, :] = zpad
        vaug_scr[0:LANES, :LANES] = zpad
        vaug_scr[LANES + TOK:, :LANES] = zpad
        kpad_scr[LANES:LANES + TOK, :] = _rope(kva[:, :LANES], rc, rsa, rsb, 16).astype(BF16)
        vaug_scr[LANES:LANES + TOK, :LANES] = kva[:, LANES:].astype(BF16)
        vaug_scr[:, LANES:] = jnp.ones((TOK + 2 * LANES, LANES), BF16)
        ckb_scr[...] = ck_ref[...].astype(BF16)
        cvaug_scr[:, :LANES] = cv_ref[...].astype(BF16)
        cvaug_scr[:, LANES:] = jnp.ones((PAST_LEN, LANES), BF16)
        qi, kj = _row_iota((LANES, LANES)), _lane_iota((LANES, LANES))
        bias_scr[0] = jnp.full((LANES, LANES), -jnp.inf, F32)
        bias_scr[1] = jnp.where(kj >= qi, 0.0, -jnp.inf)
        bias_scr[2] = jnp.where(kj <= qi, 0.0, -jnp.inf)
        lane_lo_b = _lane_iota((LANES, LANES)) < 64

        def scores(n, b):
            q = qst_scr[n]
            sbuf[b][:, :KEYS_LOC] = _dot_nt(q, kpad_scr[_ds(n * LANES, KEYS_LOC), :])
            sbuf[b][:, KEYS_LOC:] = _dot_nt(q, ckb_scr[...])
            ret[0].scores([n], b)

        def softmax(n, b):
            if isinstance(n, int):
                i_prev, i_next = (1 if n > 0 else 0), (2 if n < N_BLK - 1 else 0)
            else:
                i_prev, i_next = jnp.where(n > 0, 1, 0), jnp.where(n < N_BLK - 1, 2, 0)
            b_prev, b_next = bias_scr[i_prev], bias_scr[i_next]
            for h in range(WIN_HEADS):
                hr = slice(h * LANES, (h + 1) * LANES)
                parts = [sbuf[b][hr, 0:LANES] + b_prev, sbuf[b][hr, LANES:2 * LANES],
                         sbuf[b][hr, 2 * LANES:KEYS_LOC] + b_next, sbuf[b][hr, KEYS_LOC:]]
                ps, extra = _softmax_tile(parts, floor=jnp.full((LANES, 1), sink_ref[h], F32))
                pbuf[b][hr, 0:LANES] = ps[0]
                pbuf[b][hr, LANES:2 * LANES] = ps[1]
                pbuf[b][hr, 2 * LANES:KEYS_LOC] = ps[2]
                pbuf[b][hr, KEYS_LOC:] = ps[3]
                ebuf[b][hr, :] = jnp.broadcast_to(extra, (LANES, LANES))
            ret[0].mask([n], b)

        def values(n, b):
            ret[0].values([n], b)
            oa = (_dot(pbuf[b][:, :KEYS_LOC], vaug_scr[_ds(n * LANES, KEYS_LOC), :])
                  + _dot(pbuf[b][:, KEYS_LOC:], cvaug_scr[...]))
            o = oa[:, :LANES] / (oa[:, LANES:] + ebuf[b][...])
            for g in range(4):
                merged = jnp.where(lane_lo_b, o[g * LANES:(g + 1) * LANES],
                                   o[(4 + g) * LANES:(5 + g) * LANES])
                mix_scr[_ds(n * LANES, LANES), M_A + g * LANES:M_A + (g + 1) * LANES] = (
                    merged.astype(BF16))

        _pipeline(N_BLK, scores, softmax, values)

    window_scratch = (
        pltpu.VMEM((N_BLK, WIN_ROWS, LANES), BF16),
        pltpu.VMEM((TOK + 2 * LANES, LANES), BF16),
        pltpu.VMEM((TOK + 2 * LANES, 2 * LANES), BF16),
        pltpu.VMEM((PAST_LEN, LANES), BF16),
        pltpu.VMEM((PAST_LEN, 2 * LANES), BF16),
        pltpu.VMEM((3, LANES, LANES), F32),
        pltpu.VMEM((WIN_ROWS, LANES), F32), pltpu.VMEM((WIN_ROWS, LANES), F32),
    )

    def retention(*scratch):
        r = _Retention(scratch, gn_ref, mix_scr)
        r.st[0] = s0_ref[...]
        r.prepare(proj, decay_ref, [(0, N_CHUNK)])
        ret.append(r)
        pl.run_scoped(window, *window_scratch)

    assert N_CHUNK == N_BLK
    pl.run_scoped(retention, *_ret_scratch(1, 1))

    def latent(qc_scr, kcat_scr, kvaug_scr, mixc_scr):
        lane_lo = _lane_iota((TOK, LANES)) < 64
        mc, msa, msb = mc_ref[...], msa_ref[...], msb_ref[...]
        qc = proj(C_QC, C_CKV)
        for h in range(MLA_HEADS):
            qc_scr[h] = _rope(qc[:, h * LANES:(h + 1) * LANES], mc, msa, msb, 8).astype(BF16)
        ckr = proj(C_CKV, D_IN_P)
        ckv = ckr[:, :LANES]
        kr = _rope(ckr[:, LANES:], mc, msa, msb, 8)
        ckv_n = ckv * lax.rsqrt(jnp.mean(ckv * ckv, axis=-1, keepdims=True) + EPS) * kvn_ref[...]
        kv = _dot(ckv_n.astype(BF16), wkvb_ref[...])
        kv_c = _dot(cckv_ref[...].astype(BF16), wkvb_ref[...])
        kr_c = ckr_ref[...]
        lane_lo_c = _lane_iota((PAST_LEN, LANES)) < 64
        for h in range(MLA_HEADS):
            kvh, kvh_c = kv[:, h * LANES:(h + 1) * LANES], kv_c[:, h * LANES:(h + 1) * LANES]
            kcat_scr[h, 0:TOK, :] = jnp.where(lane_lo, kvh, kr).astype(BF16)
            kcat_scr[h, TOK:, :] = jnp.where(lane_lo_c, kvh_c, kr_c).astype(BF16)
            kvaug_scr[h, 0:TOK, :LANES] = kvh.astype(BF16)
            kvaug_scr[h, TOK:, :LANES] = kvh_c.astype(BF16)
            kvaug_scr[h, :, LANES:] = jnp.ones((MLA_KEYS, LANES), BF16)
        lane_lo_m = _lane_iota((MLA_QB, LANES)) < 64
        n_qb = DEC_SEQ // MLA_QB

        def split(t):
            if isinstance(t, int):
                return t // n_qb, t % n_qb
            return lax.shift_right_logical(t, 2), lax.bitwise_and(t, n_qb - 1)

        def scores(t, b):
            jp, qb = split(t)
            for i in range(2):
                h = 2 * jp + i
                q = qc_scr[h, _ds(qb * MLA_QB, MLA_QB), :]
                for part in range(2):
                    r0 = (2 * i + part) * MLA_QB
                    sbuf[b][r0:r0 + MLA_QB, :] = _dot_nt(
                        q, kcat_scr[h, part * MLA_HALF:(part + 1) * MLA_HALF, :])

        def softmax(t, b):
            for i in range(2):
                for rt in range(MLA_QB // LANES):
                    ra = 2 * i * MLA_QB + rt * LANES
                    rb = ra + MLA_QB
                    ps, _ = _softmax_tile([sbuf[b][ra:ra + LANES, :], sbuf[b][rb:rb + LANES, :]],
                                          scale=MLA_SCALE)
                    pbuf[b][ra:ra + LANES, :] = ps[0]
                    pbuf[b][rb:rb + LANES, :] = ps[1]

        def values(t, b):
            jp, qb = split(t)
            outs = []
            for i in range(2):
                h = 2 * jp + i
                r0 = 2 * i * MLA_QB
                oc = (_dot(pbuf[b][r0:r0 + MLA_QB, :], kvaug_scr[h, 0:MLA_HALF, :])
                      + _dot(pbuf[b][r0 + MLA_QB:r0 + 2 * MLA_QB, :], kvaug_scr[h, MLA_HALF:, :]))
                outs.append(oc[:, :LANES] / oc[:, LANES:])
            merged = jnp.where(lane_lo_m, pltpu.roll(outs[0], 64, 1), outs[1])
            mixc_scr[jp, _ds(qb * MLA_QB, MLA_QB), :] = merged.astype(BF16)

        _pipeline(2 * n_qb, scores, softmax, values)
        for jp in range(2):
            mix_scr[:, M_C + jp * LANES:M_C + (jp + 1) * LANES] = mixc_scr[jp]

    pl.run_scoped(
        latent,
        pltpu.VMEM((MLA_HEADS, TOK, LANES), BF16),
        pltpu.VMEM((MLA_HEADS, MLA_KEYS, LANES), BF16),
        pltpu.VMEM((MLA_HEADS, MLA_KEYS, 2 * LANES), BF16),
        pltpu.VMEM((2, TOK, LANES), BF16),
    )
    wc.convert()
    xs.epilogue(mod_ref, wout_ref, mix_scr)


def _lat_mixer(layer, x, mod, n1, w_in_p, sink, decay, gn, kvn, w_kv_b, w_out_p,
               ck, cv, cckv, ckr, s0, rope_a, rope_m, w_up, w_down):
    n_tok = x.shape[0]
    n_seq = n_tok // DEC_SEQ
    assert n_seq == 2 * FF_PIECES
    hbm_spec = pl.BlockSpec(memory_space=pl.ANY)
    seq_spec = lambda shape: pl.BlockSpec(
        (None, None) + shape, lambda i: (i, layer) + (0,) * len(shape))
    assert WIN_ROWS == 4 * MLA_QB and KEYS_WIN == MLA_HALF
    return pl.pallas_call(
        functools.partial(_lat_mixer_kernel, layer),
        out_shape=(jax.ShapeDtypeStruct((n_tok, D_MODEL), F32),
                   jax.ShapeDtypeStruct((D_MODEL, D_FF), BF16),
                   jax.ShapeDtypeStruct((D_FF, D_MODEL), BF16)),
        grid=(n_seq,),
        in_specs=[
            hbm_spec,
            pl.BlockSpec((None, N_MOD, D_MODEL), lambda i: (i, 0, 0)),
            _const_spec((1, D_MODEL)),
            _const_spec((D_IN_P, D_MODEL), layer),
            _smem_spec(),
            _smem_spec(),
            _const_spec((1, RET_HEADS * HEAD_DIM)),
            _const_spec((1, MLA_KV_RANK)),
            _const_spec((MLA_KV_RANK, MLA_HEADS * LANES), layer),
            _const_spec((D_MODEL, D_MODEL), layer),
            seq_spec((PAST_LEN, LANES)),
            seq_spec((PAST_LEN, LANES)),
            seq_spec((PAST_LEN, MLA_KV_RANK)),
            seq_spec((PAST_LEN, LANES)),
            seq_spec((2, 2, LANES, LANES)),
        ] + [_const_spec((DEC_SEQ, LANES))] * 6 + [hbm_spec, hbm_spec],
        out_specs=(hbm_spec, hbm_spec, hbm_spec),
        scratch_shapes=[
            pltpu.VMEM((TOK, D_MODEL), BF16),
            pltpu.VMEM((TOK, D_MODEL), BF16),
            pltpu.VMEM((WIN_ROWS, KEYS_WIN), F32), pltpu.VMEM((WIN_ROWS, KEYS_WIN), F32),
            pltpu.VMEM((WIN_ROWS, KEYS_WIN), BF16), pltpu.VMEM((WIN_ROWS, KEYS_WIN), BF16),
        ] + _XStream.SCRATCH + _WeightCast.SCRATCH,
        compiler_params=pltpu.CompilerParams(
            dimension_semantics=("arbitrary",), vmem_limit_bytes=VMEM_LIMIT),
        name="lat_mixer",
    )(x, mod, n1, w_in_p, sink, decay, gn, kvn, w_kv_b, w_out_p, ck, cv, cckv, ckr, s0,
      *rope_a, *rope_m, w_up, w_down)


def _mlp_kernel(final, x_ref, mod_ref, n2_ref, wup_ref, wdn_ref, fn_ref, o_ref):
    x = x_ref[...]
    h2 = _norm_mod(x, n2_ref[...], mod_ref[4:5, :], mod_ref[3:4, :]).astype(BF16)
    acc = None
    for c in range(D_FF // FF_CHUNK):
        cols = slice(c * FF_CHUNK, (c + 1) * FF_CHUNK)
        u = jnp.maximum(_dot(h2, wup_ref[:, cols]), 0.0)
        part = _dot((u * u).astype(BF16), wdn_ref[cols, :])
        acc = part if acc is None else acc + part
    y = x + mod_ref[5:6, :] * acc
    if final:
        y = y * lax.rsqrt(jnp.mean(y * y, axis=-1, keepdims=True) + EPS) * fn_ref[...]
    o_ref[...] = y


def _mlp(x, mod, n2, w_up, w_down, final_norm, final):
    n_tok = x.shape[0]
    per_mod = n_tok // mod.shape[0] // MLP_ROWS
    return pl.pallas_call(
        functools.partial(_mlp_kernel, final),
        out_shape=jax.ShapeDtypeStruct((n_tok, D_MODEL), F32),
        grid=(n_tok // MLP_ROWS,),
        in_specs=[
            pl.BlockSpec((MLP_ROWS, D_MODEL), lambda i: (i, 0)),
            pl.BlockSpec((None, N_MOD, D_MODEL), lambda i: (i // per_mod, 0, 0)),
            _const_spec((1, D_MODEL)),
            _const_spec((D_MODEL, D_FF)),
            _const_spec((D_FF, D_MODEL)),
            _const_spec((1, D_MODEL)),
        ],
        out_specs=pl.BlockSpec((MLP_ROWS, D_MODEL), lambda i: (i, 0)),
        compiler_params=pltpu.CompilerParams(
            dimension_semantics=("arbitrary",), vmem_limit_bytes=VMEM_LIMIT),
        name="mlp",
    )(x, mod, n2, w_up, w_down, final_norm)


def kernel(x_prompt, x_sample, cache_win_k, cache_win_v, cache_mla_ckv, cache_mla_krope, state_ret,
           c, c_ctx, w_mod, b_mod, norm1, norm2, w_in, win_sink, ret_decay, ret_gn, mla_kv_norm,
           w_kv_b, w_out, w_up, w_down, final_norm):
    n_ctx, n_lat = x_prompt.shape[0], x_sample.shape[0]

    w_in_p, w_out_p = _proj_layout(jnp.swapaxes(w_in, 1, 2), w_out)
    w_kv_b16 = w_kv_b.astype(BF16)

    c_rows = jnp.zeros((16, D_MODEL), F32).at[0].set(c_ctx).at[1:1 + n_lat].set(c)
    mod = _modulation(c_rows, w_mod, b_mod).reshape(DEPTH, 16, N_MOD, D_MODEL)

    rope_a = _rope_tables(DEC_SEQ, HEAD_DIM, 0, HEAD_DIM)
    rope_m = _rope_tables(DEC_SEQ, MLA_ROPE, MLA_NOPE, LANES)

    ck = cache_win_k.reshape(n_lat, DEPTH, PAST_LEN, LANES)
    cv = cache_win_v.reshape(n_lat, DEPTH, PAST_LEN, LANES)
    ckr = jnp.pad(cache_mla_krope, ((0, 0), (0, 0), (0, 0), (MLA_NOPE, LANES - MLA_NOPE - MLA_ROPE)))
    sr = state_ret.reshape(n_lat, DEPTH, 2, 2, 2, HEAD_DIM, HEAD_DIM)
    zero = jnp.zeros_like(sr[:, :, :, :, 0])
    s0 = jnp.concatenate([jnp.concatenate([sr[:, :, :, :, 0], zero], axis=-1),
                          jnp.concatenate([zero, sr[:, :, :, :, 1]], axis=-1)], axis=-2)

    xp = x_prompt.reshape(n_ctx * SEQ, D_MODEL)
    xs = x_sample.reshape(n_lat * DEC_SEQ, D_MODEL)
    state = ()
    for l in range(DEPTH):
        last = l == DEPTH - 1
        shared = (norm1[l][None], w_in_p, win_sink[l], ret_decay[l], ret_gn[l][None],
                  mla_kv_norm[l][None], w_kv_b16, w_out_p)
        mod_ctx, mod_lat = mod[l, 0:1], mod[l, 1:1 + n_lat]
        xs, w_up16, w_down16 = _lat_mixer(l, xs, mod_lat, *shared, ck, cv, cache_mla_ckv, ckr, s0,
                                          rope_a, rope_m, w_up, w_down)
        xp, *state = _ctx_mixer(l, xp, mod_ctx[0], *shared, state)
        xp = _mlp(xp, mod_ctx, norm2[l][None], w_up16, w_down16, final_norm[None], last)
        xs = _mlp(xs, mod_lat, norm2[l][None], w_up16, w_down16, final_norm[None], last)
    new_k, new_v, new_ckv, new_kr, new_s = state
    return (xp.reshape(n_ctx, SEQ, D_MODEL), xs.reshape(n_lat, DEC_SEQ, D_MODEL),
            new_k.reshape(n_ctx, DEPTH, SEQ, 2, HEAD_DIM), new_v.reshape(n_ctx, DEPTH, SEQ, 2, HEAD_DIM),
            new_ckv, new_kr, new_s)
```
